```python
import jax, jax.numpy as jnp
from jax import lax
import numpy as np

D_MODEL = 1024
BATCH = 16
SEQ = 256
DEPTH = 1
DEC_BATCH = 8
DEC_SEQ = 1024
PAST_LEN = 512

GRID_W = 64
MIX_W = D_MODEL
RET_W = MIX_W // 2
RET_HEADS = 4
RET_HEAD_DIM = RET_W // RET_HEADS
SSD_W = MIX_W - RET_W
SSD_HEAD_DIM = 64
SSD_HEADS = SSD_W // SSD_HEAD_DIM
SSD_GROUPS = 2
SSD_STATE = 128
HEADS_PER_GROUP = SSD_HEADS // SSD_GROUPS
CONV_W = 5
CONV_CH = SSD_W + 2 * SSD_GROUPS * SSD_STATE
CHUNK = 64
D_FF = -(-8 * D_MODEL // (3 * 256)) * 256
ROPE_BASE = 10000.0
EPS = 1e-6
ALPHA = (2.0 * DEPTH) ** 0.25
BETA = (8.0 * DEPTH) ** -0.25
IN_SPLITS = (RET_W, 2 * RET_W, 3 * RET_W, 4 * RET_W, 4 * RET_W + SSD_W, 4 * RET_W + SSD_W + CONV_CH)
IN_COLS = 4 * RET_W + SSD_W + CONV_CH + SSD_HEADS

kernel_name = "hymba_retention_ssd_prefix_flow_step"


def layer_norm(x, g, b):
    xf = x.astype(jnp.float32)
    mu = jnp.mean(xf, axis=-1, keepdims=True)
    var = jnp.mean(jnp.square(xf - mu), axis=-1, keepdims=True)
    y = (xf - mu) * lax.rsqrt(var + EPS) * g.astype(jnp.float32) + b.astype(jnp.float32)
    return y.astype(x.dtype)


def rms_norm(x):
    xf = x.astype(jnp.float32)
    return xf * lax.rsqrt(jnp.mean(jnp.square(xf), axis=-1, keepdims=True) + EPS)


def rope_2d(n_tokens):
    rows = n_tokens // GRID_W
    t_row = jnp.repeat(jnp.arange(rows, dtype=jnp.float32), GRID_W)
    t_col = jnp.tile(jnp.arange(GRID_W, dtype=jnp.float32), rows)
    nf = RET_HEAD_DIM // 4
    inv = ROPE_BASE ** (-jnp.arange(nf, dtype=jnp.float32) / nf)
    ang = jnp.concatenate([t_row[:, None] * inv, t_col[:, None] * inv], axis=-1)
    return jnp.cos(ang), jnp.sin(ang)


def apply_rope(x, cos, sin):
    half = x.shape[-1] // 2
    x1 = x[..., :half].astype(jnp.float32)
    x2 = x[..., half:].astype(jnp.float32)
    return jnp.concatenate([x1 * cos - x2 * sin, x1 * sin + x2 * cos], axis=-1).astype(x.dtype)


def chunked_scan(q, k, v, log_a, s0, inclusive):
    f32 = jnp.float32
    b, h, L, dk = q.shape
    dv = v.shape[-1]
    n = L // CHUNK
    qc = q.astype(f32).reshape(b, h, n, CHUNK, dk)
    kc = k.astype(f32).reshape(b, h, n, CHUNK, dk)
    vc = v.astype(f32).reshape(b, h, n, CHUNK, dv)
    cum = jnp.cumsum(log_a.astype(f32).reshape(b, h, n, CHUNK), axis=-1)
    idx = jnp.arange(CHUNK)
    mask = (idx[:, None] >= idx[None, :]) if inclusive else (idx[:, None] > idx[None, :])
    decay = jnp.exp(jnp.where(mask, cum[..., :, None] - cum[..., None, :], -jnp.inf))
    scores = jnp.einsum('bhnid,bhnjd->bhnij', qc, kc) * decay
    intra = jnp.einsum('bhnij,bhnje->bhnie', scores, vc)
    tail = jnp.exp(cum[..., -1:] - cum)
    chunk_state = jnp.einsum('bhnj,bhnjd,bhnje->bhnde', tail, kc, vc)
    chunk_decay = jnp.exp(cum[..., -1])

    def step(s, inp):
        dec, cs = inp
        return dec[..., None, None] * s + cs, s

    s_final, s_enter = lax.scan(step, s0.astype(f32),
                                (jnp.moveaxis(chunk_decay, 2, 0), jnp.moveaxis(chunk_state, 2, 0)))
    s_enter = jnp.moveaxis(s_enter, 0, 2)
    cross = jnp.einsum('bhnid,bhnde->bhnie', qc * jnp.exp(cum)[..., None], s_enter)
    return (intra + cross).reshape(b, h, L, dv), s_final


def bidir_scan(q, k, v_f, v_b, log_a_f, log_a_b, s0_f, s0_b):
    o_f, s_f = chunked_scan(q, k, v_f, log_a_f, s0_f, True)
    flip = lambda t: jnp.flip(t, axis=2)
    o_b, s_b = chunked_scan(flip(q), flip(k), flip(v_b), flip(log_a_b), s0_b, False)
    return o_f + flip(o_b), s_f, s_b


def depthwise_conv(x, w, b):
    y = lax.conv_general_dilated(x, w[:, None, :], window_strides=(1,),
                                 padding=[(CONV_W // 2, CONV_W // 2)],
                                 dimension_numbers=('NWC', 'WIO', 'NWC'),
                                 feature_group_count=x.shape[-1])
    return y + b


def adaln(cond, w_ada, b_ada):
    return jax.nn.silu(cond) @ w_ada + b_ada


def trunk_layer(x, mod, rope, s_ret0, s_ssd0, p):
    bsz, L, _ = x.shape
    sh1, sc1, g1, sh2, sc2, g2 = jnp.split(mod, 6, axis=-1)
    h = x * (1 + sc1) + sh1
    proj = h @ p['w_in']
    q, k, v, g, z, xbc, dt_raw = jnp.split(proj, IN_SPLITS, axis=-1)

    heads = lambda t: t.reshape(bsz, L, RET_HEADS, RET_HEAD_DIM).transpose(0, 2, 1, 3)
    q, k, v = heads(q), heads(k), heads(v)
    if rope is not None:
        q = apply_rope(q, rope[0], rope[1])
        k = apply_rope(k, rope[0], rope[1])
    k = k * (RET_HEAD_DIM ** -0.5)
    lg_f = jnp.broadcast_to(jax.nn.log_sigmoid(p['ret_decay_fwd'].astype(jnp.float32))[None, :, None], (bsz, RET_HEADS, L))
    lg_b = jnp.broadcast_to(jax.nn.log_sigmoid(p['ret_decay_bwd'].astype(jnp.float32))[None, :, None], (bsz, RET_HEADS, L))
    o_ret, sr_f, sr_b = bidir_scan(q, k, v, v, lg_f, lg_b, s_ret0[0], s_ret0[1])
    o_ret = rms_norm(o_ret).transpose(0, 2, 1, 3).reshape(bsz, L, RET_W).astype(x.dtype)
    o_ret = jax.nn.silu(g) * o_ret

    xbc = jax.nn.silu(depthwise_conv(xbc, p['conv_w'], p['conv_b']))
    xs, bm, cm = jnp.split(xbc, (SSD_W, SSD_W + SSD_GROUPS * SSD_STATE), axis=-1)
    xs = xs.reshape(bsz, L, SSD_HEADS, SSD_HEAD_DIM).transpose(0, 2, 1, 3)
    grp = lambda t: jnp.repeat(t.reshape(bsz, L, SSD_GROUPS, SSD_STATE), HEADS_PER_GROUP, axis=2).transpose(0, 2, 1, 3)
    bm, cm = grp(bm), grp(cm)
    dt_raw = dt_raw.astype(jnp.float32)
    dt_f = jax.nn.softplus(dt_raw + p['dt_bias_fwd'].astype(jnp.float32)).transpose(0, 2, 1)
    dt_b = jax.nn.softplus(dt_raw + p['dt_bias_bwd'].astype(jnp.float32)).transpose(0, 2, 1)
    la_f = dt_f * (-jnp.exp(p['a_log_fwd'].astype(jnp.float32)))[None, :, None]
    la_b = dt_b * (-jnp.exp(p['a_log_bwd'].astype(jnp.float32)))[None, :, None]
    xsf = xs.astype(jnp.float32)
    y, ss_f, ss_b = bidir_scan(cm, bm, xsf * dt_f[..., None], xsf * dt_b[..., None], la_f, la_b, s_ssd0[0], s_ssd0[1])
    y = y + p['d_skip'].astype(jnp.float32)[None, :, None, None] * xsf
    y = y.transpose(0, 2, 1, 3).reshape(bsz, L, SSD_W)
    o_ssd = (rms_norm(y * jax.nn.silu(z.astype(jnp.float32))) * p['ssd_norm_w'].astype(jnp.float32)).astype(x.dtype)

    mix = jnp.concatenate([o_ret, o_ssd], axis=-1) @ p['w_out']
    x = layer_norm(ALPHA * x + g1 * mix, p['ln1_g'], p['ln1_b'])
    h2 = x * (1 + sc2) + sh2
    ffn = (jax.nn.silu(h2 @ p['w_gate']) * (h2 @ p['w_up'])) @ p['w_down']
    x = layer_norm(ALPHA * x + g2 * ffn, p['ln2_g'], p['ln2_b'])
    s_ret = jnp.stack([sr_f, sr_b], axis=1).astype(x.dtype)
    s_ssd = jnp.stack([ss_f, ss_b], axis=1).astype(x.dtype)
    return x, s_ret, s_ssd


def setup_inputs(seed: int = 0) -> dict:
    key = jax.random.key(seed)
    ks = jax.random.split(key, 32)
    f32 = jnp.float32
    nrm = lambda k, shape, s: jax.random.normal(k, shape, f32) * s
    gamma0 = 1.0 - 2.0 ** (-5.0 - np.arange(RET_HEADS))
    logit0 = jnp.asarray(np.log(gamma0 / (1.0 - gamma0)), f32)
    dt_f = jnp.exp(jax.random.uniform(ks[12], (DEPTH, SSD_HEADS), f32, np.log(1e-3), np.log(1e-1)))
    dt_b = jnp.exp(jax.random.uniform(ks[13], (DEPTH, SSD_HEADS), f32, np.log(1e-3), np.log(1e-1)))
    inv_sp = lambda d: d + jnp.log(-jnp.expm1(-d))
    return {
        'x_prompt': nrm(ks[0], (BATCH, SEQ, D_MODEL), 1.0),
        'x_sample': nrm(ks[1], (DEC_BATCH, DEC_SEQ, D_MODEL), 1.0),
        'state_ret': nrm(ks[2], (DEC_BATCH, DEPTH, 2, RET_HEADS, RET_HEAD_DIM, RET_HEAD_DIM), 0.1),
        'state_ssd': nrm(ks[3], (DEC_BATCH, DEPTH, 2, SSD_HEADS, SSD_STATE, SSD_HEAD_DIM), 0.1),
        'c': nrm(ks[4], (DEC_BATCH, D_MODEL), 1.0),
        'c_ctx': nrm(ks[5], (D_MODEL,), 0.5),
        'w_in': nrm(ks[6], (DEPTH, D_MODEL, IN_COLS), D_MODEL ** -0.5),
        'ret_decay_fwd': logit0[None, :] + nrm(ks[7], (DEPTH, RET_HEADS), 0.05),
        'ret_decay_bwd': logit0[None, :] + nrm(ks[8], (DEPTH, RET_HEADS), 0.05),
        'conv_w': nrm(ks[9], (DEPTH, CONV_W, CONV_CH), CONV_W ** -0.5),
        'conv_b': nrm(ks[10], (DEPTH, CONV_CH), 0.02),
        'dt_bias_fwd': inv_sp(dt_f),
        'dt_bias_bwd': inv_sp(dt_b),
        'a_log_fwd': jnp.log(jax.random.uniform(ks[14], (DEPTH, SSD_HEADS), f32, 1.0, 16.0)),
        'a_log_bwd': jnp.log(jax.random.uniform(ks[15], (DEPTH, SSD_HEADS), f32, 1.0, 16.0)),
        'd_skip': 1.0 + nrm(ks[16], (DEPTH, SSD_HEADS), 0.1),
        'ssd_norm_w': 1.0 + nrm(ks[17], (DEPTH, SSD_W), 0.1),
        'w_out': nrm(ks[18], (DEPTH, MIX_W, D_MODEL), BETA * MIX_W ** -0.5),
        'ln1_g': 1.0 + nrm(ks[19], (DEPTH, D_MODEL), 0.05),
        'ln1_b': nrm(ks[20], (DEPTH, D_MODEL), 0.02),
        'w_gate': nrm(ks[21], (DEPTH, D_MODEL, D_FF), D_MODEL ** -0.5),
        'w_up': nrm(ks[22], (DEPTH, D_MODEL, D_FF), D_MODEL ** -0.5),
        'w_down': nrm(ks[23], (DEPTH, D_FF, D_MODEL), BETA * D_FF ** -0.5),
        'ln2_g': 1.0 + nrm(ks[24], (DEPTH, D_MODEL), 0.05),
        'ln2_b': nrm(ks[25], (DEPTH, D_MODEL), 0.02),
        'w_ada': nrm(ks[26], (DEPTH, D_MODEL, 6 * D_MODEL), 0.5 * D_MODEL ** -0.5),
        'b_ada': nrm(ks[27], (DEPTH, 6 * D_MODEL), 0.02),
    }


def reference(x_prompt, x_sample, state_ret, state_ssd, c, c_ctx, w_in, ret_decay_fwd, ret_decay_bwd,
              conv_w, conv_b, dt_bias_fwd, dt_bias_bwd, a_log_fwd, a_log_bwd, d_skip, ssd_norm_w,
              w_out, ln1_g, ln1_b, w_gate, w_up, w_down, ln2_g, ln2_b, w_ada, b_ada):
    rope = rope_2d(x_sample.shape[1])
    xp, xs = x_prompt, x_sample
    bp = x_prompt.shape[0]
    new_ret, new_ssd = [], []
    for l in range(DEPTH):
        p = {'w_in': w_in[l], 'ret_decay_fwd': ret_decay_fwd[l], 'ret_decay_bwd': ret_decay_bwd[l],
             'conv_w': conv_w[l], 'conv_b': conv_b[l], 'dt_bias_fwd': dt_bias_fwd[l], 'dt_bias_bwd': dt_bias_bwd[l],
             'a_log_fwd': a_log_fwd[l], 'a_log_bwd': a_log_bwd[l], 'd_skip': d_skip[l], 'ssd_norm_w': ssd_norm_w[l],
             'w_out': w_out[l], 'ln1_g': ln1_g[l], 'ln1_b': ln1_b[l], 'w_gate': w_gate[l], 'w_up': w_up[l],
             'w_down': w_down[l], 'ln2_g': ln2_g[l], 'ln2_b': ln2_b[l]}
        mod_ctx = adaln(c_ctx[None, :], w_ada[l], b_ada[l])[:, None, :]
        mod_lat = adaln(c, w_ada[l], b_ada[l])[:, None, :]
        zr = jnp.zeros((bp, RET_HEADS, RET_HEAD_DIM, RET_HEAD_DIM), xp.dtype)
        zs = jnp.zeros((bp, SSD_HEADS, SSD_STATE, SSD_HEAD_DIM), xp.dtype)
        xp, s_ret, s_ssd = trunk_layer(xp, mod_ctx, None, (zr, zr), (zs, zs), p)
        new_ret.append(s_ret)
        new_ssd.append(s_ssd)
        xs, _, _ = trunk_layer(xs, mod_lat, rope, (state_ret[:, l, 0], state_ret[:, l, 1]),
                               (state_ssd[:, l, 0], state_ssd[:, l, 1]), p)
    new_state_ret = jnp.stack(new_ret, axis=1)
    new_state_ssd = jnp.stack(new_ssd, axis=1)
    return (xp, xs, new_state_ret, new_state_ssd)
```

```python
import functools
import math

import jax
import jax.numpy as jnp
from jax import lax
from jax.experimental import pallas as pl
from jax.experimental.pallas import tpu as pltpu

F32 = jnp.float32
BF16 = jnp.bfloat16

D_MODEL = 1024
RET_W = 512
RET_HEADS = 4
RET_HD = 128
SSD_W = 512
SSD_HD = 64
SSD_HEADS = 8
SSD_GROUPS = 2
SSD_STATE = 128
HPG = SSD_HEADS // SSD_GROUPS
GROUP_W = HPG * SSD_HD
CONV_W = 5
CONV_CH = SSD_W + 2 * SSD_GROUPS * SSD_STATE
D_FF = 2816
GRID_W = 64
GRID_SHIFT = 6
ROPE_BASE = 10000.0
EPS = 1e-6
ALPHA = 2.0 ** 0.25
MAIN_COLS = 4 * RET_W + SSD_W + CONV_CH
N_DT = 2 * SSD_HEADS

CHUNK = 256
CONV_PAD = 8
FFN_ROWS = 512
ADA_COLS = 512
LANES = 128
VMEM_LIMIT = 60 * 1024 * 1024


def _dot(a, b):
    return jnp.dot(a, b, preferred_element_type=F32)


def _dot_nt(a, b):
    return lax.dot_general(a, b, (((1,), (1,)), ((), ())), preferred_element_type=F32)


def _silu(x):
    return x * jax.nn.sigmoid(x)


def _softplus(x):
    return jnp.maximum(x, 0.0) + jnp.log1p(jnp.exp(-jnp.abs(x)))


def _layer_norm(y, g, b):
    mu = jnp.mean(y, axis=-1, keepdims=True)
    yc = y - mu
    var = jnp.mean(yc * yc, axis=-1, keepdims=True)
    return yc * lax.rsqrt(var + EPS) * g + b


def _cumsum(x, axis):
    n = x.shape[axis]
    idx = lax.broadcasted_iota(jnp.int32, x.shape, axis)
    s = 1
    while s < n:
        x = x + jnp.where(idx >= s, pltpu.roll(x, s, axis), 0.0)
        s *= 2
    return x


def _expand4(cols, lane):
    a = jnp.where(lane < SSD_HD, cols[0], cols[1])
    b = jnp.where(lane < SSD_HD, cols[2], cols[3])
    return jnp.concatenate([a, b], axis=1)


def _ada_kernel(cond_ref, w_ref, b_ref, o_ref):
    s = _silu(cond_ref[...]).astype(BF16)
    o_ref[...] = _dot(s, w_ref[...].astype(BF16)) + b_ref[...]


def _ada_call(cond, w_ada, b_ada):
    rows = cond.shape[0]
    n = w_ada.shape[1]
    return pl.pallas_call(
        _ada_kernel,
        grid=(n // ADA_COLS,),
        in_specs=[
            pl.BlockSpec((rows, D_MODEL), lambda j: (0, 0)),
            pl.BlockSpec((D_MODEL, ADA_COLS), lambda j: (0, j)),
            pl.BlockSpec((1, ADA_COLS), lambda j: (0, j)),
        ],
        out_specs=pl.BlockSpec((rows, ADA_COLS), lambda j: (0, j)),
        out_shape=jax.ShapeDtypeStruct((rows, n), F32),
        compiler_params=pltpu.CompilerParams(dimension_semantics=("arbitrary",)),
        name="adaln_mod",
    )(cond, w_ada, b_ada)


def _mixer_kernel(*refs, L, has_state, use_rope, emit_state):
    nc = L // CHUNK
    C = CHUNK
    cross = has_state or nc > 1
    it = iter(refs)
    x_ref, mod_ref, wmain_ref, wdtc_ref, wdtr_ref = (next(it) for _ in range(5))
    convw_ref, convb_ref, hp_ref, hpc_ref, normw_ref = (next(it) for _ in range(5))
    wout_ref, ln1g_ref, ln1b_ref = (next(it) for _ in range(3))
    if has_state:
        sret0_ref, sssd0_ref = next(it), next(it)
    x1_ref = next(it)
    if emit_state:
        nret_ref, nssd_ref = next(it), next(it)
    (q_s, kT_s, v_s, g_s, z_s, xbc_s, xs_s, bT_s, c_s, lacol_s, dtcol_s, inccol_s, larow_s,
     dtrow_s, y_s, mix_s, rloc_s, sloc_s, rent_s, sent_s) = (next(it) for _ in range(20))

    step = pl.program_id(1)

    hp = hp_ref[...]
    dt_bias_row = hp[0:1, :]
    nega_row = -jnp.exp(hp[1:2, :])
    lg_row = -_softplus(-hp[2:3, :])
    dskip_row = hp[3:4, :]
    hpc = hpc_ref[...]
    dt_bias_col = hpc[:, 0:1]
    nega_col = -jnp.exp(hpc[:, 1:2])

    lane = lax.broadcasted_iota(jnp.int32, (1, LANES), 1)
    icol = lax.broadcasted_iota(jnp.int32, (C, 1), 0).astype(F32)

    def ret_decays(hd):
        lgf = lg_row[:, hd:hd + 1]
        lgb = lg_row[:, RET_HEADS + hd:RET_HEADS + hd + 1]
        return lgf, lgb

    def group_heads(gi):
        return [gi * HPG + k for k in range(HPG)]

    def expand_f(arr, gi):
        return _expand4([arr[:, hh:hh + 1] for hh in group_heads(gi)], lane)

    def expand_b(arr, gi):
        return _expand4([arr[:, SSD_HEADS + hh:SSD_HEADS + hh + 1] for hh in group_heads(gi)], lane)

    def project(c):
        sh1 = mod_ref[0, 0:1, :]
        sc1 = mod_ref[0, 1:2, :]
        h = (x_ref[0] * (1.0 + sc1) + sh1).astype(BF16)

        if use_rope:
            t = lax.broadcasted_iota(jnp.int32, (C, LANES), 0) + c * C
            ln = lax.broadcasted_iota(jnp.int32, (C, LANES), 1)
            nf = RET_HD // 4
            inv = jnp.exp((ln & (nf - 1)).astype(F32) * (-math.log(ROPE_BASE) / nf))
            pos = jnp.where((ln & (2 * nf - 1)) < nf, t >> GRID_SHIFT, t & (GRID_W - 1)).astype(F32)
            ang = pos * inv
            cosf = jnp.cos(ang)
            sinf = jnp.where(ln < RET_HD // 2, -jnp.sin(ang), jnp.sin(ang))

        def rope(a):
            if not use_rope:
                return a
            return a * cosf + pltpu.roll(a, RET_HD // 2, 1) * sinf

        pq = _dot(h, wmain_ref[:, 0:RET_W])
        for hd in range(RET_HEADS):
            sl = slice(hd * RET_HD, (hd + 1) * RET_HD)
            q_s[c, :, sl] = rope(pq[:, sl]).astype(BF16)
        pk = _dot(h, wmain_ref[:, RET_W:2 * RET_W])
        for hd in range(RET_HEADS):
            sl = slice(hd * RET_HD, (hd + 1) * RET_HD)
            kh = rope(pk[:, sl]) * (RET_HD ** -0.5)
            kT_s[c, sl, :] = kh.T.astype(BF16)
        v_s[c] = _dot(h, wmain_ref[:, 2 * RET_W:3 * RET_W]).astype(BF16)
        g_s[c] = _silu(_dot(h, wmain_ref[:, 3 * RET_W:4 * RET_W])).astype(BF16)
        z_s[c] = _silu(_dot(h, wmain_ref[:, 4 * RET_W:4 * RET_W + SSD_W]))
        row0 = pl.multiple_of(CONV_PAD + c * C, 8)
        xbc_s[pl.ds(row0, C), :] = _dot(h, wmain_ref[:, 4 * RET_W + SSD_W:MAIN_COLS])

        dt_c = _softplus(_dot(h, wdtc_ref[...]) + dt_bias_row)
        dtcol_s[c] = dt_c
        lacol_s[c] = dt_c * nega_row
        dt_r = _softplus(_dot_nt(wdtr_ref[...], h) + dt_bias_col)
        dtrow_s[c] = dt_r
        larow_s[c] = dt_r * nega_col

    def build_states():
        xbc_s[0:CONV_PAD, :] = jnp.zeros((CONV_PAD, CONV_CH), F32)
        xbc_s[CONV_PAD + L:CONV_PAD + L + CONV_PAD, :] = jnp.zeros((CONV_PAD, CONV_CH), F32)
        half = CONV_W // 2
        for c in range(nc):
            r0 = c * C
            for cb in range(CONV_CH // 512):
                cs = slice(cb * 512, (cb + 1) * 512)
                acc = jnp.broadcast_to(convb_ref[0:1, cs], (C, 512))
                for k in range(CONV_W):
                    o = CONV_PAD + r0 + k - half
                    acc = acc + convw_ref[k:k + 1, cs] * xbc_s[o:o + C, cs]
                a = _silu(acc)
                if cb == 0:
                    xs_s[c] = a
                else:
                    bT_s[c] = a[:, 0:SSD_GROUPS * SSD_STATE].T.astype(BF16)
                    c_s[c] = a[:, SSD_GROUPS * SSD_STATE:].astype(BF16)

        def local_states(c, carry):
            lac = lacol_s[c]
            dtc = dtcol_s[c]
            inc_col = _cumsum(lac, 0)
            inccol_s[c] = inc_col
            tot_col = inc_col[C - 1:C, :]
            sf_col = jnp.exp(tot_col - inc_col) * dtc
            sb_col = jnp.exp(inc_col - lac) * dtc
            for gi in range(SSD_GROUPS):
                xg = xs_s[c, :, gi * GROUP_W:(gi + 1) * GROUP_W]
                vcat = jnp.concatenate([xg * expand_f(sf_col, gi), xg * expand_b(sb_col, gi)],
                                       axis=1).astype(BF16)
                sloc_s[c, gi] = _dot(bT_s[c, gi * SSD_STATE:(gi + 1) * SSD_STATE, :], vcat)
            for hd in range(RET_HEADS):
                sl = slice(hd * RET_HD, (hd + 1) * RET_HD)
                lgf, lgb = ret_decays(hd)
                vf = v_s[c, :, sl].astype(F32)
                vcat = jnp.concatenate([vf * jnp.exp((C - 1.0 - icol) * lgf), vf * jnp.exp(icol * lgb)],
                                       axis=1).astype(BF16)
                rloc_s[c, hd] = _dot(kT_s[c, sl, :], vcat)
            return carry

        lax.fori_loop(0, nc, local_states, 0)

        for hd in range(RET_HEADS):
            lgf, lgb = ret_decays(hd)
            dec_f = jnp.exp(C * lgf)
            dec_b = jnp.exp(C * lgb)
            if has_state:
                ent_f = sret0_ref[0, 0, hd]
                ent_b = sret0_ref[0, 1, hd]
            else:
                ent_f = jnp.zeros((RET_HD, RET_HD), F32)
                ent_b = jnp.zeros((RET_HD, RET_HD), F32)
            for c in range(nc):
                if cross:
                    rent_s[c, hd, :, 0:RET_HD] = ent_f.astype(BF16)
                ent_f = dec_f * ent_f + rloc_s[c, hd, :, 0:RET_HD]
            for c in range(nc - 1, -1, -1):
                if cross:
                    rent_s[c, hd, :, RET_HD:] = ent_b.astype(BF16)
                ent_b = dec_b * ent_b + rloc_s[c, hd, :, RET_HD:]
            if emit_state:
                nret_ref[0, 0, hd] = ent_f
                nret_ref[0, 1, hd] = ent_b

        for gi in range(SSD_GROUPS):
            heads = group_heads(gi)
            if has_state:
                ent_f = jnp.concatenate([sssd0_ref[0, 0, hh] for hh in heads], axis=1)
                ent_b = jnp.concatenate([sssd0_ref[0, 1, hh] for hh in heads], axis=1)
            else:
                ent_f = jnp.zeros((SSD_STATE, GROUP_W), F32)
                ent_b = jnp.zeros((SSD_STATE, GROUP_W), F32)
            for c in range(nc):
                if cross:
                    sent_s[c, gi, :, 0:GROUP_W] = ent_f.astype(BF16)
                dec = jnp.exp(inccol_s[c, C - 1:C, :])
                ent_f = expand_f(dec, gi) * ent_f + sloc_s[c, gi, :, 0:GROUP_W]
            for c in range(nc - 1, -1, -1):
                if cross:
                    sent_s[c, gi, :, GROUP_W:] = ent_b.astype(BF16)
                dec = jnp.exp(inccol_s[c, C - 1:C, :])
                ent_b = expand_b(dec, gi) * ent_b + sloc_s[c, gi, :, GROUP_W:]
            if emit_state:
                for k, hh in enumerate(heads):
                    nssd_ref[0, 0, hh] = ent_f[:, k * SSD_HD:(k + 1) * SSD_HD]
                    nssd_ref[0, 1, hh] = ent_b[:, k * SSD_HD:(k + 1) * SSD_HD]

    def emit_chunk(c):
        ii = lax.broadcasted_iota(jnp.int32, (C, C), 0)
        jj = lax.broadcasted_iota(jnp.int32, (C, C), 1)
        causal = jj <= ii
        dmat = (ii - jj).astype(F32)

        for hd in range(RET_HEADS):
            sl = slice(hd * RET_HD, (hd + 1) * RET_HD)
            lgf, lgb = ret_decays(hd)
            wdec = jnp.exp(jnp.where(causal, dmat * lgf, -dmat * lgb))
            q = q_s[c, :, sl]
            m = (_dot(q, kT_s[c, sl, :]) * wdec).astype(BF16)
            o = _dot(m, v_s[c, :, sl])
            if cross:
                yc = _dot(q, rent_s[c, hd])
                o = o + jnp.exp((icol + 1.0) * lgf) * yc[:, 0:RET_HD]
                o = o + jnp.exp((C - icol) * lgb) * yc[:, RET_HD:]
            o = o * lax.rsqrt(jnp.mean(o * o, axis=-1, keepdims=True) + EPS)
            mix_s[:, sl] = (g_s[c, :, sl].astype(F32) * o).astype(BF16)

        lac = lacol_s[c]
        inc_col = inccol_s[c]
        exc_col = inc_col - lac
        tot_col = inc_col[C - 1:C, :]
        lar = larow_s[c]
        dtr = dtrow_s[c]
        inc_row = _cumsum(lar, 1)
        exc_row = inc_row - lar
        if cross:
            cf_col = jnp.exp(inc_col)
            cb_col = jnp.exp(tot_col - exc_col)
        for gi in range(SSD_GROUPS):
            heads = group_heads(gi)
            gs = slice(gi * SSD_STATE, (gi + 1) * SSD_STATE)
            xsl = slice(gi * GROUP_W, (gi + 1) * GROUP_W)
            cm = c_s[c, :, gs]
            scores = _dot(cm, bT_s[c, gs, :])
            xg = xs_s[c, :, xsl]
            xgb = xg.astype(BF16)
            for k, hh in enumerate(heads):
                hb = SSD_HEADS + hh
                arg = jnp.where(causal,
                                inc_col[:, hh:hh + 1] - inc_row[hh:hh + 1, :],
                                exc_row[hb:hb + 1, :] - exc_col[:, hb:hb + 1])
                w = jnp.exp(arg) * jnp.where(causal, dtr[hh:hh + 1, :], dtr[hb:hb + 1, :])
                m = (scores * w).astype(BF16)
                y_s[:, hh * SSD_HD:(hh + 1) * SSD_HD] = _dot(m, xgb[:, k * SSD_HD:(k + 1) * SSD_HD])
            yg = y_s[:, xsl] + expand_f(dskip_row, gi) * xg
            if cross:
                yc = _dot(cm, sent_s[c, gi])
                yg = yg + expand_f(cf_col, gi) * yc[:, 0:GROUP_W]
                yg = yg + expand_b(cb_col, gi) * yc[:, GROUP_W:]
            y_s[:, xsl] = yg

        yz = y_s[...] * z_s[c]
        yn = yz * lax.rsqrt(jnp.mean(yz * yz, axis=-1, keepdims=True) + EPS) * normw_ref[...]
        mix_s[:, RET_W:] = yn.astype(BF16)
        g1 = mod_ref[0, 2:3, :]
        y = ALPHA * x_ref[0] + g1 * _dot(mix_s[...], wout_ref[...])
        x1_ref[0] = _layer_norm(y, ln1g_ref[...], ln1b_ref[...])

    @pl.when(step < nc)
    def _():
        project(step)

    @pl.when(step == nc)
    def _():
        build_states()

    @pl.when(step >= nc)
    def _():
        emit_chunk(step - nc)


def _const_spec(shape, ngrid):
    nd = len(shape)
    if ngrid == 1:
        return pl.BlockSpec(shape, lambda i: (0,) * nd, pipeline_mode=pl.Buffered(1))
    return pl.BlockSpec(shape, lambda b, s: (0,) * nd, pipeline_mode=pl.Buffered(1))


def _mixer_call(x, mod, mod_per_seq, weights, states, emit_state, use_rope, name):
    nb, L, _ = x.shape
    nc = L // CHUNK
    has_state = states is not None
    mod_map = (lambda b, s: (b, 0, 0)) if mod_per_seq else (lambda b, s: (0, 0, 0))
    in_specs = [
        pl.BlockSpec((1, CHUNK, D_MODEL), lambda b, s: (b, jnp.where(s < nc, s, s - nc), 0)),
        pl.BlockSpec((1, 6, D_MODEL), mod_map),
    ] + [_const_spec(w.shape, 2) for w in weights]
    args = [x, mod] + list(weights)
    ret_block = (1, 2, RET_HEADS, RET_HD, RET_HD)
    ssd_block = (1, 2, SSD_HEADS, SSD_STATE, SSD_HD)
    state_map = lambda b, s: (b, 0, 0, 0, 0)
    if has_state:
        in_specs += [pl.BlockSpec(ret_block, state_map), pl.BlockSpec(ssd_block, state_map)]
        args += list(states)
    out_shape = [jax.ShapeDtypeStruct((nb, L, D_MODEL), F32)]
    out_specs = [pl.BlockSpec((1, CHUNK, D_MODEL), lambda b, s: (b, jnp.maximum(s - nc, 0), 0))]
    if emit_state:
        out_shape += [jax.ShapeDtypeStruct((nb,) + ret_block[1:], F32),
                      jax.ShapeDtypeStruct((nb,) + ssd_block[1:], F32)]
        out_specs += [pl.BlockSpec(ret_block, state_map), pl.BlockSpec(ssd_block, state_map)]
    scratch = [
        pltpu.VMEM((nc, CHUNK, RET_W), BF16),
        pltpu.VMEM((nc, RET_W, CHUNK), BF16),
        pltpu.VMEM((nc, CHUNK, RET_W), BF16),
        pltpu.VMEM((nc, CHUNK, RET_W), BF16),
        pltpu.VMEM((nc, CHUNK, SSD_W), F32),
        pltpu.VMEM((L + 2 * CONV_PAD, CONV_CH), F32),
        pltpu.VMEM((nc, CHUNK, SSD_W), F32),
        pltpu.VMEM((nc, SSD_GROUPS * SSD_STATE, CHUNK), BF16),
        pltpu.VMEM((nc, CHUNK, SSD_GROUPS * SSD_STATE), BF16),
        pltpu.VMEM((nc, CHUNK, LANES), F32),
        pltpu.VMEM((nc, CHUNK, LANES), F32),
        pltpu.VMEM((nc, CHUNK, LANES), F32),
        pltpu.VMEM((nc, N_DT, CHUNK), F32),
        pltpu.VMEM((nc, N_DT, CHUNK), F32),
        pltpu.VMEM((CHUNK, SSD_W), F32),
        pltpu.VMEM((CHUNK, 2 * RET_W), BF16),
        pltpu.VMEM((nc, RET_HEADS, RET_HD, 2 * RET_HD), F32),
        pltpu.VMEM((nc, SSD_GROUPS, SSD_STATE, 2 * GROUP_W), F32),
        pltpu.VMEM((nc, RET_HEADS, RET_HD, 2 * RET_HD), BF16),
        pltpu.VMEM((nc, SSD_GROUPS, SSD_STATE, 2 * GROUP_W), BF16),
    ]
    kern = functools.partial(_mixer_kernel, L=L, has_state=has_state, use_rope=use_rope,
                             emit_state=emit_state)
    return pl.pallas_call(
        kern,
        grid=(nb, 2 * nc),
        in_specs=in_specs,
        out_specs=out_specs,
        out_shape=out_shape,
        scratch_shapes=scratch,
        compiler_params=pltpu.CompilerParams(dimension_semantics=("arbitrary", "arbitrary"),
                                             vmem_limit_bytes=VMEM_LIMIT),
        name=name,
    )(*args)


def _ffn_kernel(x_ref, mod_ref, wg_ref, wu_ref, wd_ref, g_ref, b_ref, o_ref):
    x = x_ref[0]
    sh2 = mod_ref[0, 3:4, :]
    sc2 = mod_ref[0, 4:5, :]
    g2 = mod_ref[0, 5:6, :]
    h2 = (x * (1.0 + sc2) + sh2).astype(BF16)
    hid = (_silu(_dot(h2, wg_ref[...])) * _dot(h2, wu_ref[...])).astype(BF16)
    y = ALPHA * x + g2 * _dot(hid, wd_ref[...])
    o_ref[0] = _layer_norm(y, g_ref[...], b_ref[...])


def _ffn_call(x1, mod, blocks_per_mod, weights, name):
    nblk = x1.shape[0]
    if blocks_per_mod:
        mod_map = lambda i: (i // blocks_per_mod, 0, 0)
    else:
        mod_map = lambda i: (0, 0, 0)
    return pl.pallas_call(
        _ffn_kernel,
        grid=(nblk,),
        in_specs=[
            pl.BlockSpec((1, FFN_ROWS, D_MODEL), lambda i: (i, 0, 0)),
            pl.BlockSpec((1, 6, D_MODEL), mod_map),
        ] + [_const_spec(w.shape, 1) for w in weights],
        out_specs=pl.BlockSpec((1, FFN_ROWS, D_MODEL), lambda i: (i, 0, 0)),
        out_shape=jax.ShapeDtypeStruct(x1.shape, F32),
        compiler_params=pltpu.CompilerParams(dimension_semantics=("arbitrary",),
                                             vmem_limit_bytes=VMEM_LIMIT),
        name=name,
    )(x1, mod, *weights)


def kernel(x_prompt, x_sample, state_ret, state_ssd, c, c_ctx, w_in, ret_decay_fwd, ret_decay_bwd, conv_w, conv_b, dt_bias_fwd, dt_bias_bwd, a_log_fwd, a_log_bwd, d_skip, ssd_norm_w, w_out, ln1_g, ln1_b, w_gate, w_up, w_down, ln2_g, ln2_b, w_ada, b_ada):
    depth = w_in.shape[0]
    assert depth == 1, "single trunk layer"
    bp, lp, _ = x_prompt.shape
    bs, ls, _ = x_sample.shape
    assert lp % CHUNK == 0 and ls % CHUNK == 0 and ls % GRID_W == 0

    rows = -(-(bs + 1) // 8) * 8
    cond = jnp.zeros((rows, D_MODEL), F32).at[:bs].set(c).at[bs].set(c_ctx)
    mod = _ada_call(cond, w_ada[0], b_ada[0][None, :]).reshape(rows, 6, D_MODEL)
    mod_lat = mod[:bs]
    mod_ctx = mod[bs:bs + 1]

    w = w_in[0]
    wmain = w[:, :MAIN_COLS].astype(BF16)
    wdt = w[:, MAIN_COLS:]
    wdt2 = jnp.concatenate([wdt, wdt], axis=1)
    wdtc = jnp.zeros((D_MODEL, LANES), F32).at[:, :N_DT].set(wdt2).astype(BF16)
    wdtr = wdt2.T.astype(BF16)
    convw = jnp.zeros((8, CONV_CH), F32).at[:CONV_W].set(conv_w[0])
    convb = conv_b[0][None, :]
    dt_bias = jnp.concatenate([dt_bias_fwd[0], dt_bias_bwd[0]])
    a_log = jnp.concatenate([a_log_fwd[0], a_log_bwd[0]])
    ret_decay = jnp.concatenate([ret_decay_fwd[0], ret_decay_bwd[0]])
    hp = jnp.zeros((8, LANES), F32)
    hp = hp.at[0, :N_DT].set(dt_bias).at[1, :N_DT].set(a_log)
    hp = hp.at[2, :2 * RET_HEADS].set(ret_decay).at[3, :SSD_HEADS].set(d_skip[0])
    hpc = jnp.zeros((N_DT, LANES), F32).at[:, 0].set(dt_bias).at[:, 1].set(a_log)
    mixer_w = (wmain, wdtc, wdtr, convw, convb, hp, hpc, ssd_norm_w[0][None, :],
               w_out[0].astype(BF16), ln1_g[0][None, :], ln1_b[0][None, :])
    ffn_w = (w_gate[0].astype(BF16), w_up[0].astype(BF16), w_down[0].astype(BF16),
             ln2_g[0][None, :], ln2_b[0][None, :])

    x1p, new_ret, new_ssd = _mixer_call(x_prompt, mod_ctx, False, mixer_w, None, True, False, "mixer_ctx")
    yp = _ffn_call(x1p.reshape(bp * lp // FFN_ROWS, FFN_ROWS, D_MODEL), mod_ctx, 0, ffn_w, "ffn_ctx")
    yp = yp.reshape(bp, lp, D_MODEL)

    states = (state_ret.reshape(bs, 2, RET_HEADS, RET_HD, RET_HD),
              state_ssd.reshape(bs, 2, SSD_HEADS, SSD_STATE, SSD_HD))
    (x1s,) = _mixer_call(x_sample, mod_lat, True, mixer_w, states, False, True, "mixer_lat")
    ys = _ffn_call(x1s.reshape(bs * ls // FFN_ROWS, FFN_ROWS, D_MODEL), mod_lat, ls // FFN_ROWS, ffn_w, "ffn_lat")
    ys = ys.reshape(bs, ls, D_MODEL)

    return (yp, ys, new_ret[:, None], new_ssd[:, None])
```

```python
import functools
import math

import jax
import jax.numpy as jnp
from jax import lax
from jax.experimental import pallas as pl
from jax.experimental.pallas import tpu as pltpu

F32 = jnp.float32
BF16 = jnp.bfloat16

D_MODEL = 1024
RET_W = 512
RET_HEADS = 4
RET_HD = 128
SSD_W = 512
SSD_HD = 64
SSD_HEADS = 8
SSD_GROUPS = 2
SSD_STATE = 128
HPG = SSD_HEADS // SSD_GROUPS
GROUP_W = HPG * SSD_HD
CONV_W = 5
CONV_CH = SSD_W + 2 * SSD_GROUPS * SSD_STATE
D_FF = 2816
GRID_W = 64
GRID_SHIFT = 6
ROPE_BASE = 10000.0
EPS = 1e-6
ALPHA = 2.0 ** 0.25
MAIN_COLS = 4 * RET_W + SSD_W + CONV_CH
XBC_COL0 = 4 * RET_W + SSD_W
N_DT = 2 * SSD_HEADS

CHUNK = 256
HALO = 8
FFN_ROWS = 512
ADA_COLS = 512
LANES = 128
VMEM_LIMIT = 60 * 1024 * 1024


def _dot(a, b):
    return jnp.dot(a, b, preferred_element_type=F32)


def _dot_nt(a, b):
    return lax.dot_general(a, b, (((1,), (1,)), ((), ())), preferred_element_type=F32)


def _silu(x):
    return x * jax.nn.sigmoid(x)


def _softplus(x):
    return jnp.maximum(x, 0.0) + jnp.log1p(jnp.exp(-jnp.abs(x)))


def _layer_norm(y, g, b):
    mu = jnp.mean(y, axis=-1, keepdims=True)
    yc = y - mu
    var = jnp.mean(yc * yc, axis=-1, keepdims=True)
    return yc * lax.rsqrt(var + EPS) * g + b


def _cumsum(x, axis):
    n = x.shape[axis]
    idx = lax.broadcasted_iota(jnp.int32, x.shape, axis)
    s = 1
    while s < n:
        x = x + jnp.where(idx >= s, pltpu.roll(x, s, axis), 0.0)
        s *= 2
    return x


def _expand4(cols, lane):
    a = jnp.where(lane < SSD_HD, cols[0], cols[1])
    b = jnp.where(lane < SSD_HD, cols[2], cols[3])
    return jnp.concatenate([a, b], axis=1)


def _ada_kernel(cond_ref, w_ref, b_ref, o_ref):
    s = _silu(cond_ref[...]).astype(BF16)
    o_ref[...] = _dot(s, w_ref[...].astype(BF16)) + b_ref[...]


def _ada_call(cond, w_ada, b_ada):
    rows = cond.shape[0]
    n = w_ada.shape[1]
    return pl.pallas_call(
        _ada_kernel,
        grid=(n // ADA_COLS,),
        in_specs=[
            pl.BlockSpec((rows, D_MODEL), lambda j: (0, 0)),
            pl.BlockSpec((D_MODEL, ADA_COLS), lambda j: (0, j)),
            pl.BlockSpec((1, ADA_COLS), lambda j: (0, j)),
        ],
        out_specs=pl.BlockSpec((rows, ADA_COLS), lambda j: (0, j)),
        out_shape=jax.ShapeDtypeStruct((rows, n), F32),
        compiler_params=pltpu.CompilerParams(dimension_semantics=("arbitrary",)),
        name="adaln_mod",
    )(cond, w_ada, b_ada)


RV_TAIL_F, RV_TAIL_B, RV_CROSS_F, RV_CROSS_B = range(4)


def _mixer_kernel(*refs, L, has_state, use_rope, emit_state):
    nc = L // CHUNK
    C = CHUNK
    cross = has_state or nc > 1
    it = iter(refs)
    x_ref, xprev_ref, xnext_ref, mod_ref, wmain_ref, wdtc_ref, wdtr_ref = (next(it) for _ in range(7))
    convw_ref, convb_ref, hp_ref, hpc_ref, normw_ref = (next(it) for _ in range(5))
    wout_ref, ln1g_ref, ln1b_ref = (next(it) for _ in range(3))
    if has_state:
        sret0_ref, sssd0_ref = next(it), next(it)
    x1_ref = next(it)
    if emit_state:
        nret_ref, nssd_ref = next(it), next(it)
    (q_s, kT_s, v_s, g_s, z_s, stage_s, xs_s, bT_s, c_s, inccol_s, exccol_s, cfcb_s, dec_s,
     rowarg_s, y_s, mix_s, rloc_s, sloc_s, rent_s, sent_s, wdec_s, rvec_s) = (next(it) for _ in range(22))
    if use_rope:
        cos_s, sin_s = next(it), next(it)

    seq = pl.program_id(0)
    step = pl.program_id(1)

    hp = hp_ref[...]
    dt_bias_row = hp[0:1, :]
    nega_row = -jnp.exp(hp[1:2, :])
    lg_row = -_softplus(-hp[2:3, :])
    dskip_row = hp[3:4, :]
    hpc = hpc_ref[...]
    dt_bias_col = hpc[:, 0:1]
    nega_col = -jnp.exp(hpc[:, 1:2])

    lane = lax.broadcasted_iota(jnp.int32, (1, LANES), 1)

    def ret_decays(hd):
        lgf = lg_row[:, hd:hd + 1]
        lgb = lg_row[:, RET_HEADS + hd:RET_HEADS + hd + 1]
        return lgf, lgb

    def group_heads(gi):
        return [gi * HPG + k for k in range(HPG)]

    def expand_f(arr, gi):
        return _expand4([arr[:, hh:hh + 1] for hh in group_heads(gi)], lane)

    def expand_b(arr, gi):
        return _expand4([arr[:, SSD_HEADS + hh:SSD_HEADS + hh + 1] for hh in group_heads(gi)], lane)

    @pl.when((seq == 0) & (step == 0))
    def _():
        ii = lax.broadcasted_iota(jnp.int32, (C, C), 0)
        jj = lax.broadcasted_iota(jnp.int32, (C, C), 1)
        dmat = (ii - jj).astype(F32)
        irow = lax.broadcasted_iota(jnp.int32, (C, LANES), 0).astype(F32)
        for hd in range(RET_HEADS):
            lgf, lgb = ret_decays(hd)
            wdec_s[hd] = jnp.exp(jnp.where(jj <= ii, dmat * lgf, -dmat * lgb))
            rvec_s[hd, RV_TAIL_F] = jnp.exp((C - 1.0 - irow) * lgf)
            rvec_s[hd, RV_TAIL_B] = jnp.exp(irow * lgb)
            rvec_s[hd, RV_CROSS_F] = jnp.exp((irow + 1.0) * lgf)
            rvec_s[hd, RV_CROSS_B] = jnp.exp((C - irow) * lgb)
        if use_rope:
            ln = lax.broadcasted_iota(jnp.int32, (C, LANES), 1)
            nf = RET_HD // 4
            inv = jnp.exp((ln & (nf - 1)).astype(F32) * (-math.log(ROPE_BASE) / nf))
            for cc in range(nc):
                t = lax.broadcasted_iota(jnp.int32, (C, LANES), 0) + cc * C
                pos = jnp.where((ln & (2 * nf - 1)) < nf, t >> GRID_SHIFT, t & (GRID_W - 1)).astype(F32)
                ang = pos * inv
                cos_s[cc] = jnp.cos(ang)
                sin_s[cc] = jnp.where(ln < RET_HD // 2, -jnp.sin(ang), jnp.sin(ang))

    def project(c):
        sh1 = mod_ref[0, 0:1, :]
        sc1 = mod_ref[0, 1:2, :]
        xe = jnp.concatenate([xprev_ref[0], x_ref[0], xnext_ref[0]], axis=0)
        xm = xe * (1.0 + sc1) + sh1
        he = xm.astype(BF16)
        h = xm[HALO:HALO + C].astype(BF16)

        dt_c = _softplus(_dot(h, wdtc_ref[...]) + dt_bias_row)
        lac = dt_c * nega_row
        inc_col = _cumsum(lac, 0)
        exc_col = inc_col - lac
        tot_col = inc_col[C - 1:C, :]
        inccol_s[c] = inc_col
        exccol_s[c] = exc_col
        dec_s[c] = jnp.broadcast_to(jnp.exp(tot_col), (8, LANES))
        if cross:
            cfcb_s[c] = jnp.exp(jnp.where(lane < SSD_HEADS, inc_col, tot_col - exc_col))
        sf_col = jnp.exp(tot_col - inc_col) * dt_c
        sb_col = jnp.exp(exc_col) * dt_c

        dt_r = _softplus(_dot_nt(wdtr_ref[...], h) + dt_bias_col)
        lar = dt_r * nega_col
        inc_row = _cumsum(lar, 1)
        ldt = jnp.log(dt_r)
        rid = lax.broadcasted_iota(jnp.int32, (N_DT, C), 0)
        rowarg_s[c] = jnp.where(rid < SSD_HEADS, inc_row - ldt, inc_row - lar + ldt)

        pe = _dot(he, wmain_ref[:, XBC_COL0:MAIN_COLS])
        stage_s[0:HALO, :] = jnp.where(c > 0, pe[0:HALO], 0.0)
        stage_s[HALO:HALO + C, :] = pe[HALO:HALO + C]
        stage_s[HALO + C:, :] = jnp.where(c < nc - 1, pe[HALO + C:], 0.0)

        def conv_block(cb):
            cs = slice(cb * 512, (cb + 1) * 512)
            half = CONV_W // 2
            acc = jnp.broadcast_to(convb_ref[0:1, cs], (C, 512))
            for k in range(CONV_W):
                o = HALO + k - half
                acc = acc + convw_ref[k:k + 1, cs] * stage_s[o:o + C, cs]
            return _silu(acc)

        def rope(a):
            if not use_rope:
                return a
            return a * cos_s[c] + pltpu.roll(a, RET_HD // 2, 1) * sin_s[c]

        pq = _dot(h, wmain_ref[:, 0:RET_W])
        for hd in range(RET_HEADS):
            sl = slice(hd * RET_HD, (hd + 1) * RET_HD)
            q_s[c, :, sl] = rope(pq[:, sl]).astype(BF16)

        xs_s[c] = conv_block(0)

        pk = _dot(h, wmain_ref[:, RET_W:2 * RET_W])
        for hd in range(RET_HEADS):
            sl = slice(hd * RET_HD, (hd + 1) * RET_HD)
            kh = rope(pk[:, sl]) * (RET_HD ** -0.5)
            kT_s[c, sl, :] = kh.T.astype(BF16)

        a = conv_block(1)
        bT_s[c] = a[:, 0:SSD_GROUPS * SSD_STATE].T.astype(BF16)
        c_s[c] = a[:, SSD_GROUPS * SSD_STATE:].astype(BF16)

        pv = _dot(h, wmain_ref[:, 2 * RET_W:3 * RET_W])
        v_s[c] = pv.astype(BF16)

        for gi in range(SSD_GROUPS):
            xg = xs_s[c, :, gi * GROUP_W:(gi + 1) * GROUP_W]
            vcat = jnp.concatenate([xg * expand_f(sf_col, gi), xg * expand_b(sb_col, gi)],
                                   axis=1).astype(BF16)
            sloc_s[c, gi] = _dot(bT_s[c, gi * SSD_STATE:(gi + 1) * SSD_STATE, :], vcat)

        g_s[c] = _silu(_dot(h, wmain_ref[:, 3 * RET_W:4 * RET_W])).astype(BF16)

        for hd in range(RET_HEADS):
            sl = slice(hd * RET_HD, (hd + 1) * RET_HD)
            vf = pv[:, sl]
            vcat = jnp.concatenate([vf * rvec_s[hd, RV_TAIL_F], vf * rvec_s[hd, RV_TAIL_B]],
                                   axis=1).astype(BF16)
            rloc_s[c, hd] = _dot(kT_s[c, sl, :], vcat)

        z_s[c] = _silu(_dot(h, wmain_ref[:, 4 * RET_W:XBC_COL0])).astype(BF16)

    def recurrences():
        for hd in range(RET_HEADS):
            lgf, lgb = ret_decays(hd)
            dec_f = jnp.exp(C * lgf)
            dec_b = jnp.exp(C * lgb)
            if has_state:
                ent_f = sret0_ref[0, 0, 0, hd]
                ent_b = sret0_ref[0, 0, 1, hd]
            else:
                ent_f = jnp.zeros((RET_HD, RET_HD), F32)
                ent_b = jnp.zeros((RET_HD, RET_HD), F32)
            for c in range(nc):
                if cross:
                    rent_s[c, hd, :, 0:RET_HD] = ent_f.astype(BF16)
                ent_f = dec_f * ent_f + rloc_s[c, hd, :, 0:RET_HD]
            for c in range(nc - 1, -1, -1):
                if cross:
                    rent_s[c, hd, :, RET_HD:] = ent_b.astype(BF16)
                ent_b = dec_b * ent_b + rloc_s[c, hd, :, RET_HD:]
            if emit_state:
                nret_ref[0, 0, 0, hd] = ent_f
                nret_ref[0, 0, 1, hd] = ent_b

        for gi in range(SSD_GROUPS):
            heads = group_heads(gi)
            if has_state:
                ent_f = jnp.concatenate([sssd0_ref[0, 0, 0, hh] for hh in heads], axis=1)
                ent_b = jnp.concatenate([sssd0_ref[0, 0, 1, hh] for hh in heads], axis=1)
            else:
                ent_f = jnp.zeros((SSD_STATE, GROUP_W), F32)
                ent_b = jnp.zeros((SSD_STATE, GROUP_W), F32)
            for c in range(nc):
                if cross:
                    sent_s[c, gi, :, 0:GROUP_W] = ent_f.astype(BF16)
                ent_f = expand_f(dec_s[c, 0:1, :], gi) * ent_f + sloc_s[c, gi, :, 0:GROUP_W]
            for c in range(nc - 1, -1, -1):
                if cross:
                    sent_s[c, gi, :, GROUP_W:] = ent_b.astype(BF16)
                ent_b = expand_b(dec_s[c, 0:1, :], gi) * ent_b + sloc_s[c, gi, :, GROUP_W:]
            if emit_state:
                for k, hh in enumerate(heads):
                    nssd_ref[0, 0, 0, hh] = ent_f[:, k * SSD_HD:(k + 1) * SSD_HD]
                    nssd_ref[0, 0, 1, hh] = ent_b[:, k * SSD_HD:(k + 1) * SSD_HD]

    def emit_chunk(c):
        ii = lax.broadcasted_iota(jnp.int32, (C, C), 0)
        jj = lax.broadcasted_iota(jnp.int32, (C, C), 1)
        causal = jj <= ii
        rsl = [slice(hd * RET_HD, (hd + 1) * RET_HD) for hd in range(RET_HEADS)]
        gsl = [slice(gi * SSD_STATE, (gi + 1) * SSD_STATE) for gi in range(SSD_GROUPS)]
        xsl = [slice(gi * GROUP_W, (gi + 1) * GROUP_W) for gi in range(SSD_GROUPS)]

        qs = [q_s[c, :, sl] for sl in rsl]
        cms = [c_s[c, :, sl] for sl in gsl]
        sc_s = [_dot(cms[gi], bT_s[c, gsl[gi], :]) for gi in range(SSD_GROUPS)]
        sc_r = [_dot(qs[hd], kT_s[c, rsl[hd], :]) for hd in range(RET_HEADS)]
        if cross:
            yc_s = [_dot(cms[gi], sent_s[c, gi]) for gi in range(SSD_GROUPS)]
            yc_r = [_dot(qs[hd], rent_s[c, hd]) for hd in range(RET_HEADS)]

        inc_col = inccol_s[c]
        exc_col = exccol_s[c]
        rowarg = rowarg_s[c]
        for gi in range(SSD_GROUPS):
            xg = xs_s[c, :, xsl[gi]]
            xgb = xg.astype(BF16)
            for k, hh in enumerate(group_heads(gi)):
                hb = SSD_HEADS + hh
                arg = jnp.where(causal,
                                inc_col[:, hh:hh + 1] - rowarg[hh:hh + 1, :],
                                rowarg[hb:hb + 1, :] - exc_col[:, hb:hb + 1])
                m = (sc_s[gi] * jnp.exp(arg)).astype(BF16)
                y_s[:, hh * SSD_HD:(hh + 1) * SSD_HD] = _dot(m, xgb[:, k * SSD_HD:(k + 1) * SSD_HD])
            yg = y_s[:, xsl[gi]] + expand_f(dskip_row, gi) * xg
            if cross:
                cfcb = cfcb_s[c]
                yg = (yg + expand_f(cfcb, gi) * yc_s[gi][:, 0:GROUP_W]
                      + expand_b(cfcb, gi) * yc_s[gi][:, GROUP_W:])
            y_s[:, xsl[gi]] = yg

        for hd in range(RET_HEADS):
            m = (sc_r[hd] * wdec_s[hd]).astype(BF16)
            o = _dot(m, v_s[c, :, rsl[hd]])
            if cross:
                o = (o + rvec_s[hd, RV_CROSS_F] * yc_r[hd][:, 0:RET_HD]
                     + rvec_s[hd, RV_CROSS_B] * yc_r[hd][:, RET_HD:])
            o = o * lax.rsqrt(jnp.mean(o * o, axis=-1, keepdims=True) + EPS)
            mix_s[:, rsl[hd]] = (g_s[c, :, rsl[hd]].astype(F32) * o).astype(BF16)

        yz = y_s[...] * z_s[c].astype(F32)
        yn = yz * lax.rsqrt(jnp.mean(yz * yz, axis=-1, keepdims=True) + EPS) * normw_ref[...]
        mix_s[:, RET_W:] = yn.astype(BF16)
        g1 = mod_ref[0, 2:3, :]
        y = ALPHA * x_ref[0] + g1 * _dot(mix_s[...], wout_ref[...])
        x1_ref[0] = _layer_norm(y, ln1g_ref[...], ln1b_ref[...])

    @pl.when(step < nc)
    def _():
        project(step)

    @pl.when(step == nc)
    def _():
        recurrences()

    @pl.when(step >= nc)
    def _():
        emit_chunk(step - nc)


def _const_spec(shape, ngrid):
    nd = len(shape)
    if ngrid == 1:
        return pl.BlockSpec(shape, lambda i: (0,) * nd, pipeline_mode=pl.Buffered(1))
    return pl.BlockSpec(shape, lambda b, s: (0,) * nd, pipeline_mode=pl.Buffered(1))


def _mixer_call(x, mod, mod_per_seq, weights, states, emit_state, use_rope, name):
    nb, L, _ = x.shape
    nc = L // CHUNK
    hpc_blocks = CHUNK // HALO
    has_state = states is not None
    mod_map = (lambda b, s: (b, 0, 0)) if mod_per_seq else (lambda b, s: (0, 0, 0))

    def chunk_of(s):
        return jnp.where(s < nc, s, s - nc)

    in_specs = [
        pl.BlockSpec((1, CHUNK, D_MODEL), lambda b, s: (b, chunk_of(s), 0)),
        pl.BlockSpec((1, HALO, D_MODEL),
                     lambda b, s: (b, jnp.maximum(chunk_of(s) * hpc_blocks - 1, 0), 0)),
        pl.BlockSpec((1, HALO, D_MODEL),
                     lambda b, s: (b, jnp.minimum((chunk_of(s) + 1) * hpc_blocks, nc * hpc_blocks - 1), 0)),
        pl.BlockSpec((1, 6, D_MODEL), mod_map),
    ] + [_const_spec(w.shape, 2) for w in weights]
    args = [x, x, x, mod] + list(weights)
    ret_block = (1, 1, 2, RET_HEADS, RET_HD, RET_HD)
    ssd_block = (1, 1, 2, SSD_HEADS, SSD_STATE, SSD_HD)
    state_map = lambda b, s: (b, 0, 0, 0, 0, 0)
    if has_state:
        in_specs += [pl.BlockSpec(ret_block, state_map), pl.BlockSpec(ssd_block, state_map)]
        args += list(states)
    out_shape = [jax.ShapeDtypeStruct((nb, L, D_MODEL), F32)]
    out_specs = [pl.BlockSpec((1, CHUNK, D_MODEL), lambda b, s: (b, jnp.maximum(s - nc, 0), 0))]
    if emit_state:
        out_shape += [jax.ShapeDtypeStruct((nb,) + ret_block[1:], F32),
                      jax.ShapeDtypeStruct((nb,) + ssd_block[1:], F32)]
        out_specs += [pl.BlockSpec(ret_block, state_map), pl.BlockSpec(ssd_block, state_map)]
    scratch = [
        pltpu.VMEM((nc, CHUNK, RET_W), BF16),
        pltpu.VMEM((nc, RET_W, CHUNK), BF16),
        pltpu.VMEM((nc, CHUNK, RET_W), BF16),
        pltpu.VMEM((nc, CHUNK, RET_W), BF16),
        pltpu.VMEM((nc, CHUNK, SSD_W), BF16),
        pltpu.VMEM((CHUNK + 2 * HALO, CONV_CH), F32),
        pltpu.VMEM((nc, CHUNK, SSD_W), F32),
        pltpu.VMEM((nc, SSD_GROUPS * SSD_STATE, CHUNK), BF16),
        pltpu.VMEM((nc, CHUNK, SSD_GROUPS * SSD_STATE), BF16),
        pltpu.VMEM((nc, CHUNK, LANES), F32),
        pltpu.VMEM((nc, CHUNK, LANES), F32),
        pltpu.VMEM((nc, CHUNK, LANES), F32),
        pltpu.VMEM((nc, 8, LANES), F32),
        pltpu.VMEM((nc, N_DT, CHUNK), F32),
        pltpu.VMEM((CHUNK, SSD_W), F32),
        pltpu.VMEM((CHUNK, 2 * RET_W), BF16),
        pltpu.VMEM((nc, RET_HEADS, RET_HD, 2 * RET_HD), F32),
        pltpu.VMEM((nc, SSD_GROUPS, SSD_STATE, 2 * GROUP_W), F32),
        pltpu.VMEM((nc, RET_HEADS, RET_HD, 2 * RET_HD), BF16),
        pltpu.VMEM((nc, SSD_GROUPS, SSD_STATE, 2 * GROUP_W), BF16),
        pltpu.VMEM((RET_HEADS, CHUNK, CHUNK), F32),
        pltpu.VMEM((RET_HEADS, 4, CHUNK, LANES), F32),
    ]
    if use_rope:
        scratch += [pltpu.VMEM((nc, CHUNK, LANES), F32)] * 2
    kern = functools.partial(_mixer_kernel, L=L, has_state=has_state, use_rope=use_rope,
                             emit_state=emit_state)
    return pl.pallas_call(
        kern,
        grid=(nb, 2 * nc),
        in_specs=in_specs,
        out_specs=out_specs,
        out_shape=out_shape,
        scratch_shapes=scratch,
        compiler_params=pltpu.CompilerParams(dimension_semantics=("arbitrary", "arbitrary"),
                                             vmem_limit_bytes=VMEM_LIMIT),
        name=name,
    )(*args)


def _ffn_kernel(x_ref, mod_ref, wg_ref, wu_ref, wd_ref, g_ref, b_ref, o_ref):
    x = x_ref[0]
    sh2 = mod_ref[0, 3:4, :]
    sc2 = mod_ref[0, 4:5, :]
    g2 = mod_ref[0, 5:6, :]
    h2 = (x * (1.0 + sc2) + sh2).astype(BF16)
    hid = (_silu(_dot(h2, wg_ref[...])) * _dot(h2, wu_ref[...])).astype(BF16)
    y = ALPHA * x + g2 * _dot(hid, wd_ref[...])
    o_ref[0] = _layer_norm(y, g_ref[...], b_ref[...])


def _ffn_call(x1, mod, blocks_per_mod, weights, name):
    nblk = x1.shape[0]
    if blocks_per_mod:
        mod_map = lambda i: (i // blocks_per_mod, 0, 0)
    else:
        mod_map = lambda i: (0, 0, 0)
    return pl.pallas_call(
        _ffn_kernel,
        grid=(nblk,),
        in_specs=[
            pl.BlockSpec((1, FFN_ROWS, D_MODEL), lambda i: (i, 0, 0)),
            pl.BlockSpec((1, 6, D_MODEL), mod_map),
        ] + [_const_spec(w.shape, 1) for w in weights],
        out_specs=pl.BlockSpec((1, FFN_ROWS, D_MODEL), lambda i: (i, 0, 0)),
        out_shape=jax.ShapeDtypeStruct(x1.shape, F32),
        compiler_params=pltpu.CompilerParams(dimension_semantics=("arbitrary",),
                                             vmem_limit_bytes=VMEM_LIMIT),
        name=name,
    )(x1, mod, *weights)


def kernel(x_prompt, x_sample, state_ret, state_ssd, c, c_ctx, w_in, ret_decay_fwd, ret_decay_bwd, conv_w, conv_b, dt_bias_fwd, dt_bias_bwd, a_log_fwd, a_log_bwd, d_skip, ssd_norm_w, w_out, ln1_g, ln1_b, w_gate, w_up, w_down, ln2_g, ln2_b, w_ada, b_ada):
    depth = w_in.shape[0]
    assert depth == 1, "single trunk layer"
    bp, lp, _ = x_prompt.shape
    bs, ls, _ = x_sample.shape
    assert lp % CHUNK == 0 and ls % CHUNK == 0 and ls % GRID_W == 0

    rows = -(-(bs + 1) // 8) * 8
    cond = jnp.zeros((rows, D_MODEL), F32).at[:bs].set(c).at[bs].set(c_ctx)
    mod = _ada_call(cond, w_ada[0], b_ada[0][None, :]).reshape(rows, 6, D_MODEL)
    mod_lat = mod[:bs]
    mod_ctx = mod[bs:bs + 1]

    wdt = w_in[0, :, MAIN_COLS:]
    wdt2 = jnp.concatenate([wdt, wdt], axis=1)
    wdtc = jnp.zeros((D_MODEL, LANES), F32).at[:, :N_DT].set(wdt2).astype(BF16)
    wdtr = wdt2.T.astype(BF16)
    convw = jnp.zeros((8, CONV_CH), F32).at[:CONV_W].set(conv_w[0])
    convb = conv_b[0][None, :]
    dt_bias = jnp.concatenate([dt_bias_fwd[0], dt_bias_bwd[0]])
    a_log = jnp.concatenate([a_log_fwd[0], a_log_bwd[0]])
    ret_decay = jnp.concatenate([ret_decay_fwd[0], ret_decay_bwd[0]])
    hp = jnp.zeros((8, LANES), F32)
    hp = hp.at[0, :N_DT].set(dt_bias).at[1, :N_DT].set(a_log)
    hp = hp.at[2, :2 * RET_HEADS].set(ret_decay).at[3, :SSD_HEADS].set(d_skip[0])
    hpc = jnp.zeros((N_DT, LANES), F32).at[:, 0].set(dt_bias).at[:, 1].set(a_log)
    mixer_w = (w_in[0].astype(BF16), wdtc, wdtr, convw, convb, hp, hpc, ssd_norm_w[0][None, :],
               w_out[0].astype(BF16), ln1_g[0][None, :], ln1_b[0][None, :])
    ffn_w = (w_gate[0].astype(BF16), w_up[0].astype(BF16), w_down[0].astype(BF16),
             ln2_g[0][None, :], ln2_b[0][None, :])

    x1p, new_ret, new_ssd = _mixer_call(x_prompt, mod_ctx, False, mixer_w, None, True, False, "mixer_ctx")
    yp = _ffn_call(x1p.reshape(bp * lp // FFN_ROWS, FFN_ROWS, D_MODEL), mod_ctx, 0, ffn_w, "ffn_ctx")
    yp = yp.reshape(bp, lp, D_MODEL)

    (x1s,) = _mixer_call(x_sample, mod_lat, True, mixer_w, (state_ret, state_ssd), False, True, "mixer_lat")
    ys = _ffn_call(x1s.reshape(bs * ls // FFN_ROWS, FFN_ROWS, D_MODEL), mod_lat, ls // FFN_ROWS, ffn_w, "ffn_lat")
    ys = ys.reshape(bs, ls, D_MODEL)

    return (yp, ys, new_ret, new_ssd)
```

```python
import functools
import math

import jax
import jax.numpy as jnp
from jax import lax
from jax.experimental import pallas as pl
from jax.experimental.pallas import tpu as pltpu

F32 = jnp.float32
BF16 = jnp.bfloat16

D_MODEL = 1024
RET_W = 512
RET_HEADS = 4
RET_HD = 128
SSD_W = 512
SSD_HD = 64
SSD_HEADS = 8
SSD_GROUPS = 2
SSD_STATE = 128
HPG = SSD_HEADS // SSD_GROUPS
GROUP_W = HPG * SSD_HD
CONV_W = 5
CONV_CH = SSD_W + 2 * SSD_GROUPS * SSD_STATE
D_FF = 2816
GRID_W = 64
GRID_SHIFT = 6
ROPE_BASE = 10000.0
EPS = 1e-6
ALPHA = 2.0 ** 0.25
MAIN_COLS = 4 * RET_W + SSD_W + CONV_CH
XBC_COL0 = 4 * RET_W + SSD_W
N_DT = 2 * SSD_HEADS

CHUNK = 256
HALO = 8
FF_BLK = 256
ADA_COLS = 512
LANES = 128
VMEM_LIMIT = 62 * 1024 * 1024


def _dot(a, b):
    return jnp.dot(a, b, preferred_element_type=F32)


def _dot_nt(a, b):
    return lax.dot_general(a, b, (((1,), (1,)), ((), ())), preferred_element_type=F32)


def _silu(x):
    return x * jax.nn.sigmoid(x)


def _softplus(x):
    return jnp.maximum(x, 0.0) + jnp.log1p(jnp.exp(-jnp.abs(x)))


def _layer_norm(y, g, b):
    mu = jnp.mean(y, axis=-1, keepdims=True)
    yc = y - mu
    var = jnp.mean(yc * yc, axis=-1, keepdims=True)
    return yc * lax.rsqrt(var + EPS) * g + b


def _cumsum(x, axis):
    n = x.shape[axis]
    idx = lax.broadcasted_iota(jnp.int32, x.shape, axis)
    s = 1
    while s < n:
        x = x + jnp.where(idx >= s, pltpu.roll(x, s, axis), 0.0)
        s *= 2
    return x


def _expand4(cols, lane):
    a = jnp.where(lane < SSD_HD, cols[0], cols[1])
    b = jnp.where(lane < SSD_HD, cols[2], cols[3])
    return jnp.concatenate([a, b], axis=1)


def _interleave(a, b):
    ia = ib = 0
    while ia < len(a) or ib < len(b):
        if ib >= len(b) or (ia < len(a) and ia * len(b) <= ib * len(a)):
            a[ia]()
            ia += 1
        else:
            b[ib]()
            ib += 1


def _ada_kernel(cond_ref, w_ref, b_ref, o_ref):
    s = _silu(cond_ref[...]).astype(BF16)
    o_ref[...] = _dot(s, w_ref[...].astype(BF16)) + b_ref[...]


def _ada_call(cond, w_ada, b_ada):
    rows = cond.shape[0]
    n = w_ada.shape[1]
    return pl.pallas_call(
        _ada_kernel,
        grid=(n // ADA_COLS,),
        in_specs=[
            pl.BlockSpec((rows, D_MODEL), lambda j: (0, 0)),
            pl.BlockSpec((D_MODEL, ADA_COLS), lambda j: (0, j)),
            pl.BlockSpec((1, ADA_COLS), lambda j: (0, j)),
        ],
        out_specs=pl.BlockSpec((rows, ADA_COLS), lambda j: (0, j)),
        out_shape=jax.ShapeDtypeStruct((rows, n), F32),
        compiler_params=pltpu.CompilerParams(dimension_semantics=("arbitrary",)),
        name="adaln_mod",
    )(cond, w_ada, b_ada)


RV_TAIL_F, RV_TAIL_B, RV_CROSS_F, RV_CROSS_B = range(4)


def _layer_kernel(*refs, L, nb, has_state, use_rope, emit_state):
    nc = L // CHUNK
    C = CHUNK
    cross = has_state or nc > 1
    it = iter(refs)
    x_ref, xprev_ref, xnext_ref, mod_ref, modf_ref, wmain_ref, wdtc_ref, wdtr_ref = (next(it) for _ in range(8))
    convw_ref, convb_ref, hp_ref, hpc_ref, normw_ref = (next(it) for _ in range(5))
    wout_ref, ln1g_ref, ln1b_ref = (next(it) for _ in range(3))
    wg_ref, wu_ref, wd_ref, ln2g_ref, ln2b_ref = (next(it) for _ in range(5))
    if has_state:
        sret0_ref, sssd0_ref = next(it), next(it)
    out_ref = next(it)
    if emit_state:
        nret_ref, nssd_ref = next(it), next(it)
    (q_s, kT_s, v_s, g_s, z_s, stage_s, xs_s, bT_s, c_s, inccol_s, exccol_s, cfcb_s, dec_s,
     rowarg_s, y_s, mix_s, rloc_s, sloc_s, rent_s, sent_s, wdec_s, rvec_s, x1_s, hid_s) = (
         next(it) for _ in range(24))
    if use_rope:
        cos_s, sin_s = next(it), next(it)

    seq = pl.program_id(0)
    step = pl.program_id(1)

    hp = hp_ref[...]
    dt_bias_row = hp[0:1, :]
    nega_row = -jnp.exp(hp[1:2, :])
    lg_row = -_softplus(-hp[2:3, :])
    dskip_row = hp[3:4, :]
    hpc = hpc_ref[...]
    dt_bias_col = hpc[:, 0:1]
    nega_col = -jnp.exp(hpc[:, 1:2])

    lane = lax.broadcasted_iota(jnp.int32, (1, LANES), 1)

    def ret_decays(hd):
        lgf = lg_row[:, hd:hd + 1]
        lgb = lg_row[:, RET_HEADS + hd:RET_HEADS + hd + 1]
        return lgf, lgb

    def group_heads(gi):
        return [gi * HPG + k for k in range(HPG)]

    def expand_f(arr, gi):
        return _expand4([arr[:, hh:hh + 1] for hh in group_heads(gi)], lane)

    def expand_b(arr, gi):
        return _expand4([arr[:, SSD_HEADS + hh:SSD_HEADS + hh + 1] for hh in group_heads(gi)], lane)

    @pl.when((seq == 0) & (step == 0))
    def _():
        ii = lax.broadcasted_iota(jnp.int32, (C, C), 0)
        jj = lax.broadcasted_iota(jnp.int32, (C, C), 1)
        dmat = (ii - jj).astype(F32)
        irow = lax.broadcasted_iota(jnp.int32, (C, LANES), 0).astype(F32)
        for hd in range(RET_HEADS):
            lgf, lgb = ret_decays(hd)
            wdec_s[hd] = jnp.exp(jnp.where(jj <= ii, dmat * lgf, -dmat * lgb))
            rvec_s[hd, RV_TAIL_F] = jnp.exp((C - 1.0 - irow) * lgf)
            rvec_s[hd, RV_TAIL_B] = jnp.exp(irow * lgb)
            rvec_s[hd, RV_CROSS_F] = jnp.exp((irow + 1.0) * lgf)
            rvec_s[hd, RV_CROSS_B] = jnp.exp((C - irow) * lgb)
        if use_rope:
            ln = lax.broadcasted_iota(jnp.int32, (C, LANES), 1)
            nf = RET_HD // 4
            inv = jnp.exp((ln & (nf - 1)).astype(F32) * (-math.log(ROPE_BASE) / nf))
            for cc in range(nc):
                t = lax.broadcasted_iota(jnp.int32, (C, LANES), 0) + cc * C
                pos = jnp.where((ln & (2 * nf - 1)) < nf, t >> GRID_SHIFT, t & (GRID_W - 1)).astype(F32)
                ang = pos * inv
                cos_s[cc] = jnp.cos(ang)
                sin_s[cc] = jnp.where(ln < RET_HD // 2, -jnp.sin(ang), jnp.sin(ang))

    def project(c):
        sh1 = mod_ref[0, 0:1, :]
        sc1 = mod_ref[0, 1:2, :]
        xe = jnp.concatenate([xprev_ref[0], x_ref[0], xnext_ref[0]], axis=0)
        xm = xe * (1.0 + sc1) + sh1
        he = xm.astype(BF16)
        h = xm[HALO:HALO + C].astype(BF16)

        dt_c = _softplus(_dot(h, wdtc_ref[...]) + dt_bias_row)
        lac = dt_c * nega_row
        inc_col = _cumsum(lac, 0)
        exc_col = inc_col - lac
        tot_col = inc_col[C - 1:C, :]
        inccol_s[c] = inc_col
        exccol_s[c] = exc_col
        dec_s[c] = jnp.broadcast_to(jnp.exp(tot_col), (8, LANES))
        if cross:
            cfcb_s[c] = jnp.exp(jnp.where(lane < SSD_HEADS, inc_col, tot_col - exc_col))
        sf_col = jnp.exp(tot_col - inc_col) * dt_c
        sb_col = jnp.exp(exc_col) * dt_c

        dt_r = _softplus(_dot_nt(wdtr_ref[...], h) + dt_bias_col)
        lar = dt_r * nega_col
        inc_row = _cumsum(lar, 1)
        ldt = jnp.log(dt_r)
        rid = lax.broadcasted_iota(jnp.int32, (N_DT, C), 0)
        rowarg_s[c] = jnp.where(rid < SSD_HEADS, inc_row - ldt, inc_row - lar + ldt)

        pe = _dot(he, wmain_ref[:, XBC_COL0:MAIN_COLS])
        stage_s[0:HALO, :] = jnp.where(c > 0, pe[0:HALO], 0.0)
        stage_s[HALO:HALO + C, :] = pe[HALO:HALO + C]
        stage_s[HALO + C:, :] = jnp.where(c < nc - 1, pe[HALO + C:], 0.0)

        def conv_block(cb):
            cs = slice(cb * 512, (cb + 1) * 512)
            half = CONV_W // 2
            acc = jnp.broadcast_to(convb_ref[0:1, cs], (C, 512))
            for k in range(CONV_W):
                o = HALO + k - half
                acc = acc + convw_ref[k:k + 1, cs] * stage_s[o:o + C, cs]
            return _silu(acc)

        def rope(a):
            if not use_rope:
                return a
            return a * cos_s[c] + pltpu.roll(a, RET_HD // 2, 1) * sin_s[c]

        pq = _dot(h, wmain_ref[:, 0:RET_W])
        for hd in range(RET_HEADS):
            sl = slice(hd * RET_HD, (hd + 1) * RET_HD)
            q_s[c, :, sl] = rope(pq[:, sl]).astype(BF16)

        xs_s[c] = conv_block(0)

        pk = _dot(h, wmain_ref[:, RET_W:2 * RET_W])
        for hd in range(RET_HEADS):
            sl = slice(hd * RET_HD, (hd + 1) * RET_HD)
            kh = rope(pk[:, sl]) * (RET_HD ** -0.5)
            kT_s[c, sl, :] = kh.T.astype(BF16)

        a = conv_block(1)
        bT_s[c] = a[:, 0:SSD_GROUPS * SSD_STATE].T.astype(BF16)
        c_s[c] = a[:, SSD_GROUPS * SSD_STATE:].astype(BF16)

        pv = _dot(h, wmain_ref[:, 2 * RET_W:3 * RET_W])
        v_s[c] = pv.astype(BF16)

        for gi in range(SSD_GROUPS):
            xg = xs_s[c, :, gi * GROUP_W:(gi + 1) * GROUP_W]
            vcat = jnp.concatenate([xg * expand_f(sf_col, gi), xg * expand_b(sb_col, gi)],
                                   axis=1).astype(BF16)
            sloc_s[c, gi] = _dot(bT_s[c, gi * SSD_STATE:(gi + 1) * SSD_STATE, :], vcat).astype(sloc_s.dtype)

        g_s[c] = _silu(_dot(h, wmain_ref[:, 3 * RET_W:4 * RET_W])).astype(BF16)

        for hd in range(RET_HEADS):
            sl = slice(hd * RET_HD, (hd + 1) * RET_HD)
            vf = pv[:, sl]
            vcat = jnp.concatenate([vf * rvec_s[hd, RV_TAIL_F], vf * rvec_s[hd, RV_TAIL_B]],
                                   axis=1).astype(BF16)
            rloc_s[c, hd] = _dot(kT_s[c, sl, :], vcat).astype(rloc_s.dtype)

        z_s[c] = _silu(_dot(h, wmain_ref[:, 4 * RET_W:XBC_COL0])).astype(BF16)

    def recurrences():
        for hd in range(RET_HEADS):
            lgf, lgb = ret_decays(hd)
            dec_f = jnp.exp(C * lgf)
            dec_b = jnp.exp(C * lgb)
            if has_state:
                ent_f = sret0_ref[0, 0, 0, hd]
                ent_b = sret0_ref[0, 0, 1, hd]
            else:
                ent_f = jnp.zeros((RET_HD, RET_HD), F32)
                ent_b = jnp.zeros((RET_HD, RET_HD), F32)
            for c in range(nc):
                if cross:
                    rent_s[c, hd, :, 0:RET_HD] = ent_f.astype(BF16)
                ent_f = dec_f * ent_f + rloc_s[c, hd, :, 0:RET_HD]
            for c in range(nc - 1, -1, -1):
                if cross:
                    rent_s[c, hd, :, RET_HD:] = ent_b.astype(BF16)
                ent_b = dec_b * ent_b + rloc_s[c, hd, :, RET_HD:]
            if emit_state:
                nret_ref[0, 0, 0, hd] = ent_f
                nret_ref[0, 0, 1, hd] = ent_b

        for gi in range(SSD_GROUPS):
            heads = group_heads(gi)
            if has_state:
                ent_f = jnp.concatenate([sssd0_ref[0, 0, 0, hh] for hh in heads], axis=1)
                ent_b = jnp.concatenate([sssd0_ref[0, 0, 1, hh] for hh in heads], axis=1)
            else:
                ent_f = jnp.zeros((SSD_STATE, GROUP_W), F32)
                ent_b = jnp.zeros((SSD_STATE, GROUP_W), F32)
            for c in range(nc):
                if cross:
                    sent_s[c, gi, :, 0:GROUP_W] = ent_f.astype(BF16)
                ent_f = expand_f(dec_s[c, 0:1, :], gi) * ent_f + sloc_s[c, gi, :, 0:GROUP_W]
            for c in range(nc - 1, -1, -1):
                if cross:
                    sent_s[c, gi, :, GROUP_W:] = ent_b.astype(BF16)
                ent_b = expand_b(dec_s[c, 0:1, :], gi) * ent_b + sloc_s[c, gi, :, GROUP_W:]
            if emit_state:
                for k, hh in enumerate(heads):
                    nssd_ref[0, 0, 0, hh] = ent_f[:, k * SSD_HD:(k + 1) * SSD_HD]
                    nssd_ref[0, 0, 1, hh] = ent_b[:, k * SSD_HD:(k + 1) * SSD_HD]

    def emit_pieces(c, slot):
        rsl = [slice(hd * RET_HD, (hd + 1) * RET_HD) for hd in range(RET_HEADS)]
        gsl = [slice(gi * SSD_STATE, (gi + 1) * SSD_STATE) for gi in range(SSD_GROUPS)]
        xsl = [slice(gi * GROUP_W, (gi + 1) * GROUP_W) for gi in range(SSD_GROUPS)]
        st = {}

        def scores():
            ii = lax.broadcasted_iota(jnp.int32, (C, C), 0)
            jj = lax.broadcasted_iota(jnp.int32, (C, C), 1)
            st["causal"] = jj <= ii
            qs = [q_s[c, :, sl] for sl in rsl]
            cms = [c_s[c, :, sl] for sl in gsl]
            st["sc_s"] = [_dot(cms[gi], bT_s[c, gsl[gi], :]) for gi in range(SSD_GROUPS)]
            st["sc_r"] = [_dot(qs[hd], kT_s[c, rsl[hd], :]) for hd in range(RET_HEADS)]
            if cross:
                st["yc_s"] = [_dot(cms[gi], sent_s[c, gi]) for gi in range(SSD_GROUPS)]
                st["yc_r"] = [_dot(qs[hd], rent_s[c, hd]) for hd in range(RET_HEADS)]
            st["inc_col"] = inccol_s[c]
            st["exc_col"] = exccol_s[c]
            st["rowarg"] = rowarg_s[c]

        def ssd_head(gi, k):
            def run():
                hh = gi * HPG + k
                hb = SSD_HEADS + hh
                arg = jnp.where(st["causal"],
                                st["inc_col"][:, hh:hh + 1] - st["rowarg"][hh:hh + 1, :],
                                st["rowarg"][hb:hb + 1, :] - st["exc_col"][:, hb:hb + 1])
                m = (st["sc_s"][gi] * jnp.exp(arg)).astype(BF16)
                xh = xs_s[c, :, hh * SSD_HD:(hh + 1) * SSD_HD].astype(BF16)
                y_s[:, hh * SSD_HD:(hh + 1) * SSD_HD] = _dot(m, xh)
            return run

        def ssd_group(gi):
            def run():
                xg = xs_s[c, :, xsl[gi]]
                yg = y_s[:, xsl[gi]] + expand_f(dskip_row, gi) * xg
                if cross:
                    cfcb = cfcb_s[c]
                    yc = st["yc_s"][gi]
                    yg = yg + expand_f(cfcb, gi) * yc[:, 0:GROUP_W] + expand_b(cfcb, gi) * yc[:, GROUP_W:]
                y_s[:, xsl[gi]] = yg
            return run

        def ret_head(hd):
            def run():
                m = (st["sc_r"][hd] * wdec_s[hd]).astype(BF16)
                o = _dot(m, v_s[c, :, rsl[hd]])
                if cross:
                    yc = st["yc_r"][hd]
                    o = o + rvec_s[hd, RV_CROSS_F] * yc[:, 0:RET_HD] + rvec_s[hd, RV_CROSS_B] * yc[:, RET_HD:]
                o = o * lax.rsqrt(jnp.mean(o * o, axis=-1, keepdims=True) + EPS)
                mix_s[:, rsl[hd]] = (g_s[c, :, rsl[hd]].astype(F32) * o).astype(BF16)
            return run

        def ssd_norm():
            yz = y_s[...] * z_s[c].astype(F32)
            yn = yz * lax.rsqrt(jnp.mean(yz * yz, axis=-1, keepdims=True) + EPS) * normw_ref[...]
            mix_s[:, RET_W:] = yn.astype(BF16)

        def out_proj():
            g1 = mod_ref[0, 2:3, :]
            y = ALPHA * x_ref[0] + g1 * _dot(mix_s[...], wout_ref[...])
            x1_s[slot] = _layer_norm(y, ln1g_ref[...], ln1b_ref[...])

        pieces = [scores]
        for gi in range(SSD_GROUPS):
            pieces += [ssd_head(gi, k) for k in range(HPG)] + [ssd_group(gi)]
        pieces += [ret_head(hd) for hd in range(RET_HEADS)] + [ssd_norm, out_proj]
        return pieces

    def ffn_pieces(slot):
        st = {}

        def start():
            sh2 = modf_ref[0, 3:4, :]
            sc2 = modf_ref[0, 4:5, :]
            st["h2"] = (x1_s[slot] * (1.0 + sc2) + sh2).astype(BF16)

        def hidden(j):
            def run():
                js = slice(j * FF_BLK, (j + 1) * FF_BLK)
                h2 = st["h2"]
                hid_s[:, js] = (_silu(_dot(h2, wg_ref[:, js])) * _dot(h2, wu_ref[:, js])).astype(BF16)
            return run

        def finish():
            g2 = modf_ref[0, 5:6, :]
            y = ALPHA * x1_s[slot] + g2 * _dot(hid_s[...], wd_ref[...])
            out_ref[0] = _layer_norm(y, ln2g_ref[...], ln2b_ref[...])

        return [start] + [hidden(j) for j in range(D_FF // FF_BLK)] + [finish]

    def run_all(pieces):
        for p in pieces:
            p()

    @pl.when((step < nc) & (seq < nb))
    def _():
        project(step)

    @pl.when((step == nc) & (seq < nb))
    def _():
        recurrences()

    kk = step - nc
    slot = (seq * nc + kk) & 1
    first = (seq == 0) & (step == nc)

    @pl.when(first)
    def _():
        run_all(emit_pieces(kk, slot))

    @pl.when((step >= nc) & (seq < nb) & jnp.logical_not(first))
    def _():
        _interleave(ffn_pieces(1 - slot), emit_pieces(kk, slot))

    @pl.when((seq == nb) & (step == nc))
    def _():
        run_all(ffn_pieces((nb * nc - 1) & 1))


def _const_spec(shape):
    nd = len(shape)
    return pl.BlockSpec(shape, lambda b, s: (0,) * nd, pipeline_mode=pl.Buffered(1))


def _layer_call(x, mod, mod_per_seq, weights, states, emit_state, use_rope, name):
    nb, L, _ = x.shape
    nc = L // CHUNK
    hpc_blocks = CHUNK // HALO
    has_state = states is not None
    last = nb - 1

    def chunk_of(s):
        return jnp.where(s < nc, s, s - nc)

    def seq_of(b):
        return jnp.minimum(b, last)

    def lag_seq(b, s):
        return jnp.minimum(jnp.where(s > nc, b, jnp.maximum(b - 1, 0)), last)

    def out_map(b, s):
        live = (s > nc) & (b < nb)
        idle_chunk = jnp.where(b == 0, 0, nc - 1)
        return (jnp.where(live, b, jnp.maximum(b - 1, 0)), jnp.where(live, s - nc - 1, idle_chunk), 0)

    if mod_per_seq:
        mod_map = lambda b, s: (seq_of(b), 0, 0)
        modf_map = lambda b, s: (lag_seq(b, s), 0, 0)
    else:
        mod_map = modf_map = lambda b, s: (0, 0, 0)

    in_specs = [
        pl.BlockSpec((1, CHUNK, D_MODEL), lambda b, s: (seq_of(b), chunk_of(s), 0)),
        pl.BlockSpec((1, HALO, D_MODEL),
                     lambda b, s: (seq_of(b), jnp.maximum(chunk_of(s) * hpc_blocks - 1, 0), 0)),
        pl.BlockSpec((1, HALO, D_MODEL),
                     lambda b, s: (seq_of(b), jnp.minimum((chunk_of(s) + 1) * hpc_blocks, nc * hpc_blocks - 1), 0)),
        pl.BlockSpec((1, 6, D_MODEL), mod_map),
        pl.BlockSpec((1, 6, D_MODEL), modf_map),
    ] + [_const_spec(w.shape) for w in weights]
    args = [x, x, x, mod, mod] + list(weights)
    ret_block = (1, 1, 2, RET_HEADS, RET_HD, RET_HD)
    ssd_block = (1, 1, 2, SSD_HEADS, SSD_STATE, SSD_HD)
    state_map = lambda b, s: (seq_of(b), 0, 0, 0, 0, 0)
    if has_state:
        in_specs += [pl.BlockSpec(ret_block, state_map, pipeline_mode=pl.Buffered(1)),
                     pl.BlockSpec(ssd_block, state_map, pipeline_mode=pl.Buffered(1))]
        args += list(states)
    out_shape = [jax.ShapeDtypeStruct((nb, L, D_MODEL), F32)]
    out_specs = [pl.BlockSpec((1, CHUNK, D_MODEL), out_map)]
    if emit_state:
        out_shape += [jax.ShapeDtypeStruct((nb,) + ret_block[1:], F32),
                      jax.ShapeDtypeStruct((nb,) + ssd_block[1:], F32)]
        out_specs += [pl.BlockSpec(ret_block, state_map), pl.BlockSpec(ssd_block, state_map)]
    state_dt = F32 if emit_state else BF16
    scratch = [
        pltpu.VMEM((nc, CHUNK, RET_W), BF16),
        pltpu.VMEM((nc, RET_W, CHUNK), BF16),
        pltpu.VMEM((nc, CHUNK, RET_W), BF16),
        pltpu.VMEM((nc, CHUNK, RET_W), BF16),
        pltpu.VMEM((nc, CHUNK, SSD_W), BF16),
        pltpu.VMEM((CHUNK + 2 * HALO, CONV_CH), F32),
        pltpu.VMEM((nc, CHUNK, SSD_W), F32),
        pltpu.VMEM((nc, SSD_GROUPS * SSD_STATE, CHUNK), BF16),
        pltpu.VMEM((nc, CHUNK, SSD_GROUPS * SSD_STATE), BF16),
        pltpu.VMEM((nc, CHUNK, LANES), F32),
        pltpu.VMEM((nc, CHUNK, LANES), F32),
        pltpu.VMEM((nc, CHUNK, LANES), F32),
        pltpu.VMEM((nc, 8, LANES), F32),
        pltpu.VMEM((nc, N_DT, CHUNK), F32),
        pltpu.VMEM((CHUNK, SSD_W), F32),
        pltpu.VMEM((CHUNK, 2 * RET_W), BF16),
        pltpu.VMEM((nc, RET_HEADS, RET_HD, 2 * RET_HD), state_dt),
        pltpu.VMEM((nc, SSD_GROUPS, SSD_STATE, 2 * GROUP_W), state_dt),
        pltpu.VMEM((nc, RET_HEADS, RET_HD, 2 * RET_HD), BF16),
        pltpu.VMEM((nc, SSD_GROUPS, SSD_STATE, 2 * GROUP_W), BF16),
        pltpu.VMEM((RET_HEADS, CHUNK, CHUNK), F32),
        pltpu.VMEM((RET_HEADS, 4, CHUNK, LANES), F32),
        pltpu.VMEM((2, CHUNK, D_MODEL), F32),
        pltpu.VMEM((CHUNK, D_FF), BF16),
    ]
    if use_rope:
        scratch += [pltpu.VMEM((nc, CHUNK, LANES), F32)] * 2
    kern = functools.partial(_layer_kernel, L=L, nb=nb, has_state=has_state, use_rope=use_rope,
                             emit_state=emit_state)
    return pl.pallas_call(
        kern,
        grid=(nb + 1, 2 * nc),
        in_specs=in_specs,
        out_specs=out_specs,
        out_shape=out_shape,
        scratch_shapes=scratch,
        compiler_params=pltpu.CompilerParams(dimension_semantics=("arbitrary", "arbitrary"),
                                             vmem_limit_bytes=VMEM_LIMIT),
        name=name,
    )(*args)


def kernel(x_prompt, x_sample, state_ret, state_ssd, c, c_ctx, w_in, ret_decay_fwd, ret_decay_bwd, conv_w, conv_b, dt_bias_fwd, dt_bias_bwd, a_log_fwd, a_log_bwd, d_skip, ssd_norm_w, w_out, ln1_g, ln1_b, w_gate, w_up, w_down, ln2_g, ln2_b, w_ada, b_ada):
    depth = w_in.shape[0]
    assert depth == 1, "single trunk layer"
    bp, lp, _ = x_prompt.shape
    bs, ls, _ = x_sample.shape
    assert lp % CHUNK == 0 and ls % CHUNK == 0 and ls % GRID_W == 0 and D_FF % FF_BLK == 0

    rows = -(-(bs + 1) // 8) * 8
    cond = jnp.zeros((rows, D_MODEL), F32).at[:bs].set(c).at[bs].set(c_ctx)
    mod = _ada_call(cond, w_ada[0], b_ada[0][None, :]).reshape(rows, 6, D_MODEL)
    mod_lat = mod[:bs]
    mod_ctx = mod[bs:bs + 1]

    wdt = w_in[0, :, MAIN_COLS:]
    wdt2 = jnp.concatenate([wdt, wdt], axis=1)
    wdtc = jnp.zeros((D_MODEL, LANES), F32).at[:, :N_DT].set(wdt2).astype(BF16)
    wdtr = wdt2.T.astype(BF16)
    convw = jnp.zeros((8, CONV_CH), F32).at[:CONV_W].set(conv_w[0])
    convb = conv_b[0][None, :]
    dt_bias = jnp.concatenate([dt_bias_fwd[0], dt_bias_bwd[0]])
    a_log = jnp.concatenate([a_log_fwd[0], a_log_bwd[0]])
    ret_decay = jnp.concatenate([ret_decay_fwd[0], ret_decay_bwd[0]])
    hp = jnp.zeros((8, LANES), F32)
    hp = hp.at[0, :N_DT].set(dt_bias).at[1, :N_DT].set(a_log)
    hp = hp.at[2, :2 * RET_HEADS].set(ret_decay).at[3, :SSD_HEADS].set(d_skip[0])
    hpc = jnp.zeros((N_DT, LANES), F32).at[:, 0].set(dt_bias).at[:, 1].set(a_log)
    weights = (w_in[0].astype(BF16), wdtc, wdtr, convw, convb, hp, hpc, ssd_norm_w[0][None, :],
               w_out[0].astype(BF16), ln1_g[0][None, :], ln1_b[0][None, :],
               w_gate[0].astype(BF16), w_up[0].astype(BF16), w_down[0].astype(BF16),
               ln2_g[0][None, :], ln2_b[0][None, :])

    yp, new_ret, new_ssd = _layer_call(x_prompt, mod_ctx, False, weights, None, True, False, "layer_ctx")
    (ys,) = _layer_call(x_sample, mod_lat, True, weights, (state_ret, state_ssd), False, True, "layer_lat")
    return (yp, ys, new_ret, new_ssd)
```

```python
import functools
import math

import jax
import jax.numpy as jnp
from jax import lax
from jax.experimental import pallas as pl
from jax.experimental.pallas import tpu as pltpu

F32 = jnp.float32
BF16 = jnp.bfloat16

D_MODEL = 1024
RET_W = 512
RET_HEADS = 4
RET_HD = 128
SSD_W = 512
SSD_HD = 64
SSD_HEADS = 8
SSD_GROUPS = 2
SSD_STATE = 128
HPG = SSD_HEADS // SSD_GROUPS
GROUP_W = HPG * SSD_HD
CONV_W = 5
CONV_CH = SSD_W + 2 * SSD_GROUPS * SSD_STATE
D_FF = 2816
GRID_W = 64
GRID_SHIFT = 6
ROPE_BASE = 10000.0
EPS = 1e-6
ALPHA = 2.0 ** 0.25
MAIN_COLS = 4 * RET_W + SSD_W + CONV_CH
XBC_COL0 = 4 * RET_W + SSD_W
N_DT = 2 * SSD_HEADS

CHUNK = 256
HALO = 8
FF_BLK = 256
ADA_COLS = 512
LANES = 128
VMEM_LIMIT = 62 * 1024 * 1024


def _dot(a, b):
    return jnp.dot(a, b, preferred_element_type=F32)


def _dot_nt(a, b):
    return lax.dot_general(a, b, (((1,), (1,)), ((), ())), preferred_element_type=F32)


def _silu(x):
    return x * jax.nn.sigmoid(x)


def _softplus(x):
    return jnp.maximum(x, 0.0) + jnp.log1p(jnp.exp(-jnp.abs(x)))


def _layer_norm(y, g, b):
    mu = jnp.mean(y, axis=-1, keepdims=True)
    yc = y - mu
    var = jnp.mean(yc * yc, axis=-1, keepdims=True)
    return yc * lax.rsqrt(var + EPS) * g + b


def _cumsum(x, axis):
    n = x.shape[axis]
    idx = lax.broadcasted_iota(jnp.int32, x.shape, axis)
    s = 1
    while s < n:
        x = x + jnp.where(idx >= s, pltpu.roll(x, s, axis), 0.0)
        s *= 2
    return x


def _expand4(cols, lane):
    a = jnp.where(lane < SSD_HD, cols[0], cols[1])
    b = jnp.where(lane < SSD_HD, cols[2], cols[3])
    return jnp.concatenate([a, b], axis=1)


def _interleave(a, b):
    ia = ib = 0
    while ia < len(a) or ib < len(b):
        if ib >= len(b) or (ia < len(a) and ia * len(b) <= ib * len(a)):
            a[ia]()
            ia += 1
        else:
            b[ib]()
            ib += 1


def _ada_kernel(cond_ref, w_ref, b_ref, o_ref):
    s = _silu(cond_ref[...]).astype(BF16)
    o_ref[...] = _dot(s, w_ref[...].astype(BF16)) + b_ref[...]


def _ada_call(cond, w_ada, b_ada):
    rows = cond.shape[0]
    n = w_ada.shape[1]
    return pl.pallas_call(
        _ada_kernel,
        grid=(n // ADA_COLS,),
        in_specs=[
            pl.BlockSpec((rows, D_MODEL), lambda j: (0, 0)),
            pl.BlockSpec((D_MODEL, ADA_COLS), lambda j: (0, j)),
            pl.BlockSpec((1, ADA_COLS), lambda j: (0, j)),
        ],
        out_specs=pl.BlockSpec((rows, ADA_COLS), lambda j: (0, j)),
        out_shape=jax.ShapeDtypeStruct((rows, n), F32),
        compiler_params=pltpu.CompilerParams(dimension_semantics=("arbitrary",)),
        name="adaln_mod",
    )(cond, w_ada, b_ada)


RV_TAIL_F, RV_TAIL_B, RV_CROSS_F, RV_CROSS_B = range(4)


def _layer_kernel(*refs, L, nb, has_state, use_rope, emit_state):
    nc = L // CHUNK
    C = CHUNK
    cross = has_state or nc > 1
    it = iter(refs)
    x_ref, xprev_ref, xnext_ref, mod_ref, modf_ref, wmain_ref, wdtc_ref, wdtr_ref = (next(it) for _ in range(8))
    convw_ref, convb_ref, hp_ref, hpc_ref, normw_ref = (next(it) for _ in range(5))
    wout_ref, ln1g_ref, ln1b_ref = (next(it) for _ in range(3))
    wg_ref, wu_ref, wd_ref, ln2g_ref, ln2b_ref = (next(it) for _ in range(5))
    if has_state:
        sret0_ref, sssd0_ref = next(it), next(it)
    out_ref = next(it)
    if emit_state:
        nret_ref, nssd_ref = next(it), next(it)
    (q_s, kT_s, v_s, g_s, z_s, stage_s, xs_s, bT_s, c_s, inccol_s, exccol_s, cfcb_s, dec_s,
     rowarg_s, y_s, mix_s, rloc_s, sloc_s, rent_s, sent_s, wdec_s, rvec_s, x1_s, hid_s) = (
         next(it) for _ in range(24))
    if use_rope:
        cos_s, sin_s = next(it), next(it)

    seq = pl.program_id(0)
    step = pl.program_id(1)

    hp = hp_ref[...]
    dt_bias_row = hp[0:1, :]
    nega_row = -jnp.exp(hp[1:2, :])
    lg_row = -_softplus(-hp[2:3, :])
    dskip_row = hp[3:4, :]
    hpc = hpc_ref[...]
    dt_bias_col = hpc[:, 0:1]
    nega_col = -jnp.exp(hpc[:, 1:2])

    lane = lax.broadcasted_iota(jnp.int32, (1, LANES), 1)

    def ret_decays(hd):
        lgf = lg_row[:, hd:hd + 1]
        lgb = lg_row[:, RET_HEADS + hd:RET_HEADS + hd + 1]
        return lgf, lgb

    def group_heads(gi):
        return [gi * HPG + k for k in range(HPG)]

    def expand_f(arr, gi):
        return _expand4([arr[:, hh:hh + 1] for hh in group_heads(gi)], lane)

    def expand_b(arr, gi):
        return _expand4([arr[:, SSD_HEADS + hh:SSD_HEADS + hh + 1] for hh in group_heads(gi)], lane)

    @pl.when((seq == 0) & (step == 0))
    def _():
        ii = lax.broadcasted_iota(jnp.int32, (C, C), 0)
        jj = lax.broadcasted_iota(jnp.int32, (C, C), 1)
        dmat = (ii - jj).astype(F32)
        irow = lax.broadcasted_iota(jnp.int32, (C, LANES), 0).astype(F32)
        for hd in range(RET_HEADS):
            lgf, lgb = ret_decays(hd)
            wdec_s[hd] = jnp.exp(jnp.where(jj <= ii, dmat * lgf, -dmat * lgb))
            rvec_s[hd, RV_TAIL_F] = jnp.exp((C - 1.0 - irow) * lgf)
            rvec_s[hd, RV_TAIL_B] = jnp.exp(irow * lgb)
            rvec_s[hd, RV_CROSS_F] = jnp.exp((irow + 1.0) * lgf)
            rvec_s[hd, RV_CROSS_B] = jnp.exp((C - irow) * lgb)
        if use_rope:
            ln = lax.broadcasted_iota(jnp.int32, (C, LANES), 1)
            nf = RET_HD // 4
            inv = jnp.exp((ln & (nf - 1)).astype(F32) * (-math.log(ROPE_BASE) / nf))
            for cc in range(nc):
                t = lax.broadcasted_iota(jnp.int32, (C, LANES), 0) + cc * C
                pos = jnp.where((ln & (2 * nf - 1)) < nf, t >> GRID_SHIFT, t & (GRID_W - 1)).astype(F32)
                ang = pos * inv
                cos_s[cc] = jnp.cos(ang)
                sin_s[cc] = jnp.where(ln < RET_HD // 2, -jnp.sin(ang), jnp.sin(ang))

    def project(c):
        st = {}
        CB = 256
        half = CONV_W // 2

        def v_mod():
            sh1 = mod_ref[0, 0:1, :]
            sc1 = mod_ref[0, 1:2, :]
            xe = jnp.concatenate([xprev_ref[0], x_ref[0], xnext_ref[0]], axis=0)
            xm = xe * (1.0 + sc1) + sh1
            st["he"] = xm.astype(BF16)
            st["h"] = xm[HALO:HALO + C].astype(BF16)

        def m_main(name, lo, hi, halo=False):
            def run():
                st[name] = _dot(st["he" if halo else "h"], wmain_ref[:, lo:hi])
            return run

        def m_dt():
            st["dtc"] = _dot(st["h"], wdtc_ref[...])
            st["dtr"] = _dot_nt(wdtr_ref[...], st["h"])

        def v_dt():
            dt_c = _softplus(st["dtc"] + dt_bias_row)
            lac = dt_c * nega_row
            inc_col = _cumsum(lac, 0)
            exc_col = inc_col - lac
            tot_col = inc_col[C - 1:C, :]
            inccol_s[c] = inc_col
            exccol_s[c] = exc_col
            dec_s[c] = jnp.broadcast_to(jnp.exp(tot_col), (8, LANES))
            if cross:
                cfcb_s[c] = jnp.exp(jnp.where(lane < SSD_HEADS, inc_col, tot_col - exc_col))
            st["sf"] = jnp.exp(tot_col - inc_col) * dt_c
            st["sb"] = jnp.exp(exc_col) * dt_c
            dt_r = _softplus(st["dtr"] + dt_bias_col)
            lar = dt_r * nega_col
            inc_row = _cumsum(lar, 1)
            ldt = jnp.log(dt_r)
            rid = lax.broadcasted_iota(jnp.int32, (N_DT, C), 0)
            rowarg_s[c] = jnp.where(rid < SSD_HEADS, inc_row - ldt, inc_row - lar + ldt)

        def v_stage(hf):
            def run():
                pe = st["pe%d" % hf]
                cs = slice(hf * 512, (hf + 1) * 512)
                stage_s[0:HALO, cs] = jnp.where(c > 0, pe[0:HALO], 0.0)
                stage_s[HALO:HALO + C, cs] = pe[HALO:HALO + C]
                stage_s[HALO + C:, cs] = jnp.where(c < nc - 1, pe[HALO + C:], 0.0)
            return run

        def conv_block(cb):
            cs = slice(cb * CB, (cb + 1) * CB)
            acc = jnp.broadcast_to(convb_ref[0:1, cs], (C, CB))
            for k in range(CONV_W):
                o = HALO + k - half
                acc = acc + convw_ref[k:k + 1, cs] * stage_s[o:o + C, cs]
            return _silu(acc)

        def v_conv_x(cb):
            def run():
                xs_s[c, :, cb * CB:(cb + 1) * CB] = conv_block(cb)
            return run

        def v_conv_b():
            bT_s[c] = conv_block(SSD_W // CB).T.astype(BF16)

        def v_conv_c():
            c_s[c] = conv_block(SSD_W // CB + 1).astype(BF16)

        def rope(a):
            if not use_rope:
                return a
            return a * cos_s[c] + pltpu.roll(a, RET_HD // 2, 1) * sin_s[c]

        def v_q():
            for hd in range(RET_HEADS):
                sl = slice(hd * RET_HD, (hd + 1) * RET_HD)
                q_s[c, :, sl] = rope(st["pq"][:, sl]).astype(BF16)

        def v_k():
            for hd in range(RET_HEADS):
                sl = slice(hd * RET_HD, (hd + 1) * RET_HD)
                kh = rope(st["pk"][:, sl]) * (RET_HD ** -0.5)
                kT_s[c, sl, :] = kh.T.astype(BF16)

        def v_v():
            v_s[c] = st["pv"].astype(BF16)

        def v_g():
            g_s[c] = _silu(st["pg"]).astype(BF16)

        def v_z():
            z_s[c] = _silu(st["pz"]).astype(BF16)

        def m_sloc(gi):
            def run():
                xg = xs_s[c, :, gi * GROUP_W:(gi + 1) * GROUP_W]
                vcat = jnp.concatenate([xg * expand_f(st["sf"], gi), xg * expand_b(st["sb"], gi)],
                                       axis=1).astype(BF16)
                sloc_s[c, gi] = _dot(bT_s[c, gi * SSD_STATE:(gi + 1) * SSD_STATE, :], vcat).astype(sloc_s.dtype)
            return run

        def m_rloc(hd):
            def run():
                sl = slice(hd * RET_HD, (hd + 1) * RET_HD)
                vf = st["pv"][:, sl]
                vcat = jnp.concatenate([vf * rvec_s[hd, RV_TAIL_F], vf * rvec_s[hd, RV_TAIL_B]],
                                       axis=1).astype(BF16)
                rloc_s[c, hd] = _dot(kT_s[c, sl, :], vcat).astype(rloc_s.dtype)
            return run

        m_pe0 = m_main("pe0", XBC_COL0, XBC_COL0 + 512, halo=True)
        m_pe1 = m_main("pe1", XBC_COL0 + 512, MAIN_COLS, halo=True)
        m_q = m_main("pq", 0, RET_W)
        m_k = m_main("pk", RET_W, 2 * RET_W)
        m_v = m_main("pv", 2 * RET_W, 3 * RET_W)
        m_g = m_main("pg", 3 * RET_W, 4 * RET_W)
        m_z = m_main("pz", 4 * RET_W, XBC_COL0)
        order = [v_mod, m_dt, m_pe0, m_pe1, v_dt, m_q, v_stage(0), v_stage(1), m_k, v_conv_x(0), m_v,
                 v_conv_x(1), v_q, m_g, v_conv_b, v_k, m_z, v_conv_c, v_v,
                 m_rloc(0), m_rloc(1), v_g, m_rloc(2), m_rloc(3), m_sloc(0), m_sloc(1), v_z]
        for piece in order:
            piece()

    def recurrences():
        for hd in range(RET_HEADS):
            lgf, lgb = ret_decays(hd)
            dec_f = jnp.exp(C * lgf)
            dec_b = jnp.exp(C * lgb)
            if has_state:
                ent_f = sret0_ref[0, 0, 0, hd]
                ent_b = sret0_ref[0, 0, 1, hd]
            else:
                ent_f = jnp.zeros((RET_HD, RET_HD), F32)
                ent_b = jnp.zeros((RET_HD, RET_HD), F32)
            for c in range(nc):
                if cross:
                    rent_s[c, hd, :, 0:RET_HD] = ent_f.astype(BF16)
                ent_f = dec_f * ent_f + rloc_s[c, hd, :, 0:RET_HD]
            for c in range(nc - 1, -1, -1):
                if cross:
                    rent_s[c, hd, :, RET_HD:] = ent_b.astype(BF16)
                ent_b = dec_b * ent_b + rloc_s[c, hd, :, RET_HD:]
            if emit_state:
                nret_ref[0, 0, 0, hd] = ent_f
                nret_ref[0, 0, 1, hd] = ent_b

        for gi in range(SSD_GROUPS):
            heads = group_heads(gi)
            if has_state:
                ent_f = jnp.concatenate([sssd0_ref[0, 0, 0, hh] for hh in heads], axis=1)
                ent_b = jnp.concatenate([sssd0_ref[0, 0, 1, hh] for hh in heads], axis=1)
            else:
                ent_f = jnp.zeros((SSD_STATE, GROUP_W), F32)
                ent_b = jnp.zeros((SSD_STATE, GROUP_W), F32)
            for c in range(nc):
                if cross:
                    sent_s[c, gi, :, 0:GROUP_W] = ent_f.astype(BF16)
                ent_f = expand_f(dec_s[c, 0:1, :], gi) * ent_f + sloc_s[c, gi, :, 0:GROUP_W]
            for c in range(nc - 1, -1, -1):
                if cross:
                    sent_s[c, gi, :, GROUP_W:] = ent_b.astype(BF16)
                ent_b = expand_b(dec_s[c, 0:1, :], gi) * ent_b + sloc_s[c, gi, :, GROUP_W:]
            if emit_state:
                for k, hh in enumerate(heads):
                    nssd_ref[0, 0, 0, hh] = ent_f[:, k * SSD_HD:(k + 1) * SSD_HD]
                    nssd_ref[0, 0, 1, hh] = ent_b[:, k * SSD_HD:(k + 1) * SSD_HD]

    def emit_pieces(c, slot):
        rsl = [slice(hd * RET_HD, (hd + 1) * RET_HD) for hd in range(RET_HEADS)]
        gsl = [slice(gi * SSD_STATE, (gi + 1) * SSD_STATE) for gi in range(SSD_GROUPS)]
        xsl = [slice(gi * GROUP_W, (gi + 1) * GROUP_W) for gi in range(SSD_GROUPS)]
        st = {}

        def scores():
            ii = lax.broadcasted_iota(jnp.int32, (C, C), 0)
            jj = lax.broadcasted_iota(jnp.int32, (C, C), 1)
            st["causal"] = jj <= ii
            qs = [q_s[c, :, sl] for sl in rsl]
            cms = [c_s[c, :, sl] for sl in gsl]
            st["sc_s"] = [_dot(cms[gi], bT_s[c, gsl[gi], :]) for gi in range(SSD_GROUPS)]
            st["sc_r"] = [_dot(qs[hd], kT_s[c, rsl[hd], :]) for hd in range(RET_HEADS)]
            if cross:
                st["yc_s"] = [_dot(cms[gi], sent_s[c, gi]) for gi in range(SSD_GROUPS)]
                st["yc_r"] = [_dot(qs[hd], rent_s[c, hd]) for hd in range(RET_HEADS)]
            st["inc_col"] = inccol_s[c]
            st["exc_col"] = exccol_s[c]
            st["rowarg"] = rowarg_s[c]

        def ssd_head(gi, k):
            def run():
                hh = gi * HPG + k
                hb = SSD_HEADS + hh
                arg = jnp.where(st["causal"],
                                st["inc_col"][:, hh:hh + 1] - st["rowarg"][hh:hh + 1, :],
                                st["rowarg"][hb:hb + 1, :] - st["exc_col"][:, hb:hb + 1])
                m = (st["sc_s"][gi] * jnp.exp(arg)).astype(BF16)
                xh = xs_s[c, :, hh * SSD_HD:(hh + 1) * SSD_HD].astype(BF16)
                y_s[:, hh * SSD_HD:(hh + 1) * SSD_HD] = _dot(m, xh)
            return run

        def ssd_group(gi):
            def run():
                xg = xs_s[c, :, xsl[gi]]
                yg = y_s[:, xsl[gi]] + expand_f(dskip_row, gi) * xg
                if cross:
                    cfcb = cfcb_s[c]
                    yc = st["yc_s"][gi]
                    yg = yg + expand_f(cfcb, gi) * yc[:, 0:GROUP_W] + expand_b(cfcb, gi) * yc[:, GROUP_W:]
                y_s[:, xsl[gi]] = yg
            return run

        def ret_head(hd):
            def run():
                m = (st["sc_r"][hd] * wdec_s[hd]).astype(BF16)
                o = _dot(m, v_s[c, :, rsl[hd]])
                if cross:
                    yc = st["yc_r"][hd]
                    o = o + rvec_s[hd, RV_CROSS_F] * yc[:, 0:RET_HD] + rvec_s[hd, RV_CROSS_B] * yc[:, RET_HD:]
                o = o * lax.rsqrt(jnp.mean(o * o, axis=-1, keepdims=True) + EPS)
                mix_s[:, rsl[hd]] = (g_s[c, :, rsl[hd]].astype(F32) * o).astype(BF16)
            return run

        def ssd_norm():
            yz = y_s[...] * z_s[c].astype(F32)
            yn = yz * lax.rsqrt(jnp.mean(yz * yz, axis=-1, keepdims=True) + EPS) * normw_ref[...]
            mix_s[:, RET_W:] = yn.astype(BF16)

        def out_proj():
            g1 = mod_ref[0, 2:3, :]
            y = ALPHA * x_ref[0] + g1 * _dot(mix_s[...], wout_ref[...])
            x1_s[slot] = _layer_norm(y, ln1g_ref[...], ln1b_ref[...])

        pieces = [scores]
        for gi in range(SSD_GROUPS):
            pieces += [ssd_head(gi, k) for k in range(HPG)] + [ssd_group(gi)]
        pieces += [ret_head(hd) for hd in range(RET_HEADS)] + [ssd_norm, out_proj]
        return pieces

    def ffn_pieces(slot):
        st = {}

        def start():
            sh2 = modf_ref[0, 3:4, :]
            sc2 = modf_ref[0, 4:5, :]
            st["h2"] = (x1_s[slot] * (1.0 + sc2) + sh2).astype(BF16)

        def hidden(j):
            def run():
                js = slice(j * FF_BLK, (j + 1) * FF_BLK)
                h2 = st["h2"]
                hid_s[:, js] = (_silu(_dot(h2, wg_ref[:, js])) * _dot(h2, wu_ref[:, js])).astype(BF16)
            return run

        def finish():
            g2 = modf_ref[0, 5:6, :]
            y = ALPHA * x1_s[slot] + g2 * _dot(hid_s[...], wd_ref[...])
            out_ref[0] = _layer_norm(y, ln2g_ref[...], ln2b_ref[...])

        return [start] + [hidden(j) for j in range(D_FF // FF_BLK)] + [finish]

    def run_all(pieces):
        for p in pieces:
            p()

    @pl.when((step < nc) & (seq < nb))
    def _():
        project(step)

    @pl.when((step == nc) & (seq < nb))
    def _():
        recurrences()

    kk = step - nc
    slot = (seq * nc + kk) & 1
    first = (seq == 0) & (step == nc)

    @pl.when(first)
    def _():
        run_all(emit_pieces(kk, slot))

    @pl.when((step >= nc) & (seq < nb) & jnp.logical_not(first))
    def _():
        _interleave(ffn_pieces(1 - slot), emit_pieces(kk, slot))

    @pl.when((seq == nb) & (step == nc))
    def _():
        run_all(ffn_pieces((nb * nc - 1) & 1))


def _const_spec(shape):
    nd = len(shape)
    return pl.BlockSpec(shape, lambda b, s: (0,) * nd, pipeline_mode=pl.Buffered(1))


def _layer_call(x, mod, mod_per_seq, weights, states, emit_state, use_rope, name):
    nb, L, _ = x.shape
    nc = L // CHUNK
    hpc_blocks = CHUNK // HALO
    has_state = states is not None
    last = nb - 1

    def chunk_of(s):
        return jnp.where(s < nc, s, s - nc)

    def seq_of(b):
        return jnp.minimum(b, last)

    def lag_seq(b, s):
        return jnp.minimum(jnp.where(s > nc, b, jnp.maximum(b - 1, 0)), last)

    def out_map(b, s):
        live = (s > nc) & (b < nb)
        idle_chunk = jnp.where(b == 0, 0, nc - 1)
        return (jnp.where(live, b, jnp.maximum(b - 1, 0)), jnp.where(live, s - nc - 1, idle_chunk), 0)

    if mod_per_seq:
        mod_map = lambda b, s: (seq_of(b), 0, 0)
        modf_map = lambda b, s: (lag_seq(b, s), 0, 0)
    else:
        mod_map = modf_map = lambda b, s: (0, 0, 0)

    in_specs = [
        pl.BlockSpec((1, CHUNK, D_MODEL), lambda b, s: (seq_of(b), chunk_of(s), 0)),
        pl.BlockSpec((1, HALO, D_MODEL),
                     lambda b, s: (seq_of(b), jnp.maximum(chunk_of(s) * hpc_blocks - 1, 0), 0)),
        pl.BlockSpec((1, HALO, D_MODEL),
                     lambda b, s: (seq_of(b), jnp.minimum((chunk_of(s) + 1) * hpc_blocks, nc * hpc_blocks - 1), 0)),
        pl.BlockSpec((1, 6, D_MODEL), mod_map),
        pl.BlockSpec((1, 6, D_MODEL), modf_map),
    ] + [_const_spec(w.shape) for w in weights]
    args = [x, x, x, mod, mod] + list(weights)
    ret_block = (1, 1, 2, RET_HEADS, RET_HD, RET_HD)
    ssd_block = (1, 1, 2, SSD_HEADS, SSD_STATE, SSD_HD)
    state_map = lambda b, s: (seq_of(b), 0, 0, 0, 0, 0)
    if has_state:
        in_specs += [pl.BlockSpec(ret_block, state_map, pipeline_mode=pl.Buffered(1)),
                     pl.BlockSpec(ssd_block, state_map, pipeline_mode=pl.Buffered(1))]
        args += list(states)
    out_shape = [jax.ShapeDtypeStruct((nb, L, D_MODEL), F32)]
    out_specs = [pl.BlockSpec((1, CHUNK, D_MODEL), out_map)]
    if emit_state:
        out_shape += [jax.ShapeDtypeStruct((nb,) + ret_block[1:], F32),
                      jax.ShapeDtypeStruct((nb,) + ssd_block[1:], F32)]
        out_specs += [pl.BlockSpec(ret_block, state_map), pl.BlockSpec(ssd_block, state_map)]
    state_dt = F32 if emit_state else BF16
    scratch = [
        pltpu.VMEM((nc, CHUNK, RET_W), BF16),
        pltpu.VMEM((nc, RET_W, CHUNK), BF16),
        pltpu.VMEM((nc, CHUNK, RET_W), BF16),
        pltpu.VMEM((nc, CHUNK, RET_W), BF16),
        pltpu.VMEM((nc, CHUNK, SSD_W), BF16),
        pltpu.VMEM((CHUNK + 2 * HALO, CONV_CH), F32),
        pltpu.VMEM((nc, CHUNK, SSD_W), F32),
        pltpu.VMEM((nc, SSD_GROUPS * SSD_STATE, CHUNK), BF16),
        pltpu.VMEM((nc, CHUNK, SSD_GROUPS * SSD_STATE), BF16),
        pltpu.VMEM((nc, CHUNK, LANES), F32),
        pltpu.VMEM((nc, CHUNK, LANES), F32),
        pltpu.VMEM((nc, CHUNK, LANES), F32),
        pltpu.VMEM((nc, 8, LANES), F32),
        pltpu.VMEM((nc, N_DT, CHUNK), F32),
        pltpu.VMEM((CHUNK, SSD_W), F32),
        pltpu.VMEM((CHUNK, 2 * RET_W), BF16),
        pltpu.VMEM((nc, RET_HEADS, RET_HD, 2 * RET_HD), state_dt),
        pltpu.VMEM((nc, SSD_GROUPS, SSD_STATE, 2 * GROUP_W), state_dt),
        pltpu.VMEM((nc, RET_HEADS, RET_HD, 2 * RET_HD), BF16),
        pltpu.VMEM((nc, SSD_GROUPS, SSD_STATE, 2 * GROUP_W), BF16),
        pltpu.VMEM((RET_HEADS, CHUNK, CHUNK), F32),
        pltpu.VMEM((RET_HEADS, 4, CHUNK, LANES), F32),
        pltpu.VMEM((2, CHUNK, D_MODEL), F32),
        pltpu.VMEM((CHUNK, D_FF), BF16),
    ]
    if use_rope:
        scratch += [pltpu.VMEM((nc, CHUNK, LANES), F32)] * 2
    kern = functools.partial(_layer_kernel, L=L, nb=nb, has_state=has_state, use_rope=use_rope,
                             emit_state=emit_state)
    return pl.pallas_call(
        kern,
        grid=(nb + 1, 2 * nc),
        in_specs=in_specs,
        out_specs=out_specs,
        out_shape=out_shape,
        scratch_shapes=scratch,
        compiler_params=pltpu.CompilerParams(dimension_semantics=("arbitrary", "arbitrary"),
                                             vmem_limit_bytes=VMEM_LIMIT),
        name=name,
    )(*args)


def kernel(x_prompt, x_sample, state_ret, state_ssd, c, c_ctx, w_in, ret_decay_fwd, ret_decay_bwd, conv_w, conv_b, dt_bias_fwd, dt_bias_bwd, a_log_fwd, a_log_bwd, d_skip, ssd_norm_w, w_out, ln1_g, ln1_b, w_gate, w_up, w_down, ln2_g, ln2_b, w_ada, b_ada):
    depth = w_in.shape[0]
    assert depth == 1, "single trunk layer"
    bp, lp, _ = x_prompt.shape
    bs, ls, _ = x_sample.shape
    assert lp % CHUNK == 0 and ls % CHUNK == 0 and ls % GRID_W == 0 and D_FF % FF_BLK == 0

    rows = -(-(bs + 1) // 8) * 8
    cond = jnp.zeros((rows, D_MODEL), F32).at[:bs].set(c).at[bs].set(c_ctx)
    mod = _ada_call(cond, w_ada[0], b_ada[0][None, :]).reshape(rows, 6, D_MODEL)
    mod_lat = mod[:bs]
    mod_ctx = mod[bs:bs + 1]

    wdt = w_in[0, :, MAIN_COLS:]
    wdt2 = jnp.concatenate([wdt, wdt], axis=1)
    wdtc = jnp.zeros((D_MODEL, LANES), F32).at[:, :N_DT].set(wdt2).astype(BF16)
    wdtr = wdt2.T.astype(BF16)
    convw = jnp.zeros((8, CONV_CH), F32).at[:CONV_W].set(conv_w[0])
    convb = conv_b[0][None, :]
    dt_bias = jnp.concatenate([dt_bias_fwd[0], dt_bias_bwd[0]])
    a_log = jnp.concatenate([a_log_fwd[0], a_log_bwd[0]])
    ret_decay = jnp.concatenate([ret_decay_fwd[0], ret_decay_bwd[0]])
    hp = jnp.zeros((8, LANES), F32)
    hp = hp.at[0, :N_DT].set(dt_bias).at[1, :N_DT].set(a_log)
    hp = hp.at[2, :2 * RET_HEADS].set(ret_decay).at[3, :SSD_HEADS].set(d_skip[0])
    hpc = jnp.zeros((N_DT, LANES), F32).at[:, 0].set(dt_bias).at[:, 1].set(a_log)
    weights = (w_in[0].astype(BF16), wdtc, wdtr, convw, convb, hp, hpc, ssd_norm_w[0][None, :],
               w_out[0].astype(BF16), ln1_g[0][None, :], ln1_b[0][None, :],
               w_gate[0].astype(BF16), w_up[0].astype(BF16), w_down[0].astype(BF16),
               ln2_g[0][None, :], ln2_b[0][None, :])

    yp, new_ret, new_ssd = _layer_call(x_prompt, mod_ctx, False, weights, None, True, False, "layer_ctx")
    (ys,) = _layer_call(x_sample, mod_lat, True, weights, (state_ret, state_ssd), False, True, "layer_lat")
    return (yp, ys, new_ret, new_ssd)
```

```python
import functools
import math

import jax
import jax.numpy as jnp
from jax import lax
from jax.experimental import pallas as pl
from jax.experimental.pallas import tpu as pltpu

F32 = jnp.float32
BF16 = jnp.bfloat16

D_MODEL = 1024
RET_W = 512
RET_HEADS = 4
RET_HD = 128
SSD_W = 512
SSD_HD = 64
SSD_HEADS = 8
SSD_GROUPS = 2
SSD_STATE = 128
HPG = SSD_HEADS // SSD_GROUPS
GROUP_W = HPG * SSD_HD
CONV_W = 5
CONV_CH = SSD_W + 2 * SSD_GROUPS * SSD_STATE
D_FF = 2816
GRID_W = 64
GRID_SHIFT = 6
ROPE_BASE = 10000.0
EPS = 1e-6
ALPHA = 2.0 ** 0.25
MAIN_COLS = 4 * RET_W + SSD_W + CONV_CH
XBC_COL0 = 4 * RET_W + SSD_W
N_DT = 2 * SSD_HEADS

CHUNK = 256
HALO = 8
FF_BLK = 256
ADA_COLS = 1024
CAST_STEPS = 8
LANES = 128
VMEM_LIMIT = 62 * 1024 * 1024


def _dot(a, b):
    return jnp.dot(a, b, preferred_element_type=F32)


def _dot_nt(a, b):
    return lax.dot_general(a, b, (((1,), (1,)), ((), ())), preferred_element_type=F32)


def _silu(x):
    return x * jax.nn.sigmoid(x)


def _softplus(x):
    return jnp.maximum(x, 0.0) + jnp.log1p(jnp.exp(-jnp.abs(x)))


def _layer_norm(y, g, b):
    mu = jnp.mean(y, axis=-1, keepdims=True)
    yc = y - mu
    var = jnp.mean(yc * yc, axis=-1, keepdims=True)
    return yc * lax.rsqrt(var + EPS) * g + b


def _cumsum(x, axis):
    n = x.shape[axis]
    idx = lax.broadcasted_iota(jnp.int32, x.shape, axis)
    s = 1
    while s < n:
        x = x + jnp.where(idx >= s, pltpu.roll(x, s, axis), 0.0)
        s *= 2
    return x


def _expand4(cols, lane):
    a = jnp.where(lane < SSD_HD, cols[0], cols[1])
    b = jnp.where(lane < SSD_HD, cols[2], cols[3])
    return jnp.concatenate([a, b], axis=1)


def _interleave(a, b):
    ia = ib = 0
    while ia < len(a) or ib < len(b):
        if ib >= len(b) or (ia < len(a) and ia * len(b) <= ib * len(a)):
            a[ia]()
            ia += 1
        else:
            b[ib]()
            ib += 1


def _ada_kernel(cond_ref, w_ref, b_ref, o_ref):
    s = _silu(cond_ref[...]).astype(BF16)
    o_ref[...] = _dot(s, w_ref[...].astype(BF16)) + b_ref[...]


def _ada_call(cond, w_ada, b_ada):
    rows = cond.shape[0]
    n = w_ada.shape[1]
    return pl.pallas_call(
        _ada_kernel,
        grid=(n // ADA_COLS,),
        in_specs=[
            pl.BlockSpec((rows, D_MODEL), lambda j: (0, 0)),
            pl.BlockSpec((D_MODEL, ADA_COLS), lambda j: (0, j)),
            pl.BlockSpec((1, ADA_COLS), lambda j: (0, j)),
        ],
        out_specs=pl.BlockSpec((rows, ADA_COLS), lambda j: (0, j)),
        out_shape=jax.ShapeDtypeStruct((rows, n), F32),
        compiler_params=pltpu.CompilerParams(dimension_semantics=("arbitrary",)),
        name="adaln_mod",
    )(cond, w_ada, b_ada)


def _cast_kernel(*refs):
    n = len(refs) // 2
    for src, dst in zip(refs[:n], refs[n:]):
        cols = src.shape[1]
        if dst.shape[1] != cols:
            dst[...] = jnp.zeros(dst.shape, BF16)
        dst[:, 0:cols] = src[...].astype(BF16)


def _cast_call(ws):
    for w in ws:
        assert w.shape[0] % (CAST_STEPS * 16) == 0, w.shape
    padded = [-(-w.shape[1] // LANES) * LANES for w in ws]
    in_specs = [pl.BlockSpec((w.shape[0] // CAST_STEPS, w.shape[1]), lambda i: (i, 0)) for w in ws]
    out_specs = [pl.BlockSpec((w.shape[0] // CAST_STEPS, n), lambda i: (i, 0)) for w, n in zip(ws, padded)]
    return pl.pallas_call(
        _cast_kernel,
        grid=(CAST_STEPS,),
        in_specs=in_specs,
        out_specs=out_specs,
        out_shape=[jax.ShapeDtypeStruct((w.shape[0], n), BF16) for w, n in zip(ws, padded)],
        compiler_params=pltpu.CompilerParams(dimension_semantics=("arbitrary",),
                                             vmem_limit_bytes=VMEM_LIMIT),
        name="weights_to_bf16",
    )(*ws)


RV_TAIL_F, RV_TAIL_B, RV_CROSS_F, RV_CROSS_B = range(4)


def _layer_kernel(*refs, L, nb, has_state, use_rope, emit_state):
    nc = L // CHUNK
    C = CHUNK
    cross = has_state or nc > 1
    it = iter(refs)
    x_ref, xprev_ref, xnext_ref, mod_ref, modf_ref, wmain_ref = (next(it) for _ in range(6))
    convw_ref, convb_ref, hp_ref, hpc_ref, normw_ref = (next(it) for _ in range(5))
    wout_ref, ln1g_ref, ln1b_ref = (next(it) for _ in range(3))
    wg_ref, wu_ref, wd_ref, ln2g_ref, ln2b_ref = (next(it) for _ in range(5))
    if has_state:
        sret0_ref, sssd0_ref = next(it), next(it)
    out_ref = next(it)
    if emit_state:
        nret_ref, nssd_ref = next(it), next(it)
    (q_s, kT_s, v_s, g_s, z_s, stage_s, xs_s, bT_s, c_s, inccol_s, exccol_s, cfcb_s, dec_s,
     rowarg_s, y_s, mix_s, rloc_s, sloc_s, rent_s, sent_s, wdec_s, rvec_s, x1_s, hid_s, wdtr_s) = (
         next(it) for _ in range(25))
    if use_rope:
        cos_s, sin_s = next(it), next(it)

    seq = pl.program_id(0)
    step = pl.program_id(1)

    hp = hp_ref[...]
    dt_bias_row = hp[0:1, :]
    nega_row = -jnp.exp(hp[1:2, :])
    lg_row = -_softplus(-hp[2:3, :])
    dskip_row = hp[3:4, :]
    hpc = hpc_ref[...]
    dt_bias_col = hpc[:, 0:1]
    nega_col = -jnp.exp(hpc[:, 1:2])

    lane = lax.broadcasted_iota(jnp.int32, (1, LANES), 1)

    def ret_decays(hd):
        lgf = lg_row[:, hd:hd + 1]
        lgb = lg_row[:, RET_HEADS + hd:RET_HEADS + hd + 1]
        return lgf, lgb

    def group_heads(gi):
        return [gi * HPG + k for k in range(HPG)]

    def expand_f(arr, gi):
        return _expand4([arr[:, hh:hh + 1] for hh in group_heads(gi)], lane)

    def expand_b(arr, gi):
        return _expand4([arr[:, SSD_HEADS + hh:SSD_HEADS + hh + 1] for hh in group_heads(gi)], lane)

    @pl.when((seq == 0) & (step == 0))
    def _():
        ii = lax.broadcasted_iota(jnp.int32, (C, C), 0)
        jj = lax.broadcasted_iota(jnp.int32, (C, C), 1)
        dmat = (ii - jj).astype(F32)
        irow = lax.broadcasted_iota(jnp.int32, (C, LANES), 0).astype(F32)
        for hd in range(RET_HEADS):
            lgf, lgb = ret_decays(hd)
            wdec_s[hd] = jnp.exp(jnp.where(jj <= ii, dmat * lgf, -dmat * lgb))
            rvec_s[hd, RV_TAIL_F] = jnp.exp((C - 1.0 - irow) * lgf)
            rvec_s[hd, RV_TAIL_B] = jnp.exp(irow * lgb)
            rvec_s[hd, RV_CROSS_F] = jnp.exp((irow + 1.0) * lgf)
            rvec_s[hd, RV_CROSS_B] = jnp.exp((C - irow) * lgb)
        wdtr_s[...] = wmain_ref[:, MAIN_COLS:MAIN_COLS + LANES].astype(F32).T.astype(BF16)
        if use_rope:
            ln = lax.broadcasted_iota(jnp.int32, (C, LANES), 1)
            nf = RET_HD // 4
            inv = jnp.exp((ln & (nf - 1)).astype(F32) * (-math.log(ROPE_BASE) / nf))
            for cc in range(nc):
                t = lax.broadcasted_iota(jnp.int32, (C, LANES), 0) + cc * C
                pos = jnp.where((ln & (2 * nf - 1)) < nf, t >> GRID_SHIFT, t & (GRID_W - 1)).astype(F32)
                ang = pos * inv
                cos_s[cc] = jnp.cos(ang)
                sin_s[cc] = jnp.where(ln < RET_HD // 2, -jnp.sin(ang), jnp.sin(ang))

    def project(c):
        st = {}
        CB = 256
        half = CONV_W // 2

        def v_mod():
            sh1 = mod_ref[0, 0:1, :]
            sc1 = mod_ref[0, 1:2, :]
            xe = jnp.concatenate([xprev_ref[0], x_ref[0], xnext_ref[0]], axis=0)
            xm = xe * (1.0 + sc1) + sh1
            st["he"] = xm.astype(BF16)
            st["h"] = xm[HALO:HALO + C].astype(BF16)

        def m_main(name, lo, hi, halo=False):
            def run():
                st[name] = _dot(st["he" if halo else "h"], wmain_ref[:, lo:hi])
            return run

        def m_dt():
            raw_c = _dot(st["h"], wmain_ref[:, MAIN_COLS:MAIN_COLS + LANES])
            st["dtc"] = raw_c + pltpu.roll(raw_c, SSD_HEADS, 1)
            raw_r = _dot_nt(wdtr_s[0:N_DT, :], st["h"])
            st["dtr"] = raw_r + pltpu.roll(raw_r, SSD_HEADS, 0)

        def v_dt():
            dt_c = _softplus(st["dtc"] + dt_bias_row)
            lac = dt_c * nega_row
            inc_col = _cumsum(lac, 0)
            exc_col = inc_col - lac
            tot_col = inc_col[C - 1:C, :]
            inccol_s[c] = inc_col
            exccol_s[c] = exc_col
            dec_s[c] = jnp.broadcast_to(jnp.exp(tot_col), (8, LANES))
            if cross:
                cfcb_s[c] = jnp.exp(jnp.where(lane < SSD_HEADS, inc_col, tot_col - exc_col))
            st["sf"] = jnp.exp(tot_col - inc_col) * dt_c
            st["sb"] = jnp.exp(exc_col) * dt_c
            dt_r = _softplus(st["dtr"] + dt_bias_col)
            lar = dt_r * nega_col
            inc_row = _cumsum(lar, 1)
            ldt = jnp.log(dt_r)
            rid = lax.broadcasted_iota(jnp.int32, (N_DT, C), 0)
            rowarg_s[c] = jnp.where(rid < SSD_HEADS, inc_row - ldt, inc_row - lar + ldt)

        def v_stage(hf):
            def run():
                pe = st["pe%d" % hf]
                cs = slice(hf * 512, (hf + 1) * 512)
                stage_s[0:HALO, cs] = jnp.where(c > 0, pe[0:HALO], 0.0)
                stage_s[HALO:HALO + C, cs] = pe[HALO:HALO + C]
                stage_s[HALO + C:, cs] = jnp.where(c < nc - 1, pe[HALO + C:], 0.0)
            return run

        def conv_block(cb):
            cs = slice(cb * CB, (cb + 1) * CB)
            acc = jnp.broadcast_to(convb_ref[0:1, cs], (C, CB))
            for k in range(CONV_W):
                o = HALO + k - half
                acc = acc + convw_ref[k:k + 1, cs] * stage_s[o:o + C, cs]
            return _silu(acc)

        def v_conv_x(cb):
            def run():
                xs_s[c, :, cb * CB:(cb + 1) * CB] = conv_block(cb)
            return run

        def v_conv_b():
            bT_s[c] = conv_block(SSD_W // CB).T.astype(BF16)

        def v_conv_c():
            c_s[c] = conv_block(SSD_W // CB + 1).astype(BF16)

        def rope(a):
            if not use_rope:
                return a
            return a * cos_s[c] + pltpu.roll(a, RET_HD // 2, 1) * sin_s[c]

        def v_q():
            for hd in range(RET_HEADS):
                sl = slice(hd * RET_HD, (hd + 1) * RET_HD)
                q_s[c, :, sl] = rope(st["pq"][:, sl]).astype(BF16)

        def v_k():
            for hd in range(RET_HEADS):
                sl = slice(hd * RET_HD, (hd + 1) * RET_HD)
                kh = rope(st["pk"][:, sl]) * (RET_HD ** -0.5)
                kT_s[c, sl, :] = kh.T.astype(BF16)

        def v_v():
            v_s[c] = st["pv"].astype(BF16)

        def v_g():
            g_s[c] = _silu(st["pg"]).astype(BF16)

        def v_z():
            z_s[c] = _silu(st["pz"]).astype(BF16)

        def m_sloc(gi):
            def run():
                xg = xs_s[c, :, gi * GROUP_W:(gi + 1) * GROUP_W]
                vcat = jnp.concatenate([xg * expand_f(st["sf"], gi), xg * expand_b(st["sb"], gi)],
                                       axis=1).astype(BF16)
                sloc_s[c, gi] = _dot(bT_s[c, gi * SSD_STATE:(gi + 1) * SSD_STATE, :], vcat).astype(sloc_s.dtype)
            return run

        def m_rloc(hd):
            def run():
                sl = slice(hd * RET_HD, (hd + 1) * RET_HD)
                vf = st["pv"][:, sl]
                vcat = jnp.concatenate([vf * rvec_s[hd, RV_TAIL_F], vf * rvec_s[hd, RV_TAIL_B]],
                                       axis=1).astype(BF16)
                rloc_s[c, hd] = _dot(kT_s[c, sl, :], vcat).astype(rloc_s.dtype)
            return run

        m_pe0 = m_main("pe0", XBC_COL0, XBC_COL0 + 512, halo=True)
        m_pe1 = m_main("pe1", XBC_COL0 + 512, MAIN_COLS, halo=True)
        m_q = m_main("pq", 0, RET_W)
        m_k = m_main("pk", RET_W, 2 * RET_W)
        m_v = m_main("pv", 2 * RET_W, 3 * RET_W)
        m_g = m_main("pg", 3 * RET_W, 4 * RET_W)
        m_z = m_main("pz", 4 * RET_W, XBC_COL0)
        order = [v_mod, m_dt, m_pe0, m_pe1, v_dt, m_q, v_stage(0), v_stage(1), m_k, v_conv_x(0), m_v,
                 v_conv_x(1), v_q, m_g, v_conv_b, v_k, m_z, v_conv_c, v_v,
                 m_rloc(0), m_rloc(1), v_g, m_rloc(2), m_rloc(3), m_sloc(0), m_sloc(1), v_z]
        for piece in order:
            piece()

    def recurrences():
        for hd in range(RET_HEADS):
            lgf, lgb = ret_decays(hd)
            dec_f = jnp.exp(C * lgf)
            dec_b = jnp.exp(C * lgb)
            if has_state:
                ent_f = sret0_ref[0, 0, 0, hd]
                ent_b = sret0_ref[0, 0, 1, hd]
            else:
                ent_f = jnp.zeros((RET_HD, RET_HD), F32)
                ent_b = jnp.zeros((RET_HD, RET_HD), F32)
            for c in range(nc):
                if cross:
                    rent_s[c, hd, :, 0:RET_HD] = ent_f.astype(BF16)
                ent_f = dec_f * ent_f + rloc_s[c, hd, :, 0:RET_HD]
            for c in range(nc - 1, -1, -1):
                if cross:
                    rent_s[c, hd, :, RET_HD:] = ent_b.astype(BF16)
                ent_b = dec_b * ent_b + rloc_s[c, hd, :, RET_HD:]
            if emit_state:
                nret_ref[0, 0, 0, hd] = ent_f
                nret_ref[0, 0, 1, hd] = ent_b

        for gi in range(SSD_GROUPS):
            heads = group_heads(gi)
            if has_state:
                ent_f = jnp.concatenate([sssd0_ref[0, 0, 0, hh] for hh in heads], axis=0).T
                ent_b = jnp.concatenate([sssd0_ref[0, 0, 1, hh] for hh in heads], axis=0).T
            else:
                ent_f = jnp.zeros((SSD_STATE, GROUP_W), F32)
                ent_b = jnp.zeros((SSD_STATE, GROUP_W), F32)
            for c in range(nc):
                if cross:
                    sent_s[c, gi, :, 0:GROUP_W] = ent_f.astype(BF16)
                ent_f = expand_f(dec_s[c, 0:1, :], gi) * ent_f + sloc_s[c, gi, :, 0:GROUP_W]
            for c in range(nc - 1, -1, -1):
                if cross:
                    sent_s[c, gi, :, GROUP_W:] = ent_b.astype(BF16)
                ent_b = expand_b(dec_s[c, 0:1, :], gi) * ent_b + sloc_s[c, gi, :, GROUP_W:]
            if emit_state:
                ent_ft = ent_f.T
                ent_bt = ent_b.T
                for k, hh in enumerate(heads):
                    nssd_ref[0, 0, 0, hh] = ent_ft[k * SSD_HD:(k + 1) * SSD_HD, :]
                    nssd_ref[0, 0, 1, hh] = ent_bt[k * SSD_HD:(k + 1) * SSD_HD, :]

    def emit_pieces(c, slot):
        rsl = [slice(hd * RET_HD, (hd + 1) * RET_HD) for hd in range(RET_HEADS)]
        gsl = [slice(gi * SSD_STATE, (gi + 1) * SSD_STATE) for gi in range(SSD_GROUPS)]
        xsl = [slice(gi * GROUP_W, (gi + 1) * GROUP_W) for gi in range(SSD_GROUPS)]
        st = {}

        def scores():
            ii = lax.broadcasted_iota(jnp.int32, (C, C), 0)
            jj = lax.broadcasted_iota(jnp.int32, (C, C), 1)
            st["causal"] = jj <= ii
            qs = [q_s[c, :, sl] for sl in rsl]
            cms = [c_s[c, :, sl] for sl in gsl]
            st["sc_s"] = [_dot(cms[gi], bT_s[c, gsl[gi], :]) for gi in range(SSD_GROUPS)]
            st["sc_r"] = [_dot(qs[hd], kT_s[c, rsl[hd], :]) for hd in range(RET_HEADS)]
            if cross:
                st["yc_s"] = [_dot(cms[gi], sent_s[c, gi]) for gi in range(SSD_GROUPS)]
                st["yc_r"] = [_dot(qs[hd], rent_s[c, hd]) for hd in range(RET_HEADS)]
            st["inc_col"] = inccol_s[c]
            st["exc_col"] = exccol_s[c]
            st["rowarg"] = rowarg_s[c]

        def ssd_head(gi, k):
            def run():
                hh = gi * HPG + k
                hb = SSD_HEADS + hh
                arg = jnp.where(st["causal"],
                                st["inc_col"][:, hh:hh + 1] - st["rowarg"][hh:hh + 1, :],
                                st["rowarg"][hb:hb + 1, :] - st["exc_col"][:, hb:hb + 1])
                m = (st["sc_s"][gi] * jnp.exp(arg)).astype(BF16)
                xh = xs_s[c, :, hh * SSD_HD:(hh + 1) * SSD_HD].astype(BF16)
                y_s[:, hh * SSD_HD:(hh + 1) * SSD_HD] = _dot(m, xh)
            return run

        def ssd_group(gi):
            def run():
                xg = xs_s[c, :, xsl[gi]]
                yg = y_s[:, xsl[gi]] + expand_f(dskip_row, gi) * xg
                if cross:
                    cfcb = cfcb_s[c]
                    yc = st["yc_s"][gi]
                    yg = yg + expand_f(cfcb, gi) * yc[:, 0:GROUP_W] + expand_b(cfcb, gi) * yc[:, GROUP_W:]
                y_s[:, xsl[gi]] = yg
            return run

        def ret_head(hd):
            def run():
                m = (st["sc_r"][hd] * wdec_s[hd]).astype(BF16)
                o = _dot(m, v_s[c, :, rsl[hd]])
                if cross:
                    yc = st["yc_r"][hd]
                    o = o + rvec_s[hd, RV_CROSS_F] * yc[:, 0:RET_HD] + rvec_s[hd, RV_CROSS_B] * yc[:, RET_HD:]
                o = o * lax.rsqrt(jnp.mean(o * o, axis=-1, keepdims=True) + EPS)
                mix_s[:, rsl[hd]] = (g_s[c, :, rsl[hd]].astype(F32) * o).astype(BF16)
            return run

        def ssd_norm():
            yz = y_s[...] * z_s[c].astype(F32)
            yn = yz * lax.rsqrt(jnp.mean(yz * yz, axis=-1, keepdims=True) + EPS) * normw_ref[...]
            mix_s[:, RET_W:] = yn.astype(BF16)

        def out_proj():
            g1 = mod_ref[0, 2:3, :]
            y = ALPHA * x_ref[0] + g1 * _dot(mix_s[...], wout_ref[...])
            x1_s[slot] = _layer_norm(y, ln1g_ref[...], ln1b_ref[...])

        pieces = [scores]
        for gi in range(SSD_GROUPS):
            pieces += [ssd_head(gi, k) for k in range(HPG)] + [ssd_group(gi)]
        pieces += [ret_head(hd) for hd in range(RET_HEADS)] + [ssd_norm, out_proj]
        return pieces

    def ffn_pieces(slot):
        st = {}

        def start():
            sh2 = modf_ref[0, 3:4, :]
            sc2 = modf_ref[0, 4:5, :]
            st["h2"] = (x1_s[slot] * (1.0 + sc2) + sh2).astype(BF16)

        def hidden(j):
            def run():
                js = slice(j * FF_BLK, (j + 1) * FF_BLK)
                h2 = st["h2"]
                hid_s[:, js] = (_silu(_dot(h2, wg_ref[:, js])) * _dot(h2, wu_ref[:, js])).astype(BF16)
            return run

        def finish():
            g2 = modf_ref[0, 5:6, :]
            y = ALPHA * x1_s[slot] + g2 * _dot(hid_s[...], wd_ref[...])
            out_ref[0] = _layer_norm(y, ln2g_ref[...], ln2b_ref[...])

        return [start] + [hidden(j) for j in range(D_FF // FF_BLK)] + [finish]

    def run_all(pieces):
        for p in pieces:
            p()

    @pl.when((step < nc) & (seq < nb))
    def _():
        project(step)

    @pl.when((step == nc) & (seq < nb))
    def _():
        recurrences()

    kk = step - nc
    slot = (seq * nc + kk) & 1
    first = (seq == 0) & (step == nc)

    @pl.when(first)
    def _():
        run_all(emit_pieces(kk, slot))

    @pl.when((step >= nc) & (seq < nb) & jnp.logical_not(first))
    def _():
        _interleave(ffn_pieces(1 - slot), emit_pieces(kk, slot))

    @pl.when((seq == nb) & (step == nc))
    def _():
        run_all(ffn_pieces((nb * nc - 1) & 1))


def _const_spec(shape):
    nd = len(shape)
    return pl.BlockSpec(shape, lambda b, s: (0,) * nd, pipeline_mode=pl.Buffered(1))


def _layer_call(x, mod, mod_per_seq, weights, states, emit_state, use_rope, name):
    nb, L, _ = x.shape
    nc = L // CHUNK
    hpc_blocks = CHUNK // HALO
    has_state = states is not None
    last = nb - 1

    def chunk_of(s):
        return jnp.where(s < nc, s, s - nc)

    def seq_of(b):
        return jnp.minimum(b, last)

    def lag_seq(b, s):
        return jnp.minimum(jnp.where(s > nc, b, jnp.maximum(b - 1, 0)), last)

    def out_map(b, s):
        live = (s > nc) & (b < nb)
        idle_chunk = jnp.where(b == 0, 0, nc - 1)
        return (jnp.where(live, b, jnp.maximum(b - 1, 0)), jnp.where(live, s - nc - 1, idle_chunk), 0)

    if mod_per_seq:
        mod_map = lambda b, s: (seq_of(b), 0, 0)
        modf_map = lambda b, s: (lag_seq(b, s), 0, 0)
    else:
        mod_map = modf_map = lambda b, s: (0, 0, 0)

    in_specs = [
        pl.BlockSpec((1, CHUNK, D_MODEL), lambda b, s: (seq_of(b), chunk_of(s), 0)),
        pl.BlockSpec((1, HALO, D_MODEL),
                     lambda b, s: (seq_of(b), jnp.maximum(chunk_of(s) * hpc_blocks - 1, 0), 0)),
        pl.BlockSpec((1, HALO, D_MODEL),
                     lambda b, s: (seq_of(b), jnp.minimum((chunk_of(s) + 1) * hpc_blocks, nc * hpc_blocks - 1), 0)),
        pl.BlockSpec((1, 6, D_MODEL), mod_map),
        pl.BlockSpec((1, 6, D_MODEL), modf_map),
    ] + [_const_spec(w.shape) for w in weights]
    args = [x, x, x, mod, mod] + list(weights)
    ret_block = (1, 1, 2, RET_HEADS, RET_HD, RET_HD)
    ssd_block = (1, 1, 2, SSD_HEADS, SSD_HD, SSD_STATE)
    state_map = lambda b, s: (seq_of(b), 0, 0, 0, 0, 0)
    if has_state:
        in_specs += [pl.BlockSpec(ret_block, state_map, pipeline_mode=pl.Buffered(1)),
                     pl.BlockSpec(ssd_block, state_map, pipeline_mode=pl.Buffered(1))]
        args += list(states)
    out_shape = [jax.ShapeDtypeStruct((nb, L, D_MODEL), F32)]
    out_specs = [pl.BlockSpec((1, CHUNK, D_MODEL), out_map)]
    if emit_state:
        out_shape += [jax.ShapeDtypeStruct((nb,) + ret_block[1:], F32),
                      jax.ShapeDtypeStruct((nb,) + ssd_block[1:], F32)]
        out_specs += [pl.BlockSpec(ret_block, state_map), pl.BlockSpec(ssd_block, state_map)]
    state_dt = F32 if emit_state else BF16
    scratch = [
        pltpu.VMEM((nc, CHUNK, RET_W), BF16),
        pltpu.VMEM((nc, RET_W, CHUNK), BF16),
        pltpu.VMEM((nc, CHUNK, RET_W), BF16),
        pltpu.VMEM((nc, CHUNK, RET_W), BF16),
        pltpu.VMEM((nc, CHUNK, SSD_W), BF16),
        pltpu.VMEM((CHUNK + 2 * HALO, CONV_CH), F32),
        pltpu.VMEM((nc, CHUNK, SSD_W), F32),
        pltpu.VMEM((nc, SSD_GROUPS * SSD_STATE, CHUNK), BF16),
        pltpu.VMEM((nc, CHUNK, SSD_GROUPS * SSD_STATE), BF16),
        pltpu.VMEM((nc, CHUNK, LANES), F32),
        pltpu.VMEM((nc, CHUNK, LANES), F32),
        pltpu.VMEM((nc, CHUNK, LANES), F32),
        pltpu.VMEM((nc, 8, LANES), F32),
        pltpu.VMEM((nc, N_DT, CHUNK), F32),
        pltpu.VMEM((CHUNK, SSD_W), F32),
        pltpu.VMEM((CHUNK, 2 * RET_W), BF16),
        pltpu.VMEM((nc, RET_HEADS, RET_HD, 2 * RET_HD), state_dt),
        pltpu.VMEM((nc, SSD_GROUPS, SSD_STATE, 2 * GROUP_W), state_dt),
        pltpu.VMEM((nc, RET_HEADS, RET_HD, 2 * RET_HD), BF16),
        pltpu.VMEM((nc, SSD_GROUPS, SSD_STATE, 2 * GROUP_W), BF16),
        pltpu.VMEM((RET_HEADS, CHUNK, CHUNK), F32),
        pltpu.VMEM((RET_HEADS, 4, CHUNK, LANES), F32),
        pltpu.VMEM((2, CHUNK, D_MODEL), F32),
        pltpu.VMEM((CHUNK, D_FF), BF16),
        pltpu.VMEM((LANES, D_MODEL), BF16),
    ]
    if use_rope:
        scratch += [pltpu.VMEM((nc, CHUNK, LANES), F32)] * 2
    kern = functools.partial(_layer_kernel, L=L, nb=nb, has_state=has_state, use_rope=use_rope,
                             emit_state=emit_state)
    return pl.pallas_call(
        kern,
        grid=(nb + 1, 2 * nc),
        in_specs=in_specs,
        out_specs=out_specs,
        out_shape=out_shape,
        scratch_shapes=scratch,
        compiler_params=pltpu.CompilerParams(dimension_semantics=("arbitrary", "arbitrary"),
                                             vmem_limit_bytes=VMEM_LIMIT),
        name=name,
    )(*args)


def kernel(x_prompt, x_sample, state_ret, state_ssd, c, c_ctx, w_in, ret_decay_fwd, ret_decay_bwd, conv_w, conv_b, dt_bias_fwd, dt_bias_bwd, a_log_fwd, a_log_bwd, d_skip, ssd_norm_w, w_out, ln1_g, ln1_b, w_gate, w_up, w_down, ln2_g, ln2_b, w_ada, b_ada):
    depth = w_in.shape[0]
    assert depth == 1, "single trunk layer"
    bp, lp, _ = x_prompt.shape
    bs, ls, _ = x_sample.shape
    assert lp % CHUNK == 0 and ls % CHUNK == 0 and ls % GRID_W == 0 and D_FF % FF_BLK == 0

    rows = -(-(bs + 1) // 8) * 8
    cond = jnp.zeros((rows, D_MODEL), F32).at[:bs].set(c).at[bs].set(c_ctx)
    mod = _ada_call(cond, w_ada[0], b_ada[0][None, :]).reshape(rows, 6, D_MODEL)
    mod_lat = mod[:bs]
    mod_ctx = mod[bs:bs + 1]

    convw = jnp.zeros((8, CONV_CH), F32).at[:CONV_W].set(conv_w[0])
    convb = conv_b[0][None, :]
    dt_bias = jnp.concatenate([dt_bias_fwd[0], dt_bias_bwd[0]])
    a_log = jnp.concatenate([a_log_fwd[0], a_log_bwd[0]])
    ret_decay = jnp.concatenate([ret_decay_fwd[0], ret_decay_bwd[0]])
    hp = jnp.zeros((8, LANES), F32)
    hp = hp.at[0, :N_DT].set(dt_bias).at[1, :N_DT].set(a_log)
    hp = hp.at[2, :2 * RET_HEADS].set(ret_decay).at[3, :SSD_HEADS].set(d_skip[0])
    hpc = jnp.zeros((N_DT, LANES), F32).at[:, 0].set(dt_bias).at[:, 1].set(a_log)
    w_in_b, w_out_b, w_gate_b, w_up_b, w_down_b = _cast_call([w_in[0], w_out[0], w_gate[0], w_up[0], w_down[0]])
    weights = (w_in_b, convw, convb, hp, hpc, ssd_norm_w[0][None, :],
               w_out_b, ln1_g[0][None, :], ln1_b[0][None, :],
               w_gate_b, w_up_b, w_down_b, ln2_g[0][None, :], ln2_b[0][None, :])

    yp, new_ret, new_ssd = _layer_call(x_prompt, mod_ctx, False, weights, None, True, False, "layer_ctx")
    states = (state_ret, jnp.swapaxes(state_ssd, -1, -2))
    (ys,) = _layer_call(x_sample, mod_lat, True, weights, states, False, True, "layer_lat")
    return (yp, ys, new_ret, jnp.swapaxes(new_ssd, -1, -2))
```

```python
import functools
import math

import jax
import jax.numpy as jnp
from jax import lax
from jax.experimental import pallas as pl
from jax.experimental.pallas import tpu as pltpu

F32 = jnp.float32
BF16 = jnp.bfloat16

D_MODEL = 1024
RET_W = 512
RET_HEADS = 4
RET_HD = 128
SSD_W = 512
SSD_HD = 64
SSD_HEADS = 8
SSD_GROUPS = 2
SSD_STATE = 128
HPG = SSD_HEADS // SSD_GROUPS
GROUP_W = HPG * SSD_HD
CONV_W = 5
CONV_CH = SSD_W + 2 * SSD_GROUPS * SSD_STATE
D_FF = 2816
GRID_W = 64
GRID_SHIFT = 6
ROPE_BASE = 10000.0
EPS = 1e-6
ALPHA = 2.0 ** 0.25
MAIN_COLS = 4 * RET_W + SSD_W + CONV_CH
XBC_COL0 = 4 * RET_W + SSD_W
N_DT = 2 * SSD_HEADS

CHUNK = 256
HALO = 8
FF_BLK = 256
ADA_COLS = 2048
CAST_STEPS = 8
CAST_WT_ROWS = 512
LANES = 128
VMEM_LIMIT = 62 * 1024 * 1024


def _dot(a, b):
    return jnp.dot(a, b, preferred_element_type=F32)


def _dot_nt(a, b):
    return lax.dot_general(a, b, (((1,), (1,)), ((), ())), preferred_element_type=F32)


def _silu(x):
    return x * jax.nn.sigmoid(x)


def _softplus(x):
    return jnp.maximum(x, 0.0) + jnp.log1p(jnp.exp(-jnp.abs(x)))


def _layer_norm(y, g, b):
    mu = jnp.mean(y, axis=-1, keepdims=True)
    yc = y - mu
    var = jnp.mean(yc * yc, axis=-1, keepdims=True)
    return yc * lax.rsqrt(var + EPS) * g + b


def _cumsum(x, axis):
    n = x.shape[axis]
    idx = lax.broadcasted_iota(jnp.int32, x.shape, axis)
    s = 1
    while s < n:
        x = x + jnp.where(idx >= s, pltpu.roll(x, s, axis), 0.0)
        s *= 2
    return x


def _expand4(cols, lane):
    a = jnp.where(lane < SSD_HD, cols[0], cols[1])
    b = jnp.where(lane < SSD_HD, cols[2], cols[3])
    return jnp.concatenate([a, b], axis=1)


def _interleave(a, b):
    ia = ib = 0
    while ia < len(a) or ib < len(b):
        if ib >= len(b) or (ia < len(a) and ia * len(b) <= ib * len(a)):
            a[ia]()
            ia += 1
        else:
            b[ib]()
            ib += 1


def _ada_kernel(cond_ref, w_ref, b_ref, o_ref):
    s = _silu(cond_ref[...]).astype(BF16)
    o_ref[...] = _dot(s, w_ref[...].astype(BF16)) + b_ref[...]


def _ada_call(cond, w_ada, b_ada):
    rows = cond.shape[0]
    n = w_ada.shape[1]
    return pl.pallas_call(
        _ada_kernel,
        grid=(n // ADA_COLS,),
        in_specs=[
            pl.BlockSpec((rows, D_MODEL), lambda j: (0, 0)),
            pl.BlockSpec((D_MODEL, ADA_COLS), lambda j: (0, j)),
            pl.BlockSpec((1, ADA_COLS), lambda j: (0, j)),
        ],
        out_specs=pl.BlockSpec((rows, ADA_COLS), lambda j: (0, j)),
        out_shape=jax.ShapeDtypeStruct((rows, n), F32),
        compiler_params=pltpu.CompilerParams(dimension_semantics=("arbitrary",)),
        name="adaln_mod",
    )(cond, w_ada, b_ada)


def _cast_kernel(wt_ref, *refs, wt_rows):
    blk = wt_ref.shape[0]
    row = lax.broadcasted_iota(jnp.int32, (blk, 1), 0) + pl.program_id(0) * blk
    n = len(refs) // 2
    refs[n][...] = jnp.where(row < wt_rows, wt_ref[...], 0.0).T.astype(BF16)
    for src, dst in zip(refs[:n], refs[n + 1:]):
        dst[...] = src[...].astype(BF16)


def _cast_call(wt, ws):
    for w in ws:
        assert w.shape[0] % (CAST_STEPS * 16) == 0, w.shape
    assert wt.shape[0] <= CAST_STEPS * CAST_WT_ROWS
    specs = [pl.BlockSpec((w.shape[0] // CAST_STEPS, w.shape[1]), lambda i: (i, 0)) for w in ws]
    return pl.pallas_call(
        functools.partial(_cast_kernel, wt_rows=wt.shape[0]),
        grid=(CAST_STEPS,),
        in_specs=[pl.BlockSpec((CAST_WT_ROWS, wt.shape[1]), lambda i: (i, 0))] + specs,
        out_specs=[pl.BlockSpec((wt.shape[1], CAST_WT_ROWS), lambda i: (0, i))] + specs,
        out_shape=[jax.ShapeDtypeStruct((wt.shape[1], -(-wt.shape[0] // LANES) * LANES), BF16)]
        + [jax.ShapeDtypeStruct(w.shape, BF16) for w in ws],
        compiler_params=pltpu.CompilerParams(dimension_semantics=("arbitrary",),
                                             vmem_limit_bytes=VMEM_LIMIT),
        name="weights_to_bf16",
    )(wt, *ws)


RV_TAIL_F, RV_TAIL_B, RV_CROSS_F, RV_CROSS_B = range(4)


def _layer_kernel(*refs, L, nb, has_state, use_rope, emit_state):
    nc = L // CHUNK
    C = CHUNK
    cross = has_state or nc > 1
    it = iter(refs)
    x_ref, xprev_ref, xnext_ref, mod_ref, modf_ref, wmain_ref = (next(it) for _ in range(6))
    convw_ref, convb_ref, hp_ref, hpc_ref, normw_ref = (next(it) for _ in range(5))
    wout_ref, ln1g_ref, ln1b_ref = (next(it) for _ in range(3))
    wg_ref, wu_ref, wd_ref, ln2g_ref, ln2b_ref = (next(it) for _ in range(5))
    if has_state:
        sret0_ref, sssd0_ref = next(it), next(it)
    out_ref = next(it)
    if emit_state:
        nret_ref, nssd_ref = next(it), next(it)
    (q_s, kT_s, v_s, g_s, z_s, stage_s, xs_s, bT_s, c_s, inccol_s, exccol_s, cfcb_s, dec_s,
     rowarg_s, y_s, mix_s, rloc_s, sloc_s, rent_s, sent_s, wdec_s, rvec_s, x1_s, hid_s, wdtr_s) = (
         next(it) for _ in range(25))
    if use_rope:
        cos_s, sin_s = next(it), next(it)

    seq = pl.program_id(0)
    step = pl.program_id(1)

    hp = hp_ref[...]
    dt_bias_row = hp[0:1, :]
    nega_row = -jnp.exp(hp[1:2, :])
    lg_row = -_softplus(-hp[2:3, :])
    dskip_row = hp[3:4, :]
    hpc = hpc_ref[...]
    dt_bias_col = hpc[:, 0:1]
    nega_col = -jnp.exp(hpc[:, 1:2])

    lane = lax.broadcasted_iota(jnp.int32, (1, LANES), 1)

    def ret_decays(hd):
        lgf = lg_row[:, hd:hd + 1]
        lgb = lg_row[:, RET_HEADS + hd:RET_HEADS + hd + 1]
        return lgf, lgb

    def group_heads(gi):
        return [gi * HPG + k for k in range(HPG)]

    def expand_f(arr, gi):
        return _expand4([arr[:, hh:hh + 1] for hh in group_heads(gi)], lane)

    def expand_b(arr, gi):
        return _expand4([arr[:, SSD_HEADS + hh:SSD_HEADS + hh + 1] for hh in group_heads(gi)], lane)

    @pl.when((seq == 0) & (step == 0))
    def _():
        ii = lax.broadcasted_iota(jnp.int32, (C, C), 0)
        jj = lax.broadcasted_iota(jnp.int32, (C, C), 1)
        dmat = (ii - jj).astype(F32)
        irow = lax.broadcasted_iota(jnp.int32, (C, LANES), 0).astype(F32)
        for hd in range(RET_HEADS):
            lgf, lgb = ret_decays(hd)
            wdec_s[hd] = jnp.exp(jnp.where(jj <= ii, dmat * lgf, -dmat * lgb))
            rvec_s[hd, RV_TAIL_F] = jnp.exp((C - 1.0 - irow) * lgf)
            rvec_s[hd, RV_TAIL_B] = jnp.exp(irow * lgb)
            rvec_s[hd, RV_CROSS_F] = jnp.exp((irow + 1.0) * lgf)
            rvec_s[hd, RV_CROSS_B] = jnp.exp((C - irow) * lgb)
        wdtr_s[...] = wmain_ref[:, MAIN_COLS:MAIN_COLS + LANES].astype(F32).T.astype(BF16)
        if use_rope:
            ln = lax.broadcasted_iota(jnp.int32, (C, LANES), 1)
            nf = RET_HD // 4
            inv = jnp.exp((ln & (nf - 1)).astype(F32) * (-math.log(ROPE_BASE) / nf))
            for cc in range(nc):
                t = lax.broadcasted_iota(jnp.int32, (C, LANES), 0) + cc * C
                pos = jnp.where((ln & (2 * nf - 1)) < nf, t >> GRID_SHIFT, t & (GRID_W - 1)).astype(F32)
                ang = pos * inv
                cos_s[cc] = jnp.cos(ang)
                sin_s[cc] = jnp.where(ln < RET_HD // 2, -jnp.sin(ang), jnp.sin(ang))

    def project(c):
        st = {}
        CB = 256
        half = CONV_W // 2

        def v_mod():
            sh1 = mod_ref[0, 0:1, :]
            sc1 = mod_ref[0, 1:2, :]
            xe = jnp.concatenate([xprev_ref[0], x_ref[0], xnext_ref[0]], axis=0)
            xm = xe * (1.0 + sc1) + sh1
            st["he"] = xm.astype(BF16)
            st["h"] = xm[HALO:HALO + C].astype(BF16)

        def m_main(name, lo, hi, halo=False):
            def run():
                st[name] = _dot(st["he" if halo else "h"], wmain_ref[:, lo:hi])
            return run

        def m_dt():
            raw_c = _dot(st["h"], wmain_ref[:, MAIN_COLS:MAIN_COLS + LANES])
            st["dtc"] = raw_c + pltpu.roll(raw_c, SSD_HEADS, 1)
            raw_r = _dot_nt(wdtr_s[0:N_DT, :], st["h"])
            st["dtr"] = raw_r + pltpu.roll(raw_r, SSD_HEADS, 0)

        def v_dt():
            dt_c = _softplus(st["dtc"] + dt_bias_row)
            lac = dt_c * nega_row
            inc_col = _cumsum(lac, 0)
            exc_col = inc_col - lac
            tot_col = inc_col[C - 1:C, :]
            inccol_s[c] = inc_col
            exccol_s[c] = exc_col
            dec_s[c] = jnp.broadcast_to(jnp.exp(tot_col), (8, LANES))
            if cross:
                cfcb_s[c] = jnp.exp(jnp.where(lane < SSD_HEADS, inc_col, tot_col - exc_col))
            st["sf"] = jnp.exp(tot_col - inc_col) * dt_c
            st["sb"] = jnp.exp(exc_col) * dt_c
            dt_r = _softplus(st["dtr"] + dt_bias_col)
            lar = dt_r * nega_col
            inc_row = _cumsum(lar, 1)
            ldt = jnp.log(dt_r)
            rid = lax.broadcasted_iota(jnp.int32, (N_DT, C), 0)
            rowarg_s[c] = jnp.where(rid < SSD_HEADS, inc_row - ldt, inc_row - lar + ldt)

        def v_stage(hf):
            def run():
                pe = st["pe%d" % hf]
                cs = slice(hf * 512, (hf + 1) * 512)
                stage_s[0:HALO, cs] = jnp.where(c > 0, pe[0:HALO], 0.0)
                stage_s[HALO:HALO + C, cs] = pe[HALO:HALO + C]
                stage_s[HALO + C:, cs] = jnp.where(c < nc - 1, pe[HALO + C:], 0.0)
            return run

        def conv_block(cb):
            cs = slice(cb * CB, (cb + 1) * CB)
            acc = jnp.broadcast_to(convb_ref[0:1, cs], (C, CB))
            for k in range(CONV_W):
                o = HALO + k - half
                acc = acc + convw_ref[k:k + 1, cs] * stage_s[o:o + C, cs]
            return _silu(acc)

        def v_conv_x(cb):
            def run():
                xs_s[c, :, cb * CB:(cb + 1) * CB] = conv_block(cb)
            return run

        def v_conv_b():
            bT_s[c] = conv_block(SSD_W // CB).T.astype(BF16)

        def v_conv_c():
            c_s[c] = conv_block(SSD_W // CB + 1).astype(BF16)

        def rope(a):
            if not use_rope:
                return a
            return a * cos_s[c] + pltpu.roll(a, RET_HD // 2, 1) * sin_s[c]

        def v_q():
            for hd in range(RET_HEADS):
                sl = slice(hd * RET_HD, (hd + 1) * RET_HD)
                q_s[c, :, sl] = rope(st["pq"][:, sl]).astype(BF16)

        def v_k():
            for hd in range(RET_HEADS):
                sl = slice(hd * RET_HD, (hd + 1) * RET_HD)
                kh = rope(st["pk"][:, sl]) * (RET_HD ** -0.5)
                kT_s[c, sl, :] = kh.T.astype(BF16)

        def v_v():
            v_s[c] = st["pv"].astype(BF16)

        def v_g():
            g_s[c] = _silu(st["pg"]).astype(BF16)

        def v_z():
            z_s[c] = _silu(st["pz"]).astype(BF16)

        def m_sloc(gi):
            def run():
                xg = xs_s[c, :, gi * GROUP_W:(gi + 1) * GROUP_W]
                vcat = jnp.concatenate([xg * expand_f(st["sf"], gi), xg * expand_b(st["sb"], gi)],
                                       axis=1).astype(BF16)
                sloc_s[c, gi] = _dot(bT_s[c, gi * SSD_STATE:(gi + 1) * SSD_STATE, :], vcat).astype(sloc_s.dtype)
            return run

        def m_rloc(hd):
            def run():
                sl = slice(hd * RET_HD, (hd + 1) * RET_HD)
                vf = st["pv"][:, sl]
                vcat = jnp.concatenate([vf * rvec_s[hd, RV_TAIL_F], vf * rvec_s[hd, RV_TAIL_B]],
                                       axis=1).astype(BF16)
                rloc_s[c, hd] = _dot(kT_s[c, sl, :], vcat).astype(rloc_s.dtype)
            return run

        m_pe0 = m_main("pe0", XBC_COL0, XBC_COL0 + 512, halo=True)
        m_pe1 = m_main("pe1", XBC_COL0 + 512, MAIN_COLS, halo=True)
        m_q = m_main("pq", 0, RET_W)
        m_k = m_main("pk", RET_W, 2 * RET_W)
        m_v = m_main("pv", 2 * RET_W, 3 * RET_W)
        m_g = m_main("pg", 3 * RET_W, 4 * RET_W)
        m_z = m_main("pz", 4 * RET_W, XBC_COL0)
        order = [v_mod, m_dt, m_pe0, m_pe1, v_dt, m_q, v_stage(0), v_stage(1), m_k, v_conv_x(0), m_v,
                 v_conv_x(1), v_q, m_g, v_conv_b, v_k, m_z, v_conv_c, v_v,
                 m_rloc(0), m_rloc(1), v_g, m_rloc(2), m_rloc(3), m_sloc(0), m_sloc(1), v_z]
        for piece in order:
            piece()

    def recurrences():
        for hd in range(RET_HEADS):
            lgf, lgb = ret_decays(hd)
            dec_f = jnp.exp(C * lgf)
            dec_b = jnp.exp(C * lgb)
            if has_state:
                ent_f = sret0_ref[0, 0, 0, hd]
                ent_b = sret0_ref[0, 0, 1, hd]
            else:
                ent_f = jnp.zeros((RET_HD, RET_HD), F32)
                ent_b = jnp.zeros((RET_HD, RET_HD), F32)
            for c in range(nc):
                if cross:
                    rent_s[c, hd, :, 0:RET_HD] = ent_f.astype(BF16)
                ent_f = dec_f * ent_f + rloc_s[c, hd, :, 0:RET_HD]
            for c in range(nc - 1, -1, -1):
                if cross:
                    rent_s[c, hd, :, RET_HD:] = ent_b.astype(BF16)
                ent_b = dec_b * ent_b + rloc_s[c, hd, :, RET_HD:]
            if emit_state:
                nret_ref[0, 0, 0, hd] = ent_f
                nret_ref[0, 0, 1, hd] = ent_b

        for gi in range(SSD_GROUPS):
            heads = group_heads(gi)
            if has_state:
                ent_f = jnp.concatenate([sssd0_ref[0, 0, 0, hh] for hh in heads], axis=0).T
                ent_b = jnp.concatenate([sssd0_ref[0, 0, 1, hh] for hh in heads], axis=0).T
            else:
                ent_f = jnp.zeros((SSD_STATE, GROUP_W), F32)
                ent_b = jnp.zeros((SSD_STATE, GROUP_W), F32)
            for c in range(nc):
                if cross:
                    sent_s[c, gi, :, 0:GROUP_W] = ent_f.astype(BF16)
                ent_f = expand_f(dec_s[c, 0:1, :], gi) * ent_f + sloc_s[c, gi, :, 0:GROUP_W]
            for c in range(nc - 1, -1, -1):
                if cross:
                    sent_s[c, gi, :, GROUP_W:] = ent_b.astype(BF16)
                ent_b = expand_b(dec_s[c, 0:1, :], gi) * ent_b + sloc_s[c, gi, :, GROUP_W:]
            if emit_state:
                ent_ft = ent_f.T
                ent_bt = ent_b.T
                for k, hh in enumerate(heads):
                    nssd_ref[0, 0, 0, hh] = ent_ft[k * SSD_HD:(k + 1) * SSD_HD, :]
                    nssd_ref[0, 0, 1, hh] = ent_bt[k * SSD_HD:(k + 1) * SSD_HD, :]

    def emit_pieces(c, slot):
        rsl = [slice(hd * RET_HD, (hd + 1) * RET_HD) for hd in range(RET_HEADS)]
        gsl = [slice(gi * SSD_STATE, (gi + 1) * SSD_STATE) for gi in range(SSD_GROUPS)]
        xsl = [slice(gi * GROUP_W, (gi + 1) * GROUP_W) for gi in range(SSD_GROUPS)]
        st = {}

        def scores():
            ii = lax.broadcasted_iota(jnp.int32, (C, C), 0)
            jj = lax.broadcasted_iota(jnp.int32, (C, C), 1)
            st["causal"] = jj <= ii
            qs = [q_s[c, :, sl] for sl in rsl]
            cms = [c_s[c, :, sl] for sl in gsl]
            st["sc_s"] = [_dot(cms[gi], bT_s[c, gsl[gi], :]) for gi in range(SSD_GROUPS)]
            st["sc_r"] = [_dot(qs[hd], kT_s[c, rsl[hd], :]) for hd in range(RET_HEADS)]
            if cross:
                st["yc_s"] = [_dot(cms[gi], sent_s[c, gi]) for gi in range(SSD_GROUPS)]
                st["yc_r"] = [_dot(qs[hd], rent_s[c, hd]) for hd in range(RET_HEADS)]
            st["inc_col"] = inccol_s[c]
            st["exc_col"] = exccol_s[c]
            st["rowarg"] = rowarg_s[c]

        def ssd_head(gi, k):
            def run():
                hh = gi * HPG + k
                hb = SSD_HEADS + hh
                arg = jnp.where(st["causal"],
                                st["inc_col"][:, hh:hh + 1] - st["rowarg"][hh:hh + 1, :],
                                st["rowarg"][hb:hb + 1, :] - st["exc_col"][:, hb:hb + 1])
                m = (st["sc_s"][gi] * jnp.exp(arg)).astype(BF16)
                xh = xs_s[c, :, hh * SSD_HD:(hh + 1) * SSD_HD].astype(BF16)
                y_s[:, hh * SSD_HD:(hh + 1) * SSD_HD] = _dot(m, xh)
            return run

        def ssd_group(gi):
            def run():
                xg = xs_s[c, :, xsl[gi]]
                yg = y_s[:, xsl[gi]] + expand_f(dskip_row, gi) * xg
                if cross:
                    cfcb = cfcb_s[c]
                    yc = st["yc_s"][gi]
                    yg = yg + expand_f(cfcb, gi) * yc[:, 0:GROUP_W] + expand_b(cfcb, gi) * yc[:, GROUP_W:]
                y_s[:, xsl[gi]] = yg
            return run

        def ret_head(hd):
            def run():
                m = (st["sc_r"][hd] * wdec_s[hd]).astype(BF16)
                o = _dot(m, v_s[c, :, rsl[hd]])
                if cross:
                    yc = st["yc_r"][hd]
                    o = o + rvec_s[hd, RV_CROSS_F] * yc[:, 0:RET_HD] + rvec_s[hd, RV_CROSS_B] * yc[:, RET_HD:]
                o = o * lax.rsqrt(jnp.mean(o * o, axis=-1, keepdims=True) + EPS)
                mix_s[:, rsl[hd]] = (g_s[c, :, rsl[hd]].astype(F32) * o).astype(BF16)
            return run

        def ssd_norm():
            yz = y_s[...] * z_s[c].astype(F32)
            yn = yz * lax.rsqrt(jnp.mean(yz * yz, axis=-1, keepdims=True) + EPS) * normw_ref[...]
            mix_s[:, RET_W:] = yn.astype(BF16)

        def out_proj():
            g1 = mod_ref[0, 2:3, :]
            y = ALPHA * x_ref[0] + g1 * _dot(mix_s[...], wout_ref[...])
            x1_s[slot] = _layer_norm(y, ln1g_ref[...], ln1b_ref[...])

        pieces = [scores]
        for gi in range(SSD_GROUPS):
            pieces += [ssd_head(gi, k) for k in range(HPG)] + [ssd_group(gi)]
        pieces += [ret_head(hd) for hd in range(RET_HEADS)] + [ssd_norm, out_proj]
        return pieces

    def ffn_pieces(slot):
        st = {}

        def start():
            sh2 = modf_ref[0, 3:4, :]
            sc2 = modf_ref[0, 4:5, :]
            st["h2"] = (x1_s[slot] * (1.0 + sc2) + sh2).astype(BF16)

        def hidden(j):
            def run():
                js = slice(j * FF_BLK, (j + 1) * FF_BLK)
                h2 = st["h2"]
                hid_s[:, js] = (_silu(_dot(h2, wg_ref[:, js])) * _dot(h2, wu_ref[:, js])).astype(BF16)
            return run

        def finish():
            g2 = modf_ref[0, 5:6, :]
            y = ALPHA * x1_s[slot] + g2 * _dot(hid_s[...], wd_ref[...])
            out_ref[0] = _layer_norm(y, ln2g_ref[...], ln2b_ref[...])

        return [start] + [hidden(j) for j in range(D_FF // FF_BLK)] + [finish]

    def run_all(pieces):
        for p in pieces:
            p()

    @pl.when((step < nc) & (seq < nb))
    def _():
        project(step)

    @pl.when((step == nc) & (seq < nb))
    def _():
        recurrences()

    kk = step - nc
    slot = (seq * nc + kk) & 1
    first = (seq == 0) & (step == nc)

    @pl.when(first)
    def _():
        run_all(emit_pieces(kk, slot))

    @pl.when((step >= nc) & (seq < nb) & jnp.logical_not(first))
    def _():
        _interleave(ffn_pieces(1 - slot), emit_pieces(kk, slot))

    @pl.when((seq == nb) & (step == nc))
    def _():
        run_all(ffn_pieces((nb * nc - 1) & 1))


def _const_spec(shape):
    nd = len(shape)
    return pl.BlockSpec(shape, lambda b, s: (0,) * nd, pipeline_mode=pl.Buffered(1))


def _layer_call(x, mod, mod_per_seq, weights, states, emit_state, use_rope, name):
    nb, L, _ = x.shape
    nc = L // CHUNK
    hpc_blocks = CHUNK // HALO
    has_state = states is not None
    last = nb - 1

    def chunk_of(b, s):
        return jnp.where(b > last, nc - 1, jnp.where(s < nc, s, s - nc))

    def seq_of(b):
        return jnp.minimum(b, last)

    def lag_seq(b, s):
        return jnp.minimum(jnp.where(s > nc, b, jnp.maximum(b - 1, 0)), last)

    def out_map(b, s):
        live = (s > nc) & (b < nb)
        idle_chunk = jnp.where(b == 0, 0, nc - 1)
        return (jnp.where(live, b, jnp.maximum(b - 1, 0)), jnp.where(live, s - nc - 1, idle_chunk), 0)

    if mod_per_seq:
        mod_map = lambda b, s: (seq_of(b), 0, 0)
        modf_map = lambda b, s: (lag_seq(b, s), 0, 0)
    else:
        mod_map = modf_map = lambda b, s: (0, 0, 0)

    in_specs = [
        pl.BlockSpec((1, CHUNK, D_MODEL), lambda b, s: (seq_of(b), chunk_of(b, s), 0)),
        pl.BlockSpec((1, HALO, D_MODEL),
                     lambda b, s: (seq_of(b), jnp.maximum(chunk_of(b, s) * hpc_blocks - 1, 0), 0)),
        pl.BlockSpec((1, HALO, D_MODEL),
                     lambda b, s: (seq_of(b), jnp.minimum((chunk_of(b, s) + 1) * hpc_blocks, nc * hpc_blocks - 1), 0)),
        pl.BlockSpec((1, 6, D_MODEL), mod_map),
        pl.BlockSpec((1, 6, D_MODEL), modf_map),
    ] + [_const_spec(w.shape) for w in weights]
    args = [x, x, x, mod, mod] + list(weights)
    ret_block = (1, 1, 2, RET_HEADS, RET_HD, RET_HD)
    ssd_block = (1, 1, 2, SSD_HEADS, SSD_HD, SSD_STATE)
    state_map = lambda b, s: (seq_of(b), 0, 0, 0, 0, 0)
    if has_state:
        in_specs += [pl.BlockSpec(ret_block, state_map, pipeline_mode=pl.Buffered(1)),
                     pl.BlockSpec(ssd_block, state_map, pipeline_mode=pl.Buffered(1))]
        args += list(states)
    out_shape = [jax.ShapeDtypeStruct((nb, L, D_MODEL), F32)]
    out_specs = [pl.BlockSpec((1, CHUNK, D_MODEL), out_map)]
    if emit_state:
        out_shape += [jax.ShapeDtypeStruct((nb,) + ret_block[1:], F32),
                      jax.ShapeDtypeStruct((nb,) + ssd_block[1:], F32)]
        out_specs += [pl.BlockSpec(ret_block, state_map), pl.BlockSpec(ssd_block, state_map)]
    state_dt = F32 if emit_state else BF16
    scratch = [
        pltpu.VMEM((nc, CHUNK, RET_W), BF16),
        pltpu.VMEM((nc, RET_W, CHUNK), BF16),
        pltpu.VMEM((nc, CHUNK, RET_W), BF16),
        pltpu.VMEM((nc, CHUNK, RET_W), BF16),
        pltpu.VMEM((nc, CHUNK, SSD_W), BF16),
        pltpu.VMEM((CHUNK + 2 * HALO, CONV_CH), F32),
        pltpu.VMEM((nc, CHUNK, SSD_W), F32),
        pltpu.VMEM((nc, SSD_GROUPS * SSD_STATE, CHUNK), BF16),
        pltpu.VMEM((nc, CHUNK, SSD_GROUPS * SSD_STATE), BF16),
        pltpu.VMEM((nc, CHUNK, LANES), F32),
        pltpu.VMEM((nc, CHUNK, LANES), F32),
        pltpu.VMEM((nc, CHUNK, LANES), F32),
        pltpu.VMEM((nc, 8, LANES), F32),
        pltpu.VMEM((nc, N_DT, CHUNK), F32),
        pltpu.VMEM((CHUNK, SSD_W), F32),
        pltpu.VMEM((CHUNK, 2 * RET_W), BF16),
        pltpu.VMEM((nc, RET_HEADS, RET_HD, 2 * RET_HD), state_dt),
        pltpu.VMEM((nc, SSD_GROUPS, SSD_STATE, 2 * GROUP_W), state_dt),
        pltpu.VMEM((nc, RET_HEADS, RET_HD, 2 * RET_HD), BF16),
        pltpu.VMEM((nc, SSD_GROUPS, SSD_STATE, 2 * GROUP_W), BF16),
        pltpu.VMEM((RET_HEADS, CHUNK, CHUNK), F32),
        pltpu.VMEM((RET_HEADS, 4, CHUNK, LANES), F32),
        pltpu.VMEM((2, CHUNK, D_MODEL), F32),
        pltpu.VMEM((CHUNK, D_FF), BF16),
        pltpu.VMEM((LANES, D_MODEL), BF16),
    ]
    if use_rope:
        scratch += [pltpu.VMEM((nc, CHUNK, LANES), F32)] * 2
    kern = functools.partial(_layer_kernel, L=L, nb=nb, has_state=has_state, use_rope=use_rope,
                             emit_state=emit_state)
    return pl.pallas_call(
        kern,
        grid=(nb + 1, 2 * nc),
        in_specs=in_specs,
        out_specs=out_specs,
        out_shape=out_shape,
        scratch_shapes=scratch,
        compiler_params=pltpu.CompilerParams(dimension_semantics=("arbitrary", "arbitrary"),
                                             vmem_limit_bytes=VMEM_LIMIT),
        name=name,
    )(*args)


def kernel(x_prompt, x_sample, state_ret, state_ssd, c, c_ctx, w_in, ret_decay_fwd, ret_decay_bwd, conv_w, conv_b, dt_bias_fwd, dt_bias_bwd, a_log_fwd, a_log_bwd, d_skip, ssd_norm_w, w_out, ln1_g, ln1_b, w_gate, w_up, w_down, ln2_g, ln2_b, w_ada, b_ada):
    depth = w_in.shape[0]
    assert depth == 1, "single trunk layer"
    bp, lp, _ = x_prompt.shape
    bs, ls, _ = x_sample.shape
    assert lp % CHUNK == 0 and ls % CHUNK == 0 and ls % GRID_W == 0 and D_FF % FF_BLK == 0

    rows = -(-(bs + 1) // 8) * 8
    cond = jnp.zeros((rows, D_MODEL), F32).at[:bs].set(c).at[bs].set(c_ctx)
    mod = _ada_call(cond, w_ada[0], b_ada[0][None, :]).reshape(rows, 6, D_MODEL)
    mod_lat = mod[:bs]
    mod_ctx = mod[bs:bs + 1]

    convw = jnp.zeros((8, CONV_CH), F32).at[:CONV_W].set(conv_w[0])
    convb = conv_b[0][None, :]
    dt_bias = jnp.concatenate([dt_bias_fwd[0], dt_bias_bwd[0]])
    a_log = jnp.concatenate([a_log_fwd[0], a_log_bwd[0]])
    ret_decay = jnp.concatenate([ret_decay_fwd[0], ret_decay_bwd[0]])
    hp = jnp.zeros((8, LANES), F32)
    hp = hp.at[0, :N_DT].set(dt_bias).at[1, :N_DT].set(a_log)
    hp = hp.at[2, :2 * RET_HEADS].set(ret_decay).at[3, :SSD_HEADS].set(d_skip[0])
    hpc = jnp.zeros((N_DT, LANES), F32).at[:, 0].set(dt_bias).at[:, 1].set(a_log)
    w_in_b, w_out_b, w_gate_b, w_up_b, w_down_b = _cast_call(
        jnp.swapaxes(w_in[0], 0, 1), [w_out[0], w_gate[0], w_up[0], w_down[0]])
    weights = (w_in_b, convw, convb, hp, hpc, ssd_norm_w[0][None, :],
               w_out_b, ln1_g[0][None, :], ln1_b[0][None, :],
               w_gate_b, w_up_b, w_down_b, ln2_g[0][None, :], ln2_b[0][None, :])

    yp, new_ret, new_ssd = _layer_call(x_prompt, mod_ctx, False, weights, None, True, False, "layer_ctx")
    states = (state_ret, jnp.swapaxes(state_ssd, -1, -2))
    (ys,) = _layer_call(x_sample, mod_lat, True, weights, states, False, True, "layer_lat")
    return (yp, ys, new_ret, jnp.swapaxes(new_ssd, -1, -2))
```

```python
import functools
import math

import jax
import jax.numpy as jnp
from jax import lax
from jax.experimental import pallas as pl
from jax.experimental.pallas import tpu as pltpu

F32 = jnp.float32
BF16 = jnp.bfloat16

D_MODEL = 1024
RET_W = 512
RET_HEADS = 4
RET_HD = 128
SSD_W = 512
SSD_HD = 64
SSD_HEADS = 8
SSD_GROUPS = 2
SSD_STATE = 128
HPG = SSD_HEADS // SSD_GROUPS
GROUP_W = HPG * SSD_HD
CONV_W = 5
CONV_CH = SSD_W + 2 * SSD_GROUPS * SSD_STATE
D_FF = 2816
GRID_W = 64
GRID_SHIFT = 6
ROPE_BASE = 10000.0
EPS = 1e-6
ALPHA = 2.0 ** 0.25
MAIN_COLS = 4 * RET_W + SSD_W + CONV_CH
XBC_COL0 = 4 * RET_W + SSD_W
N_DT = 2 * SSD_HEADS

CHUNK = 256
HALO = 8
FF_BLK = 256
ADA_COLS = 2048
CAST_STEPS = 8
CAST_WT_ROWS = 512
LANES = 128
VMEM_LIMIT = 62 * 1024 * 1024


def _dot(a, b):
    return jnp.dot(a, b, preferred_element_type=F32)


def _dot_nt(a, b):
    return lax.dot_general(a, b, (((1,), (1,)), ((), ())), preferred_element_type=F32)


def _silu(x):
    return x * jax.nn.sigmoid(x)


def _softplus(x):
    return jnp.maximum(x, 0.0) + jnp.log1p(jnp.exp(-jnp.abs(x)))


def _layer_norm(y, g, b):
    mu = jnp.mean(y, axis=-1, keepdims=True)
    yc = y - mu
    var = jnp.mean(yc * yc, axis=-1, keepdims=True)
    return yc * lax.rsqrt(var + EPS) * g + b


def _cumsum(x, axis):
    n = x.shape[axis]
    idx = lax.broadcasted_iota(jnp.int32, x.shape, axis)
    s = 1
    while s < n:
        x = x + jnp.where(idx >= s, pltpu.roll(x, s, axis), 0.0)
        s *= 2
    return x


def _expand4(cols, lane):
    a = jnp.where(lane < SSD_HD, cols[0], cols[1])
    b = jnp.where(lane < SSD_HD, cols[2], cols[3])
    return jnp.concatenate([a, b], axis=1)


def _interleave(a, b):
    ia = ib = 0
    while ia < len(a) or ib < len(b):
        if ib >= len(b) or (ia < len(a) and ia * len(b) <= ib * len(a)):
            a[ia]()
            ia += 1
        else:
            b[ib]()
            ib += 1


def _ada_kernel(cond_ref, w_ref, b_ref, o_ref):
    s = _silu(cond_ref[...]).astype(BF16)
    o_ref[...] = _dot(s, w_ref[...].astype(BF16)) + b_ref[...]


def _ada_call(cond, w_ada, b_ada):
    rows = cond.shape[0]
    n = w_ada.shape[1]
    return pl.pallas_call(
        _ada_kernel,
        grid=(n // ADA_COLS,),
        in_specs=[
            pl.BlockSpec((rows, D_MODEL), lambda j: (0, 0)),
            pl.BlockSpec((D_MODEL, ADA_COLS), lambda j: (0, j)),
            pl.BlockSpec((1, ADA_COLS), lambda j: (0, j)),
        ],
        out_specs=pl.BlockSpec((rows, ADA_COLS), lambda j: (0, j)),
        out_shape=jax.ShapeDtypeStruct((rows, n), F32),
        compiler_params=pltpu.CompilerParams(dimension_semantics=("arbitrary",)),
        name="adaln_mod",
    )(cond, w_ada, b_ada)


def _cast_kernel(wt_ref, *refs, wt_rows):
    blk = wt_ref.shape[0]
    row = lax.broadcasted_iota(jnp.int32, (blk, 1), 0) + pl.program_id(0) * blk
    n = len(refs) // 2
    refs[n][...] = jnp.where(row < wt_rows, wt_ref[...], 0.0).T.astype(BF16)
    for src, dst in zip(refs[:n], refs[n + 1:]):
        dst[...] = src[...].astype(BF16)


def _cast_call(wt, ws):
    for w in ws:
        assert w.shape[0] % (CAST_STEPS * 16) == 0, w.shape
    assert wt.shape[0] <= CAST_STEPS * CAST_WT_ROWS
    specs = [pl.BlockSpec((w.shape[0] // CAST_STEPS, w.shape[1]), lambda i: (i, 0)) for w in ws]
    return pl.pallas_call(
        functools.partial(_cast_kernel, wt_rows=wt.shape[0]),
        grid=(CAST_STEPS,),
        in_specs=[pl.BlockSpec((CAST_WT_ROWS, wt.shape[1]), lambda i: (i, 0))] + specs,
        out_specs=[pl.BlockSpec((wt.shape[1], CAST_WT_ROWS), lambda i: (0, i))] + specs,
        out_shape=[jax.ShapeDtypeStruct((wt.shape[1], -(-wt.shape[0] // LANES) * LANES), BF16)]
        + [jax.ShapeDtypeStruct(w.shape, BF16) for w in ws],
        compiler_params=pltpu.CompilerParams(dimension_semantics=("arbitrary",),
                                             vmem_limit_bytes=VMEM_LIMIT),
        name="weights_to_bf16",
    )(wt, *ws)


RV_TAIL_F, RV_TAIL_B, RV_CROSS_F, RV_CROSS_B = range(4)
BIG_WEIGHTS = (0, 6, 9, 10, 11)


def _layer_kernel(*refs, L, nb, has_state, use_rope, emit_state):
    nc = L // CHUNK
    C = CHUNK
    cross = has_state or nc > 1
    it = iter(refs)
    x_ref, xprev_ref, xnext_ref, mod_ref, modf_ref, wmain_hbm = (next(it) for _ in range(6))
    convw_ref, convb_ref, hp_ref, hpc_ref, normw_ref = (next(it) for _ in range(5))
    wout_hbm, ln1g_ref, ln1b_ref = (next(it) for _ in range(3))
    wg_hbm, wu_hbm, wd_hbm, ln2g_ref, ln2b_ref = (next(it) for _ in range(5))
    if has_state:
        sret0_ref, sssd0_ref = next(it), next(it)
    out_ref = next(it)
    if emit_state:
        nret_ref, nssd_ref = next(it), next(it)
    (q_s, kT_s, v_s, g_s, z_s, stage_s, xs_s, bT_s, c_s, inccol_s, exccol_s, cfcb_s, dec_s,
     rowarg_s, y_s, mix_s, rloc_s, sloc_s, rent_s, sent_s, wdec_s, rvec_s, x1_s, hid_s, wdtr_s) = (
         next(it) for _ in range(25))
    wmain_ref, wout_ref, wg_ref, wu_ref, wd_ref, wsem = (next(it) for _ in range(6))
    if use_rope:
        cos_s, sin_s = next(it), next(it)

    seq = pl.program_id(0)
    step = pl.program_id(1)

    big_weights = ((wmain_hbm, wmain_ref), (wout_hbm, wout_ref), (wg_hbm, wg_ref), (wu_hbm, wu_ref),
                   (wd_hbm, wd_ref))

    def weight_copy(i):
        return pltpu.make_async_copy(big_weights[i][0], big_weights[i][1], wsem.at[i])

    hp = hp_ref[...]
    dt_bias_row = hp[0:1, :]
    nega_row = -jnp.exp(hp[1:2, :])
    lg_row = -_softplus(-hp[2:3, :])
    dskip_row = hp[3:4, :]
    hpc = hpc_ref[...]
    dt_bias_col = hpc[:, 0:1]
    nega_col = -jnp.exp(hpc[:, 1:2])

    lane = lax.broadcasted_iota(jnp.int32, (1, LANES), 1)

    def ret_decays(hd):
        lgf = lg_row[:, hd:hd + 1]
        lgb = lg_row[:, RET_HEADS + hd:RET_HEADS + hd + 1]
        return lgf, lgb

    def group_heads(gi):
        return [gi * HPG + k for k in range(HPG)]

    def expand_f(arr, gi):
        return _expand4([arr[:, hh:hh + 1] for hh in group_heads(gi)], lane)

    def expand_b(arr, gi):
        return _expand4([arr[:, SSD_HEADS + hh:SSD_HEADS + hh + 1] for hh in group_heads(gi)], lane)

    @pl.when((seq == 0) & (step == 0))
    def _():
        for i in range(len(big_weights)):
            weight_copy(i).start()
        weight_copy(0).wait()
        ii = lax.broadcasted_iota(jnp.int32, (C, C), 0)
        jj = lax.broadcasted_iota(jnp.int32, (C, C), 1)
        dmat = (ii - jj).astype(F32)
        irow = lax.broadcasted_iota(jnp.int32, (C, LANES), 0).astype(F32)
        for hd in range(RET_HEADS):
            lgf, lgb = ret_decays(hd)
            wdec_s[hd] = jnp.exp(jnp.where(jj <= ii, dmat * lgf, -dmat * lgb))
            rvec_s[hd, RV_TAIL_F] = jnp.exp((C - 1.0 - irow) * lgf)
            rvec_s[hd, RV_TAIL_B] = jnp.exp(irow * lgb)
            rvec_s[hd, RV_CROSS_F] = jnp.exp((irow + 1.0) * lgf)
            rvec_s[hd, RV_CROSS_B] = jnp.exp((C - irow) * lgb)
        wdtr_s[...] = wmain_ref[:, MAIN_COLS:MAIN_COLS + LANES].astype(F32).T.astype(BF16)
        if use_rope:
            ln = lax.broadcasted_iota(jnp.int32, (C, LANES), 1)
            nf = RET_HD // 4
            inv = jnp.exp((ln & (nf - 1)).astype(F32) * (-math.log(ROPE_BASE) / nf))
            for cc in range(nc):
                t = lax.broadcasted_iota(jnp.int32, (C, LANES), 0) + cc * C
                pos = jnp.where((ln & (2 * nf - 1)) < nf, t >> GRID_SHIFT, t & (GRID_W - 1)).astype(F32)
                ang = pos * inv
                cos_s[cc] = jnp.cos(ang)
                sin_s[cc] = jnp.where(ln < RET_HD // 2, -jnp.sin(ang), jnp.sin(ang))

    def project(c):
        st = {}
        CB = 256
        half = CONV_W // 2

        def v_mod():
            sh1 = mod_ref[0, 0:1, :]
            sc1 = mod_ref[0, 1:2, :]
            xe = jnp.concatenate([xprev_ref[0], x_ref[0], xnext_ref[0]], axis=0)
            xm = xe * (1.0 + sc1) + sh1
            st["he"] = xm.astype(BF16)
            st["h"] = xm[HALO:HALO + C].astype(BF16)

        def m_main(name, lo, hi, halo=False):
            def run():
                st[name] = _dot(st["he" if halo else "h"], wmain_ref[:, lo:hi])
            return run

        def m_dt():
            raw_c = _dot(st["h"], wmain_ref[:, MAIN_COLS:MAIN_COLS + LANES])
            st["dtc"] = raw_c + pltpu.roll(raw_c, SSD_HEADS, 1)
            raw_r = _dot_nt(wdtr_s[0:N_DT, :], st["h"])
            st["dtr"] = raw_r + pltpu.roll(raw_r, SSD_HEADS, 0)

        def v_dt():
            dt_c = _softplus(st["dtc"] + dt_bias_row)
            lac = dt_c * nega_row
            inc_col = _cumsum(lac, 0)
            exc_col = inc_col - lac
            tot_col = inc_col[C - 1:C, :]
            inccol_s[c] = inc_col
            exccol_s[c] = exc_col
            dec_s[c] = jnp.broadcast_to(jnp.exp(tot_col), (8, LANES))
            if cross:
                cfcb_s[c] = jnp.exp(jnp.where(lane < SSD_HEADS, inc_col, tot_col - exc_col))
            st["sf"] = jnp.exp(tot_col - inc_col) * dt_c
            st["sb"] = jnp.exp(exc_col) * dt_c
            dt_r = _softplus(st["dtr"] + dt_bias_col)
            lar = dt_r * nega_col
            inc_row = _cumsum(lar, 1)
            ldt = jnp.log(dt_r)
            rid = lax.broadcasted_iota(jnp.int32, (N_DT, C), 0)
            rowarg_s[c] = jnp.where(rid < SSD_HEADS, inc_row - ldt, inc_row - lar + ldt)

        def v_stage(hf):
            def run():
                pe = st["pe%d" % hf]
                cs = slice(hf * 512, (hf + 1) * 512)
                stage_s[0:HALO, cs] = jnp.where(c > 0, pe[0:HALO], 0.0)
                stage_s[HALO:HALO + C, cs] = pe[HALO:HALO + C]
                stage_s[HALO + C:, cs] = jnp.where(c < nc - 1, pe[HALO + C:], 0.0)
            return run

        def conv_block(cb):
            cs = slice(cb * CB, (cb + 1) * CB)
            acc = jnp.broadcast_to(convb_ref[0:1, cs], (C, CB))
            for k in range(CONV_W):
                o = HALO + k - half
                acc = acc + convw_ref[k:k + 1, cs] * stage_s[o:o + C, cs]
            return _silu(acc)

        def v_conv_x(cb):
            def run():
                xs_s[c, :, cb * CB:(cb + 1) * CB] = conv_block(cb)
            return run

        def v_conv_b():
            bT_s[c] = conv_block(SSD_W // CB).T.astype(BF16)

        def v_conv_c():
            c_s[c] = conv_block(SSD_W // CB + 1).astype(BF16)

        def rope(a):
            if not use_rope:
                return a
            return a * cos_s[c] + pltpu.roll(a, RET_HD // 2, 1) * sin_s[c]

        def v_q():
            for hd in range(RET_HEADS):
                sl = slice(hd * RET_HD, (hd + 1) * RET_HD)
                q_s[c, :, sl] = rope(st["pq"][:, sl]).astype(BF16)

        def v_k():
            for hd in range(RET_HEADS):
                sl = slice(hd * RET_HD, (hd + 1) * RET_HD)
                kh = rope(st["pk"][:, sl]) * (RET_HD ** -0.5)
                kT_s[c, sl, :] = kh.T.astype(BF16)

        def v_v():
            v_s[c] = st["pv"].astype(BF16)

        def v_g():
            g_s[c] = _silu(st["pg"]).astype(BF16)

        def v_z():
            z_s[c] = _silu(st["pz"]).astype(BF16)

        def m_sloc(gi):
            def run():
                xg = xs_s[c, :, gi * GROUP_W:(gi + 1) * GROUP_W]
                vcat = jnp.concatenate([xg * expand_f(st["sf"], gi), xg * expand_b(st["sb"], gi)],
                                       axis=1).astype(BF16)
                sloc_s[c, gi] = _dot(bT_s[c, gi * SSD_STATE:(gi + 1) * SSD_STATE, :], vcat).astype(sloc_s.dtype)
            return run

        def m_rloc(hd):
            def run():
                sl = slice(hd * RET_HD, (hd + 1) * RET_HD)
                vf = st["pv"][:, sl]
                vcat = jnp.concatenate([vf * rvec_s[hd, RV_TAIL_F], vf * rvec_s[hd, RV_TAIL_B]],
                                       axis=1).astype(BF16)
                rloc_s[c, hd] = _dot(kT_s[c, sl, :], vcat).astype(rloc_s.dtype)
            return run

        m_pe0 = m_main("pe0", XBC_COL0, XBC_COL0 + 512, halo=True)
        m_pe1 = m_main("pe1", XBC_COL0 + 512, MAIN_COLS, halo=True)
        m_q = m_main("pq", 0, RET_W)
        m_k = m_main("pk", RET_W, 2 * RET_W)
        m_v = m_main("pv", 2 * RET_W, 3 * RET_W)
        m_g = m_main("pg", 3 * RET_W, 4 * RET_W)
        m_z = m_main("pz", 4 * RET_W, XBC_COL0)
        order = [v_mod, m_dt, m_pe0, m_pe1, v_dt, m_q, v_stage(0), v_stage(1), m_k, v_conv_x(0), m_v,
                 v_conv_x(1), v_q, m_g, v_conv_b, v_k, m_z, v_conv_c, v_v,
                 m_rloc(0), m_rloc(1), v_g, m_rloc(2), m_rloc(3), m_sloc(0), m_sloc(1), v_z]
        for piece in order:
            piece()

    def recurrences():
        for hd in range(RET_HEADS):
            lgf, lgb = ret_decays(hd)
            dec_f = jnp.exp(C * lgf)
            dec_b = jnp.exp(C * lgb)
            if has_state:
                ent_f = sret0_ref[0, 0, 0, hd]
                ent_b = sret0_ref[0, 0, 1, hd]
            else:
                ent_f = jnp.zeros((RET_HD, RET_HD), F32)
                ent_b = jnp.zeros((RET_HD, RET_HD), F32)
            for c in range(nc):
                if cross:
                    rent_s[c, hd, :, 0:RET_HD] = ent_f.astype(BF16)
                ent_f = dec_f * ent_f + rloc_s[c, hd, :, 0:RET_HD]
            for c in range(nc - 1, -1, -1):
                if cross:
                    rent_s[c, hd, :, RET_HD:] = ent_b.astype(BF16)
                ent_b = dec_b * ent_b + rloc_s[c, hd, :, RET_HD:]
            if emit_state:
                nret_ref[0, 0, 0, hd] = ent_f
                nret_ref[0, 0, 1, hd] = ent_b

        for gi in range(SSD_GROUPS):
            heads = group_heads(gi)
            if has_state:
                ent_f = jnp.concatenate([sssd0_ref[0, 0, 0, hh] for hh in heads], axis=0).T
                ent_b = jnp.concatenate([sssd0_ref[0, 0, 1, hh] for hh in heads], axis=0).T
            else:
                ent_f = jnp.zeros((SSD_STATE, GROUP_W), F32)
                ent_b = jnp.zeros((SSD_STATE, GROUP_W), F32)
            for c in range(nc):
                if cross:
                    sent_s[c, gi, :, 0:GROUP_W] = ent_f.astype(BF16)
                ent_f = expand_f(dec_s[c, 0:1, :], gi) * ent_f + sloc_s[c, gi, :, 0:GROUP_W]
            for c in range(nc - 1, -1, -1):
                if cross:
                    sent_s[c, gi, :, GROUP_W:] = ent_b.astype(BF16)
                ent_b = expand_b(dec_s[c, 0:1, :], gi) * ent_b + sloc_s[c, gi, :, GROUP_W:]
            if emit_state:
                ent_ft = ent_f.T
                ent_bt = ent_b.T
                for k, hh in enumerate(heads):
                    nssd_ref[0, 0, 0, hh] = ent_ft[k * SSD_HD:(k + 1) * SSD_HD, :]
                    nssd_ref[0, 0, 1, hh] = ent_bt[k * SSD_HD:(k + 1) * SSD_HD, :]

    def emit_pieces(c, slot):
        rsl = [slice(hd * RET_HD, (hd + 1) * RET_HD) for hd in range(RET_HEADS)]
        gsl = [slice(gi * SSD_STATE, (gi + 1) * SSD_STATE) for gi in range(SSD_GROUPS)]
        xsl = [slice(gi * GROUP_W, (gi + 1) * GROUP_W) for gi in range(SSD_GROUPS)]
        st = {}

        def scores():
            ii = lax.broadcasted_iota(jnp.int32, (C, C), 0)
            jj = lax.broadcasted_iota(jnp.int32, (C, C), 1)
            st["causal"] = jj <= ii
            qs = [q_s[c, :, sl] for sl in rsl]
            cms = [c_s[c, :, sl] for sl in gsl]
            st["sc_s"] = [_dot(cms[gi], bT_s[c, gsl[gi], :]) for gi in range(SSD_GROUPS)]
            st["sc_r"] = [_dot(qs[hd], kT_s[c, rsl[hd], :]) for hd in range(RET_HEADS)]
            if cross:
                st["yc_s"] = [_dot(cms[gi], sent_s[c, gi]) for gi in range(SSD_GROUPS)]
                st["yc_r"] = [_dot(qs[hd], rent_s[c, hd]) for hd in range(RET_HEADS)]
            st["inc_col"] = inccol_s[c]
            st["exc_col"] = exccol_s[c]
            st["rowarg"] = rowarg_s[c]

        def ssd_head(gi, k):
            def run():
                hh = gi * HPG + k
                hb = SSD_HEADS + hh
                arg = jnp.where(st["causal"],
                                st["inc_col"][:, hh:hh + 1] - st["rowarg"][hh:hh + 1, :],
                                st["rowarg"][hb:hb + 1, :] - st["exc_col"][:, hb:hb + 1])
                m = (st["sc_s"][gi] * jnp.exp(arg)).astype(BF16)
                xh = xs_s[c, :, hh * SSD_HD:(hh + 1) * SSD_HD].astype(BF16)
                y_s[:, hh * SSD_HD:(hh + 1) * SSD_HD] = _dot(m, xh)
            return run

        def ssd_group(gi):
            def run():
                xg = xs_s[c, :, xsl[gi]]
                yg = y_s[:, xsl[gi]] + expand_f(dskip_row, gi) * xg
                if cross:
                    cfcb = cfcb_s[c]
                    yc = st["yc_s"][gi]
                    yg = yg + expand_f(cfcb, gi) * yc[:, 0:GROUP_W] + expand_b(cfcb, gi) * yc[:, GROUP_W:]
                y_s[:, xsl[gi]] = yg
            return run

        def ret_head(hd):
            def run():
                m = (st["sc_r"][hd] * wdec_s[hd]).astype(BF16)
                o = _dot(m, v_s[c, :, rsl[hd]])
                if cross:
                    yc = st["yc_r"][hd]
                    o = o + rvec_s[hd, RV_CROSS_F] * yc[:, 0:RET_HD] + rvec_s[hd, RV_CROSS_B] * yc[:, RET_HD:]
                o = o * lax.rsqrt(jnp.mean(o * o, axis=-1, keepdims=True) + EPS)
                mix_s[:, rsl[hd]] = (g_s[c, :, rsl[hd]].astype(F32) * o).astype(BF16)
            return run

        def ssd_norm():
            yz = y_s[...] * z_s[c].astype(F32)
            yn = yz * lax.rsqrt(jnp.mean(yz * yz, axis=-1, keepdims=True) + EPS) * normw_ref[...]
            mix_s[:, RET_W:] = yn.astype(BF16)

        def out_proj():
            g1 = mod_ref[0, 2:3, :]
            y = ALPHA * x_ref[0] + g1 * _dot(mix_s[...], wout_ref[...])
            x1_s[slot] = _layer_norm(y, ln1g_ref[...], ln1b_ref[...])

        pieces = [scores]
        for gi in range(SSD_GROUPS):
            pieces += [ssd_head(gi, k) for k in range(HPG)] + [ssd_group(gi)]
        pieces += [ret_head(hd) for hd in range(RET_HEADS)] + [ssd_norm, out_proj]
        return pieces

    def ffn_pieces(slot):
        st = {}

        def start():
            sh2 = modf_ref[0, 3:4, :]
            sc2 = modf_ref[0, 4:5, :]
            st["h2"] = (x1_s[slot] * (1.0 + sc2) + sh2).astype(BF16)

        def hidden(j):
            def run():
                js = slice(j * FF_BLK, (j + 1) * FF_BLK)
                h2 = st["h2"]
                hid_s[:, js] = (_silu(_dot(h2, wg_ref[:, js])) * _dot(h2, wu_ref[:, js])).astype(BF16)
            return run

        def finish():
            g2 = modf_ref[0, 5:6, :]
            y = ALPHA * x1_s[slot] + g2 * _dot(hid_s[...], wd_ref[...])
            out_ref[0] = _layer_norm(y, ln2g_ref[...], ln2b_ref[...])

        return [start] + [hidden(j) for j in range(D_FF // FF_BLK)] + [finish]

    def run_all(pieces):
        for p in pieces:
            p()

    @pl.when((step < nc) & (seq < nb))
    def _():
        project(step)

    @pl.when((step == nc) & (seq < nb))
    def _():
        recurrences()

    kk = step - nc
    slot = (seq * nc + kk) & 1
    first = (seq == 0) & (step == nc)

    @pl.when(first)
    def _():
        for i in range(1, len(big_weights)):
            weight_copy(i).wait()
        run_all(emit_pieces(kk, slot))

    @pl.when((step >= nc) & (seq < nb) & jnp.logical_not(first))
    def _():
        _interleave(ffn_pieces(1 - slot), emit_pieces(kk, slot))

    @pl.when((seq == nb) & (step == nc))
    def _():
        run_all(ffn_pieces((nb * nc - 1) & 1))


def _const_spec(shape):
    nd = len(shape)
    return pl.BlockSpec(shape, lambda b, s: (0,) * nd, pipeline_mode=pl.Buffered(1))


def _layer_call(x, mod, mod_per_seq, weights, states, emit_state, use_rope, name):
    nb, L, _ = x.shape
    nc = L // CHUNK
    hpc_blocks = CHUNK // HALO
    has_state = states is not None
    last = nb - 1

    def chunk_of(b, s):
        return jnp.where(b > last, nc - 1, jnp.where(s < nc, s, s - nc))

    def halo_chunk(b, s):
        return jnp.where(b > last, nc - 1, jnp.minimum(s, nc - 1))

    def seq_of(b):
        return jnp.minimum(b, last)

    def lag_seq(b, s):
        return jnp.minimum(jnp.where(s > nc, b, jnp.maximum(b - 1, 0)), last)

    def out_map(b, s):
        live = (s > nc) & (b < nb)
        idle_chunk = jnp.where(b == 0, 0, nc - 1)
        return (jnp.where(live, b, jnp.maximum(b - 1, 0)), jnp.where(live, s - nc - 1, idle_chunk), 0)

    if mod_per_seq:
        mod_map = lambda b, s: (seq_of(b), 0, 0)
        modf_map = lambda b, s: (lag_seq(b, s), 0, 0)
    else:
        mod_map = modf_map = lambda b, s: (0, 0, 0)

    in_specs = [
        pl.BlockSpec((1, CHUNK, D_MODEL), lambda b, s: (seq_of(b), chunk_of(b, s), 0)),
        pl.BlockSpec((1, HALO, D_MODEL),
                     lambda b, s: (seq_of(b), jnp.maximum(halo_chunk(b, s) * hpc_blocks - 1, 0), 0)),
        pl.BlockSpec((1, HALO, D_MODEL),
                     lambda b, s: (seq_of(b), jnp.minimum((halo_chunk(b, s) + 1) * hpc_blocks, nc * hpc_blocks - 1), 0)),
        pl.BlockSpec((1, 6, D_MODEL), mod_map),
        pl.BlockSpec((1, 6, D_MODEL), modf_map),
    ] + [pl.BlockSpec(memory_space=pl.ANY) if i in BIG_WEIGHTS else _const_spec(w.shape)
         for i, w in enumerate(weights)]
    args = [x, x, x, mod, mod] + list(weights)
    ret_block = (1, 1, 2, RET_HEADS, RET_HD, RET_HD)
    ssd_block = (1, 1, 2, SSD_HEADS, SSD_HD, SSD_STATE)
    state_map = lambda b, s: (seq_of(b), 0, 0, 0, 0, 0)
    if has_state:
        in_specs += [pl.BlockSpec(ret_block, state_map, pipeline_mode=pl.Buffered(1)),
                     pl.BlockSpec(ssd_block, state_map, pipeline_mode=pl.Buffered(1))]
        args += list(states)
    out_shape = [jax.ShapeDtypeStruct((nb, L, D_MODEL), F32)]
    out_specs = [pl.BlockSpec((1, CHUNK, D_MODEL), out_map)]
    if emit_state:
        out_shape += [jax.ShapeDtypeStruct((nb,) + ret_block[1:], F32),
                      jax.ShapeDtypeStruct((nb,) + ssd_block[1:], F32)]
        out_specs += [pl.BlockSpec(ret_block, state_map), pl.BlockSpec(ssd_block, state_map)]
    state_dt = F32 if emit_state else BF16
    scratch = [
        pltpu.VMEM((nc, CHUNK, RET_W), BF16),
        pltpu.VMEM((nc, RET_W, CHUNK), BF16),
        pltpu.VMEM((nc, CHUNK, RET_W), BF16),
        pltpu.VMEM((nc, CHUNK, RET_W), BF16),
        pltpu.VMEM((nc, CHUNK, SSD_W), BF16),
        pltpu.VMEM((CHUNK + 2 * HALO, CONV_CH), F32),
        pltpu.VMEM((nc, CHUNK, SSD_W), F32),
        pltpu.VMEM((nc, SSD_GROUPS * SSD_STATE, CHUNK), BF16),
        pltpu.VMEM((nc, CHUNK, SSD_GROUPS * SSD_STATE), BF16),
        pltpu.VMEM((nc, CHUNK, LANES), F32),
        pltpu.VMEM((nc, CHUNK, LANES), F32),
        pltpu.VMEM((nc, CHUNK, LANES), F32),
        pltpu.VMEM((nc, 8, LANES), F32),
        pltpu.VMEM((nc, N_DT, CHUNK), F32),
        pltpu.VMEM((CHUNK, SSD_W), F32),
        pltpu.VMEM((CHUNK, 2 * RET_W), BF16),
        pltpu.VMEM((nc, RET_HEADS, RET_HD, 2 * RET_HD), state_dt),
        pltpu.VMEM((nc, SSD_GROUPS, SSD_STATE, 2 * GROUP_W), state_dt),
        pltpu.VMEM((nc, RET_HEADS, RET_HD, 2 * RET_HD), BF16),
        pltpu.VMEM((nc, SSD_GROUPS, SSD_STATE, 2 * GROUP_W), BF16),
        pltpu.VMEM((RET_HEADS, CHUNK, CHUNK), F32),
        pltpu.VMEM((RET_HEADS, 4, CHUNK, LANES), F32),
        pltpu.VMEM((2, CHUNK, D_MODEL), F32),
        pltpu.VMEM((CHUNK, D_FF), BF16),
        pltpu.VMEM((LANES, D_MODEL), BF16),
    ]
    scratch += [pltpu.VMEM(weights[i].shape, BF16) for i in BIG_WEIGHTS]
    scratch += [pltpu.SemaphoreType.DMA((len(BIG_WEIGHTS),))]
    if use_rope:
        scratch += [pltpu.VMEM((nc, CHUNK, LANES), F32)] * 2
    kern = functools.partial(_layer_kernel, L=L, nb=nb, has_state=has_state, use_rope=use_rope,
                             emit_state=emit_state)
    return pl.pallas_call(
        kern,
        grid=(nb + 1, 2 * nc),
        in_specs=in_specs,
        out_specs=out_specs,
        out_shape=out_shape,
        scratch_shapes=scratch,
        compiler_params=pltpu.CompilerParams(dimension_semantics=("arbitrary", "arbitrary"),
                                             vmem_limit_bytes=VMEM_LIMIT),
        name=name,
    )(*args)


def kernel(x_prompt, x_sample, state_ret, state_ssd, c, c_ctx, w_in, ret_decay_fwd, ret_decay_bwd, conv_w, conv_b, dt_bias_fwd, dt_bias_bwd, a_log_fwd, a_log_bwd, d_skip, ssd_norm_w, w_out, ln1_g, ln1_b, w_gate, w_up, w_down, ln2_g, ln2_b, w_ada, b_ada):
    depth = w_in.shape[0]
    assert depth == 1, "single trunk layer"
    bp, lp, _ = x_prompt.shape
    bs, ls, _ = x_sample.shape
    assert lp % CHUNK == 0 and ls % CHUNK == 0 and ls % GRID_W == 0 and D_FF % FF_BLK == 0

    rows = -(-(bs + 1) // 8) * 8
    cond = jnp.zeros((rows, D_MODEL), F32).at[:bs].set(c).at[bs].set(c_ctx)
    mod = _ada_call(cond, w_ada[0], b_ada[0][None, :]).reshape(rows, 6, D_MODEL)
    mod_lat = mod[:bs]
    mod_ctx = mod[bs:bs + 1]

    convw = jnp.zeros((8, CONV_CH), F32).at[:CONV_W].set(conv_w[0])
    convb = conv_b[0][None, :]
    dt_bias = jnp.concatenate([dt_bias_fwd[0], dt_bias_bwd[0]])
    a_log = jnp.concatenate([a_log_fwd[0], a_log_bwd[0]])
    ret_decay = jnp.concatenate([ret_decay_fwd[0], ret_decay_bwd[0]])
    hp = jnp.zeros((8, LANES), F32)
    hp = hp.at[0, :N_DT].set(dt_bias).at[1, :N_DT].set(a_log)
    hp = hp.at[2, :2 * RET_HEADS].set(ret_decay).at[3, :SSD_HEADS].set(d_skip[0])
    hpc = jnp.zeros((N_DT, LANES), F32).at[:, 0].set(dt_bias).at[:, 1].set(a_log)
    w_in_b, w_out_b, w_gate_b, w_up_b, w_down_b = _cast_call(
        jnp.swapaxes(w_in[0], 0, 1), [w_out[0], w_gate[0], w_up[0], w_down[0]])
    weights = (w_in_b, convw, convb, hp, hpc, ssd_norm_w[0][None, :],
               w_out_b, ln1_g[0][None, :], ln1_b[0][None, :],
               w_gate_b, w_up_b, w_down_b, ln2_g[0][None, :], ln2_b[0][None, :])

    yp, new_ret, new_ssd = _layer_call(x_prompt, mod_ctx, False, weights, None, True, False, "layer_ctx")
    states = (state_ret, jnp.swapaxes(state_ssd, -1, -2))
    (ys,) = _layer_call(x_sample, mod_lat, True, weights, states, False, True, "layer_lat")
    return (yp, ys, new_ret, jnp.swapaxes(new_ssd, -1, -2))
```

```python
import functools
import math

import jax
import jax.numpy as jnp
from jax import lax
from jax.experimental import pallas as pl
from jax.experimental.pallas import tpu as pltpu

F32 = jnp.float32
BF16 = jnp.bfloat16

D_MODEL = 1024
RET_W = 512
RET_HEADS = 4
RET_HD = 128
SSD_W = 512
SSD_HD = 64
SSD_HEADS = 8
SSD_GROUPS = 2
SSD_STATE = 128
HPG = SSD_HEADS // SSD_GROUPS
GROUP_W = HPG * SSD_HD
CONV_W = 5
CONV_CH = SSD_W + 2 * SSD_GROUPS * SSD_STATE
D_FF = 2816
GRID_W = 64
GRID_SHIFT = 6
ROPE_BASE = 10000.0
EPS = 1e-6
ALPHA = 2.0 ** 0.25
MAIN_COLS = 4 * RET_W + SSD_W + CONV_CH
XBC_COL0 = 4 * RET_W + SSD_W
N_DT = 2 * SSD_HEADS

CHUNK = 256
HALO = 8
FF_BLK = 256
ADA_COLS = 2048
CAST_STEPS = 8
CAST_WT_ROWS = 512
LANES = 128
VMEM_LIMIT = 62 * 1024 * 1024


def _dot(a, b):
    return jnp.dot(a, b, preferred_element_type=F32)


def _dot_nt(a, b):
    return lax.dot_general(a, b, (((1,), (1,)), ((), ())), preferred_element_type=F32)


def _silu(x):
    return x * jax.nn.sigmoid(x)


def _softplus(x):
    return jnp.maximum(x, 0.0) + jnp.log1p(jnp.exp(-jnp.abs(x)))


def _layer_norm(y, g, b):
    mu = jnp.mean(y, axis=-1, keepdims=True)
    yc = y - mu
    var = jnp.mean(yc * yc, axis=-1, keepdims=True)
    return yc * lax.rsqrt(var + EPS) * g + b


def _cumsum(x, axis):
    n = x.shape[axis]
    idx = lax.broadcasted_iota(jnp.int32, x.shape, axis)
    s = 1
    while s < n:
        x = x + jnp.where(idx >= s, pltpu.roll(x, s, axis), 0.0)
        s *= 2
    return x


def _expand4(cols, lane):
    a = jnp.where(lane < SSD_HD, cols[0], cols[1])
    b = jnp.where(lane < SSD_HD, cols[2], cols[3])
    return jnp.concatenate([a, b], axis=1)


def _interleave(a, b):
    ia = ib = 0
    while ia < len(a) or ib < len(b):
        if ib >= len(b) or (ia < len(a) and ia * len(b) <= ib * len(a)):
            a[ia]()
            ia += 1
        else:
            b[ib]()
            ib += 1


def _ada_kernel(cond_ref, w_ref, b_ref, o_ref):
    s = _silu(cond_ref[...]).astype(BF16)
    o_ref[...] = _dot(s, w_ref[...].astype(BF16)) + b_ref[...]


def _ada_call(cond, w_ada, b_ada):
    rows = cond.shape[0]
    n = w_ada.shape[1]
    return pl.pallas_call(
        _ada_kernel,
        grid=(n // ADA_COLS,),
        in_specs=[
            pl.BlockSpec((rows, D_MODEL), lambda j: (0, 0)),
            pl.BlockSpec((D_MODEL, ADA_COLS), lambda j: (0, j)),
            pl.BlockSpec((1, ADA_COLS), lambda j: (0, j)),
        ],
        out_specs=pl.BlockSpec((rows, ADA_COLS), lambda j: (0, j)),
        out_shape=jax.ShapeDtypeStruct((rows, n), F32),
        compiler_params=pltpu.CompilerParams(dimension_semantics=("arbitrary",)),
        name="adaln_mod",
    )(cond, w_ada, b_ada)


def _cast_kernel(wt_ref, *refs, wt_rows):
    blk = wt_ref.shape[0]
    row = lax.broadcasted_iota(jnp.int32, (blk, 1), 0) + pl.program_id(0) * blk
    n = len(refs) // 2
    refs[n][...] = jnp.where(row < wt_rows, wt_ref[...], 0.0).T.astype(BF16)
    for src, dst in zip(refs[:n], refs[n + 1:]):
        dst[...] = src[...].astype(BF16)


def _cast_call(wt, ws):
    for w in ws:
        assert w.shape[0] % (CAST_STEPS * 16) == 0, w.shape
    assert wt.shape[0] <= CAST_STEPS * CAST_WT_ROWS
    specs = [pl.BlockSpec((w.shape[0] // CAST_STEPS, w.shape[1]), lambda i: (i, 0)) for w in ws]
    return pl.pallas_call(
        functools.partial(_cast_kernel, wt_rows=wt.shape[0]),
        grid=(CAST_STEPS,),
        in_specs=[pl.BlockSpec((CAST_WT_ROWS, wt.shape[1]), lambda i: (i, 0))] + specs,
        out_specs=[pl.BlockSpec((wt.shape[1], CAST_WT_ROWS), lambda i: (0, i))] + specs,
        out_shape=[jax.ShapeDtypeStruct((wt.shape[1], -(-wt.shape[0] // LANES) * LANES), BF16)]
        + [jax.ShapeDtypeStruct(w.shape, BF16) for w in ws],
        compiler_params=pltpu.CompilerParams(dimension_semantics=("arbitrary",),
                                             vmem_limit_bytes=VMEM_LIMIT),
        name="weights_to_bf16",
    )(wt, *ws)


RV_TAIL_F, RV_TAIL_B, RV_CROSS_F, RV_CROSS_B = range(4)
BIG_WEIGHTS = (0, 6, 9, 10, 11)


def _layer_kernel(*refs, L, nb, has_state, use_rope, emit_state):
    nc = L // CHUNK
    C = CHUNK
    cross = has_state or nc > 1
    it = iter(refs)
    x_ref, xprev_ref, xnext_ref, mod_ref, modf_ref, wmain_hbm = (next(it) for _ in range(6))
    convw_ref, convb_ref, hp_ref, hpc_ref, normw_ref = (next(it) for _ in range(5))
    wout_hbm, ln1g_ref, ln1b_ref = (next(it) for _ in range(3))
    wg_hbm, wu_hbm, wd_hbm, ln2g_ref, ln2b_ref = (next(it) for _ in range(5))
    if has_state:
        sret0_ref, sssd0_ref = next(it), next(it)
    out_ref = next(it)
    if emit_state:
        nret_ref, nssd_ref = next(it), next(it)
    (q_s, kT_s, v_s, g_s, z_s, stage_s, xs_s, bT_s, c_s, inccol_s, exccol_s, cfcb_s, dec_s,
     rowarg_s, y_s, mix_s, rloc_s, sloc_s, rent_s, sent_s, wdec_s, rvec_s, x1_s, hid_s, wdtr_s) = (
         next(it) for _ in range(25))
    wmain_ref, wout_ref, wg_ref, wu_ref, wd_ref, wsem = (next(it) for _ in range(6))
    if use_rope:
        cos_s, sin_s = next(it), next(it)

    seq = pl.program_id(0)
    step = pl.program_id(1)

    big_weights = ((wmain_hbm, wmain_ref), (wout_hbm, wout_ref), (wg_hbm, wg_ref), (wu_hbm, wu_ref),
                   (wd_hbm, wd_ref))

    def weight_copy(i):
        return pltpu.make_async_copy(big_weights[i][0], big_weights[i][1], wsem.at[i])

    hp = hp_ref[...]
    dt_bias_row = hp[0:1, :]
    nega_row = -jnp.exp(hp[1:2, :])
    lg_row = -_softplus(-hp[2:3, :])
    dskip_row = hp[3:4, :]
    hpc = hpc_ref[...]
    dt_bias_col = hpc[:, 0:1]
    nega_col = -jnp.exp(hpc[:, 1:2])

    lane = lax.broadcasted_iota(jnp.int32, (1, LANES), 1)

    def ret_decays(hd):
        lgf = lg_row[:, hd:hd + 1]
        lgb = lg_row[:, RET_HEADS + hd:RET_HEADS + hd + 1]
        return lgf, lgb

    def group_heads(gi):
        return [gi * HPG + k for k in range(HPG)]

    def expand_f(arr, gi):
        return _expand4([arr[:, hh:hh + 1] for hh in group_heads(gi)], lane)

    def expand_b(arr, gi):
        return _expand4([arr[:, SSD_HEADS + hh:SSD_HEADS + hh + 1] for hh in group_heads(gi)], lane)

    @pl.when((seq == 0) & (step == 0))
    def _():
        for i in range(len(big_weights)):
            weight_copy(i).start()
        weight_copy(0).wait()
        ii = lax.broadcasted_iota(jnp.int32, (C, C), 0)
        jj = lax.broadcasted_iota(jnp.int32, (C, C), 1)
        dmat = (ii - jj).astype(F32)
        irow = lax.broadcasted_iota(jnp.int32, (C, LANES), 0).astype(F32)
        for hd in range(RET_HEADS):
            lgf, lgb = ret_decays(hd)
            wdec_s[hd] = jnp.exp(jnp.where(jj <= ii, dmat * lgf, -dmat * lgb))
            rvec_s[hd, RV_TAIL_F] = jnp.exp((C - 1.0 - irow) * lgf)
            rvec_s[hd, RV_TAIL_B] = jnp.exp(irow * lgb)
            rvec_s[hd, RV_CROSS_F] = jnp.exp((irow + 1.0) * lgf)
            rvec_s[hd, RV_CROSS_B] = jnp.exp((C - irow) * lgb)
        wdtr_s[...] = wmain_ref[:, MAIN_COLS:MAIN_COLS + LANES].astype(F32).T.astype(BF16)
        if use_rope:
            ln = lax.broadcasted_iota(jnp.int32, (C, LANES), 1)
            nf = RET_HD // 4
            inv = jnp.exp((ln & (nf - 1)).astype(F32) * (-math.log(ROPE_BASE) / nf))
            for cc in range(nc):
                t = lax.broadcasted_iota(jnp.int32, (C, LANES), 0) + cc * C
                pos = jnp.where((ln & (2 * nf - 1)) < nf, t >> GRID_SHIFT, t & (GRID_W - 1)).astype(F32)
                ang = pos * inv
                cos_s[cc] = jnp.cos(ang)
                sin_s[cc] = jnp.where(ln < RET_HD // 2, -jnp.sin(ang), jnp.sin(ang))

    def project(c):
        st = {}
        CB = 256
        assert CONV_W == 5 and HALO >= CONV_W // 2

        def v_mod():
            sh1 = mod_ref[0, 0:1, :]
            sc1 = mod_ref[0, 1:2, :]
            xe = jnp.concatenate([xprev_ref[0], x_ref[0], xnext_ref[0]], axis=0)
            xm = xe * (1.0 + sc1) + sh1
            st["he"] = xm.astype(BF16)
            st["h"] = xm[HALO:HALO + C].astype(BF16)

        def m_main(name, lo, hi, halo=False):
            def run():
                st[name] = _dot(st["he" if halo else "h"], wmain_ref[:, lo:hi])
            return run

        def m_dt():
            raw_c = _dot(st["h"], wmain_ref[:, MAIN_COLS:MAIN_COLS + LANES])
            st["dtc"] = raw_c + pltpu.roll(raw_c, SSD_HEADS, 1)
            raw_r = _dot_nt(wdtr_s[0:N_DT, :], st["h"])
            st["dtr"] = raw_r + pltpu.roll(raw_r, SSD_HEADS, 0)

        def v_dt():
            dt_c = _softplus(st["dtc"] + dt_bias_row)
            lac = dt_c * nega_row
            inc_col = _cumsum(lac, 0)
            exc_col = inc_col - lac
            tot_col = inc_col[C - 1:C, :]
            inccol_s[c] = inc_col
            exccol_s[c] = exc_col
            dec_s[c] = jnp.broadcast_to(jnp.exp(tot_col), (8, LANES))
            if cross:
                cfcb_s[c] = jnp.exp(jnp.where(lane < SSD_HEADS, inc_col, tot_col - exc_col))
            st["sf"] = jnp.exp(tot_col - inc_col) * dt_c
            st["sb"] = jnp.exp(exc_col) * dt_c
            dt_r = _softplus(st["dtr"] + dt_bias_col)
            lar = dt_r * nega_col
            inc_row = _cumsum(lar, 1)
            ldt = jnp.log(dt_r)
            rid = lax.broadcasted_iota(jnp.int32, (N_DT, C), 0)
            rowarg_s[c] = jnp.where(rid < SSD_HEADS, inc_row - ldt, inc_row - lar + ldt)

        def v_stage(hf):
            def run():
                pe = st["pe%d" % hf]
                cs = slice(hf * 512, (hf + 1) * 512)
                stage_s[0:HALO, cs] = jnp.where(c > 0, pe[0:HALO], 0.0)
                stage_s[HALO:HALO + C, cs] = pe[HALO:HALO + C]
                stage_s[HALO + C:, cs] = jnp.where(c < nc - 1, pe[HALO + C:], 0.0)
            return run

        def conv_block(cb):
            cs = slice(cb * CB, (cb + 1) * CB)
            rows = C + 2 * HALO
            xin = stage_s[:, cs]
            taps = [convw_ref[k:k + 1, cs] * xin for k in range(CONV_W)]
            up = lambda a: pltpu.roll(a, rows - 1, 0)
            down = lambda a: pltpu.roll(a, 1, 0)
            acc = taps[2] + up(taps[3] + up(taps[4])) + down(taps[1] + down(taps[0]))
            return _silu(acc[HALO:HALO + C] + convb_ref[0:1, cs])

        def v_conv_x(cb):
            def run():
                xs_s[c, :, cb * CB:(cb + 1) * CB] = conv_block(cb)
            return run

        def v_conv_b():
            bT_s[c] = conv_block(SSD_W // CB).T.astype(BF16)

        def v_conv_c():
            c_s[c] = conv_block(SSD_W // CB + 1).astype(BF16)

        def rope(a):
            if not use_rope:
                return a
            return a * cos_s[c] + pltpu.roll(a, RET_HD // 2, 1) * sin_s[c]

        def v_q():
            for hd in range(RET_HEADS):
                sl = slice(hd * RET_HD, (hd + 1) * RET_HD)
                q_s[c, :, sl] = rope(st["pq"][:, sl]).astype(BF16)

        def v_k():
            for hd in range(RET_HEADS):
                sl = slice(hd * RET_HD, (hd + 1) * RET_HD)
                kh = rope(st["pk"][:, sl]) * (RET_HD ** -0.5)
                kT_s[c, sl, :] = kh.T.astype(BF16)

        def v_v():
            v_s[c] = st["pv"].astype(BF16)

        def v_g():
            g_s[c] = _silu(st["pg"]).astype(BF16)

        def v_z():
            z_s[c] = _silu(st["pz"]).astype(BF16)

        def m_sloc(gi):
            def run():
                xg = xs_s[c, :, gi * GROUP_W:(gi + 1) * GROUP_W]
                vcat = jnp.concatenate([xg * expand_f(st["sf"], gi), xg * expand_b(st["sb"], gi)],
                                       axis=1).astype(BF16)
                sloc_s[c, gi] = _dot(bT_s[c, gi * SSD_STATE:(gi + 1) * SSD_STATE, :], vcat).astype(sloc_s.dtype)
            return run

        def m_rloc(hd):
            def run():
                sl = slice(hd * RET_HD, (hd + 1) * RET_HD)
                vf = st["pv"][:, sl]
                vcat = jnp.concatenate([vf * rvec_s[hd, RV_TAIL_F], vf * rvec_s[hd, RV_TAIL_B]],
                                       axis=1).astype(BF16)
                rloc_s[c, hd] = _dot(kT_s[c, sl, :], vcat).astype(rloc_s.dtype)
            return run

        m_pe0 = m_main("pe0", XBC_COL0, XBC_COL0 + 512, halo=True)
        m_pe1 = m_main("pe1", XBC_COL0 + 512, MAIN_COLS, halo=True)
        m_q = m_main("pq", 0, RET_W)
        m_k = m_main("pk", RET_W, 2 * RET_W)
        m_v = m_main("pv", 2 * RET_W, 3 * RET_W)
        m_g = m_main("pg", 3 * RET_W, 4 * RET_W)
        m_z = m_main("pz", 4 * RET_W, XBC_COL0)
        order = [v_mod, m_dt, m_pe0, m_pe1, v_dt, m_q, v_stage(0), v_stage(1), m_k, v_conv_x(0), m_v,
                 v_conv_x(1), v_q, m_g, v_conv_b, v_k, m_z, v_conv_c, v_v,
                 m_rloc(0), m_rloc(1), v_g, m_rloc(2), m_rloc(3), m_sloc(0), m_sloc(1), v_z]
        for piece in order:
            piece()

    def recurrences():
        for hd in range(RET_HEADS):
            lgf, lgb = ret_decays(hd)
            dec_f = jnp.exp(C * lgf)
            dec_b = jnp.exp(C * lgb)
            if has_state:
                ent_f = sret0_ref[0, 0, 0, hd]
                ent_b = sret0_ref[0, 0, 1, hd]
            else:
                ent_f = jnp.zeros((RET_HD, RET_HD), F32)
                ent_b = jnp.zeros((RET_HD, RET_HD), F32)
            for c in range(nc):
                if cross:
                    rent_s[c, hd, :, 0:RET_HD] = ent_f.astype(BF16)
                ent_f = dec_f * ent_f + rloc_s[c, hd, :, 0:RET_HD]
            for c in range(nc - 1, -1, -1):
                if cross:
                    rent_s[c, hd, :, RET_HD:] = ent_b.astype(BF16)
                ent_b = dec_b * ent_b + rloc_s[c, hd, :, RET_HD:]
            if emit_state:
                nret_ref[0, 0, 0, hd] = ent_f
                nret_ref[0, 0, 1, hd] = ent_b

        for gi in range(SSD_GROUPS):
            heads = group_heads(gi)
            if has_state:
                ent_f = jnp.concatenate([sssd0_ref[0, 0, 0, hh] for hh in heads], axis=0).T
                ent_b = jnp.concatenate([sssd0_ref[0, 0, 1, hh] for hh in heads], axis=0).T
            else:
                ent_f = jnp.zeros((SSD_STATE, GROUP_W), F32)
                ent_b = jnp.zeros((SSD_STATE, GROUP_W), F32)
            for c in range(nc):
                if cross:
                    sent_s[c, gi, :, 0:GROUP_W] = ent_f.astype(BF16)
                ent_f = expand_f(dec_s[c, 0:1, :], gi) * ent_f + sloc_s[c, gi, :, 0:GROUP_W]
            for c in range(nc - 1, -1, -1):
                if cross:
                    sent_s[c, gi, :, GROUP_W:] = ent_b.astype(BF16)
                ent_b = expand_b(dec_s[c, 0:1, :], gi) * ent_b + sloc_s[c, gi, :, GROUP_W:]
            if emit_state:
                ent_ft = ent_f.T
                ent_bt = ent_b.T
                for k, hh in enumerate(heads):
                    nssd_ref[0, 0, 0, hh] = ent_ft[k * SSD_HD:(k + 1) * SSD_HD, :]
                    nssd_ref[0, 0, 1, hh] = ent_bt[k * SSD_HD:(k + 1) * SSD_HD, :]

    def emit_pieces(c, slot):
        rsl = [slice(hd * RET_HD, (hd + 1) * RET_HD) for hd in range(RET_HEADS)]
        gsl = [slice(gi * SSD_STATE, (gi + 1) * SSD_STATE) for gi in range(SSD_GROUPS)]
        xsl = [slice(gi * GROUP_W, (gi + 1) * GROUP_W) for gi in range(SSD_GROUPS)]
        st = {}

        def scores():
            ii = lax.broadcasted_iota(jnp.int32, (C, C), 0)
            jj = lax.broadcasted_iota(jnp.int32, (C, C), 1)
            st["causal"] = jj <= ii
            qs = [q_s[c, :, sl] for sl in rsl]
            cms = [c_s[c, :, sl] for sl in gsl]
            st["sc_s"] = [_dot(cms[gi], bT_s[c, gsl[gi], :]) for gi in range(SSD_GROUPS)]
            st["sc_r"] = [_dot(qs[hd], kT_s[c, rsl[hd], :]) for hd in range(RET_HEADS)]
            if cross:
                st["yc_s"] = [_dot(cms[gi], sent_s[c, gi]) for gi in range(SSD_GROUPS)]
                st["yc_r"] = [_dot(qs[hd], rent_s[c, hd]) for hd in range(RET_HEADS)]
            st["inc_col"] = inccol_s[c]
            st["exc_col"] = exccol_s[c]
            st["rowarg"] = rowarg_s[c]

        def ssd_head(gi, k):
            def run():
                hh = gi * HPG + k
                hb = SSD_HEADS + hh
                arg = jnp.where(st["causal"],
                                st["inc_col"][:, hh:hh + 1] - st["rowarg"][hh:hh + 1, :],
                                st["rowarg"][hb:hb + 1, :] - st["exc_col"][:, hb:hb + 1])
                m = (st["sc_s"][gi] * jnp.exp(arg)).astype(BF16)
                xh = xs_s[c, :, hh * SSD_HD:(hh + 1) * SSD_HD].astype(BF16)
                y_s[:, hh * SSD_HD:(hh + 1) * SSD_HD] = _dot(m, xh)
            return run

        def ssd_group(gi):
            def run():
                xg = xs_s[c, :, xsl[gi]]
                yg = y_s[:, xsl[gi]] + expand_f(dskip_row, gi) * xg
                if cross:
                    cfcb = cfcb_s[c]
                    yc = st["yc_s"][gi]
                    yg = yg + expand_f(cfcb, gi) * yc[:, 0:GROUP_W] + expand_b(cfcb, gi) * yc[:, GROUP_W:]
                y_s[:, xsl[gi]] = yg
            return run

        def ret_head(hd):
            def run():
                m = (st["sc_r"][hd] * wdec_s[hd]).astype(BF16)
                o = _dot(m, v_s[c, :, rsl[hd]])
                if cross:
                    yc = st["yc_r"][hd]
                    o = o + rvec_s[hd, RV_CROSS_F] * yc[:, 0:RET_HD] + rvec_s[hd, RV_CROSS_B] * yc[:, RET_HD:]
                o = o * lax.rsqrt(jnp.mean(o * o, axis=-1, keepdims=True) + EPS)
                mix_s[:, rsl[hd]] = (g_s[c, :, rsl[hd]].astype(F32) * o).astype(BF16)
            return run

        def ssd_norm():
            yz = y_s[...] * z_s[c].astype(F32)
            yn = yz * lax.rsqrt(jnp.mean(yz * yz, axis=-1, keepdims=True) + EPS) * normw_ref[...]
            mix_s[:, RET_W:] = yn.astype(BF16)

        def out_proj():
            g1 = mod_ref[0, 2:3, :]
            y = ALPHA * x_ref[0] + g1 * _dot(mix_s[...], wout_ref[...])
            x1_s[slot] = _layer_norm(y, ln1g_ref[...], ln1b_ref[...])

        pieces = [scores]
        for gi in range(SSD_GROUPS):
            pieces += [ssd_head(gi, k) for k in range(HPG)] + [ssd_group(gi)]
        pieces += [ret_head(hd) for hd in range(RET_HEADS)] + [ssd_norm, out_proj]
        return pieces

    def ffn_pieces(slot):
        st = {}

        def start():
            sh2 = modf_ref[0, 3:4, :]
            sc2 = modf_ref[0, 4:5, :]
            st["h2"] = (x1_s[slot] * (1.0 + sc2) + sh2).astype(BF16)

        def hidden(j):
            def run():
                js = slice(j * FF_BLK, (j + 1) * FF_BLK)
                h2 = st["h2"]
                hid_s[:, js] = (_silu(_dot(h2, wg_ref[:, js])) * _dot(h2, wu_ref[:, js])).astype(BF16)
            return run

        def finish():
            g2 = modf_ref[0, 5:6, :]
            y = ALPHA * x1_s[slot] + g2 * _dot(hid_s[...], wd_ref[...])
            out_ref[0] = _layer_norm(y, ln2g_ref[...], ln2b_ref[...])

        return [start] + [hidden(j) for j in range(D_FF // FF_BLK)] + [finish]

    def run_all(pieces):
        for p in pieces:
            p()

    @pl.when((step < nc) & (seq < nb))
    def _():
        project(step)

    @pl.when((step == nc) & (seq < nb))
    def _():
        recurrences()

    kk = step - nc
    slot = (seq * nc + kk) & 1
    first = (seq == 0) & (step == nc)

    @pl.when(first)
    def _():
        for i in range(1, len(big_weights)):
            weight_copy(i).wait()
        run_all(emit_pieces(kk, slot))

    @pl.when((step >= nc) & (seq < nb) & jnp.logical_not(first))
    def _():
        _interleave(ffn_pieces(1 - slot), emit_pieces(kk, slot))

    @pl.when((seq == nb) & (step == nc))
    def _():
        run_all(ffn_pieces((nb * nc - 1) & 1))


def _const_spec(shape):
    nd = len(shape)
    return pl.BlockSpec(shape, lambda b, s: (0,) * nd, pipeline_mode=pl.Buffered(1))


def _layer_call(x, mod, mod_per_seq, weights, states, emit_state, use_rope, name):
    nb, L, _ = x.shape
    nc = L // CHUNK
    hpc_blocks = CHUNK // HALO
    has_state = states is not None
    last = nb - 1

    def chunk_of(b, s):
        return jnp.where(b > last, nc - 1, jnp.where(s < nc, s, s - nc))

    def halo_chunk(b, s):
        return jnp.where(b > last, nc - 1, jnp.minimum(s, nc - 1))

    def seq_of(b):
        return jnp.minimum(b, last)

    def lag_seq(b, s):
        return jnp.minimum(jnp.where(s > nc, b, jnp.maximum(b - 1, 0)), last)

    def out_map(b, s):
        live = (s > nc) & (b < nb)
        idle_chunk = jnp.where(b == 0, 0, nc - 1)
        return (jnp.where(live, b, jnp.maximum(b - 1, 0)), jnp.where(live, s - nc - 1, idle_chunk), 0)

    if mod_per_seq:
        mod_map = lambda b, s: (seq_of(b), 0, 0)
        modf_map = lambda b, s: (lag_seq(b, s), 0, 0)
    else:
        mod_map = modf_map = lambda b, s: (0, 0, 0)

    in_specs = [
        pl.BlockSpec((1, CHUNK, D_MODEL), lambda b, s: (seq_of(b), chunk_of(b, s), 0)),
        pl.BlockSpec((1, HALO, D_MODEL),
                     lambda b, s: (seq_of(b), jnp.maximum(halo_chunk(b, s) * hpc_blocks - 1, 0), 0)),
        pl.BlockSpec((1, HALO, D_MODEL),
                     lambda b, s: (seq_of(b), jnp.minimum((halo_chunk(b, s) + 1) * hpc_blocks, nc * hpc_blocks - 1), 0)),
        pl.BlockSpec((1, 6, D_MODEL), mod_map),
        pl.BlockSpec((1, 6, D_MODEL), modf_map),
    ] + [pl.BlockSpec(memory_space=pl.ANY) if i in BIG_WEIGHTS else _const_spec(w.shape)
         for i, w in enumerate(weights)]
    args = [x, x, x, mod, mod] + list(weights)
    ret_block = (1, 1, 2, RET_HEADS, RET_HD, RET_HD)
    ssd_block = (1, 1, 2, SSD_HEADS, SSD_HD, SSD_STATE)
    state_map = lambda b, s: (seq_of(b), 0, 0, 0, 0, 0)
    if has_state:
        in_specs += [pl.BlockSpec(ret_block, state_map, pipeline_mode=pl.Buffered(1)),
                     pl.BlockSpec(ssd_block, state_map, pipeline_mode=pl.Buffered(1))]
        args += list(states)
    out_shape = [jax.ShapeDtypeStruct((nb, L, D_MODEL), F32)]
    out_specs = [pl.BlockSpec((1, CHUNK, D_MODEL), out_map)]
    if emit_state:
        out_shape += [jax.ShapeDtypeStruct((nb,) + ret_block[1:], F32),
                      jax.ShapeDtypeStruct((nb,) + ssd_block[1:], F32)]
        out_specs += [pl.BlockSpec(ret_block, state_map), pl.BlockSpec(ssd_block, state_map)]
    state_dt = F32 if emit_state else BF16
    scratch = [
        pltpu.VMEM((nc, CHUNK, RET_W), BF16),
        pltpu.VMEM((nc, RET_W, CHUNK), BF16),
        pltpu.VMEM((nc, CHUNK, RET_W), BF16),
        pltpu.VMEM((nc, CHUNK, RET_W), BF16),
        pltpu.VMEM((nc, CHUNK, SSD_W), BF16),
        pltpu.VMEM((CHUNK + 2 * HALO, CONV_CH), F32),
        pltpu.VMEM((nc, CHUNK, SSD_W), F32),
        pltpu.VMEM((nc, SSD_GROUPS * SSD_STATE, CHUNK), BF16),
        pltpu.VMEM((nc, CHUNK, SSD_GROUPS * SSD_STATE), BF16),
        pltpu.VMEM((nc, CHUNK, LANES), F32),
        pltpu.VMEM((nc, CHUNK, LANES), F32),
        pltpu.VMEM((nc, CHUNK, LANES), F32),
        pltpu.VMEM((nc, 8, LANES), F32),
        pltpu.VMEM((nc, N_DT, CHUNK), F32),
        pltpu.VMEM((CHUNK, SSD_W), F32),
        pltpu.VMEM((CHUNK, 2 * RET_W), BF16),
        pltpu.VMEM((nc, RET_HEADS, RET_HD, 2 * RET_HD), state_dt),
        pltpu.VMEM((nc, SSD_GROUPS, SSD_STATE, 2 * GROUP_W), state_dt),
        pltpu.VMEM((nc, RET_HEADS, RET_HD, 2 * RET_HD), BF16),
        pltpu.VMEM((nc, SSD_GROUPS, SSD_STATE, 2 * GROUP_W), BF16),
        pltpu.VMEM((RET_HEADS, CHUNK, CHUNK), F32),
        pltpu.VMEM((RET_HEADS, 4, CHUNK, LANES), F32),
        pltpu.VMEM((2, CHUNK, D_MODEL), F32),
        pltpu.VMEM((CHUNK, D_FF), BF16),
        pltpu.VMEM((LANES, D_MODEL), BF16),
    ]
    scratch += [pltpu.VMEM(weights[i].shape, BF16) for i in BIG_WEIGHTS]
    scratch += [pltpu.SemaphoreType.DMA((len(BIG_WEIGHTS),))]
    if use_rope:
        scratch += [pltpu.VMEM((nc, CHUNK, LANES), F32)] * 2
    kern = functools.partial(_layer_kernel, L=L, nb=nb, has_state=has_state, use_rope=use_rope,
                             emit_state=emit_state)
    return pl.pallas_call(
        kern,
        grid=(nb + 1, 2 * nc),
        in_specs=in_specs,
        out_specs=out_specs,
        out_shape=out_shape,
        scratch_shapes=scratch,
        compiler_params=pltpu.CompilerParams(dimension_semantics=("arbitrary", "arbitrary"),
                                             vmem_limit_bytes=VMEM_LIMIT),
        name=name,
    )(*args)


def kernel(x_prompt, x_sample, state_ret, state_ssd, c, c_ctx, w_in, ret_decay_fwd, ret_decay_bwd, conv_w, conv_b, dt_bias_fwd, dt_bias_bwd, a_log_fwd, a_log_bwd, d_skip, ssd_norm_w, w_out, ln1_g, ln1_b, w_gate, w_up, w_down, ln2_g, ln2_b, w_ada, b_ada):
    depth = w_in.shape[0]
    assert depth == 1, "single trunk layer"
    bp, lp, _ = x_prompt.shape
    bs, ls, _ = x_sample.shape
    assert lp % CHUNK == 0 and ls % CHUNK == 0 and ls % GRID_W == 0 and D_FF % FF_BLK == 0

    rows = -(-(bs + 1) // 8) * 8
    cond = jnp.zeros((rows, D_MODEL), F32).at[:bs].set(c).at[bs].set(c_ctx)
    mod = _ada_call(cond, w_ada[0], b_ada[0][None, :]).reshape(rows, 6, D_MODEL)
    mod_lat = mod[:bs]
    mod_ctx = mod[bs:bs + 1]

    convw = jnp.zeros((8, CONV_CH), F32).at[:CONV_W].set(conv_w[0])
    convb = conv_b[0][None, :]
    dt_bias = jnp.concatenate([dt_bias_fwd[0], dt_bias_bwd[0]])
    a_log = jnp.concatenate([a_log_fwd[0], a_log_bwd[0]])
    ret_decay = jnp.concatenate([ret_decay_fwd[0], ret_decay_bwd[0]])
    hp = jnp.zeros((8, LANES), F32)
    hp = hp.at[0, :N_DT].set(dt_bias).at[1, :N_DT].set(a_log)
    hp = hp.at[2, :2 * RET_HEADS].set(ret_decay).at[3, :SSD_HEADS].set(d_skip[0])
    hpc = jnp.zeros((N_DT, LANES), F32).at[:, 0].set(dt_bias).at[:, 1].set(a_log)
    w_in_b, w_out_b, w_gate_b, w_up_b, w_down_b = _cast_call(
        jnp.swapaxes(w_in[0], 0, 1), [w_out[0], w_gate[0], w_up[0], w_down[0]])
    weights = (w_in_b, convw, convb, hp, hpc, ssd_norm_w[0][None, :],
               w_out_b, ln1_g[0][None, :], ln1_b[0][None, :],
               w_gate_b, w_up_b, w_down_b, ln2_g[0][None, :], ln2_b[0][None, :])

    yp, new_ret, new_ssd = _layer_call(x_prompt, mod_ctx, False, weights, None, True, False, "layer_ctx")
    states = (state_ret, jnp.swapaxes(state_ssd, -1, -2))
    (ys,) = _layer_call(x_sample, mod_lat, True, weights, states, False, True, "layer_lat")
    return (yp, ys, new_ret, jnp.swapaxes(new_ssd, -1, -2))
```

```python
import functools
import math

import jax
import jax.numpy as jnp
from jax import lax
from jax.experimental import pallas as pl
from jax.experimental.pallas import tpu as pltpu

F32 = jnp.float32
BF16 = jnp.bfloat16

D_MODEL = 1024
RET_W = 512
RET_HEADS = 4
RET_HD = 128
SSD_W = 512
SSD_HD = 64
SSD_HEADS = 8
SSD_GROUPS = 2
SSD_STATE = 128
HPG = SSD_HEADS // SSD_GROUPS
GROUP_W = HPG * SSD_HD
CONV_W = 5
CONV_CH = SSD_W + 2 * SSD_GROUPS * SSD_STATE
D_FF = 2816
GRID_W = 64
GRID_SHIFT = 6
ROPE_BASE = 10000.0
EPS = 1e-6
ALPHA = 2.0 ** 0.25
MAIN_COLS = 4 * RET_W + SSD_W + CONV_CH
XBC_COL0 = 4 * RET_W + SSD_W
N_DT = 2 * SSD_HEADS

CHUNK = 256
HALO = 8
FF_BLK = 256
ADA_COLS = 2048
CAST_STEPS = 8
CAST_WT_ROWS = 512
LANES = 128
VMEM_LIMIT = 63 * 1024 * 1024


def _dot(a, b):
    return jnp.dot(a, b, preferred_element_type=F32)


def _dot_nt(a, b):
    return lax.dot_general(a, b, (((1,), (1,)), ((), ())), preferred_element_type=F32)


def _silu(x):
    return x * jax.nn.sigmoid(x)


def _softplus(x):
    return jnp.maximum(x, 0.0) + jnp.log1p(jnp.exp(-jnp.abs(x)))


def _layer_norm(y, g, b):
    mu = jnp.mean(y, axis=-1, keepdims=True)
    yc = y - mu
    var = jnp.mean(yc * yc, axis=-1, keepdims=True)
    return yc * lax.rsqrt(var + EPS) * g + b


def _cumsum(x, axis):
    n = x.shape[axis]
    idx = lax.broadcasted_iota(jnp.int32, x.shape, axis)
    s = 1
    while s < n:
        x = x + jnp.where(idx >= s, pltpu.roll(x, s, axis), 0.0)
        s *= 2
    return x


def _expand4(cols, lane):
    a = jnp.where(lane < SSD_HD, cols[0], cols[1])
    b = jnp.where(lane < SSD_HD, cols[2], cols[3])
    return jnp.concatenate([a, b], axis=1)


def _interleave(a, b):
    ia = ib = 0
    while ia < len(a) or ib < len(b):
        if ib >= len(b) or (ia < len(a) and ia * len(b) <= ib * len(a)):
            a[ia]()
            ia += 1
        else:
            b[ib]()
            ib += 1


def _ada_kernel(cond_ref, w_ref, b_ref, o_ref):
    s = _silu(cond_ref[...]).astype(BF16)
    o_ref[...] = _dot(s, w_ref[...].astype(BF16)) + b_ref[...]


def _ada_call(cond, w_ada, b_ada):
    rows = cond.shape[0]
    n = w_ada.shape[1]
    return pl.pallas_call(
        _ada_kernel,
        grid=(n // ADA_COLS,),
        in_specs=[
            pl.BlockSpec((rows, D_MODEL), lambda j: (0, 0)),
            pl.BlockSpec((D_MODEL, ADA_COLS), lambda j: (0, j)),
            pl.BlockSpec((1, ADA_COLS), lambda j: (0, j)),
        ],
        out_specs=pl.BlockSpec((rows, ADA_COLS), lambda j: (0, j)),
        out_shape=jax.ShapeDtypeStruct((rows, n), F32),
        compiler_params=pltpu.CompilerParams(dimension_semantics=("arbitrary",)),
        name="adaln_mod",
    )(cond, w_ada, b_ada)


def _cast_kernel(wt_ref, *refs, wt_rows):
    blk = wt_ref.shape[0]
    row = lax.broadcasted_iota(jnp.int32, (blk, 1), 0) + pl.program_id(0) * blk
    n = len(refs) // 2
    refs[n][...] = jnp.where(row < wt_rows, wt_ref[...], 0.0).T.astype(BF16)
    for src, dst in zip(refs[:n], refs[n + 1:]):
        dst[...] = src[...].astype(BF16)


def _cast_call(wt, ws):
    for w in ws:
        assert w.shape[0] % (CAST_STEPS * 16) == 0, w.shape
    assert wt.shape[0] <= CAST_STEPS * CAST_WT_ROWS
    specs = [pl.BlockSpec((w.shape[0] // CAST_STEPS, w.shape[1]), lambda i: (i, 0)) for w in ws]
    return pl.pallas_call(
        functools.partial(_cast_kernel, wt_rows=wt.shape[0]),
        grid=(CAST_STEPS,),
        in_specs=[pl.BlockSpec((CAST_WT_ROWS, wt.shape[1]), lambda i: (i, 0))] + specs,
        out_specs=[pl.BlockSpec((wt.shape[1], CAST_WT_ROWS), lambda i: (0, i))] + specs,
        out_shape=[jax.ShapeDtypeStruct((wt.shape[1], -(-wt.shape[0] // LANES) * LANES), BF16)]
        + [jax.ShapeDtypeStruct(w.shape, BF16) for w in ws],
        compiler_params=pltpu.CompilerParams(dimension_semantics=("arbitrary",),
                                             vmem_limit_bytes=VMEM_LIMIT),
        name="weights_to_bf16",
    )(wt, *ws)


RV_TAIL_F, RV_TAIL_B, RV_CROSS_F, RV_CROSS_B = range(4)
BIG_WEIGHTS = (0, 6, 9, 10, 11)


def _layer_kernel(*refs, L, nb, has_state, use_rope, emit_state):
    nc = L // CHUNK
    C = CHUNK
    cross = has_state or nc > 1
    it = iter(refs)
    x_ref, xprev_ref, xnext_ref, mod_ref, modf_ref, wmain_hbm = (next(it) for _ in range(6))
    convw_ref, convb_ref, hp_ref, hpc_ref, normw_ref = (next(it) for _ in range(5))
    wout_hbm, ln1g_ref, ln1b_ref = (next(it) for _ in range(3))
    wg_hbm, wu_hbm, wd_hbm, ln2g_ref, ln2b_ref = (next(it) for _ in range(5))
    if has_state:
        sret0_ref, sssd0_ref = next(it), next(it)
    out_ref = next(it)
    if emit_state:
        nret_ref, nssd_ref = next(it), next(it)
    (q_s, kT_s, v_s, g_s, z_s, stage_s, xs_s, bT_s, c_s, inccol_s, exccol_s, cfcb_s, dec_s,
     rowarg_s, y_s, mix_s, rloc_s, sloc_s, rent_s, sent_s, wdec_s, rvec_s, x1_s, hid_s, wdtr_s) = (
         next(it) for _ in range(25))
    wmain_ref, wout_ref, wg_ref, wu_ref, wd_ref, wsem = (next(it) for _ in range(6))
    if use_rope:
        cos_s, sin_s = next(it), next(it)

    seq = pl.program_id(0)
    step = pl.program_id(1)

    big_weights = ((wmain_hbm, wmain_ref), (wout_hbm, wout_ref), (wg_hbm, wg_ref), (wu_hbm, wu_ref),
                   (wd_hbm, wd_ref))

    def weight_copy(i):
        return pltpu.make_async_copy(big_weights[i][0], big_weights[i][1], wsem.at[i])

    hp = hp_ref[...]
    dt_bias_row = hp[0:1, :]
    nega_row = -jnp.exp(hp[1:2, :])
    lg_row = -_softplus(-hp[2:3, :])
    dskip_row = hp[3:4, :]
    hpc = hpc_ref[...]
    dt_bias_col = hpc[:, 0:1]
    nega_col = -jnp.exp(hpc[:, 1:2])

    lane = lax.broadcasted_iota(jnp.int32, (1, LANES), 1)

    def ret_decays(hd):
        lgf = lg_row[:, hd:hd + 1]
        lgb = lg_row[:, RET_HEADS + hd:RET_HEADS + hd + 1]
        return lgf, lgb

    def group_heads(gi):
        return [gi * HPG + k for k in range(HPG)]

    def expand_f(arr, gi):
        return _expand4([arr[:, hh:hh + 1] for hh in group_heads(gi)], lane)

    def expand_b(arr, gi):
        return _expand4([arr[:, SSD_HEADS + hh:SSD_HEADS + hh + 1] for hh in group_heads(gi)], lane)

    @pl.when((seq == 0) & (step == 0))
    def _():
        for i in range(len(big_weights)):
            weight_copy(i).start()
        weight_copy(0).wait()
        ii = lax.broadcasted_iota(jnp.int32, (C, C), 0)
        jj = lax.broadcasted_iota(jnp.int32, (C, C), 1)
        dmat = (ii - jj).astype(F32)
        irow = lax.broadcasted_iota(jnp.int32, (C, LANES), 0).astype(F32)
        for hd in range(RET_HEADS):
            lgf, lgb = ret_decays(hd)
            wdec_s[hd] = jnp.exp(jnp.where(jj <= ii, dmat * lgf, -dmat * lgb))
            rvec_s[hd, RV_TAIL_F] = jnp.exp((C - 1.0 - irow) * lgf)
            rvec_s[hd, RV_TAIL_B] = jnp.exp(irow * lgb)
            rvec_s[hd, RV_CROSS_F] = jnp.exp((irow + 1.0) * lgf)
            rvec_s[hd, RV_CROSS_B] = jnp.exp((C - irow) * lgb)
        wdtr_s[...] = wmain_ref[:, MAIN_COLS:MAIN_COLS + LANES].astype(F32).T.astype(BF16)
        if use_rope:
            ln = lax.broadcasted_iota(jnp.int32, (C, LANES), 1)
            nf = RET_HD // 4
            inv = jnp.exp((ln & (nf - 1)).astype(F32) * (-math.log(ROPE_BASE) / nf))
            for cc in range(nc):
                t = lax.broadcasted_iota(jnp.int32, (C, LANES), 0) + cc * C
                pos = jnp.where((ln & (2 * nf - 1)) < nf, t >> GRID_SHIFT, t & (GRID_W - 1)).astype(F32)
                ang = pos * inv
                cos_s[cc] = jnp.cos(ang)
                sin_s[cc] = jnp.where(ln < RET_HD // 2, -jnp.sin(ang), jnp.sin(ang))

    def project(c):
        st = {}
        CB = 256
        assert CONV_W == 5 and HALO >= CONV_W // 2

        def v_mod():
            sh1 = mod_ref[0, 0:1, :]
            sc1 = mod_ref[0, 1:2, :]
            xe = jnp.concatenate([xprev_ref[0], x_ref[0], xnext_ref[0]], axis=0)
            xm = xe * (1.0 + sc1) + sh1
            st["he"] = xm.astype(BF16)
            st["h"] = xm[HALO:HALO + C].astype(BF16)

        def m_main(name, lo, hi, halo=False):
            def run():
                st[name] = _dot(st["he" if halo else "h"], wmain_ref[:, lo:hi])
            return run

        def m_dt():
            raw_c = _dot(st["h"], wmain_ref[:, MAIN_COLS:MAIN_COLS + LANES])
            st["dtc"] = raw_c + pltpu.roll(raw_c, SSD_HEADS, 1)
            raw_r = _dot_nt(wdtr_s[0:N_DT, :], st["h"])
            st["dtr"] = raw_r + pltpu.roll(raw_r, SSD_HEADS, 0)

        def v_dt():
            dt_c = _softplus(st["dtc"] + dt_bias_row)
            lac = dt_c * nega_row
            inc_col = _cumsum(lac, 0)
            exc_col = inc_col - lac
            tot_col = inc_col[C - 1:C, :]
            inccol_s[c] = inc_col
            exccol_s[c] = exc_col
            dec_s[c] = jnp.broadcast_to(jnp.exp(tot_col), (8, LANES))
            if cross:
                cfcb_s[c] = jnp.exp(jnp.where(lane < SSD_HEADS, inc_col, tot_col - exc_col))
            st["sf"] = jnp.exp(tot_col - inc_col) * dt_c
            st["sb"] = jnp.exp(exc_col) * dt_c
            dt_r = _softplus(st["dtr"] + dt_bias_col)
            lar = dt_r * nega_col
            inc_row = _cumsum(lar, 1)
            ldt = jnp.log(dt_r)
            rid = lax.broadcasted_iota(jnp.int32, (N_DT, C), 0)
            rowarg_s[c] = jnp.where(rid < SSD_HEADS, inc_row - ldt, inc_row - lar + ldt)

        def v_stage(hf):
            def run():
                pe = st["pe%d" % hf]
                cs = slice(hf * 512, (hf + 1) * 512)
                stage_s[0:HALO, cs] = jnp.where(c > 0, pe[0:HALO], 0.0)
                stage_s[HALO:HALO + C, cs] = pe[HALO:HALO + C]
                stage_s[HALO + C:, cs] = jnp.where(c < nc - 1, pe[HALO + C:], 0.0)
            return run

        def conv_block(cb):
            cs = slice(cb * CB, (cb + 1) * CB)
            rows = C + 2 * HALO
            xin = stage_s[:, cs]
            taps = [convw_ref[k:k + 1, cs] * xin for k in range(CONV_W)]
            up = lambda a: pltpu.roll(a, rows - 1, 0)
            down = lambda a: pltpu.roll(a, 1, 0)
            acc = taps[2] + up(taps[3] + up(taps[4])) + down(taps[1] + down(taps[0]))
            return _silu(acc[HALO:HALO + C] + convb_ref[0:1, cs])

        def v_conv_x(cb):
            def run():
                xs_s[c, :, cb * CB:(cb + 1) * CB] = conv_block(cb)
            return run

        def v_conv_b():
            bT_s[c] = conv_block(SSD_W // CB).T.astype(BF16)

        def v_conv_c():
            c_s[c] = conv_block(SSD_W // CB + 1).astype(BF16)

        def rope(a):
            if not use_rope:
                return a
            return a * cos_s[c] + pltpu.roll(a, RET_HD // 2, 1) * sin_s[c]

        def v_q():
            for hd in range(RET_HEADS):
                sl = slice(hd * RET_HD, (hd + 1) * RET_HD)
                q_s[c, :, sl] = rope(st["pq"][:, sl]).astype(BF16)

        def v_k():
            for hd in range(RET_HEADS):
                sl = slice(hd * RET_HD, (hd + 1) * RET_HD)
                kh = rope(st["pk"][:, sl]) * (RET_HD ** -0.5)
                kT_s[c, sl, :] = kh.T.astype(BF16)

        def v_v():
            v_s[c] = st["pv"].astype(BF16)

        def v_g():
            g_s[c] = _silu(st["pg"]).astype(BF16)

        def v_z():
            z_s[c] = _silu(st["pz"]).astype(BF16)

        def m_sloc(gi):
            def run():
                xg = xs_s[c, :, gi * GROUP_W:(gi + 1) * GROUP_W]
                vcat = jnp.concatenate([xg * expand_f(st["sf"], gi), xg * expand_b(st["sb"], gi)],
                                       axis=1).astype(BF16)
                sloc_s[c, gi] = _dot(bT_s[c, gi * SSD_STATE:(gi + 1) * SSD_STATE, :], vcat).astype(sloc_s.dtype)
            return run

        def m_rloc(hd):
            def run():
                sl = slice(hd * RET_HD, (hd + 1) * RET_HD)
                vf = st["pv"][:, sl]
                vcat = jnp.concatenate([vf * rvec_s[hd, RV_TAIL_F], vf * rvec_s[hd, RV_TAIL_B]],
                                       axis=1).astype(BF16)
                rloc_s[c, hd] = _dot(kT_s[c, sl, :], vcat).astype(rloc_s.dtype)
            return run

        m_pe0 = m_main("pe0", XBC_COL0, XBC_COL0 + 512, halo=True)
        m_pe1 = m_main("pe1", XBC_COL0 + 512, MAIN_COLS, halo=True)
        m_q = m_main("pq", 0, RET_W)
        m_k = m_main("pk", RET_W, 2 * RET_W)
        m_v = m_main("pv", 2 * RET_W, 3 * RET_W)
        m_g = m_main("pg", 3 * RET_W, 4 * RET_W)
        m_z = m_main("pz", 4 * RET_W, XBC_COL0)
        order = [v_mod, m_dt, m_pe0, m_pe1, v_dt, m_q, v_stage(0), v_stage(1), m_k, v_conv_x(0), m_v,
                 v_conv_x(1), v_q, m_g, v_conv_b, v_k, m_z, v_conv_c, v_v,
                 m_rloc(0), m_rloc(1), v_g, m_rloc(2), m_rloc(3), m_sloc(0), m_sloc(1), v_z]
        for piece in order:
            piece()

    def recurrences():
        for hd in range(RET_HEADS):
            lgf, lgb = ret_decays(hd)
            dec_f = jnp.exp(C * lgf)
            dec_b = jnp.exp(C * lgb)
            if has_state:
                ent_f = sret0_ref[0, 0, 0, hd]
                ent_b = sret0_ref[0, 0, 1, hd]
            else:
                ent_f = jnp.zeros((RET_HD, RET_HD), F32)
                ent_b = jnp.zeros((RET_HD, RET_HD), F32)
            for c in range(nc):
                if cross:
                    rent_s[c, hd, :, 0:RET_HD] = ent_f.astype(BF16)
                ent_f = dec_f * ent_f + rloc_s[c, hd, :, 0:RET_HD]
            for c in range(nc - 1, -1, -1):
                if cross:
                    rent_s[c, hd, :, RET_HD:] = ent_b.astype(BF16)
                ent_b = dec_b * ent_b + rloc_s[c, hd, :, RET_HD:]
            if emit_state:
                nret_ref[0, 0, 0, hd] = ent_f
                nret_ref[0, 0, 1, hd] = ent_b

        for gi in range(SSD_GROUPS):
            heads = group_heads(gi)
            if has_state:
                ent_f = jnp.concatenate([sssd0_ref[0, 0, 0, hh] for hh in heads], axis=0).T
                ent_b = jnp.concatenate([sssd0_ref[0, 0, 1, hh] for hh in heads], axis=0).T
            else:
                ent_f = jnp.zeros((SSD_STATE, GROUP_W), F32)
                ent_b = jnp.zeros((SSD_STATE, GROUP_W), F32)
            for c in range(nc):
                if cross:
                    sent_s[c, gi, :, 0:GROUP_W] = ent_f.astype(BF16)
                ent_f = expand_f(dec_s[c, 0:1, :], gi) * ent_f + sloc_s[c, gi, :, 0:GROUP_W]
            for c in range(nc - 1, -1, -1):
                if cross:
                    sent_s[c, gi, :, GROUP_W:] = ent_b.astype(BF16)
                ent_b = expand_b(dec_s[c, 0:1, :], gi) * ent_b + sloc_s[c, gi, :, GROUP_W:]
            if emit_state:
                ent_ft = ent_f.T
                ent_bt = ent_b.T
                for k, hh in enumerate(heads):
                    nssd_ref[0, 0, 0, hh] = ent_ft[k * SSD_HD:(k + 1) * SSD_HD, :]
                    nssd_ref[0, 0, 1, hh] = ent_bt[k * SSD_HD:(k + 1) * SSD_HD, :]

    def emit_pieces(c, slot):
        rsl = [slice(hd * RET_HD, (hd + 1) * RET_HD) for hd in range(RET_HEADS)]
        gsl = [slice(gi * SSD_STATE, (gi + 1) * SSD_STATE) for gi in range(SSD_GROUPS)]
        xsl = [slice(gi * GROUP_W, (gi + 1) * GROUP_W) for gi in range(SSD_GROUPS)]
        st = {}

        def scores():
            ii = lax.broadcasted_iota(jnp.int32, (C, C), 0)
            jj = lax.broadcasted_iota(jnp.int32, (C, C), 1)
            st["causal"] = jj <= ii
            qs = [q_s[c, :, sl] for sl in rsl]
            cms = [c_s[c, :, sl] for sl in gsl]
            st["sc_s"] = [_dot(cms[gi], bT_s[c, gsl[gi], :]) for gi in range(SSD_GROUPS)]
            st["sc_r"] = [_dot(qs[hd], kT_s[c, rsl[hd], :]) for hd in range(RET_HEADS)]
            if cross:
                st["yc_s"] = [_dot(cms[gi], sent_s[c, gi]) for gi in range(SSD_GROUPS)]
                st["yc_r"] = [_dot(qs[hd], rent_s[c, hd]) for hd in range(RET_HEADS)]
            st["inc_col"] = inccol_s[c]
            st["exc_col"] = exccol_s[c]
            st["rowarg"] = rowarg_s[c]

        def ssd_head(gi, k):
            def run():
                hh = gi * HPG + k
                hb = SSD_HEADS + hh
                arg = jnp.where(st["causal"],
                                st["inc_col"][:, hh:hh + 1] - st["rowarg"][hh:hh + 1, :],
                                st["rowarg"][hb:hb + 1, :] - st["exc_col"][:, hb:hb + 1])
                m = (st["sc_s"][gi] * jnp.exp(arg)).astype(BF16)
                xh = xs_s[c, :, hh * SSD_HD:(hh + 1) * SSD_HD].astype(BF16)
                y_s[:, hh * SSD_HD:(hh + 1) * SSD_HD] = _dot(m, xh)
            return run

        def ssd_group(gi):
            def run():
                xg = xs_s[c, :, xsl[gi]]
                yg = y_s[:, xsl[gi]] + expand_f(dskip_row, gi) * xg
                if cross:
                    cfcb = cfcb_s[c]
                    yc = st["yc_s"][gi]
                    yg = yg + expand_f(cfcb, gi) * yc[:, 0:GROUP_W] + expand_b(cfcb, gi) * yc[:, GROUP_W:]
                y_s[:, xsl[gi]] = yg
            return run

        def ret_head(hd):
            def run():
                m = (st["sc_r"][hd] * wdec_s[hd]).astype(BF16)
                o = _dot(m, v_s[c, :, rsl[hd]])
                if cross:
                    yc = st["yc_r"][hd]
                    o = o + rvec_s[hd, RV_CROSS_F] * yc[:, 0:RET_HD] + rvec_s[hd, RV_CROSS_B] * yc[:, RET_HD:]
                o = o * lax.rsqrt(jnp.mean(o * o, axis=-1, keepdims=True) + EPS)
                mix_s[:, rsl[hd]] = (g_s[c, :, rsl[hd]].astype(F32) * o).astype(BF16)
            return run

        def ssd_norm():
            yz = y_s[...] * z_s[c].astype(F32)
            yn = yz * lax.rsqrt(jnp.mean(yz * yz, axis=-1, keepdims=True) + EPS) * normw_ref[...]
            mix_s[:, RET_W:] = yn.astype(BF16)

        def out_proj():
            g1 = mod_ref[0, 2:3, :]
            y = ALPHA * x_ref[0] + g1 * _dot(mix_s[...], wout_ref[...])
            x1_s[slot] = _layer_norm(y, ln1g_ref[...], ln1b_ref[...])

        pieces = [scores]
        for gi in range(SSD_GROUPS):
            pieces += [ssd_head(gi, k) for k in range(HPG)] + [ssd_group(gi)]
        pieces += [ret_head(hd) for hd in range(RET_HEADS)] + [ssd_norm, out_proj]
        return pieces

    def ffn_pieces(slot):
        st = {}

        def start():
            sh2 = modf_ref[0, 3:4, :]
            sc2 = modf_ref[0, 4:5, :]
            st["h2"] = (x1_s[slot] * (1.0 + sc2) + sh2).astype(BF16)

        def hidden(j):
            def run():
                js = slice(j * FF_BLK, (j + 1) * FF_BLK)
                h2 = st["h2"]
                hid_s[:, js] = (_silu(_dot(h2, wg_ref[:, js])) * _dot(h2, wu_ref[:, js])).astype(BF16)
            return run

        def finish():
            g2 = modf_ref[0, 5:6, :]
            y = ALPHA * x1_s[slot] + g2 * _dot(hid_s[...], wd_ref[...])
            out_ref[0] = _layer_norm(y, ln2g_ref[...], ln2b_ref[...])

        return [start] + [hidden(j) for j in range(D_FF // FF_BLK)] + [finish]

    def run_all(pieces):
        for p in pieces:
            p()

    @pl.when((step < nc) & (seq < nb))
    def _():
        project(step)

    kk = step - nc
    slot = (seq * nc + kk) & 1

    @pl.when((seq == 0) & (step == nc))
    def _():
        for i in range(1, len(big_weights)):
            weight_copy(i).wait()
        run_all([recurrences] + emit_pieces(kk, slot))

    @pl.when((seq > 0) & (seq < nb) & (step == nc))
    def _():
        _interleave(ffn_pieces(1 - slot), [recurrences] + emit_pieces(kk, slot))

    if nc > 1:
        @pl.when((step > nc) & (seq < nb))
        def _():
            _interleave(ffn_pieces(1 - slot), emit_pieces(kk, slot))

    @pl.when((seq == nb) & (step == nc))
    def _():
        run_all(ffn_pieces((nb * nc - 1) & 1))


def _const_spec(shape):
    nd = len(shape)
    return pl.BlockSpec(shape, lambda b, s: (0,) * nd, pipeline_mode=pl.Buffered(1))


def _layer_call(x, mod, mod_per_seq, weights, states, emit_state, use_rope, name):
    nb, L, _ = x.shape
    nc = L // CHUNK
    hpc_blocks = CHUNK // HALO
    has_state = states is not None
    last = nb - 1

    def chunk_of(b, s):
        return jnp.where(b > last, nc - 1, jnp.where(s < nc, s, s - nc))

    def halo_chunk(b, s):
        return jnp.where(b > last, nc - 1, jnp.minimum(s, nc - 1))

    def seq_of(b):
        return jnp.minimum(b, last)

    def lag_seq(b, s):
        return jnp.minimum(jnp.where(s > nc, b, jnp.maximum(b - 1, 0)), last)

    def out_map(b, s):
        live = (s > nc) & (b < nb)
        idle_chunk = jnp.where(b == 0, 0, nc - 1)
        return (jnp.where(live, b, jnp.maximum(b - 1, 0)), jnp.where(live, s - nc - 1, idle_chunk), 0)

    if mod_per_seq:
        mod_map = lambda b, s: (seq_of(b), 0, 0)
        modf_map = lambda b, s: (lag_seq(b, s), 0, 0)
    else:
        mod_map = modf_map = lambda b, s: (0, 0, 0)

    in_specs = [
        pl.BlockSpec((1, CHUNK, D_MODEL), lambda b, s: (seq_of(b), chunk_of(b, s), 0)),
        pl.BlockSpec((1, HALO, D_MODEL),
                     lambda b, s: (seq_of(b), jnp.maximum(halo_chunk(b, s) * hpc_blocks - 1, 0), 0)),
        pl.BlockSpec((1, HALO, D_MODEL),
                     lambda b, s: (seq_of(b), jnp.minimum((halo_chunk(b, s) + 1) * hpc_blocks, nc * hpc_blocks - 1), 0)),
        pl.BlockSpec((1, 6, D_MODEL), mod_map),
        pl.BlockSpec((1, 6, D_MODEL), modf_map),
    ] + [pl.BlockSpec(memory_space=pl.ANY) if i in BIG_WEIGHTS else _const_spec(w.shape)
         for i, w in enumerate(weights)]
    args = [x, x, x, mod, mod] + list(weights)
    ret_block = (1, 1, 2, RET_HEADS, RET_HD, RET_HD)
    ssd_block = (1, 1, 2, SSD_HEADS, SSD_HD, SSD_STATE)
    state_map = lambda b, s: (seq_of(b), 0, 0, 0, 0, 0)
    if has_state:
        in_specs += [pl.BlockSpec(ret_block, state_map), pl.BlockSpec(ssd_block, state_map)]
        args += list(states)
    out_shape = [jax.ShapeDtypeStruct((nb, L, D_MODEL), F32)]
    out_specs = [pl.BlockSpec((1, CHUNK, D_MODEL), out_map)]
    if emit_state:
        out_shape += [jax.ShapeDtypeStruct((nb,) + ret_block[1:], F32),
                      jax.ShapeDtypeStruct((nb,) + ssd_block[1:], F32)]
        out_specs += [pl.BlockSpec(ret_block, state_map), pl.BlockSpec(ssd_block, state_map)]
    state_dt = F32 if emit_state else BF16
    scratch = [
        pltpu.VMEM((nc, CHUNK, RET_W), BF16),
        pltpu.VMEM((nc, RET_W, CHUNK), BF16),
        pltpu.VMEM((nc, CHUNK, RET_W), BF16),
        pltpu.VMEM((nc, CHUNK, RET_W), BF16),
        pltpu.VMEM((nc, CHUNK, SSD_W), BF16),
        pltpu.VMEM((CHUNK + 2 * HALO, CONV_CH), F32),
        pltpu.VMEM((nc, CHUNK, SSD_W), F32),
        pltpu.VMEM((nc, SSD_GROUPS * SSD_STATE, CHUNK), BF16),
        pltpu.VMEM((nc, CHUNK, SSD_GROUPS * SSD_STATE), BF16),
        pltpu.VMEM((nc, CHUNK, LANES), F32),
        pltpu.VMEM((nc, CHUNK, LANES), F32),
        pltpu.VMEM((nc, CHUNK, LANES), F32),
        pltpu.VMEM((nc, 8, LANES), F32),
        pltpu.VMEM((nc, N_DT, CHUNK), F32),
        pltpu.VMEM((CHUNK, SSD_W), F32),
        pltpu.VMEM((CHUNK, 2 * RET_W), BF16),
        pltpu.VMEM((nc, RET_HEADS, RET_HD, 2 * RET_HD), state_dt),
        pltpu.VMEM((nc, SSD_GROUPS, SSD_STATE, 2 * GROUP_W), state_dt),
        pltpu.VMEM((nc, RET_HEADS, RET_HD, 2 * RET_HD), BF16),
        pltpu.VMEM((nc, SSD_GROUPS, SSD_STATE, 2 * GROUP_W), BF16),
        pltpu.VMEM((RET_HEADS, CHUNK, CHUNK), F32),
        pltpu.VMEM((RET_HEADS, 4, CHUNK, LANES), F32),
        pltpu.VMEM((2, CHUNK, D_MODEL), F32),
        pltpu.VMEM((CHUNK, D_FF), BF16),
        pltpu.VMEM((LANES, D_MODEL), BF16),
    ]
    scratch += [pltpu.VMEM(weights[i].shape, BF16) for i in BIG_WEIGHTS]
    scratch += [pltpu.SemaphoreType.DMA((len(BIG_WEIGHTS),))]
    if use_rope:
        scratch += [pltpu.VMEM((nc, CHUNK, LANES), F32)] * 2
    kern = functools.partial(_layer_kernel, L=L, nb=nb, has_state=has_state, use_rope=use_rope,
                             emit_state=emit_state)
    return pl.pallas_call(
        kern,
        grid=(nb + 1, 2 * nc),
        in_specs=in_specs,
        out_specs=out_specs,
        out_shape=out_shape,
        scratch_shapes=scratch,
        compiler_params=pltpu.CompilerParams(dimension_semantics=("arbitrary", "arbitrary"),
                                             vmem_limit_bytes=VMEM_LIMIT),
        name=name,
    )(*args)


def kernel(x_prompt, x_sample, state_ret, state_ssd, c, c_ctx, w_in, ret_decay_fwd, ret_decay_bwd, conv_w, conv_b, dt_bias_fwd, dt_bias_bwd, a_log_fwd, a_log_bwd, d_skip, ssd_norm_w, w_out, ln1_g, ln1_b, w_gate, w_up, w_down, ln2_g, ln2_b, w_ada, b_ada):
    depth = w_in.shape[0]
    assert depth == 1, "single trunk layer"
    bp, lp, _ = x_prompt.shape
    bs, ls, _ = x_sample.shape
    assert lp % CHUNK == 0 and ls % CHUNK == 0 and ls % GRID_W == 0 and D_FF % FF_BLK == 0

    rows = -(-(bs + 1) // 8) * 8
    cond = jnp.zeros((rows, D_MODEL), F32).at[:bs].set(c).at[bs].set(c_ctx)
    mod = _ada_call(cond, w_ada[0], b_ada[0][None, :]).reshape(rows, 6, D_MODEL)
    mod_lat = mod[:bs]
    mod_ctx = mod[bs:bs + 1]

    convw = jnp.zeros((8, CONV_CH), F32).at[:CONV_W].set(conv_w[0])
    convb = conv_b[0][None, :]
    dt_bias = jnp.concatenate([dt_bias_fwd[0], dt_bias_bwd[0]])
    a_log = jnp.concatenate([a_log_fwd[0], a_log_bwd[0]])
    ret_decay = jnp.concatenate([ret_decay_fwd[0], ret_decay_bwd[0]])
    hp = jnp.zeros((8, LANES), F32)
    hp = hp.at[0, :N_DT].set(dt_bias).at[1, :N_DT].set(a_log)
    hp = hp.at[2, :2 * RET_HEADS].set(ret_decay).at[3, :SSD_HEADS].set(d_skip[0])
    hpc = jnp.zeros((N_DT, LANES), F32).at[:, 0].set(dt_bias).at[:, 1].set(a_log)
    w_in_b, w_out_b, w_gate_b, w_up_b, w_down_b = _cast_call(
        jnp.swapaxes(w_in[0], 0, 1), [w_out[0], w_gate[0], w_up[0], w_down[0]])
    weights = (w_in_b, convw, convb, hp, hpc, ssd_norm_w[0][None, :],
               w_out_b, ln1_g[0][None, :], ln1_b[0][None, :],
               w_gate_b, w_up_b, w_down_b, ln2_g[0][None, :], ln2_b[0][None, :])

    yp, new_ret, new_ssd = _layer_call(x_prompt, mod_ctx, False, weights, None, True, False, "layer_ctx")
    states = (state_ret, jnp.swapaxes(state_ssd, -1, -2))
    (ys,) = _layer_call(x_sample, mod_lat, True, weights, states, False, True, "layer_lat")
    return (yp, ys, new_ret, jnp.swapaxes(new_ssd, -1, -2))
```

```python
import functools
import math

import jax
import jax.numpy as jnp
from jax import lax
from jax.experimental import pallas as pl
from jax.experimental.pallas import tpu as pltpu

F32 = jnp.float32
BF16 = jnp.bfloat16

D_MODEL = 1024
RET_W = 512
RET_HEADS = 4
RET_HD = 128
SSD_W = 512
SSD_HD = 64
SSD_HEADS = 8
SSD_GROUPS = 2
SSD_STATE = 128
HPG = SSD_HEADS // SSD_GROUPS
GROUP_W = HPG * SSD_HD
CONV_W = 5
CONV_CH = SSD_W + 2 * SSD_GROUPS * SSD_STATE
D_FF = 2816
GRID_W = 64
GRID_SHIFT = 6
ROPE_BASE = 10000.0
EPS = 1e-6
ALPHA = 2.0 ** 0.25
MAIN_COLS = 4 * RET_W + SSD_W + CONV_CH
XBC_COL0 = 4 * RET_W + SSD_W
N_DT = 2 * SSD_HEADS

CHUNK = 256
HALO = 8
FF_BLK = 256
ADA_COLS = 2048
W_IN_BLK = 512
LANES = 128
VMEM_LIMIT = 62 * 1024 * 1024


def _dot(a, b):
    return jnp.dot(a, b, preferred_element_type=F32)


def _dot_nt(a, b):
    return lax.dot_general(a, b, (((1,), (1,)), ((), ())), preferred_element_type=F32)


def _silu(x):
    return x * jax.nn.sigmoid(x)


def _softplus(x):
    return jnp.maximum(x, 0.0) + jnp.log1p(jnp.exp(-jnp.abs(x)))


def _layer_norm(y, g, b):
    mu = jnp.mean(y, axis=-1, keepdims=True)
    yc = y - mu
    var = jnp.mean(yc * yc, axis=-1, keepdims=True)
    return yc * lax.rsqrt(var + EPS) * g + b


def _cumsum(x, axis):
    n = x.shape[axis]
    idx = lax.broadcasted_iota(jnp.int32, x.shape, axis)
    s = 1
    while s < n:
        x = x + jnp.where(idx >= s, pltpu.roll(x, s, axis), 0.0)
        s *= 2
    return x


def _expand4(cols, lane):
    a = jnp.where(lane < SSD_HD, cols[0], cols[1])
    b = jnp.where(lane < SSD_HD, cols[2], cols[3])
    return jnp.concatenate([a, b], axis=1)


def _interleave(a, b):
    ia = ib = 0
    while ia < len(a) or ib < len(b):
        if ib >= len(b) or (ia < len(a) and ia * len(b) <= ib * len(a)):
            a[ia]()
            ia += 1
        else:
            b[ib]()
            ib += 1


def _ada_kernel(cond_ref, w_ref, b_ref, o_ref):
    s = _silu(cond_ref[...]).astype(BF16)
    o_ref[...] = _dot(s, w_ref[...].astype(BF16)) + b_ref[...]


def _ada_call(cond, w_ada, b_ada):
    rows = cond.shape[0]
    n = w_ada.shape[1]
    return pl.pallas_call(
        _ada_kernel,
        grid=(n // ADA_COLS,),
        in_specs=[
            pl.BlockSpec((rows, D_MODEL), lambda j: (0, 0)),
            pl.BlockSpec((D_MODEL, ADA_COLS), lambda j: (0, j)),
            pl.BlockSpec((1, ADA_COLS), lambda j: (0, j)),
        ],
        out_specs=pl.BlockSpec((rows, ADA_COLS), lambda j: (0, j)),
        out_shape=jax.ShapeDtypeStruct((rows, n), F32),
        compiler_params=pltpu.CompilerParams(dimension_semantics=("arbitrary",)),
        name="adaln_mod",
    )(cond, w_ada, b_ada)


RV_TAIL_F, RV_TAIL_B, RV_CROSS_F, RV_CROSS_B = range(4)
BIG_WEIGHTS = (0, 6, 9, 10, 11)


def _layer_kernel(*refs, L, nb, has_state, use_rope, emit_state, cast_weights):
    nc = L // CHUNK
    C = CHUNK
    cross = has_state or nc > 1
    it = iter(refs)
    x_ref, xprev_ref, xnext_ref, mod_ref, modf_ref, wmain_hbm = (next(it) for _ in range(6))
    convw_ref, convb_ref, hp_ref, hpc_ref, normw_ref = (next(it) for _ in range(5))
    wout_hbm, ln1g_ref, ln1b_ref = (next(it) for _ in range(3))
    wg_hbm, wu_hbm, wd_hbm, ln2g_ref, ln2b_ref = (next(it) for _ in range(5))
    if has_state:
        sret0_ref, sssd0_ref = next(it), next(it)
    out_ref = next(it)
    if emit_state:
        nret_ref, nssd_ref = next(it), next(it)
    if cast_weights:
        bf16_out = [next(it) for _ in range(len(BIG_WEIGHTS))]
    (q_s, kT_s, v_s, g_s, z_s, stage_s, xs_s, bT_s, c_s, inccol_s, exccol_s, cfcb_s, dec_s,
     rowarg_s, y_s, mix_s, rloc_s, sloc_s, rent_s, sent_s, wdec_s, rvec_s, x1_s, hid_s, wdtr_s) = (
         next(it) for _ in range(25))
    wmain_ref, wout_ref, wg_ref, wu_ref, wd_ref, wsem = (next(it) for _ in range(6))
    if cast_weights:
        stage_a, stage_b, csem = (next(it) for _ in range(3))
    if use_rope:
        cos_s, sin_s = next(it), next(it)

    seq = pl.program_id(0)
    step = pl.program_id(1)

    big_weights = ((wmain_hbm, wmain_ref), (wout_hbm, wout_ref), (wg_hbm, wg_ref), (wu_hbm, wu_ref),
                   (wd_hbm, wd_ref))

    def weight_copy(i):
        if cast_weights:
            return pltpu.make_async_copy(big_weights[i][1], bf16_out[i], wsem.at[i])
        return pltpu.make_async_copy(big_weights[i][0], big_weights[i][1], wsem.at[i])

    def stream_cast(src, rows, stage, put):
        n = src.shape[0] // rows

        def block_copy(i, slot):
            return pltpu.make_async_copy(src.at[pl.ds(i * rows, rows), :],
                                         stage.at[slot, pl.ds(0, rows), :], csem.at[slot])

        block_copy(0, 0).start()

        def body(i, carry):
            slot = i & 1

            @pl.when(i + 1 < n)
            def _():
                block_copy(i + 1, 1 - slot).start()

            block_copy(i, slot).wait()
            put(i, stage[slot, 0:rows, :])
            return carry

        lax.fori_loop(0, n, body, 0)
        return n

    def cast_all_weights():
        def put_in(i, blk):
            wmain_ref[i] = blk.T.astype(BF16)

        n_full = stream_cast(wmain_hbm, W_IN_BLK, stage_a, put_in)
        tail = wmain_hbm.shape[0] - n_full * W_IN_BLK
        slot = n_full & 1
        stage_a[slot] = jnp.zeros(stage_a.shape[1:], F32)
        tail_copy = pltpu.make_async_copy(wmain_hbm.at[pl.ds(n_full * W_IN_BLK, tail), :],
                                          stage_a.at[slot, pl.ds(0, tail), :], csem.at[slot])
        tail_copy.start()
        tail_copy.wait()
        wmain_ref[n_full] = stage_a[slot].T.astype(BF16)

        def put_rows(dst, rows):
            def put(i, blk):
                dst[pl.ds(pl.multiple_of(i * rows, 16), rows), :] = blk.astype(BF16)
            return put

        stream_cast(wout_hbm, 512, stage_a, put_rows(wout_ref, 512))
        stream_cast(wg_hbm, 128, stage_b, put_rows(wg_ref, 128))
        stream_cast(wu_hbm, 128, stage_b, put_rows(wu_ref, 128))
        stream_cast(wd_hbm, 352, stage_a, put_rows(wd_ref, 352))
        for i in range(len(big_weights)):
            weight_copy(i).start()

    hp = hp_ref[...]
    dt_bias_row = hp[0:1, :]
    nega_row = -jnp.exp(hp[1:2, :])
    lg_row = -_softplus(-hp[2:3, :])
    dskip_row = hp[3:4, :]
    hpc = hpc_ref[...]
    dt_bias_col = hpc[:, 0:1]
    nega_col = -jnp.exp(hpc[:, 1:2])

    lane = lax.broadcasted_iota(jnp.int32, (1, LANES), 1)

    def ret_decays(hd):
        lgf = lg_row[:, hd:hd + 1]
        lgb = lg_row[:, RET_HEADS + hd:RET_HEADS + hd + 1]
        return lgf, lgb

    def group_heads(gi):
        return [gi * HPG + k for k in range(HPG)]

    def expand_f(arr, gi):
        return _expand4([arr[:, hh:hh + 1] for hh in group_heads(gi)], lane)

    def expand_b(arr, gi):
        return _expand4([arr[:, SSD_HEADS + hh:SSD_HEADS + hh + 1] for hh in group_heads(gi)], lane)

    @pl.when((seq == 0) & (step == 0))
    def _():
        if cast_weights:
            cast_all_weights()
        else:
            for i in range(len(big_weights)):
                weight_copy(i).start()
            weight_copy(0).wait()
        ii = lax.broadcasted_iota(jnp.int32, (C, C), 0)
        jj = lax.broadcasted_iota(jnp.int32, (C, C), 1)
        dmat = (ii - jj).astype(F32)
        irow = lax.broadcasted_iota(jnp.int32, (C, LANES), 0).astype(F32)
        for hd in range(RET_HEADS):
            lgf, lgb = ret_decays(hd)
            wdec_s[hd] = jnp.exp(jnp.where(jj <= ii, dmat * lgf, -dmat * lgb))
            rvec_s[hd, RV_TAIL_F] = jnp.exp((C - 1.0 - irow) * lgf)
            rvec_s[hd, RV_TAIL_B] = jnp.exp(irow * lgb)
            rvec_s[hd, RV_CROSS_F] = jnp.exp((irow + 1.0) * lgf)
            rvec_s[hd, RV_CROSS_B] = jnp.exp((C - irow) * lgb)
        wdtr_s[...] = wmain_ref[MAIN_COLS // W_IN_BLK, :, 0:LANES].astype(F32).T.astype(BF16)
        if use_rope:
            ln = lax.broadcasted_iota(jnp.int32, (C, LANES), 1)
            nf = RET_HD // 4
            inv = jnp.exp((ln & (nf - 1)).astype(F32) * (-math.log(ROPE_BASE) / nf))
            for cc in range(nc):
                t = lax.broadcasted_iota(jnp.int32, (C, LANES), 0) + cc * C
                pos = jnp.where((ln & (2 * nf - 1)) < nf, t >> GRID_SHIFT, t & (GRID_W - 1)).astype(F32)
                ang = pos * inv
                cos_s[cc] = jnp.cos(ang)
                sin_s[cc] = jnp.where(ln < RET_HD // 2, -jnp.sin(ang), jnp.sin(ang))

    def project(c):
        st = {}
        CB = 256
        assert CONV_W == 5 and HALO >= CONV_W // 2

        def v_mod():
            sh1 = mod_ref[0, 0:1, :]
            sc1 = mod_ref[0, 1:2, :]
            xe = jnp.concatenate([xprev_ref[0], x_ref[0], xnext_ref[0]], axis=0)
            xm = xe * (1.0 + sc1) + sh1
            st["he"] = xm.astype(BF16)
            st["h"] = xm[HALO:HALO + C].astype(BF16)

        def m_main(name, lo, hi, halo=False):
            def run():
                assert lo % W_IN_BLK == 0 and hi - lo == W_IN_BLK
                st[name] = _dot(st["he" if halo else "h"], wmain_ref[lo // W_IN_BLK])
            return run

        def m_dt():
            raw_c = _dot(st["h"], wmain_ref[MAIN_COLS // W_IN_BLK, :, 0:LANES])
            st["dtc"] = raw_c + pltpu.roll(raw_c, SSD_HEADS, 1)
            raw_r = _dot_nt(wdtr_s[0:N_DT, :], st["h"])
            st["dtr"] = raw_r + pltpu.roll(raw_r, SSD_HEADS, 0)

        def v_dt():
            dt_c = _softplus(st["dtc"] + dt_bias_row)
            lac = dt_c * nega_row
            inc_col = _cumsum(lac, 0)
            exc_col = inc_col - lac
            tot_col = inc_col[C - 1:C, :]
            inccol_s[c] = inc_col
            exccol_s[c] = exc_col
            dec_s[c] = jnp.broadcast_to(jnp.exp(tot_col), (8, LANES))
            if cross:
                cfcb_s[c] = jnp.exp(jnp.where(lane < SSD_HEADS, inc_col, tot_col - exc_col))
            st["sf"] = jnp.exp(tot_col - inc_col) * dt_c
            st["sb"] = jnp.exp(exc_col) * dt_c
            dt_r = _softplus(st["dtr"] + dt_bias_col)
            lar = dt_r * nega_col
            inc_row = _cumsum(lar, 1)
            ldt = jnp.log(dt_r)
            rid = lax.broadcasted_iota(jnp.int32, (N_DT, C), 0)
            rowarg_s[c] = jnp.where(rid < SSD_HEADS, inc_row - ldt, inc_row - lar + ldt)

        def v_stage(hf):
            def run():
                pe = st["pe%d" % hf]
                cs = slice(hf * 512, (hf + 1) * 512)
                stage_s[0:HALO, cs] = jnp.where(c > 0, pe[0:HALO], 0.0)
                stage_s[HALO:HALO + C, cs] = pe[HALO:HALO + C]
                stage_s[HALO + C:, cs] = jnp.where(c < nc - 1, pe[HALO + C:], 0.0)
            return run

        def conv_block(cb):
            cs = slice(cb * CB, (cb + 1) * CB)
            rows = C + 2 * HALO
            xin = stage_s[:, cs]
            taps = [convw_ref[k:k + 1, cs] * xin for k in range(CONV_W)]
            up = lambda a: pltpu.roll(a, rows - 1, 0)
            down = lambda a: pltpu.roll(a, 1, 0)
            acc = taps[2] + up(taps[3] + up(taps[4])) + down(taps[1] + down(taps[0]))
            return _silu(acc[HALO:HALO + C] + convb_ref[0:1, cs])

        def v_conv_x(cb):
            def run():
                xs_s[c, :, cb * CB:(cb + 1) * CB] = conv_block(cb)
            return run

        def v_conv_b():
            bT_s[c] = conv_block(SSD_W // CB).T.astype(BF16)

        def v_conv_c():
            c_s[c] = conv_block(SSD_W // CB + 1).astype(BF16)

        def rope(a):
            if not use_rope:
                return a
            return a * cos_s[c] + pltpu.roll(a, RET_HD // 2, 1) * sin_s[c]

        def v_q():
            for hd in range(RET_HEADS):
                sl = slice(hd * RET_HD, (hd + 1) * RET_HD)
                q_s[c, :, sl] = rope(st["pq"][:, sl]).astype(BF16)

        def v_k():
            for hd in range(RET_HEADS):
                sl = slice(hd * RET_HD, (hd + 1) * RET_HD)
                kh = rope(st["pk"][:, sl]) * (RET_HD ** -0.5)
                kT_s[c, sl, :] = kh.T.astype(BF16)

        def v_v():
            v_s[c] = st["pv"].astype(BF16)

        def v_g():
            g_s[c] = _silu(st["pg"]).astype(BF16)

        def v_z():
            z_s[c] = _silu(st["pz"]).astype(BF16)

        def m_sloc(gi):
            def run():
                xg = xs_s[c, :, gi * GROUP_W:(gi + 1) * GROUP_W]
                vcat = jnp.concatenate([xg * expand_f(st["sf"], gi), xg * expand_b(st["sb"], gi)],
                                       axis=1).astype(BF16)
                sloc_s[c, gi] = _dot(bT_s[c, gi * SSD_STATE:(gi + 1) * SSD_STATE, :], vcat).astype(sloc_s.dtype)
            return run

        def m_rloc(hd):
            def run():
                sl = slice(hd * RET_HD, (hd + 1) * RET_HD)
                vf = st["pv"][:, sl]
                vcat = jnp.concatenate([vf * rvec_s[hd, RV_TAIL_F], vf * rvec_s[hd, RV_TAIL_B]],
                                       axis=1).astype(BF16)
                rloc_s[c, hd] = _dot(kT_s[c, sl, :], vcat).astype(rloc_s.dtype)
            return run

        m_pe0 = m_main("pe0", XBC_COL0, XBC_COL0 + 512, halo=True)
        m_pe1 = m_main("pe1", XBC_COL0 + 512, MAIN_COLS, halo=True)
        m_q = m_main("pq", 0, RET_W)
        m_k = m_main("pk", RET_W, 2 * RET_W)
        m_v = m_main("pv", 2 * RET_W, 3 * RET_W)
        m_g = m_main("pg", 3 * RET_W, 4 * RET_W)
        m_z = m_main("pz", 4 * RET_W, XBC_COL0)
        order = [v_mod, m_dt, m_pe0, m_pe1, v_dt, m_q, v_stage(0), v_stage(1), m_k, v_conv_x(0), m_v,
                 v_conv_x(1), v_q, m_g, v_conv_b, v_k, m_z, v_conv_c, v_v,
                 m_rloc(0), m_rloc(1), v_g, m_rloc(2), m_rloc(3), m_sloc(0), m_sloc(1), v_z]
        for piece in order:
            piece()

    def recurrences():
        for hd in range(RET_HEADS):
            lgf, lgb = ret_decays(hd)
            dec_f = jnp.exp(C * lgf)
            dec_b = jnp.exp(C * lgb)
            if has_state:
                ent_f = sret0_ref[0, 0, 0, hd]
                ent_b = sret0_ref[0, 0, 1, hd]
            else:
                ent_f = jnp.zeros((RET_HD, RET_HD), F32)
                ent_b = jnp.zeros((RET_HD, RET_HD), F32)
            for c in range(nc):
                if cross:
                    rent_s[c, hd, :, 0:RET_HD] = ent_f.astype(BF16)
                ent_f = dec_f * ent_f + rloc_s[c, hd, :, 0:RET_HD]
            for c in range(nc - 1, -1, -1):
                if cross:
                    rent_s[c, hd, :, RET_HD:] = ent_b.astype(BF16)
                ent_b = dec_b * ent_b + rloc_s[c, hd, :, RET_HD:]
            if emit_state:
                nret_ref[0, 0, 0, hd] = ent_f
                nret_ref[0, 0, 1, hd] = ent_b

        for gi in range(SSD_GROUPS):
            heads = group_heads(gi)
            if has_state:
                ent_f = jnp.concatenate([sssd0_ref[0, 0, 0, hh] for hh in heads], axis=0).T
                ent_b = jnp.concatenate([sssd0_ref[0, 0, 1, hh] for hh in heads], axis=0).T
            else:
                ent_f = jnp.zeros((SSD_STATE, GROUP_W), F32)
                ent_b = jnp.zeros((SSD_STATE, GROUP_W), F32)
            for c in range(nc):
                if cross:
                    sent_s[c, gi, :, 0:GROUP_W] = ent_f.astype(BF16)
                ent_f = expand_f(dec_s[c, 0:1, :], gi) * ent_f + sloc_s[c, gi, :, 0:GROUP_W]
            for c in range(nc - 1, -1, -1):
                if cross:
                    sent_s[c, gi, :, GROUP_W:] = ent_b.astype(BF16)
                ent_b = expand_b(dec_s[c, 0:1, :], gi) * ent_b + sloc_s[c, gi, :, GROUP_W:]
            if emit_state:
                ent_ft = ent_f.T
                ent_bt = ent_b.T
                for k, hh in enumerate(heads):
                    nssd_ref[0, 0, 0, hh] = ent_ft[k * SSD_HD:(k + 1) * SSD_HD, :]
                    nssd_ref[0, 0, 1, hh] = ent_bt[k * SSD_HD:(k + 1) * SSD_HD, :]

    def emit_pieces(c, slot):
        rsl = [slice(hd * RET_HD, (hd + 1) * RET_HD) for hd in range(RET_HEADS)]
        gsl = [slice(gi * SSD_STATE, (gi + 1) * SSD_STATE) for gi in range(SSD_GROUPS)]
        xsl = [slice(gi * GROUP_W, (gi + 1) * GROUP_W) for gi in range(SSD_GROUPS)]
        st = {}

        def scores():
            ii = lax.broadcasted_iota(jnp.int32, (C, C), 0)
            jj = lax.broadcasted_iota(jnp.int32, (C, C), 1)
            st["causal"] = jj <= ii
            qs = [q_s[c, :, sl] for sl in rsl]
            cms = [c_s[c, :, sl] for sl in gsl]
            st["sc_s"] = [_dot(cms[gi], bT_s[c, gsl[gi], :]) for gi in range(SSD_GROUPS)]
            st["sc_r"] = [_dot(qs[hd], kT_s[c, rsl[hd], :]) for hd in range(RET_HEADS)]
            if cross:
                st["yc_s"] = [_dot(cms[gi], sent_s[c, gi]) for gi in range(SSD_GROUPS)]
                st["yc_r"] = [_dot(qs[hd], rent_s[c, hd]) for hd in range(RET_HEADS)]
            st["inc_col"] = inccol_s[c]
            st["exc_col"] = exccol_s[c]
            st["rowarg"] = rowarg_s[c]

        def ssd_head(gi, k):
            def run():
                hh = gi * HPG + k
                hb = SSD_HEADS + hh
                arg = jnp.where(st["causal"],
                                st["inc_col"][:, hh:hh + 1] - st["rowarg"][hh:hh + 1, :],
                                st["rowarg"][hb:hb + 1, :] - st["exc_col"][:, hb:hb + 1])
                m = (st["sc_s"][gi] * jnp.exp(arg)).astype(BF16)
                xh = xs_s[c, :, hh * SSD_HD:(hh + 1) * SSD_HD].astype(BF16)
                y_s[:, hh * SSD_HD:(hh + 1) * SSD_HD] = _dot(m, xh)
            return run

        def ssd_group(gi):
            def run():
                xg = xs_s[c, :, xsl[gi]]
                yg = y_s[:, xsl[gi]] + expand_f(dskip_row, gi) * xg
                if cross:
                    cfcb = cfcb_s[c]
                    yc = st["yc_s"][gi]
                    yg = yg + expand_f(cfcb, gi) * yc[:, 0:GROUP_W] + expand_b(cfcb, gi) * yc[:, GROUP_W:]
                y_s[:, xsl[gi]] = yg
            return run

        def ret_head(hd):
            def run():
                m = (st["sc_r"][hd] * wdec_s[hd]).astype(BF16)
                o = _dot(m, v_s[c, :, rsl[hd]])
                if cross:
                    yc = st["yc_r"][hd]
                    o = o + rvec_s[hd, RV_CROSS_F] * yc[:, 0:RET_HD] + rvec_s[hd, RV_CROSS_B] * yc[:, RET_HD:]
                o = o * lax.rsqrt(jnp.mean(o * o, axis=-1, keepdims=True) + EPS)
                mix_s[:, rsl[hd]] = (g_s[c, :, rsl[hd]].astype(F32) * o).astype(BF16)
            return run

        def ssd_norm():
            yz = y_s[...] * z_s[c].astype(F32)
            yn = yz * lax.rsqrt(jnp.mean(yz * yz, axis=-1, keepdims=True) + EPS) * normw_ref[...]
            mix_s[:, RET_W:] = yn.astype(BF16)

        def out_proj():
            g1 = mod_ref[0, 2:3, :]
            y = ALPHA * x_ref[0] + g1 * _dot(mix_s[...], wout_ref[...])
            x1_s[slot] = _layer_norm(y, ln1g_ref[...], ln1b_ref[...])

        pieces = [scores]
        for gi in range(SSD_GROUPS):
            pieces += [ssd_head(gi, k) for k in range(HPG)] + [ssd_group(gi)]
        pieces += [ret_head(hd) for hd in range(RET_HEADS)] + [ssd_norm, out_proj]
        return pieces

    def ffn_pieces(slot):
        st = {}

        def start():
            sh2 = modf_ref[0, 3:4, :]
            sc2 = modf_ref[0, 4:5, :]
            st["h2"] = (x1_s[slot] * (1.0 + sc2) + sh2).astype(BF16)

        def hidden(j):
            def run():
                js = slice(j * FF_BLK, (j + 1) * FF_BLK)
                h2 = st["h2"]
                hid_s[:, js] = (_silu(_dot(h2, wg_ref[:, js])) * _dot(h2, wu_ref[:, js])).astype(BF16)
            return run

        def finish():
            g2 = modf_ref[0, 5:6, :]
            y = ALPHA * x1_s[slot] + g2 * _dot(hid_s[...], wd_ref[...])
            out_ref[0] = _layer_norm(y, ln2g_ref[...], ln2b_ref[...])

        return [start] + [hidden(j) for j in range(D_FF // FF_BLK)] + [finish]

    def run_all(pieces):
        for p in pieces:
            p()

    @pl.when((step < nc) & (seq < nb))
    def _():
        project(step)

    @pl.when((step == nc) & (seq < nb))
    def _():
        recurrences()

    kk = step - nc
    slot = (seq * nc + kk) & 1
    first = (seq == 0) & (step == nc)

    @pl.when(first)
    def _():
        if not cast_weights:
            for i in range(1, len(big_weights)):
                weight_copy(i).wait()
        run_all(emit_pieces(kk, slot))

    @pl.when((step >= nc) & (seq < nb) & jnp.logical_not(first))
    def _():
        _interleave(ffn_pieces(1 - slot), emit_pieces(kk, slot))

    @pl.when((seq == nb) & (step == nc))
    def _():
        run_all(ffn_pieces((nb * nc - 1) & 1))
        if cast_weights:
            for i in range(len(big_weights)):
                weight_copy(i).wait()


def _const_spec(shape):
    nd = len(shape)
    return pl.BlockSpec(shape, lambda b, s: (0,) * nd, pipeline_mode=pl.Buffered(1))


def _layer_call(x, mod, mod_per_seq, weights, states, emit_state, use_rope, cast_weights, name):
    nb, L, _ = x.shape
    nc = L // CHUNK
    hpc_blocks = CHUNK // HALO
    has_state = states is not None
    last = nb - 1

    def chunk_of(b, s):
        return jnp.where(b > last, nc - 1, jnp.where(s < nc, s, s - nc))

    def halo_chunk(b, s):
        return jnp.where(b > last, nc - 1, jnp.minimum(s, nc - 1))

    def seq_of(b):
        return jnp.minimum(b, last)

    def lag_seq(b, s):
        return jnp.minimum(jnp.where(s > nc, b, jnp.maximum(b - 1, 0)), last)

    def out_map(b, s):
        live = (s > nc) & (b < nb)
        idle_chunk = jnp.where(b == 0, 0, nc - 1)
        return (jnp.where(live, b, jnp.maximum(b - 1, 0)), jnp.where(live, s - nc - 1, idle_chunk), 0)

    if mod_per_seq:
        mod_map = lambda b, s: (seq_of(b), 0, 0)
        modf_map = lambda b, s: (lag_seq(b, s), 0, 0)
    else:
        mod_map = modf_map = lambda b, s: (0, 0, 0)

    in_specs = [
        pl.BlockSpec((1, CHUNK, D_MODEL), lambda b, s: (seq_of(b), chunk_of(b, s), 0)),
        pl.BlockSpec((1, HALO, D_MODEL),
                     lambda b, s: (seq_of(b), jnp.maximum(halo_chunk(b, s) * hpc_blocks - 1, 0), 0)),
        pl.BlockSpec((1, HALO, D_MODEL),
                     lambda b, s: (seq_of(b), jnp.minimum((halo_chunk(b, s) + 1) * hpc_blocks, nc * hpc_blocks - 1), 0)),
        pl.BlockSpec((1, 6, D_MODEL), mod_map),
        pl.BlockSpec((1, 6, D_MODEL), modf_map),
    ] + [pl.BlockSpec(memory_space=pl.ANY) if i in BIG_WEIGHTS else _const_spec(w.shape)
         for i, w in enumerate(weights)]
    args = [x, x, x, mod, mod] + list(weights)
    ret_block = (1, 1, 2, RET_HEADS, RET_HD, RET_HD)
    ssd_block = (1, 1, 2, SSD_HEADS, SSD_HD, SSD_STATE)
    state_map = lambda b, s: (seq_of(b), 0, 0, 0, 0, 0)
    if has_state:
        in_specs += [pl.BlockSpec(ret_block, state_map, pipeline_mode=pl.Buffered(1)),
                     pl.BlockSpec(ssd_block, state_map, pipeline_mode=pl.Buffered(1))]
        args += list(states)
    out_shape = [jax.ShapeDtypeStruct((nb, L, D_MODEL), F32)]
    out_specs = [pl.BlockSpec((1, CHUNK, D_MODEL), out_map)]
    if emit_state:
        out_shape += [jax.ShapeDtypeStruct((nb,) + ret_block[1:], F32),
                      jax.ShapeDtypeStruct((nb,) + ssd_block[1:], F32)]
        out_specs += [pl.BlockSpec(ret_block, state_map), pl.BlockSpec(ssd_block, state_map)]
    n_in_blk = -(-(MAIN_COLS + N_DT // 2) // W_IN_BLK)
    resident = [(n_in_blk, D_MODEL, W_IN_BLK), (D_MODEL, D_MODEL), (D_MODEL, D_FF), (D_MODEL, D_FF),
                (D_FF, D_MODEL)]
    if cast_weights:
        out_shape += [jax.ShapeDtypeStruct(sh, BF16) for sh in resident]
        out_specs += [pl.BlockSpec(memory_space=pl.ANY)] * len(resident)
    state_dt = F32 if emit_state else BF16
    scratch = [
        pltpu.VMEM((nc, CHUNK, RET_W), BF16),
        pltpu.VMEM((nc, RET_W, CHUNK), BF16),
        pltpu.VMEM((nc, CHUNK, RET_W), BF16),
        pltpu.VMEM((nc, CHUNK, RET_W), BF16),
        pltpu.VMEM((nc, CHUNK, SSD_W), BF16),
        pltpu.VMEM((CHUNK + 2 * HALO, CONV_CH), F32),
        pltpu.VMEM((nc, CHUNK, SSD_W), F32),
        pltpu.VMEM((nc, SSD_GROUPS * SSD_STATE, CHUNK), BF16),
        pltpu.VMEM((nc, CHUNK, SSD_GROUPS * SSD_STATE), BF16),
        pltpu.VMEM((nc, CHUNK, LANES), F32),
        pltpu.VMEM((nc, CHUNK, LANES), F32),
        pltpu.VMEM((nc, CHUNK, LANES), F32),
        pltpu.VMEM((nc, 8, LANES), F32),
        pltpu.VMEM((nc, N_DT, CHUNK), F32),
        pltpu.VMEM((CHUNK, SSD_W), F32),
        pltpu.VMEM((CHUNK, 2 * RET_W), BF16),
        pltpu.VMEM((nc, RET_HEADS, RET_HD, 2 * RET_HD), state_dt),
        pltpu.VMEM((nc, SSD_GROUPS, SSD_STATE, 2 * GROUP_W), state_dt),
        pltpu.VMEM((nc, RET_HEADS, RET_HD, 2 * RET_HD), BF16),
        pltpu.VMEM((nc, SSD_GROUPS, SSD_STATE, 2 * GROUP_W), BF16),
        pltpu.VMEM((RET_HEADS, CHUNK, CHUNK), F32),
        pltpu.VMEM((RET_HEADS, 4, CHUNK, LANES), F32),
        pltpu.VMEM((2, CHUNK, D_MODEL), F32),
        pltpu.VMEM((CHUNK, D_FF), BF16),
        pltpu.VMEM((LANES, D_MODEL), BF16),
    ]
    scratch += [pltpu.VMEM(sh, BF16) for sh in resident]
    scratch += [pltpu.SemaphoreType.DMA((len(BIG_WEIGHTS),))]
    if cast_weights:
        scratch += [pltpu.VMEM((2, W_IN_BLK, D_MODEL), F32),
                    pltpu.VMEM((2, 128, D_FF), F32),
                    pltpu.SemaphoreType.DMA((2,))]
    if use_rope:
        scratch += [pltpu.VMEM((nc, CHUNK, LANES), F32)] * 2
    kern = functools.partial(_layer_kernel, L=L, nb=nb, has_state=has_state, use_rope=use_rope,
                             emit_state=emit_state, cast_weights=cast_weights)
    return pl.pallas_call(
        kern,
        grid=(nb + 1, 2 * nc),
        in_specs=in_specs,
        out_specs=out_specs,
        out_shape=out_shape,
        scratch_shapes=scratch,
        compiler_params=pltpu.CompilerParams(dimension_semantics=("arbitrary", "arbitrary"),
                                             vmem_limit_bytes=VMEM_LIMIT),
        name=name,
    )(*args)


def kernel(x_prompt, x_sample, state_ret, state_ssd, c, c_ctx, w_in, ret_decay_fwd, ret_decay_bwd, conv_w, conv_b, dt_bias_fwd, dt_bias_bwd, a_log_fwd, a_log_bwd, d_skip, ssd_norm_w, w_out, ln1_g, ln1_b, w_gate, w_up, w_down, ln2_g, ln2_b, w_ada, b_ada):
    depth = w_in.shape[0]
    assert depth == 1, "single trunk layer"
    bp, lp, _ = x_prompt.shape
    bs, ls, _ = x_sample.shape
    assert lp % CHUNK == 0 and ls % CHUNK == 0 and ls % GRID_W == 0 and D_FF % FF_BLK == 0

    rows = -(-(bs + 1) // 8) * 8
    cond = jnp.zeros((rows, D_MODEL), F32).at[:bs].set(c).at[bs].set(c_ctx)
    mod = _ada_call(cond, w_ada[0], b_ada[0][None, :]).reshape(rows, 6, D_MODEL)
    mod_lat = mod[:bs]
    mod_ctx = mod[bs:bs + 1]

    convw = jnp.zeros((8, CONV_CH), F32).at[:CONV_W].set(conv_w[0])
    convb = conv_b[0][None, :]
    dt_bias = jnp.concatenate([dt_bias_fwd[0], dt_bias_bwd[0]])
    a_log = jnp.concatenate([a_log_fwd[0], a_log_bwd[0]])
    ret_decay = jnp.concatenate([ret_decay_fwd[0], ret_decay_bwd[0]])
    hp = jnp.zeros((8, LANES), F32)
    hp = hp.at[0, :N_DT].set(dt_bias).at[1, :N_DT].set(a_log)
    hp = hp.at[2, :2 * RET_HEADS].set(ret_decay).at[3, :SSD_HEADS].set(d_skip[0])
    hpc = jnp.zeros((N_DT, LANES), F32).at[:, 0].set(dt_bias).at[:, 1].set(a_log)
    weights = [jnp.swapaxes(w_in[0], 0, 1), convw, convb, hp, hpc, ssd_norm_w[0][None, :],
               w_out[0], ln1_g[0][None, :], ln1_b[0][None, :],
               w_gate[0], w_up[0], w_down[0], ln2_g[0][None, :], ln2_b[0][None, :]]

    yp, new_ret, new_ssd, *bf16_weights = _layer_call(
        x_prompt, mod_ctx, False, weights, None, True, False, True, "layer_ctx")
    for i, w in zip(BIG_WEIGHTS, bf16_weights):
        weights[i] = w
    states = (state_ret, jnp.swapaxes(state_ssd, -1, -2))
    (ys,) = _layer_call(x_sample, mod_lat, True, weights, states, False, True, False, "layer_lat")
    return (yp, ys, new_ret, jnp.swapaxes(new_ssd, -1, -2))
```

```python
import functools
import math

import jax
import jax.numpy as jnp
from jax import lax
from jax.experimental import pallas as pl
from jax.experimental.pallas import tpu as pltpu

F32 = jnp.float32
BF16 = jnp.bfloat16

D_MODEL = 1024
RET_W = 512
RET_HEADS = 4
RET_HD = 128
SSD_W = 512
SSD_HD = 64
SSD_HEADS = 8
SSD_GROUPS = 2
SSD_STATE = 128
HPG = SSD_HEADS // SSD_GROUPS
GROUP_W = HPG * SSD_HD
CONV_W = 5
CONV_CH = SSD_W + 2 * SSD_GROUPS * SSD_STATE
D_FF = 2816
GRID_W = 64
GRID_SHIFT = 6
ROPE_BASE = 10000.0
EPS = 1e-6
ALPHA = 2.0 ** 0.25
MAIN_COLS = 4 * RET_W + SSD_W + CONV_CH
XBC_COL0 = 4 * RET_W + SSD_W
N_DT = 2 * SSD_HEADS

CHUNK = 256
HALO = 8
FF_BLK = 256
ADA_COLS = 2048
W_IN_BLK = 512
LANES = 128
VMEM_LIMIT = 62 * 1024 * 1024


def _dot(a, b):
    return jnp.dot(a, b, preferred_element_type=F32)


def _dot_nt(a, b):
    return lax.dot_general(a, b, (((1,), (1,)), ((), ())), preferred_element_type=F32)


def _silu(x):
    return x * jax.nn.sigmoid(x)


def _softplus(x):
    return jnp.maximum(x, 0.0) + jnp.log1p(jnp.exp(-jnp.abs(x)))


def _layer_norm(y, g, b):
    mu = jnp.mean(y, axis=-1, keepdims=True)
    yc = y - mu
    var = jnp.mean(yc * yc, axis=-1, keepdims=True)
    return yc * lax.rsqrt(var + EPS) * g + b


def _cumsum(x, axis):
    n = x.shape[axis]
    idx = lax.broadcasted_iota(jnp.int32, x.shape, axis)
    s = 1
    while s < n:
        x = x + jnp.where(idx >= s, pltpu.roll(x, s, axis), 0.0)
        s *= 2
    return x


def _expand4(cols, lane):
    a = jnp.where(lane < SSD_HD, cols[0], cols[1])
    b = jnp.where(lane < SSD_HD, cols[2], cols[3])
    return jnp.concatenate([a, b], axis=1)


def _interleave(a, b):
    ia = ib = 0
    while ia < len(a) or ib < len(b):
        if ib >= len(b) or (ia < len(a) and ia * len(b) <= ib * len(a)):
            a[ia]()
            ia += 1
        else:
            b[ib]()
            ib += 1


def _ada_kernel(cond_ref, w_ref, b_ref, o_ref):
    s = _silu(cond_ref[...]).astype(BF16)
    o_ref[...] = _dot(s, w_ref[...].astype(BF16)) + b_ref[...]


def _ada_call(cond, w_ada, b_ada):
    rows = cond.shape[0]
    n = w_ada.shape[1]
    return pl.pallas_call(
        _ada_kernel,
        grid=(n // ADA_COLS,),
        in_specs=[
            pl.BlockSpec((rows, D_MODEL), lambda j: (0, 0)),
            pl.BlockSpec((D_MODEL, ADA_COLS), lambda j: (0, j)),
            pl.BlockSpec((1, ADA_COLS), lambda j: (0, j)),
        ],
        out_specs=pl.BlockSpec((rows, ADA_COLS), lambda j: (0, j)),
        out_shape=jax.ShapeDtypeStruct((rows, n), F32),
        compiler_params=pltpu.CompilerParams(dimension_semantics=("arbitrary",)),
        name="adaln_mod",
    )(cond, w_ada, b_ada)


RV_TAIL_F, RV_TAIL_B, RV_CROSS_F, RV_CROSS_B = range(4)
BIG_WEIGHTS = (0, 6, 9, 10, 11)


def _layer_kernel(*refs, L, nb, has_state, use_rope, emit_state, cast_weights):
    nc = L // CHUNK
    C = CHUNK
    cross = has_state or nc > 1
    it = iter(refs)
    x_ref, xprev_ref, xnext_ref, mod_ref, modf_ref, wmain_hbm = (next(it) for _ in range(6))
    convw_ref, convb_ref, hp_ref, hpc_ref, normw_ref = (next(it) for _ in range(5))
    wout_hbm, ln1g_ref, ln1b_ref = (next(it) for _ in range(3))
    wg_hbm, wu_hbm, wd_hbm, ln2g_ref, ln2b_ref = (next(it) for _ in range(5))
    if has_state:
        sret0_ref, sssd0_ref = next(it), next(it)
    out_ref = next(it)
    if emit_state:
        nret_ref, nssd_ref = next(it), next(it)
    if cast_weights:
        bf16_out = [next(it) for _ in range(len(BIG_WEIGHTS))]
    (q_s, kT_s, v_s, g_s, z_s, stage_s, xs_s, bT_s, c_s, inccol_s, exccol_s, cfcb_s, dec_s,
     rowarg_s, y_s, mix_s, rloc_s, sloc_s, rent_s, sent_s, wdec_s, rvec_s, x1_s, hid_s, wdtr_s) = (
         next(it) for _ in range(25))
    wmain_ref, wout_ref, wg_ref, wu_ref, wd_ref, wsem = (next(it) for _ in range(6))
    if cast_weights:
        stage_a, stage_b, csem = (next(it) for _ in range(3))
    if use_rope:
        cos_s, sin_s = next(it), next(it)

    seq = pl.program_id(0)
    step = pl.program_id(1)

    big_weights = ((wmain_hbm, wmain_ref), (wout_hbm, wout_ref), (wg_hbm, wg_ref), (wu_hbm, wu_ref),
                   (wd_hbm, wd_ref))

    def weight_copy(i):
        if cast_weights:
            return pltpu.make_async_copy(big_weights[i][1], bf16_out[i], wsem.at[i])
        return pltpu.make_async_copy(big_weights[i][0], big_weights[i][1], wsem.at[i])

    def stream_cast(src, rows, stage, put):
        n = src.shape[0] // rows

        def block_copy(i, slot):
            return pltpu.make_async_copy(src.at[pl.ds(i * rows, rows), :],
                                         stage.at[slot, pl.ds(0, rows), :], csem.at[slot])

        block_copy(0, 0).start()

        def body(i, carry):
            slot = i & 1

            @pl.when(i + 1 < n)
            def _():
                block_copy(i + 1, 1 - slot).start()

            block_copy(i, slot).wait()
            put(i, stage[slot, 0:rows, :])
            return carry

        lax.fori_loop(0, n, body, 0)
        return n

    def cast_all_weights():
        def put_in(i, blk):
            wmain_ref[i] = blk.T.astype(BF16)

        n_full = stream_cast(wmain_hbm, W_IN_BLK, stage_a, put_in)
        tail = wmain_hbm.shape[0] - n_full * W_IN_BLK
        slot = n_full & 1
        stage_a[slot] = jnp.zeros(stage_a.shape[1:], F32)
        tail_copy = pltpu.make_async_copy(wmain_hbm.at[pl.ds(n_full * W_IN_BLK, tail), :],
                                          stage_a.at[slot, pl.ds(0, tail), :], csem.at[slot])
        tail_copy.start()
        tail_copy.wait()
        wmain_ref[n_full] = stage_a[slot].T.astype(BF16)

        def put_rows(dst, rows):
            def put(i, blk):
                dst[pl.ds(pl.multiple_of(i * rows, 16), rows), :] = blk.astype(BF16)
            return put

        stream_cast(wout_hbm, 512, stage_a, put_rows(wout_ref, 512))
        ffn_block_copy(0).start()
        ffn_block_copy(1).start()

    ffn_cast_plan = [(wg_hbm, wg_ref, 128), (wu_hbm, wu_ref, 128), (wd_hbm, wd_ref, 352)]
    FFN_CAST_BLOCKS = 8

    def ffn_block(k):
        src, dst, rows = ffn_cast_plan[k // FFN_CAST_BLOCKS]
        stage = stage_b if src.shape[1] == D_FF else stage_a
        return src, dst, rows, stage, k % FFN_CAST_BLOCKS

    def ffn_block_copy(k):
        src, _, rows, stage, j = ffn_block(k)
        return pltpu.make_async_copy(src.at[pl.ds(j * rows, rows), :],
                                     stage.at[k & 1, pl.ds(0, rows), :], csem.at[k & 1])

    def ffn_cast_pieces(lo, hi):
        n_blocks = len(ffn_cast_plan) * FFN_CAST_BLOCKS

        def piece(k):
            def run():
                _, dst, rows, stage, j = ffn_block(k)
                ffn_block_copy(k).wait()
                dst[j * rows:(j + 1) * rows, :] = stage[k & 1, 0:rows, :].astype(BF16)
                if k + 2 < n_blocks:
                    ffn_block_copy(k + 2).start()
                if k == n_blocks - 1:
                    for i in range(len(big_weights)):
                        weight_copy(i).start()
            return run

        return [piece(k) for k in range(lo, hi)]

    hp = hp_ref[...]
    dt_bias_row = hp[0:1, :]
    nega_row = -jnp.exp(hp[1:2, :])
    lg_row = -_softplus(-hp[2:3, :])
    dskip_row = hp[3:4, :]
    hpc = hpc_ref[...]
    dt_bias_col = hpc[:, 0:1]
    nega_col = -jnp.exp(hpc[:, 1:2])

    lane = lax.broadcasted_iota(jnp.int32, (1, LANES), 1)

    def ret_decays(hd):
        lgf = lg_row[:, hd:hd + 1]
        lgb = lg_row[:, RET_HEADS + hd:RET_HEADS + hd + 1]
        return lgf, lgb

    def group_heads(gi):
        return [gi * HPG + k for k in range(HPG)]

    def expand_f(arr, gi):
        return _expand4([arr[:, hh:hh + 1] for hh in group_heads(gi)], lane)

    def expand_b(arr, gi):
        return _expand4([arr[:, SSD_HEADS + hh:SSD_HEADS + hh + 1] for hh in group_heads(gi)], lane)

    @pl.when((seq == 0) & (step == 0))
    def _():
        if cast_weights:
            cast_all_weights()
        else:
            for i in range(len(big_weights)):
                weight_copy(i).start()
            weight_copy(0).wait()
        ii = lax.broadcasted_iota(jnp.int32, (C, C), 0)
        jj = lax.broadcasted_iota(jnp.int32, (C, C), 1)
        dmat = (ii - jj).astype(F32)
        irow = lax.broadcasted_iota(jnp.int32, (C, LANES), 0).astype(F32)
        for hd in range(RET_HEADS):
            lgf, lgb = ret_decays(hd)
            wdec_s[hd] = jnp.exp(jnp.where(jj <= ii, dmat * lgf, -dmat * lgb))
            rvec_s[hd, RV_TAIL_F] = jnp.exp((C - 1.0 - irow) * lgf)
            rvec_s[hd, RV_TAIL_B] = jnp.exp(irow * lgb)
            rvec_s[hd, RV_CROSS_F] = jnp.exp((irow + 1.0) * lgf)
            rvec_s[hd, RV_CROSS_B] = jnp.exp((C - irow) * lgb)
        wdtr_s[...] = wmain_ref[MAIN_COLS // W_IN_BLK, :, 0:LANES].astype(F32).T.astype(BF16)
        if use_rope:
            ln = lax.broadcasted_iota(jnp.int32, (C, LANES), 1)
            nf = RET_HD // 4
            inv = jnp.exp((ln & (nf - 1)).astype(F32) * (-math.log(ROPE_BASE) / nf))
            for cc in range(nc):
                t = lax.broadcasted_iota(jnp.int32, (C, LANES), 0) + cc * C
                pos = jnp.where((ln & (2 * nf - 1)) < nf, t >> GRID_SHIFT, t & (GRID_W - 1)).astype(F32)
                ang = pos * inv
                cos_s[cc] = jnp.cos(ang)
                sin_s[cc] = jnp.where(ln < RET_HD // 2, -jnp.sin(ang), jnp.sin(ang))

    def project(c, extra=()):
        st = {}
        CB = 256
        assert CONV_W == 5 and HALO >= CONV_W // 2

        def v_mod():
            sh1 = mod_ref[0, 0:1, :]
            sc1 = mod_ref[0, 1:2, :]
            xe = jnp.concatenate([xprev_ref[0], x_ref[0], xnext_ref[0]], axis=0)
            xm = xe * (1.0 + sc1) + sh1
            st["he"] = xm.astype(BF16)
            st["h"] = xm[HALO:HALO + C].astype(BF16)

        def m_main(name, lo, hi, halo=False):
            def run():
                assert lo % W_IN_BLK == 0 and hi - lo == W_IN_BLK
                st[name] = _dot(st["he" if halo else "h"], wmain_ref[lo // W_IN_BLK])
            return run

        def m_dt():
            raw_c = _dot(st["h"], wmain_ref[MAIN_COLS // W_IN_BLK, :, 0:LANES])
            st["dtc"] = raw_c + pltpu.roll(raw_c, SSD_HEADS, 1)
            raw_r = _dot_nt(wdtr_s[0:N_DT, :], st["h"])
            st["dtr"] = raw_r + pltpu.roll(raw_r, SSD_HEADS, 0)

        def v_dt():
            dt_c = _softplus(st["dtc"] + dt_bias_row)
            lac = dt_c * nega_row
            inc_col = _cumsum(lac, 0)
            exc_col = inc_col - lac
            tot_col = inc_col[C - 1:C, :]
            inccol_s[c] = inc_col
            exccol_s[c] = exc_col
            dec_s[c] = jnp.broadcast_to(jnp.exp(tot_col), (8, LANES))
            if cross:
                cfcb_s[c] = jnp.exp(jnp.where(lane < SSD_HEADS, inc_col, tot_col - exc_col))
            st["sf"] = jnp.exp(tot_col - inc_col) * dt_c
            st["sb"] = jnp.exp(exc_col) * dt_c
            dt_r = _softplus(st["dtr"] + dt_bias_col)
            lar = dt_r * nega_col
            inc_row = _cumsum(lar, 1)
            ldt = jnp.log(dt_r)
            rid = lax.broadcasted_iota(jnp.int32, (N_DT, C), 0)
            rowarg_s[c] = jnp.where(rid < SSD_HEADS, inc_row - ldt, inc_row - lar + ldt)

        def v_stage(hf):
            def run():
                pe = st["pe%d" % hf]
                cs = slice(hf * 512, (hf + 1) * 512)
                stage_s[0:HALO, cs] = jnp.where(c > 0, pe[0:HALO], 0.0)
                stage_s[HALO:HALO + C, cs] = pe[HALO:HALO + C]
                stage_s[HALO + C:, cs] = jnp.where(c < nc - 1, pe[HALO + C:], 0.0)
            return run

        def conv_block(cb):
            cs = slice(cb * CB, (cb + 1) * CB)
            rows = C + 2 * HALO
            xin = stage_s[:, cs]
            taps = [convw_ref[k:k + 1, cs] * xin for k in range(CONV_W)]
            up = lambda a: pltpu.roll(a, rows - 1, 0)
            down = lambda a: pltpu.roll(a, 1, 0)
            acc = taps[2] + up(taps[3] + up(taps[4])) + down(taps[1] + down(taps[0]))
            return _silu(acc[HALO:HALO + C] + convb_ref[0:1, cs])

        def v_conv_x(cb):
            def run():
                xs_s[c, :, cb * CB:(cb + 1) * CB] = conv_block(cb)
            return run

        def v_conv_b():
            bT_s[c] = conv_block(SSD_W // CB).T.astype(BF16)

        def v_conv_c():
            c_s[c] = conv_block(SSD_W // CB + 1).astype(BF16)

        def rope(a):
            if not use_rope:
                return a
            return a * cos_s[c] + pltpu.roll(a, RET_HD // 2, 1) * sin_s[c]

        def v_q():
            for hd in range(RET_HEADS):
                sl = slice(hd * RET_HD, (hd + 1) * RET_HD)
                q_s[c, :, sl] = rope(st["pq"][:, sl]).astype(BF16)

        def v_k():
            for hd in range(RET_HEADS):
                sl = slice(hd * RET_HD, (hd + 1) * RET_HD)
                kh = rope(st["pk"][:, sl]) * (RET_HD ** -0.5)
                kT_s[c, sl, :] = kh.T.astype(BF16)

        def v_v():
            v_s[c] = st["pv"].astype(BF16)

        def v_g():
            g_s[c] = _silu(st["pg"]).astype(BF16)

        def v_z():
            z_s[c] = _silu(st["pz"]).astype(BF16)

        def m_sloc(gi):
            def run():
                xg = xs_s[c, :, gi * GROUP_W:(gi + 1) * GROUP_W]
                vcat = jnp.concatenate([xg * expand_f(st["sf"], gi), xg * expand_b(st["sb"], gi)],
                                       axis=1).astype(BF16)
                sloc_s[c, gi] = _dot(bT_s[c, gi * SSD_STATE:(gi + 1) * SSD_STATE, :], vcat).astype(sloc_s.dtype)
            return run

        def m_rloc(hd):
            def run():
                sl = slice(hd * RET_HD, (hd + 1) * RET_HD)
                vf = st["pv"][:, sl]
                vcat = jnp.concatenate([vf * rvec_s[hd, RV_TAIL_F], vf * rvec_s[hd, RV_TAIL_B]],
                                       axis=1).astype(BF16)
                rloc_s[c, hd] = _dot(kT_s[c, sl, :], vcat).astype(rloc_s.dtype)
            return run

        m_pe0 = m_main("pe0", XBC_COL0, XBC_COL0 + 512, halo=True)
        m_pe1 = m_main("pe1", XBC_COL0 + 512, MAIN_COLS, halo=True)
        m_q = m_main("pq", 0, RET_W)
        m_k = m_main("pk", RET_W, 2 * RET_W)
        m_v = m_main("pv", 2 * RET_W, 3 * RET_W)
        m_g = m_main("pg", 3 * RET_W, 4 * RET_W)
        m_z = m_main("pz", 4 * RET_W, XBC_COL0)
        order = [v_mod, m_dt, m_pe0, m_pe1, v_dt, m_q, v_stage(0), v_stage(1), m_k, v_conv_x(0), m_v,
                 v_conv_x(1), v_q, m_g, v_conv_b, v_k, m_z, v_conv_c, v_v,
                 m_rloc(0), m_rloc(1), v_g, m_rloc(2), m_rloc(3), m_sloc(0), m_sloc(1), v_z]
        _interleave(order, list(extra))

    def recurrences():
        for hd in range(RET_HEADS):
            lgf, lgb = ret_decays(hd)
            dec_f = jnp.exp(C * lgf)
            dec_b = jnp.exp(C * lgb)
            if has_state:
                ent_f = sret0_ref[0, 0, 0, hd]
                ent_b = sret0_ref[0, 0, 1, hd]
            else:
                ent_f = jnp.zeros((RET_HD, RET_HD), F32)
                ent_b = jnp.zeros((RET_HD, RET_HD), F32)
            for c in range(nc):
                if cross:
                    rent_s[c, hd, :, 0:RET_HD] = ent_f.astype(BF16)
                ent_f = dec_f * ent_f + rloc_s[c, hd, :, 0:RET_HD]
            for c in range(nc - 1, -1, -1):
                if cross:
                    rent_s[c, hd, :, RET_HD:] = ent_b.astype(BF16)
                ent_b = dec_b * ent_b + rloc_s[c, hd, :, RET_HD:]
            if emit_state:
                nret_ref[0, 0, 0, hd] = ent_f
                nret_ref[0, 0, 1, hd] = ent_b

        for gi in range(SSD_GROUPS):
            heads = group_heads(gi)
            if has_state:
                ent_f = jnp.concatenate([sssd0_ref[0, 0, 0, hh] for hh in heads], axis=0).T
                ent_b = jnp.concatenate([sssd0_ref[0, 0, 1, hh] for hh in heads], axis=0).T
            else:
                ent_f = jnp.zeros((SSD_STATE, GROUP_W), F32)
                ent_b = jnp.zeros((SSD_STATE, GROUP_W), F32)
            for c in range(nc):
                if cross:
                    sent_s[c, gi, :, 0:GROUP_W] = ent_f.astype(BF16)
                ent_f = expand_f(dec_s[c, 0:1, :], gi) * ent_f + sloc_s[c, gi, :, 0:GROUP_W]
            for c in range(nc - 1, -1, -1):
                if cross:
                    sent_s[c, gi, :, GROUP_W:] = ent_b.astype(BF16)
                ent_b = expand_b(dec_s[c, 0:1, :], gi) * ent_b + sloc_s[c, gi, :, GROUP_W:]
            if emit_state:
                ent_ft = ent_f.T
                ent_bt = ent_b.T
                for k, hh in enumerate(heads):
                    nssd_ref[0, 0, 0, hh] = ent_ft[k * SSD_HD:(k + 1) * SSD_HD, :]
                    nssd_ref[0, 0, 1, hh] = ent_bt[k * SSD_HD:(k + 1) * SSD_HD, :]

    def emit_pieces(c, slot):
        rsl = [slice(hd * RET_HD, (hd + 1) * RET_HD) for hd in range(RET_HEADS)]
        gsl = [slice(gi * SSD_STATE, (gi + 1) * SSD_STATE) for gi in range(SSD_GROUPS)]
        xsl = [slice(gi * GROUP_W, (gi + 1) * GROUP_W) for gi in range(SSD_GROUPS)]
        st = {}

        def scores():
            ii = lax.broadcasted_iota(jnp.int32, (C, C), 0)
            jj = lax.broadcasted_iota(jnp.int32, (C, C), 1)
            st["causal"] = jj <= ii
            qs = [q_s[c, :, sl] for sl in rsl]
            cms = [c_s[c, :, sl] for sl in gsl]
            st["sc_s"] = [_dot(cms[gi], bT_s[c, gsl[gi], :]) for gi in range(SSD_GROUPS)]
            st["sc_r"] = [_dot(qs[hd], kT_s[c, rsl[hd], :]) for hd in range(RET_HEADS)]
            if cross:
                st["yc_s"] = [_dot(cms[gi], sent_s[c, gi]) for gi in range(SSD_GROUPS)]
                st["yc_r"] = [_dot(qs[hd], rent_s[c, hd]) for hd in range(RET_HEADS)]
            st["inc_col"] = inccol_s[c]
            st["exc_col"] = exccol_s[c]
            st["rowarg"] = rowarg_s[c]

        def ssd_head(gi, k):
            def run():
                hh = gi * HPG + k
                hb = SSD_HEADS + hh
                arg = jnp.where(st["causal"],
                                st["inc_col"][:, hh:hh + 1] - st["rowarg"][hh:hh + 1, :],
                                st["rowarg"][hb:hb + 1, :] - st["exc_col"][:, hb:hb + 1])
                m = (st["sc_s"][gi] * jnp.exp(arg)).astype(BF16)
                xh = xs_s[c, :, hh * SSD_HD:(hh + 1) * SSD_HD].astype(BF16)
                y_s[:, hh * SSD_HD:(hh + 1) * SSD_HD] = _dot(m, xh)
            return run

        def ssd_group(gi):
            def run():
                xg = xs_s[c, :, xsl[gi]]
                yg = y_s[:, xsl[gi]] + expand_f(dskip_row, gi) * xg
                if cross:
                    cfcb = cfcb_s[c]
                    yc = st["yc_s"][gi]
                    yg = yg + expand_f(cfcb, gi) * yc[:, 0:GROUP_W] + expand_b(cfcb, gi) * yc[:, GROUP_W:]
                y_s[:, xsl[gi]] = yg
            return run

        def ret_head(hd):
            def run():
                m = (st["sc_r"][hd] * wdec_s[hd]).astype(BF16)
                o = _dot(m, v_s[c, :, rsl[hd]])
                if cross:
                    yc = st["yc_r"][hd]
                    o = o + rvec_s[hd, RV_CROSS_F] * yc[:, 0:RET_HD] + rvec_s[hd, RV_CROSS_B] * yc[:, RET_HD:]
                o = o * lax.rsqrt(jnp.mean(o * o, axis=-1, keepdims=True) + EPS)
                mix_s[:, rsl[hd]] = (g_s[c, :, rsl[hd]].astype(F32) * o).astype(BF16)
            return run

        def ssd_norm():
            yz = y_s[...] * z_s[c].astype(F32)
            yn = yz * lax.rsqrt(jnp.mean(yz * yz, axis=-1, keepdims=True) + EPS) * normw_ref[...]
            mix_s[:, RET_W:] = yn.astype(BF16)

        def out_proj():
            g1 = mod_ref[0, 2:3, :]
            y = ALPHA * x_ref[0] + g1 * _dot(mix_s[...], wout_ref[...])
            x1_s[slot] = _layer_norm(y, ln1g_ref[...], ln1b_ref[...])

        pieces = [scores]
        for gi in range(SSD_GROUPS):
            pieces += [ssd_head(gi, k) for k in range(HPG)] + [ssd_group(gi)]
        pieces += [ret_head(hd) for hd in range(RET_HEADS)] + [ssd_norm, out_proj]
        return pieces

    def ffn_pieces(slot):
        st = {}

        def start():
            sh2 = modf_ref[0, 3:4, :]
            sc2 = modf_ref[0, 4:5, :]
            st["h2"] = (x1_s[slot] * (1.0 + sc2) + sh2).astype(BF16)

        def hidden(j):
            def run():
                js = slice(j * FF_BLK, (j + 1) * FF_BLK)
                h2 = st["h2"]
                hid_s[:, js] = (_silu(_dot(h2, wg_ref[:, js])) * _dot(h2, wu_ref[:, js])).astype(BF16)
            return run

        def finish():
            g2 = modf_ref[0, 5:6, :]
            y = ALPHA * x1_s[slot] + g2 * _dot(hid_s[...], wd_ref[...])
            out_ref[0] = _layer_norm(y, ln2g_ref[...], ln2b_ref[...])

        return [start] + [hidden(j) for j in range(D_FF // FF_BLK)] + [finish]

    def run_all(pieces):
        for p in pieces:
            p()

    if cast_weights:
        assert nc == 1 and nb >= 2
        third = len(ffn_cast_plan) * FFN_CAST_BLOCKS // 3

        @pl.when((step == 0) & (seq == 0))
        def _():
            project(step, ffn_cast_pieces(0, third))

        @pl.when((step == 0) & (seq == 1))
        def _():
            project(step, ffn_cast_pieces(2 * third, 3 * third))

        @pl.when((step == 0) & (seq > 1) & (seq < nb))
        def _():
            project(step)
    else:
        @pl.when((step < nc) & (seq < nb))
        def _():
            project(step)

    @pl.when((step == nc) & (seq < nb))
    def _():
        recurrences()

    kk = step - nc
    slot = (seq * nc + kk) & 1
    first = (seq == 0) & (step == nc)

    @pl.when(first)
    def _():
        if cast_weights:
            _interleave(ffn_cast_pieces(third, 2 * third), emit_pieces(kk, slot))
        else:
            for i in range(1, len(big_weights)):
                weight_copy(i).wait()
            run_all(emit_pieces(kk, slot))

    @pl.when((step >= nc) & (seq < nb) & jnp.logical_not(first))
    def _():
        _interleave(ffn_pieces(1 - slot), emit_pieces(kk, slot))

    @pl.when((seq == nb) & (step == nc))
    def _():
        run_all(ffn_pieces((nb * nc - 1) & 1))
        if cast_weights:
            for i in range(len(big_weights)):
                weight_copy(i).wait()


def _const_spec(shape):
    nd = len(shape)
    return pl.BlockSpec(shape, lambda b, s: (0,) * nd, pipeline_mode=pl.Buffered(1))


def _layer_call(x, mod, mod_per_seq, weights, states, emit_state, use_rope, cast_weights, name):
    nb, L, _ = x.shape
    nc = L // CHUNK
    hpc_blocks = CHUNK // HALO
    has_state = states is not None
    last = nb - 1

    def chunk_of(b, s):
        return jnp.where(b > last, nc - 1, jnp.where(s < nc, s, s - nc))

    def halo_chunk(b, s):
        return jnp.where(b > last, nc - 1, jnp.minimum(s, nc - 1))

    def seq_of(b):
        return jnp.minimum(b, last)

    def lag_seq(b, s):
        return jnp.minimum(jnp.where(s > nc, b, jnp.maximum(b - 1, 0)), last)

    def out_map(b, s):
        live = (s > nc) & (b < nb)
        idle_chunk = jnp.where(b == 0, 0, nc - 1)
        return (jnp.where(live, b, jnp.maximum(b - 1, 0)), jnp.where(live, s - nc - 1, idle_chunk), 0)

    if mod_per_seq:
        mod_map = lambda b, s: (seq_of(b), 0, 0)
        modf_map = lambda b, s: (lag_seq(b, s), 0, 0)
    else:
        mod_map = modf_map = lambda b, s: (0, 0, 0)

    in_specs = [
        pl.BlockSpec((1, CHUNK, D_MODEL), lambda b, s: (seq_of(b), chunk_of(b, s), 0)),
        pl.BlockSpec((1, HALO, D_MODEL),
                     lambda b, s: (seq_of(b), jnp.maximum(halo_chunk(b, s) * hpc_blocks - 1, 0), 0)),
        pl.BlockSpec((1, HALO, D_MODEL),
                     lambda b, s: (seq_of(b), jnp.minimum((halo_chunk(b, s) + 1) * hpc_blocks, nc * hpc_blocks - 1), 0)),
        pl.BlockSpec((1, 6, D_MODEL), mod_map),
        pl.BlockSpec((1, 6, D_MODEL), modf_map),
    ] + [pl.BlockSpec(memory_space=pl.ANY) if i in BIG_WEIGHTS else _const_spec(w.shape)
         for i, w in enumerate(weights)]
    args = [x, x, x, mod, mod] + list(weights)
    ret_block = (1, 1, 2, RET_HEADS, RET_HD, RET_HD)
    ssd_block = (1, 1, 2, SSD_HEADS, SSD_HD, SSD_STATE)
    state_map = lambda b, s: (seq_of(b), 0, 0, 0, 0, 0)
    if has_state:
        in_specs += [pl.BlockSpec(ret_block, state_map, pipeline_mode=pl.Buffered(1)),
                     pl.BlockSpec(ssd_block, state_map, pipeline_mode=pl.Buffered(1))]
        args += list(states)
    out_shape = [jax.ShapeDtypeStruct((nb, L, D_MODEL), F32)]
    out_specs = [pl.BlockSpec((1, CHUNK, D_MODEL), out_map)]
    if emit_state:
        out_shape += [jax.ShapeDtypeStruct((nb,) + ret_block[1:], F32),
                      jax.ShapeDtypeStruct((nb,) + ssd_block[1:], F32)]
        out_specs += [pl.BlockSpec(ret_block, state_map), pl.BlockSpec(ssd_block, state_map)]
    n_in_blk = -(-(MAIN_COLS + N_DT // 2) // W_IN_BLK)
    resident = [(n_in_blk, D_MODEL, W_IN_BLK), (D_MODEL, D_MODEL), (D_MODEL, D_FF), (D_MODEL, D_FF),
                (D_FF, D_MODEL)]
    if cast_weights:
        out_shape += [jax.ShapeDtypeStruct(sh, BF16) for sh in resident]
        out_specs += [pl.BlockSpec(memory_space=pl.ANY)] * len(resident)
    state_dt = F32 if emit_state else BF16
    scratch = [
        pltpu.VMEM((nc, CHUNK, RET_W), BF16),
        pltpu.VMEM((nc, RET_W, CHUNK), BF16),
        pltpu.VMEM((nc, CHUNK, RET_W), BF16),
        pltpu.VMEM((nc, CHUNK, RET_W), BF16),
        pltpu.VMEM((nc, CHUNK, SSD_W), BF16),
        pltpu.VMEM((CHUNK + 2 * HALO, CONV_CH), F32),
        pltpu.VMEM((nc, CHUNK, SSD_W), F32),
        pltpu.VMEM((nc, SSD_GROUPS * SSD_STATE, CHUNK), BF16),
        pltpu.VMEM((nc, CHUNK, SSD_GROUPS * SSD_STATE), BF16),
        pltpu.VMEM((nc, CHUNK, LANES), F32),
        pltpu.VMEM((nc, CHUNK, LANES), F32),
        pltpu.VMEM((nc, CHUNK, LANES), F32),
        pltpu.VMEM((nc, 8, LANES), F32),
        pltpu.VMEM((nc, N_DT, CHUNK), F32),
        pltpu.VMEM((CHUNK, SSD_W), F32),
        pltpu.VMEM((CHUNK, 2 * RET_W), BF16),
        pltpu.VMEM((nc, RET_HEADS, RET_HD, 2 * RET_HD), state_dt),
        pltpu.VMEM((nc, SSD_GROUPS, SSD_STATE, 2 * GROUP_W), state_dt),
        pltpu.VMEM((nc, RET_HEADS, RET_HD, 2 * RET_HD), BF16),
        pltpu.VMEM((nc, SSD_GROUPS, SSD_STATE, 2 * GROUP_W), BF16),
        pltpu.VMEM((RET_HEADS, CHUNK, CHUNK), F32),
        pltpu.VMEM((RET_HEADS, 4, CHUNK, LANES), F32),
        pltpu.VMEM((2, CHUNK, D_MODEL), F32),
        pltpu.VMEM((CHUNK, D_FF), BF16),
        pltpu.VMEM((LANES, D_MODEL), BF16),
    ]
    scratch += [pltpu.VMEM(sh, BF16) for sh in resident]
    scratch += [pltpu.SemaphoreType.DMA((len(BIG_WEIGHTS),))]
    if cast_weights:
        scratch += [pltpu.VMEM((2, W_IN_BLK, D_MODEL), F32),
                    pltpu.VMEM((2, 128, D_FF), F32),
                    pltpu.SemaphoreType.DMA((2,))]
    if use_rope:
        scratch += [pltpu.VMEM((nc, CHUNK, LANES), F32)] * 2
    kern = functools.partial(_layer_kernel, L=L, nb=nb, has_state=has_state, use_rope=use_rope,
                             emit_state=emit_state, cast_weights=cast_weights)
    return pl.pallas_call(
        kern,
        grid=(nb + 1, 2 * nc),
        in_specs=in_specs,
        out_specs=out_specs,
        out_shape=out_shape,
        scratch_shapes=scratch,
        compiler_params=pltpu.CompilerParams(dimension_semantics=("arbitrary", "arbitrary"),
                                             vmem_limit_bytes=VMEM_LIMIT),
        name=name,
    )(*args)


def kernel(x_prompt, x_sample, state_ret, state_ssd, c, c_ctx, w_in, ret_decay_fwd, ret_decay_bwd, conv_w, conv_b, dt_bias_fwd, dt_bias_bwd, a_log_fwd, a_log_bwd, d_skip, ssd_norm_w, w_out, ln1_g, ln1_b, w_gate, w_up, w_down, ln2_g, ln2_b, w_ada, b_ada):
    depth = w_in.shape[0]
    assert depth == 1, "single trunk layer"
    bp, lp, _ = x_prompt.shape
    bs, ls, _ = x_sample.shape
    assert lp % CHUNK == 0 and ls % CHUNK == 0 and ls % GRID_W == 0 and D_FF % FF_BLK == 0

    rows = -(-(bs + 1) // 8) * 8
    cond = jnp.zeros((rows, D_MODEL), F32).at[:bs].set(c).at[bs].set(c_ctx)
    mod = _ada_call(cond, w_ada[0], b_ada[0][None, :]).reshape(rows, 6, D_MODEL)
    mod_lat = mod[:bs]
    mod_ctx = mod[bs:bs + 1]

    convw = jnp.zeros((8, CONV_CH), F32).at[:CONV_W].set(conv_w[0])
    convb = conv_b[0][None, :]
    dt_bias = jnp.concatenate([dt_bias_fwd[0], dt_bias_bwd[0]])
    a_log = jnp.concatenate([a_log_fwd[0], a_log_bwd[0]])
    ret_decay = jnp.concatenate([ret_decay_fwd[0], ret_decay_bwd[0]])
    hp = jnp.zeros((8, LANES), F32)
    hp = hp.at[0, :N_DT].set(dt_bias).at[1, :N_DT].set(a_log)
    hp = hp.at[2, :2 * RET_HEADS].set(ret_decay).at[3, :SSD_HEADS].set(d_skip[0])
    hpc = jnp.zeros((N_DT, LANES), F32).at[:, 0].set(dt_bias).at[:, 1].set(a_log)
    weights = [jnp.swapaxes(w_in[0], 0, 1), convw, convb, hp, hpc, ssd_norm_w[0][None, :],
               w_out[0], ln1_g[0][None, :], ln1_b[0][None, :],
               w_gate[0], w_up[0], w_down[0], ln2_g[0][None, :], ln2_b[0][None, :]]

    yp, new_ret, new_ssd, *bf16_weights = _layer_call(
        x_prompt, mod_ctx, False, weights, None, True, False, True, "layer_ctx")
    for i, w in zip(BIG_WEIGHTS, bf16_weights):
        weights[i] = w
    states = (state_ret, jnp.swapaxes(state_ssd, -1, -2))
    (ys,) = _layer_call(x_sample, mod_lat, True, weights, states, False, True, False, "layer_lat")
    return (yp, ys, new_ret, jnp.swapaxes(new_ssd, -1, -2))
```

```python
import functools
import math

import jax
import jax.numpy as jnp
from jax import lax
from jax.experimental import pallas as pl
from jax.experimental.pallas import tpu as pltpu

F32 = jnp.float32
BF16 = jnp.bfloat16

D_MODEL = 1024
RET_W = 512
RET_HEADS = 4
RET_HD = 128
SSD_W = 512
SSD_HD = 64
SSD_HEADS = 8
SSD_GROUPS = 2
SSD_STATE = 128
HPG = SSD_HEADS // SSD_GROUPS
GROUP_W = HPG * SSD_HD
CONV_W = 5
CONV_CH = SSD_W + 2 * SSD_GROUPS * SSD_STATE
D_FF = 2816
GRID_W = 64
GRID_SHIFT = 6
ROPE_BASE = 10000.0
EPS = 1e-6
ALPHA = 2.0 ** 0.25
MAIN_COLS = 4 * RET_W + SSD_W + CONV_CH
XBC_COL0 = 4 * RET_W + SSD_W
N_DT = 2 * SSD_HEADS

CHUNK = 256
HALO = 8
FF_BLK = 256
ADA_COLS = 2048
MOD_ROWS = 8
CAST_STEPS = 8
CAST_WT_ROWS = 512
LANES = 128
VMEM_LIMIT = 62 * 1024 * 1024


def _dot(a, b):
    return jnp.dot(a, b, preferred_element_type=F32)


def _dot_nt(a, b):
    return lax.dot_general(a, b, (((1,), (1,)), ((), ())), preferred_element_type=F32)


def _silu(x):
    return x * jax.nn.sigmoid(x)


def _softplus(x):
    return jnp.maximum(x, 0.0) + jnp.log1p(jnp.exp(-jnp.abs(x)))


def _layer_norm(y, g, b):
    mu = jnp.mean(y, axis=-1, keepdims=True)
    yc = y - mu
    var = jnp.mean(yc * yc, axis=-1, keepdims=True)
    return yc * lax.rsqrt(var + EPS) * g + b


def _cumsum(x, axis):
    n = x.shape[axis]
    idx = lax.broadcasted_iota(jnp.int32, x.shape, axis)
    s = 1
    while s < n:
        x = x + jnp.where(idx >= s, pltpu.roll(x, s, axis), 0.0)
        s *= 2
    return x


def _expand4(cols, lane):
    a = jnp.where(lane < SSD_HD, cols[0], cols[1])
    b = jnp.where(lane < SSD_HD, cols[2], cols[3])
    return jnp.concatenate([a, b], axis=1)


def _interleave(a, b):
    ia = ib = 0
    while ia < len(a) or ib < len(b):
        if ib >= len(b) or (ia < len(a) and ia * len(b) <= ib * len(a)):
            a[ia]()
            ia += 1
        else:
            b[ib]()
            ib += 1


def _ada_kernel(c_ref, cctx_ref, w_ref, b_ref, o_ref):
    cond = jnp.concatenate([c_ref[...], jnp.broadcast_to(cctx_ref[...], (MOD_ROWS, D_MODEL))], axis=0)
    s = _silu(cond).astype(BF16)
    o_ref[...] = _dot(s, w_ref[...].astype(BF16)) + b_ref[...]


def _ada_call(c, c_ctx, w_ada, b_ada):
    assert c.shape == (MOD_ROWS, D_MODEL), c.shape
    n = w_ada.shape[1]
    return pl.pallas_call(
        _ada_kernel,
        grid=(n // ADA_COLS,),
        in_specs=[
            pl.BlockSpec((MOD_ROWS, D_MODEL), lambda j: (0, 0)),
            pl.BlockSpec((1, D_MODEL), lambda j: (0, 0)),
            pl.BlockSpec((D_MODEL, ADA_COLS), lambda j: (0, j)),
            pl.BlockSpec((1, ADA_COLS), lambda j: (0, j)),
        ],
        out_specs=pl.BlockSpec((2 * MOD_ROWS, ADA_COLS), lambda j: (0, j)),
        out_shape=jax.ShapeDtypeStruct((2 * MOD_ROWS, n), F32),
        compiler_params=pltpu.CompilerParams(dimension_semantics=("arbitrary",)),
        name="adaln_mod",
    )(c, c_ctx, w_ada, b_ada)


def _cast_kernel(wt_ref, *refs, wt_rows):
    blk = wt_ref.shape[0]
    row = lax.broadcasted_iota(jnp.int32, (blk, 1), 0) + pl.program_id(0) * blk
    n = len(refs) // 2
    refs[n][...] = jnp.where(row < wt_rows, wt_ref[...], 0.0).T.astype(BF16)
    for src, dst in zip(refs[:n], refs[n + 1:]):
        dst[...] = src[...].astype(BF16)


def _cast_call(wt, ws):
    for w in ws:
        assert w.shape[0] % (CAST_STEPS * 16) == 0, w.shape
    assert wt.shape[0] <= CAST_STEPS * CAST_WT_ROWS
    specs = [pl.BlockSpec((w.shape[0] // CAST_STEPS, w.shape[1]), lambda i: (i, 0)) for w in ws]
    return pl.pallas_call(
        functools.partial(_cast_kernel, wt_rows=wt.shape[0]),
        grid=(CAST_STEPS,),
        in_specs=[pl.BlockSpec((CAST_WT_ROWS, wt.shape[1]), lambda i: (i, 0))] + specs,
        out_specs=[pl.BlockSpec((wt.shape[1], CAST_WT_ROWS), lambda i: (0, i))] + specs,
        out_shape=[jax.ShapeDtypeStruct((wt.shape[1], -(-wt.shape[0] // LANES) * LANES), BF16)]
        + [jax.ShapeDtypeStruct(w.shape, BF16) for w in ws],
        compiler_params=pltpu.CompilerParams(dimension_semantics=("arbitrary",),
                                             vmem_limit_bytes=VMEM_LIMIT),
        name="weights_to_bf16",
    )(wt, *ws)


RV_TAIL_F, RV_TAIL_B, RV_CROSS_F, RV_CROSS_B = range(4)
SM_CONVW, SM_CONVB, SM_NORMW, SM_LN1G, SM_LN1B, SM_LN2G, SM_LN2B = 0, 5, 6, 7, 8, 9, 10
SM_HP = 16
SM_HPC = 24
SM_ROWS = 40


def _layer_kernel(*refs, L, nb, mod_per_seq, has_state, use_rope, emit_state):
    nc = L // CHUNK
    C = CHUNK
    cross = has_state or nc > 1
    it = iter(refs)
    x_ref, xprev_ref, xnext_ref, mod_ref, small_ref = (next(it) for _ in range(5))
    wmain_hbm, wout_hbm, wg_hbm, wu_hbm, wd_hbm = (next(it) for _ in range(5))
    if has_state:
        sret0_ref, sssd0_ref = next(it), next(it)
    out_ref = next(it)
    if emit_state:
        nret_ref, nssd_ref = next(it), next(it)
    (q_s, kT_s, v_s, g_s, z_s, stage_s, xs_s, bT_s, c_s, inccol_s, exccol_s, cfcb_s, dec_s,
     rowarg_s, y_s, mix_s, rloc_s, sloc_s, rent_s, sent_s, wdec_s, rvec_s, x1_s, hid_s, wdtr_s) = (
         next(it) for _ in range(25))
    wmain_ref, wout_ref, wg_ref, wu_ref, wd_ref, wsem = (next(it) for _ in range(6))
    if use_rope:
        cos_s, sin_s = next(it), next(it)

    seq = pl.program_id(0)
    step = pl.program_id(1)

    if mod_per_seq:
        row_mix = jnp.minimum(seq, nb - 1)
        row_ffn = jnp.minimum(jnp.where(step > nc, seq, jnp.maximum(seq - 1, 0)), nb - 1)
    else:
        row_mix = row_ffn = 0

    def mod_row(row, k):
        return mod_ref[pl.ds(row, 1), k * D_MODEL:(k + 1) * D_MODEL]

    big_weights = ((wmain_hbm, wmain_ref), (wout_hbm, wout_ref), (wg_hbm, wg_ref), (wu_hbm, wu_ref),
                   (wd_hbm, wd_ref))

    def weight_copy(i):
        return pltpu.make_async_copy(big_weights[i][0], big_weights[i][1], wsem.at[i])

    hp = small_ref[SM_HP:SM_HP + 8, 0:LANES]
    dt_bias_row = hp[0:1, :]
    nega_row = -jnp.exp(hp[1:2, :])
    lg_row = -_softplus(-hp[2:3, :])
    dskip_row = hp[3:4, :]
    hpc = small_ref[SM_HPC:SM_HPC + N_DT, 0:LANES]
    dt_bias_col = hpc[:, 0:1]
    nega_col = -jnp.exp(hpc[:, 1:2])

    lane = lax.broadcasted_iota(jnp.int32, (1, LANES), 1)

    def ret_decays(hd):
        lgf = lg_row[:, hd:hd + 1]
        lgb = lg_row[:, RET_HEADS + hd:RET_HEADS + hd + 1]
        return lgf, lgb

    def group_heads(gi):
        return [gi * HPG + k for k in range(HPG)]

    def expand_f(arr, gi):
        return _expand4([arr[:, hh:hh + 1] for hh in group_heads(gi)], lane)

    def expand_b(arr, gi):
        return _expand4([arr[:, SSD_HEADS + hh:SSD_HEADS + hh + 1] for hh in group_heads(gi)], lane)

    @pl.when((seq == 0) & (step == 0))
    def _():
        for i in range(len(big_weights)):
            weight_copy(i).start()
        weight_copy(0).wait()
        ii = lax.broadcasted_iota(jnp.int32, (C, C), 0)
        jj = lax.broadcasted_iota(jnp.int32, (C, C), 1)
        dmat = (ii - jj).astype(F32)
        irow = lax.broadcasted_iota(jnp.int32, (C, LANES), 0).astype(F32)
        for hd in range(RET_HEADS):
            lgf, lgb = ret_decays(hd)
            wdec_s[hd] = jnp.exp(jnp.where(jj <= ii, dmat * lgf, -dmat * lgb))
            rvec_s[hd, RV_TAIL_F] = jnp.exp((C - 1.0 - irow) * lgf)
            rvec_s[hd, RV_TAIL_B] = jnp.exp(irow * lgb)
            rvec_s[hd, RV_CROSS_F] = jnp.exp((irow + 1.0) * lgf)
            rvec_s[hd, RV_CROSS_B] = jnp.exp((C - irow) * lgb)
        wdtr_s[...] = wmain_ref[:, MAIN_COLS:MAIN_COLS + LANES].astype(F32).T.astype(BF16)
        if use_rope:
            ln = lax.broadcasted_iota(jnp.int32, (C, LANES), 1)
            nf = RET_HD // 4
            inv = jnp.exp((ln & (nf - 1)).astype(F32) * (-math.log(ROPE_BASE) / nf))
            for cc in range(nc):
                t = lax.broadcasted_iota(jnp.int32, (C, LANES), 0) + cc * C
                pos = jnp.where((ln & (2 * nf - 1)) < nf, t >> GRID_SHIFT, t & (GRID_W - 1)).astype(F32)
                ang = pos * inv
                cos_s[cc] = jnp.cos(ang)
                sin_s[cc] = jnp.where(ln < RET_HD // 2, -jnp.sin(ang), jnp.sin(ang))

    def project(c):
        st = {}
        CB = 256
        assert CONV_W == 5 and HALO >= CONV_W // 2

        def v_mod():
            sh1 = mod_row(row_mix, 0)
            sc1 = mod_row(row_mix, 1)
            xe = jnp.concatenate([xprev_ref[0], x_ref[0], xnext_ref[0]], axis=0)
            xm = xe * (1.0 + sc1) + sh1
            st["he"] = xm.astype(BF16)
            st["h"] = xm[HALO:HALO + C].astype(BF16)

        def m_main(name, lo, hi, halo=False):
            def run():
                st[name] = _dot(st["he" if halo else "h"], wmain_ref[:, lo:hi])
            return run

        def m_dt():
            raw_c = _dot(st["h"], wmain_ref[:, MAIN_COLS:MAIN_COLS + LANES])
            st["dtc"] = raw_c + pltpu.roll(raw_c, SSD_HEADS, 1)
            raw_r = _dot_nt(wdtr_s[0:N_DT, :], st["h"])
            st["dtr"] = raw_r + pltpu.roll(raw_r, SSD_HEADS, 0)

        def v_dt():
            dt_c = _softplus(st["dtc"] + dt_bias_row)
            lac = dt_c * nega_row
            inc_col = _cumsum(lac, 0)
            exc_col = inc_col - lac
            tot_col = inc_col[C - 1:C, :]
            inccol_s[c] = inc_col
            exccol_s[c] = exc_col
            dec_s[c] = jnp.broadcast_to(jnp.exp(tot_col), (8, LANES))
            if cross:
                cfcb_s[c] = jnp.exp(jnp.where(lane < SSD_HEADS, inc_col, tot_col - exc_col))
            st["sf"] = jnp.exp(tot_col - inc_col) * dt_c
            st["sb"] = jnp.exp(exc_col) * dt_c
            dt_r = _softplus(st["dtr"] + dt_bias_col)
            lar = dt_r * nega_col
            inc_row = _cumsum(lar, 1)
            ldt = jnp.log(dt_r)
            rid = lax.broadcasted_iota(jnp.int32, (N_DT, C), 0)
            rowarg_s[c] = jnp.where(rid < SSD_HEADS, inc_row - ldt, inc_row - lar + ldt)

        def v_stage(hf):
            def run():
                pe = st["pe%d" % hf]
                cs = slice(hf * 512, (hf + 1) * 512)
                stage_s[0:HALO, cs] = jnp.where(c > 0, pe[0:HALO], 0.0)
                stage_s[HALO:HALO + C, cs] = pe[HALO:HALO + C]
                stage_s[HALO + C:, cs] = jnp.where(c < nc - 1, pe[HALO + C:], 0.0)
            return run

        def conv_block(cb):
            cs = slice(cb * CB, (cb + 1) * CB)
            rows = C + 2 * HALO
            xin = stage_s[:, cs]
            taps = [small_ref[SM_CONVW + k:SM_CONVW + k + 1, cs] * xin for k in range(CONV_W)]
            up = lambda a: pltpu.roll(a, rows - 1, 0)
            down = lambda a: pltpu.roll(a, 1, 0)
            acc = taps[2] + up(taps[3] + up(taps[4])) + down(taps[1] + down(taps[0]))
            return _silu(acc[HALO:HALO + C] + small_ref[SM_CONVB:SM_CONVB + 1, cs])

        def v_conv_x(cb):
            def run():
                xs_s[c, :, cb * CB:(cb + 1) * CB] = conv_block(cb)
            return run

        def v_conv_b():
            bT_s[c] = conv_block(SSD_W // CB).T.astype(BF16)

        def v_conv_c():
            c_s[c] = conv_block(SSD_W // CB + 1).astype(BF16)

        def rope(a):
            if not use_rope:
                return a
            return a * cos_s[c] + pltpu.roll(a, RET_HD // 2, 1) * sin_s[c]

        def v_q():
            for hd in range(RET_HEADS):
                sl = slice(hd * RET_HD, (hd + 1) * RET_HD)
                q_s[c, :, sl] = rope(st["pq"][:, sl]).astype(BF16)

        def v_k():
            for hd in range(RET_HEADS):
                sl = slice(hd * RET_HD, (hd + 1) * RET_HD)
                kh = rope(st["pk"][:, sl]) * (RET_HD ** -0.5)
                kT_s[c, sl, :] = kh.T.astype(BF16)

        def v_v():
            v_s[c] = st["pv"].astype(BF16)

        def v_g():
            g_s[c] = _silu(st["pg"]).astype(BF16)

        def v_z():
            z_s[c] = _silu(st["pz"]).astype(BF16)

        def m_sloc(gi):
            def run():
                xg = xs_s[c, :, gi * GROUP_W:(gi + 1) * GROUP_W]
                vcat = jnp.concatenate([xg * expand_f(st["sf"], gi), xg * expand_b(st["sb"], gi)],
                                       axis=1).astype(BF16)
                sloc_s[c, gi] = _dot(bT_s[c, gi * SSD_STATE:(gi + 1) * SSD_STATE, :], vcat).astype(sloc_s.dtype)
            return run

        def m_rloc(hd):
            def run():
                sl = slice(hd * RET_HD, (hd + 1) * RET_HD)
                vf = st["pv"][:, sl]
                vcat = jnp.concatenate([vf * rvec_s[hd, RV_TAIL_F], vf * rvec_s[hd, RV_TAIL_B]],
                                       axis=1).astype(BF16)
                rloc_s[c, hd] = _dot(kT_s[c, sl, :], vcat).astype(rloc_s.dtype)
            return run

        m_pe0 = m_main("pe0", XBC_COL0, XBC_COL0 + 512, halo=True)
        m_pe1 = m_main("pe1", XBC_COL0 + 512, MAIN_COLS, halo=True)
        m_q = m_main("pq", 0, RET_W)
        m_k = m_main("pk", RET_W, 2 * RET_W)
        m_v = m_main("pv", 2 * RET_W, 3 * RET_W)
        m_g = m_main("pg", 3 * RET_W, 4 * RET_W)
        m_z = m_main("pz", 4 * RET_W, XBC_COL0)
        order = [v_mod, m_dt, m_pe0, m_pe1, v_dt, m_q, v_stage(0), v_stage(1), m_k, v_conv_x(0), m_v,
                 v_conv_x(1), v_q, m_g, v_conv_b, v_k, m_z, v_conv_c, v_v,
                 m_rloc(0), m_rloc(1), v_g, m_rloc(2), m_rloc(3), m_sloc(0), m_sloc(1), v_z]
        for piece in order:
            piece()

    def recurrences():
        for hd in range(RET_HEADS):
            lgf, lgb = ret_decays(hd)
            dec_f = jnp.exp(C * lgf)
            dec_b = jnp.exp(C * lgb)
            if has_state:
                ent_f = sret0_ref[0, 0, 0, hd]
                ent_b = sret0_ref[0, 0, 1, hd]
            else:
                ent_f = jnp.zeros((RET_HD, RET_HD), F32)
                ent_b = jnp.zeros((RET_HD, RET_HD), F32)
            for c in range(nc):
                if cross:
                    rent_s[c, hd, :, 0:RET_HD] = ent_f.astype(BF16)
                ent_f = dec_f * ent_f + rloc_s[c, hd, :, 0:RET_HD]
            for c in range(nc - 1, -1, -1):
                if cross:
                    rent_s[c, hd, :, RET_HD:] = ent_b.astype(BF16)
                ent_b = dec_b * ent_b + rloc_s[c, hd, :, RET_HD:]
            if emit_state:
                nret_ref[0, 0, 0, hd] = ent_f
                nret_ref[0, 0, 1, hd] = ent_b

        for gi in range(SSD_GROUPS):
            heads = group_heads(gi)
            if has_state:
                ent_f = jnp.concatenate([sssd0_ref[0, 0, 0, hh] for hh in heads], axis=0).T
                ent_b = jnp.concatenate([sssd0_ref[0, 0, 1, hh] for hh in heads], axis=0).T
            else:
                ent_f = jnp.zeros((SSD_STATE, GROUP_W), F32)
                ent_b = jnp.zeros((SSD_STATE, GROUP_W), F32)
            for c in range(nc):
                if cross:
                    sent_s[c, gi, :, 0:GROUP_W] = ent_f.astype(BF16)
                ent_f = expand_f(dec_s[c, 0:1, :], gi) * ent_f + sloc_s[c, gi, :, 0:GROUP_W]
            for c in range(nc - 1, -1, -1):
                if cross:
                    sent_s[c, gi, :, GROUP_W:] = ent_b.astype(BF16)
                ent_b = expand_b(dec_s[c, 0:1, :], gi) * ent_b + sloc_s[c, gi, :, GROUP_W:]
            if emit_state:
                ent_ft = ent_f.T
                ent_bt = ent_b.T
                for k, hh in enumerate(heads):
                    nssd_ref[0, 0, 0, hh] = ent_ft[k * SSD_HD:(k + 1) * SSD_HD, :]
                    nssd_ref[0, 0, 1, hh] = ent_bt[k * SSD_HD:(k + 1) * SSD_HD, :]

    def emit_pieces(c, slot):
        rsl = [slice(hd * RET_HD, (hd + 1) * RET_HD) for hd in range(RET_HEADS)]
        gsl = [slice(gi * SSD_STATE, (gi + 1) * SSD_STATE) for gi in range(SSD_GROUPS)]
        xsl = [slice(gi * GROUP_W, (gi + 1) * GROUP_W) for gi in range(SSD_GROUPS)]
        st = {}

        def scores():
            ii = lax.broadcasted_iota(jnp.int32, (C, C), 0)
            jj = lax.broadcasted_iota(jnp.int32, (C, C), 1)
            st["causal"] = jj <= ii
            qs = [q_s[c, :, sl] for sl in rsl]
            cms = [c_s[c, :, sl] for sl in gsl]
            st["sc_s"] = [_dot(cms[gi], bT_s[c, gsl[gi], :]) for gi in range(SSD_GROUPS)]
            st["sc_r"] = [_dot(qs[hd], kT_s[c, rsl[hd], :]) for hd in range(RET_HEADS)]
            if cross:
                st["yc_s"] = [_dot(cms[gi], sent_s[c, gi]) for gi in range(SSD_GROUPS)]
                st["yc_r"] = [_dot(qs[hd], rent_s[c, hd]) for hd in range(RET_HEADS)]
            st["inc_col"] = inccol_s[c]
            st["exc_col"] = exccol_s[c]
            st["rowarg"] = rowarg_s[c]

        def ssd_head(gi, k):
            def run():
                hh = gi * HPG + k
                hb = SSD_HEADS + hh
                arg = jnp.where(st["causal"],
                                st["inc_col"][:, hh:hh + 1] - st["rowarg"][hh:hh + 1, :],
                                st["rowarg"][hb:hb + 1, :] - st["exc_col"][:, hb:hb + 1])
                m = (st["sc_s"][gi] * jnp.exp(arg)).astype(BF16)
                xh = xs_s[c, :, hh * SSD_HD:(hh + 1) * SSD_HD].astype(BF16)
                y_s[:, hh * SSD_HD:(hh + 1) * SSD_HD] = _dot(m, xh)
            return run

        def ssd_group(gi):
            def run():
                xg = xs_s[c, :, xsl[gi]]
                yg = y_s[:, xsl[gi]] + expand_f(dskip_row, gi) * xg
                if cross:
                    cfcb = cfcb_s[c]
                    yc = st["yc_s"][gi]
                    yg = yg + expand_f(cfcb, gi) * yc[:, 0:GROUP_W] + expand_b(cfcb, gi) * yc[:, GROUP_W:]
                y_s[:, xsl[gi]] = yg
            return run

        def ret_head(hd):
            def run():
                m = (st["sc_r"][hd] * wdec_s[hd]).astype(BF16)
                o = _dot(m, v_s[c, :, rsl[hd]])
                if cross:
                    yc = st["yc_r"][hd]
                    o = o + rvec_s[hd, RV_CROSS_F] * yc[:, 0:RET_HD] + rvec_s[hd, RV_CROSS_B] * yc[:, RET_HD:]
                o = o * lax.rsqrt(jnp.mean(o * o, axis=-1, keepdims=True) + EPS)
                mix_s[:, rsl[hd]] = (g_s[c, :, rsl[hd]].astype(F32) * o).astype(BF16)
            return run

        def ssd_norm():
            yz = y_s[...] * z_s[c].astype(F32)
            yn = yz * lax.rsqrt(jnp.mean(yz * yz, axis=-1, keepdims=True) + EPS) * small_ref[SM_NORMW:SM_NORMW + 1, 0:SSD_W]
            mix_s[:, RET_W:] = yn.astype(BF16)

        def out_proj():
            g1 = mod_row(row_mix, 2)
            y = ALPHA * x_ref[0] + g1 * _dot(mix_s[...], wout_ref[...])
            x1_s[slot] = _layer_norm(y, small_ref[SM_LN1G:SM_LN1G + 1, :], small_ref[SM_LN1B:SM_LN1B + 1, :])

        pieces = [scores]
        for gi in range(SSD_GROUPS):
            pieces += [ssd_head(gi, k) for k in range(HPG)] + [ssd_group(gi)]
        pieces += [ret_head(hd) for hd in range(RET_HEADS)] + [ssd_norm, out_proj]
        return pieces

    def ffn_pieces(slot):
        st = {}

        def start():
            sh2 = mod_row(row_ffn, 3)
            sc2 = mod_row(row_ffn, 4)
            st["h2"] = (x1_s[slot] * (1.0 + sc2) + sh2).astype(BF16)

        def hidden(j):
            def run():
                js = slice(j * FF_BLK, (j + 1) * FF_BLK)
                h2 = st["h2"]
                hid_s[:, js] = (_silu(_dot(h2, wg_ref[:, js])) * _dot(h2, wu_ref[:, js])).astype(BF16)
            return run

        def finish():
            g2 = mod_row(row_ffn, 5)
            y = ALPHA * x1_s[slot] + g2 * _dot(hid_s[...], wd_ref[...])
            out_ref[0] = _layer_norm(y, small_ref[SM_LN2G:SM_LN2G + 1, :], small_ref[SM_LN2B:SM_LN2B + 1, :])

        return [start] + [hidden(j) for j in range(D_FF // FF_BLK)] + [finish]

    def run_all(pieces):
        for p in pieces:
            p()

    @pl.when((step < nc) & (seq < nb))
    def _():
        project(step)

    @pl.when((step == nc) & (seq < nb))
    def _():
        recurrences()

    kk = step - nc
    slot = (seq * nc + kk) & 1
    first = (seq == 0) & (step == nc)

    @pl.when(first)
    def _():
        for i in range(1, len(big_weights)):
            weight_copy(i).wait()
        run_all(emit_pieces(kk, slot))

    @pl.when((step >= nc) & (seq < nb) & jnp.logical_not(first))
    def _():
        _interleave(ffn_pieces(1 - slot), emit_pieces(kk, slot))

    @pl.when((seq == nb) & (step == nc))
    def _():
        run_all(ffn_pieces((nb * nc - 1) & 1))


def _const_spec(shape):
    nd = len(shape)
    return pl.BlockSpec(shape, lambda b, s: (0,) * nd, pipeline_mode=pl.Buffered(1))


def _layer_call(x, mod, mod_block, mod_per_seq, small, weights, states, emit_state, use_rope, name):
    nb, L, _ = x.shape
    nc = L // CHUNK
    hpc_blocks = CHUNK // HALO
    has_state = states is not None
    last = nb - 1

    def chunk_of(b, s):
        return jnp.where(b > last, nc - 1, jnp.where(s < nc, s, s - nc))

    def halo_chunk(b, s):
        return jnp.where(b > last, nc - 1, jnp.minimum(s, nc - 1))

    def seq_of(b):
        return jnp.minimum(b, last)

    def out_map(b, s):
        live = (s > nc) & (b < nb)
        idle_chunk = jnp.where(b == 0, 0, nc - 1)
        return (jnp.where(live, b, jnp.maximum(b - 1, 0)), jnp.where(live, s - nc - 1, idle_chunk), 0)

    assert not mod_per_seq or nb <= MOD_ROWS
    in_specs = [
        pl.BlockSpec((1, CHUNK, D_MODEL), lambda b, s: (seq_of(b), chunk_of(b, s), 0)),
        pl.BlockSpec((1, HALO, D_MODEL),
                     lambda b, s: (seq_of(b), jnp.maximum(halo_chunk(b, s) * hpc_blocks - 1, 0), 0)),
        pl.BlockSpec((1, HALO, D_MODEL),
                     lambda b, s: (seq_of(b), jnp.minimum((halo_chunk(b, s) + 1) * hpc_blocks, nc * hpc_blocks - 1), 0)),
        pl.BlockSpec((MOD_ROWS, 6 * D_MODEL), lambda b, s: (mod_block, 0)),
        _const_spec(small.shape),
    ] + [pl.BlockSpec(memory_space=pl.ANY)] * len(weights)
    args = [x, x, x, mod, small] + list(weights)
    ret_block = (1, 1, 2, RET_HEADS, RET_HD, RET_HD)
    ssd_block = (1, 1, 2, SSD_HEADS, SSD_HD, SSD_STATE)
    state_map = lambda b, s: (seq_of(b), 0, 0, 0, 0, 0)
    if has_state:
        in_specs += [pl.BlockSpec(ret_block, state_map, pipeline_mode=pl.Buffered(1)),
                     pl.BlockSpec(ssd_block, state_map, pipeline_mode=pl.Buffered(1))]
        args += list(states)
    out_shape = [jax.ShapeDtypeStruct((nb, L, D_MODEL), F32)]
    out_specs = [pl.BlockSpec((1, CHUNK, D_MODEL), out_map)]
    if emit_state:
        out_shape += [jax.ShapeDtypeStruct((nb,) + ret_block[1:], F32),
                      jax.ShapeDtypeStruct((nb,) + ssd_block[1:], F32)]
        out_specs += [pl.BlockSpec(ret_block, state_map), pl.BlockSpec(ssd_block, state_map)]
    state_dt = F32 if emit_state else BF16
    scratch = [
        pltpu.VMEM((nc, CHUNK, RET_W), BF16),
        pltpu.VMEM((nc, RET_W, CHUNK), BF16),
        pltpu.VMEM((nc, CHUNK, RET_W), BF16),
        pltpu.VMEM((nc, CHUNK, RET_W), BF16),
        pltpu.VMEM((nc, CHUNK, SSD_W), BF16),
        pltpu.VMEM((CHUNK + 2 * HALO, CONV_CH), F32),
        pltpu.VMEM((nc, CHUNK, SSD_W), F32),
        pltpu.VMEM((nc, SSD_GROUPS * SSD_STATE, CHUNK), BF16),
        pltpu.VMEM((nc, CHUNK, SSD_GROUPS * SSD_STATE), BF16),
        pltpu.VMEM((nc, CHUNK, LANES), F32),
        pltpu.VMEM((nc, CHUNK, LANES), F32),
        pltpu.VMEM((nc, CHUNK, LANES), F32),
        pltpu.VMEM((nc, 8, LANES), F32),
        pltpu.VMEM((nc, N_DT, CHUNK), F32),
        pltpu.VMEM((CHUNK, SSD_W), F32),
        pltpu.VMEM((CHUNK, 2 * RET_W), BF16),
        pltpu.VMEM((nc, RET_HEADS, RET_HD, 2 * RET_HD), state_dt),
        pltpu.VMEM((nc, SSD_GROUPS, SSD_STATE, 2 * GROUP_W), state_dt),
        pltpu.VMEM((nc, RET_HEADS, RET_HD, 2 * RET_HD), BF16),
        pltpu.VMEM((nc, SSD_GROUPS, SSD_STATE, 2 * GROUP_W), BF16),
        pltpu.VMEM((RET_HEADS, CHUNK, CHUNK), F32),
        pltpu.VMEM((RET_HEADS, 4, CHUNK, LANES), F32),
        pltpu.VMEM((2, CHUNK, D_MODEL), F32),
        pltpu.VMEM((CHUNK, D_FF), BF16),
        pltpu.VMEM((LANES, D_MODEL), BF16),
    ]
    scratch += [pltpu.VMEM(w.shape, BF16) for w in weights]
    scratch += [pltpu.SemaphoreType.DMA((len(weights),))]
    if use_rope:
        scratch += [pltpu.VMEM((nc, CHUNK, LANES), F32)] * 2
    kern = functools.partial(_layer_kernel, L=L, nb=nb, mod_per_seq=mod_per_seq, has_state=has_state, use_rope=use_rope,
                             emit_state=emit_state)
    return pl.pallas_call(
        kern,
        grid=(nb + 1, 2 * nc),
        in_specs=in_specs,
        out_specs=out_specs,
        out_shape=out_shape,
        scratch_shapes=scratch,
        compiler_params=pltpu.CompilerParams(dimension_semantics=("arbitrary", "arbitrary"),
                                             vmem_limit_bytes=VMEM_LIMIT),
        name=name,
    )(*args)


def kernel(x_prompt, x_sample, state_ret, state_ssd, c, c_ctx, w_in, ret_decay_fwd, ret_decay_bwd, conv_w, conv_b, dt_bias_fwd, dt_bias_bwd, a_log_fwd, a_log_bwd, d_skip, ssd_norm_w, w_out, ln1_g, ln1_b, w_gate, w_up, w_down, ln2_g, ln2_b, w_ada, b_ada):
    depth = w_in.shape[0]
    assert depth == 1, "single trunk layer"
    bp, lp, _ = x_prompt.shape
    bs, ls, _ = x_sample.shape
    assert lp % CHUNK == 0 and ls % CHUNK == 0 and ls % GRID_W == 0 and D_FF % FF_BLK == 0

    mod = _ada_call(c, c_ctx[None, :], w_ada[0], b_ada[0][None, :])

    def row(v):
        return jnp.pad(v, (0, D_MODEL - v.shape[0]))[None, :]

    dt_bias = jnp.concatenate([dt_bias_fwd[0], dt_bias_bwd[0]])
    a_log = jnp.concatenate([a_log_fwd[0], a_log_bwd[0]])
    ret_decay = jnp.concatenate([ret_decay_fwd[0], ret_decay_bwd[0]])
    small = jnp.concatenate([
        conv_w[0], row(conv_b[0]), row(ssd_norm_w[0]), row(ln1_g[0]), row(ln1_b[0]), row(ln2_g[0]), row(ln2_b[0]),
        jnp.zeros((SM_HP - SM_LN2B - 1, D_MODEL), F32),
        row(dt_bias), row(a_log), row(ret_decay), row(d_skip[0]), jnp.zeros((SM_HPC - SM_HP - 4, D_MODEL), F32),
        jnp.pad(jnp.stack([dt_bias, a_log], axis=1), ((0, 0), (0, D_MODEL - 2))),
    ], axis=0)
    assert small.shape == (SM_ROWS, D_MODEL)
    w_in_b, w_out_b, w_gate_b, w_up_b, w_down_b = _cast_call(
        jnp.swapaxes(w_in[0], 0, 1), [w_out[0], w_gate[0], w_up[0], w_down[0]])
    weights = (w_in_b, w_out_b, w_gate_b, w_up_b, w_down_b)

    yp, new_ret, new_ssd = _layer_call(x_prompt, mod, 1, False, small, weights, None, True, False, "layer_ctx")
    states = (state_ret, jnp.swapaxes(state_ssd, -1, -2))
    (ys,) = _layer_call(x_sample, mod, 0, True, small, weights, states, False, True, "layer_lat")
    return (yp, ys, new_ret, jnp.swapaxes(new_ssd, -1, -2))
```

```python
import functools
import math

import jax
import jax.numpy as jnp
from jax import lax
from jax.experimental import pallas as pl
from jax.experimental.pallas import tpu as pltpu

F32 = jnp.float32
BF16 = jnp.bfloat16

D_MODEL = 1024
RET_W = 512
RET_HEADS = 4
RET_HD = 128
SSD_W = 512
SSD_HD = 64
SSD_HEADS = 8
SSD_GROUPS = 2
SSD_STATE = 128
HPG = SSD_HEADS // SSD_GROUPS
GROUP_W = HPG * SSD_HD
CONV_W = 5
CONV_CH = SSD_W + 2 * SSD_GROUPS * SSD_STATE
D_FF = 2816
GRID_W = 64
GRID_SHIFT = 6
ROPE_BASE = 10000.0
EPS = 1e-6
ALPHA = 2.0 ** 0.25
MAIN_COLS = 4 * RET_W + SSD_W + CONV_CH
XBC_COL0 = 4 * RET_W + SSD_W
N_DT = 2 * SSD_HEADS

CHUNK = 256
HALO = 8
FF_BLK = 256
MOD_ROWS = 8
CAST_STEPS = 8
CAST_WT_ROWS = 512
LANES = 128
VMEM_LIMIT = 62 * 1024 * 1024


def _dot(a, b):
    return jnp.dot(a, b, preferred_element_type=F32)


def _dot_nt(a, b):
    return lax.dot_general(a, b, (((1,), (1,)), ((), ())), preferred_element_type=F32)


def _silu(x):
    return x * jax.nn.sigmoid(x)


def _softplus(x):
    return jnp.maximum(x, 0.0) + jnp.log1p(jnp.exp(-jnp.abs(x)))


def _layer_norm(y, g, b):
    mu = jnp.mean(y, axis=-1, keepdims=True)
    yc = y - mu
    var = jnp.mean(yc * yc, axis=-1, keepdims=True)
    return yc * lax.rsqrt(var + EPS) * g + b


def _cumsum(x, axis):
    n = x.shape[axis]
    idx = lax.broadcasted_iota(jnp.int32, x.shape, axis)
    s = 1
    while s < n:
        x = x + jnp.where(idx >= s, pltpu.roll(x, s, axis), 0.0)
        s *= 2
    return x


def _expand4(cols, lane):
    a = jnp.where(lane < SSD_HD, cols[0], cols[1])
    b = jnp.where(lane < SSD_HD, cols[2], cols[3])
    return jnp.concatenate([a, b], axis=1)


def _interleave(a, b):
    ia = ib = 0
    while ia < len(a) or ib < len(b):
        if ib >= len(b) or (ia < len(a) and ia * len(b) <= ib * len(a)):
            a[ia]()
            ia += 1
        else:
            b[ib]()
            ib += 1


def _prep_kernel(c_ref, cctx_ref, wada_ref, bada_ref, wt_ref, *refs, wt_rows):
    n = (len(refs) - 2) // 2
    srcs, mod_ref, wt_out, dsts = refs[:n], refs[n], refs[n + 1], refs[n + 2:]
    cond = jnp.concatenate([c_ref[...], jnp.broadcast_to(cctx_ref[...], (MOD_ROWS, D_MODEL))], axis=0)
    mod_ref[...] = _dot(_silu(cond).astype(BF16), wada_ref[...].astype(BF16)) + bada_ref[...]
    blk = wt_ref.shape[0]
    row = lax.broadcasted_iota(jnp.int32, (blk, 1), 0) + pl.program_id(0) * blk
    wt_out[...] = jnp.where(row < wt_rows, wt_ref[...], 0.0).T.astype(BF16)
    for src, dst in zip(srcs, dsts):
        dst[...] = src[...].astype(BF16)


def _prep_call(c, c_ctx, w_ada, b_ada, wt, ws):
    assert c.shape == (MOD_ROWS, D_MODEL), c.shape
    for w in ws:
        assert w.shape[0] % (CAST_STEPS * 16) == 0, w.shape
    assert wt.shape[0] <= CAST_STEPS * CAST_WT_ROWS
    n_mod = w_ada.shape[1]
    ada_cols = n_mod // CAST_STEPS
    assert ada_cols % LANES == 0
    specs = [pl.BlockSpec((w.shape[0] // CAST_STEPS, w.shape[1]), lambda i: (i, 0)) for w in ws]
    return pl.pallas_call(
        functools.partial(_prep_kernel, wt_rows=wt.shape[0]),
        grid=(CAST_STEPS,),
        in_specs=[pl.BlockSpec((MOD_ROWS, D_MODEL), lambda i: (0, 0)),
                  pl.BlockSpec((1, D_MODEL), lambda i: (0, 0)),
                  pl.BlockSpec((D_MODEL, ada_cols), lambda i: (0, i)),
                  pl.BlockSpec((1, ada_cols), lambda i: (0, i)),
                  pl.BlockSpec((CAST_WT_ROWS, wt.shape[1]), lambda i: (i, 0))] + specs,
        out_specs=[pl.BlockSpec((2 * MOD_ROWS, ada_cols), lambda i: (0, i)),
                   pl.BlockSpec((wt.shape[1], CAST_WT_ROWS), lambda i: (0, i))] + specs,
        out_shape=[jax.ShapeDtypeStruct((2 * MOD_ROWS, n_mod), F32),
                   jax.ShapeDtypeStruct((wt.shape[1], -(-wt.shape[0] // LANES) * LANES), BF16)]
        + [jax.ShapeDtypeStruct(w.shape, BF16) for w in ws],
        compiler_params=pltpu.CompilerParams(dimension_semantics=("arbitrary",),
                                             vmem_limit_bytes=VMEM_LIMIT),
        name="prep_mod_and_weights",
    )(c, c_ctx, w_ada, b_ada, wt, *ws)


RV_TAIL_F, RV_TAIL_B, RV_CROSS_F, RV_CROSS_B = range(4)
BIG_WEIGHTS = (0, 6, 9, 10, 11)


def _layer_kernel(*refs, L, nb, mod_per_seq, has_state, use_rope, emit_state):
    nc = L // CHUNK
    C = CHUNK
    cross = has_state or nc > 1
    it = iter(refs)
    x_ref, xprev_ref, xnext_ref, mod_ref, wmain_hbm = (next(it) for _ in range(5))
    convw_ref, convb_ref, hp_ref, hpc_ref, normw_ref = (next(it) for _ in range(5))
    wout_hbm, ln1g_ref, ln1b_ref = (next(it) for _ in range(3))
    wg_hbm, wu_hbm, wd_hbm, ln2g_ref, ln2b_ref = (next(it) for _ in range(5))
    if has_state:
        sret0_ref, sssd0_ref = next(it), next(it)
    out_ref = next(it)
    if emit_state:
        nret_ref, nssd_ref = next(it), next(it)
    (q_s, kT_s, v_s, g_s, z_s, stage_s, xs_s, bT_s, c_s, inccol_s, exccol_s, cfcb_s, dec_s,
     rowarg_s, y_s, mix_s, rloc_s, sloc_s, rent_s, sent_s, wdec_s, rvec_s, x1_s, hid_s, wdtr_s) = (
         next(it) for _ in range(25))
    wmain_ref, wout_ref, wg_ref, wu_ref, wd_ref, wsem = (next(it) for _ in range(6))
    if use_rope:
        cos_s, sin_s = next(it), next(it)

    seq = pl.program_id(0)
    step = pl.program_id(1)

    if mod_per_seq:
        row_mix = jnp.minimum(seq, nb - 1)
        row_ffn = jnp.minimum(jnp.where(step > nc, seq, jnp.maximum(seq - 1, 0)), nb - 1)
    else:
        row_mix = row_ffn = 0

    def mod_row(row, k):
        return mod_ref[pl.ds(row, 1), k * D_MODEL:(k + 1) * D_MODEL]

    big_weights = ((wmain_hbm, wmain_ref), (wout_hbm, wout_ref), (wg_hbm, wg_ref), (wu_hbm, wu_ref),
                   (wd_hbm, wd_ref))

    def weight_copy(i):
        return pltpu.make_async_copy(big_weights[i][0], big_weights[i][1], wsem.at[i])

    hp = hp_ref[...]
    dt_bias_row = hp[0:1, :]
    nega_row = -jnp.exp(hp[1:2, :])
    lg_row = -_softplus(-hp[2:3, :])
    dskip_row = hp[3:4, :]
    hpc = hpc_ref[...]
    dt_bias_col = hpc[:, 0:1]
    nega_col = -jnp.exp(hpc[:, 1:2])

    lane = lax.broadcasted_iota(jnp.int32, (1, LANES), 1)

    def ret_decays(hd):
        lgf = lg_row[:, hd:hd + 1]
        lgb = lg_row[:, RET_HEADS + hd:RET_HEADS + hd + 1]
        return lgf, lgb

    def group_heads(gi):
        return [gi * HPG + k for k in range(HPG)]

    def expand_f(arr, gi):
        return _expand4([arr[:, hh:hh + 1] for hh in group_heads(gi)], lane)

    def expand_b(arr, gi):
        return _expand4([arr[:, SSD_HEADS + hh:SSD_HEADS + hh + 1] for hh in group_heads(gi)], lane)

    @pl.when((seq == 0) & (step == 0))
    def _():
        for i in range(len(big_weights)):
            weight_copy(i).start()
        weight_copy(0).wait()
        ii = lax.broadcasted_iota(jnp.int32, (C, C), 0)
        jj = lax.broadcasted_iota(jnp.int32, (C, C), 1)
        dmat = (ii - jj).astype(F32)
        irow = lax.broadcasted_iota(jnp.int32, (C, LANES), 0).astype(F32)
        for hd in range(RET_HEADS):
            lgf, lgb = ret_decays(hd)
            wdec_s[hd] = jnp.exp(jnp.where(jj <= ii, dmat * lgf, -dmat * lgb))
            rvec_s[hd, RV_TAIL_F] = jnp.exp((C - 1.0 - irow) * lgf)
            rvec_s[hd, RV_TAIL_B] = jnp.exp(irow * lgb)
            rvec_s[hd, RV_CROSS_F] = jnp.exp((irow + 1.0) * lgf)
            rvec_s[hd, RV_CROSS_B] = jnp.exp((C - irow) * lgb)
        wdtr_s[...] = wmain_ref[:, MAIN_COLS:MAIN_COLS + LANES].astype(F32).T.astype(BF16)
        if use_rope:
            ln = lax.broadcasted_iota(jnp.int32, (C, LANES), 1)
            nf = RET_HD // 4
            inv = jnp.exp((ln & (nf - 1)).astype(F32) * (-math.log(ROPE_BASE) / nf))
            for cc in range(nc):
                t = lax.broadcasted_iota(jnp.int32, (C, LANES), 0) + cc * C
                pos = jnp.where((ln & (2 * nf - 1)) < nf, t >> GRID_SHIFT, t & (GRID_W - 1)).astype(F32)
                ang = pos * inv
                cos_s[cc] = jnp.cos(ang)
                sin_s[cc] = jnp.where(ln < RET_HD // 2, -jnp.sin(ang), jnp.sin(ang))

    def project(c):
        st = {}
        CB = 256
        assert CONV_W == 5 and HALO >= CONV_W // 2

        def v_mod():
            sh1 = mod_row(row_mix, 0)
            sc1 = mod_row(row_mix, 1)
            xe = jnp.concatenate([xprev_ref[0], x_ref[0], xnext_ref[0]], axis=0)
            xm = xe * (1.0 + sc1) + sh1
            st["he"] = xm.astype(BF16)
            st["h"] = xm[HALO:HALO + C].astype(BF16)

        def m_main(name, lo, hi, halo=False):
            def run():
                st[name] = _dot(st["he" if halo else "h"], wmain_ref[:, lo:hi])
            return run

        def m_dt():
            raw_c = _dot(st["h"], wmain_ref[:, MAIN_COLS:MAIN_COLS + LANES])
            st["dtc"] = raw_c + pltpu.roll(raw_c, SSD_HEADS, 1)
            raw_r = _dot_nt(wdtr_s[0:N_DT, :], st["h"])
            st["dtr"] = raw_r + pltpu.roll(raw_r, SSD_HEADS, 0)

        def v_dt():
            dt_c = _softplus(st["dtc"] + dt_bias_row)
            lac = dt_c * nega_row
            inc_col = _cumsum(lac, 0)
            exc_col = inc_col - lac
            tot_col = inc_col[C - 1:C, :]
            inccol_s[c] = inc_col
            exccol_s[c] = exc_col
            dec_s[c] = jnp.broadcast_to(jnp.exp(tot_col), (8, LANES))
            if cross:
                cfcb_s[c] = jnp.exp(jnp.where(lane < SSD_HEADS, inc_col, tot_col - exc_col))
            st["sf"] = jnp.exp(tot_col - inc_col) * dt_c
            st["sb"] = jnp.exp(exc_col) * dt_c
            dt_r = _softplus(st["dtr"] + dt_bias_col)
            lar = dt_r * nega_col
            inc_row = _cumsum(lar, 1)
            ldt = jnp.log(dt_r)
            rid = lax.broadcasted_iota(jnp.int32, (N_DT, C), 0)
            rowarg_s[c] = jnp.where(rid < SSD_HEADS, inc_row - ldt, inc_row - lar + ldt)

        def v_stage(hf):
            def run():
                pe = st["pe%d" % hf]
                cs = slice(hf * 512, (hf + 1) * 512)
                stage_s[0:HALO, cs] = jnp.where(c > 0, pe[0:HALO], 0.0)
                stage_s[HALO:HALO + C, cs] = pe[HALO:HALO + C]
                stage_s[HALO + C:, cs] = jnp.where(c < nc - 1, pe[HALO + C:], 0.0)
            return run

        def conv_block(cb):
            cs = slice(cb * CB, (cb + 1) * CB)
            rows = C + 2 * HALO
            xin = stage_s[:, cs]
            taps = [convw_ref[k:k + 1, cs] * xin for k in range(CONV_W)]
            up = lambda a: pltpu.roll(a, rows - 1, 0)
            down = lambda a: pltpu.roll(a, 1, 0)
            acc = taps[2] + up(taps[3] + up(taps[4])) + down(taps[1] + down(taps[0]))
            return _silu(acc[HALO:HALO + C] + convb_ref[0:1, cs])

        def v_conv_x(cb):
            def run():
                xs_s[c, :, cb * CB:(cb + 1) * CB] = conv_block(cb)
            return run

        def v_conv_b():
            bT_s[c] = conv_block(SSD_W // CB).T.astype(BF16)

        def v_conv_c():
            c_s[c] = conv_block(SSD_W // CB + 1).astype(BF16)

        def rope(a):
            if not use_rope:
                return a
            return a * cos_s[c] + pltpu.roll(a, RET_HD // 2, 1) * sin_s[c]

        def v_q():
            for hd in range(RET_HEADS):
                sl = slice(hd * RET_HD, (hd + 1) * RET_HD)
                q_s[c, :, sl] = rope(st["pq"][:, sl]).astype(BF16)

        def v_k():
            for hd in range(RET_HEADS):
                sl = slice(hd * RET_HD, (hd + 1) * RET_HD)
                kh = rope(st["pk"][:, sl]) * (RET_HD ** -0.5)
                kT_s[c, sl, :] = kh.T.astype(BF16)

        def v_v():
            v_s[c] = st["pv"].astype(BF16)

        def v_g():
            g_s[c] = _silu(st["pg"]).astype(BF16)

        def v_z():
            z_s[c] = _silu(st["pz"]).astype(BF16)

        def m_sloc(gi):
            def run():
                xg = xs_s[c, :, gi * GROUP_W:(gi + 1) * GROUP_W]
                vcat = jnp.concatenate([xg * expand_f(st["sf"], gi), xg * expand_b(st["sb"], gi)],
                                       axis=1).astype(BF16)
                sloc_s[c, gi] = _dot(bT_s[c, gi * SSD_STATE:(gi + 1) * SSD_STATE, :], vcat).astype(sloc_s.dtype)
            return run

        def m_rloc(hd):
            def run():
                sl = slice(hd * RET_HD, (hd + 1) * RET_HD)
                vf = st["pv"][:, sl]
                vcat = jnp.concatenate([vf * rvec_s[hd, RV_TAIL_F], vf * rvec_s[hd, RV_TAIL_B]],
                                       axis=1).astype(BF16)
                rloc_s[c, hd] = _dot(kT_s[c, sl, :], vcat).astype(rloc_s.dtype)
            return run

        m_pe0 = m_main("pe0", XBC_COL0, XBC_COL0 + 512, halo=True)
        m_pe1 = m_main("pe1", XBC_COL0 + 512, MAIN_COLS, halo=True)
        m_q = m_main("pq", 0, RET_W)
        m_k = m_main("pk", RET_W, 2 * RET_W)
        m_v = m_main("pv", 2 * RET_W, 3 * RET_W)
        m_g = m_main("pg", 3 * RET_W, 4 * RET_W)
        m_z = m_main("pz", 4 * RET_W, XBC_COL0)
        order = [v_mod, m_dt, m_pe0, m_pe1, v_dt, m_q, v_stage(0), v_stage(1), m_k, v_conv_x(0), m_v,
                 v_conv_x(1), v_q, m_g, v_conv_b, v_k, m_z, v_conv_c, v_v,
                 m_rloc(0), m_rloc(1), v_g, m_rloc(2), m_rloc(3), m_sloc(0), m_sloc(1), v_z]
        for piece in order:
            piece()

    def recurrences():
        for hd in range(RET_HEADS):
            lgf, lgb = ret_decays(hd)
            dec_f = jnp.exp(C * lgf)
            dec_b = jnp.exp(C * lgb)
            if has_state:
                ent_f = sret0_ref[0, 0, 0, hd]
                ent_b = sret0_ref[0, 0, 1, hd]
            else:
                ent_f = jnp.zeros((RET_HD, RET_HD), F32)
                ent_b = jnp.zeros((RET_HD, RET_HD), F32)
            for c in range(nc):
                if cross:
                    rent_s[c, hd, :, 0:RET_HD] = ent_f.astype(BF16)
                ent_f = dec_f * ent_f + rloc_s[c, hd, :, 0:RET_HD]
            for c in range(nc - 1, -1, -1):
                if cross:
                    rent_s[c, hd, :, RET_HD:] = ent_b.astype(BF16)
                ent_b = dec_b * ent_b + rloc_s[c, hd, :, RET_HD:]
            if emit_state:
                nret_ref[0, 0, 0, hd] = ent_f
                nret_ref[0, 0, 1, hd] = ent_b

        for gi in range(SSD_GROUPS):
            heads = group_heads(gi)
            if has_state:
                ent_f = jnp.concatenate([sssd0_ref[0, 0, 0, hh] for hh in heads], axis=0).T
                ent_b = jnp.concatenate([sssd0_ref[0, 0, 1, hh] for hh in heads], axis=0).T
            else:
                ent_f = jnp.zeros((SSD_STATE, GROUP_W), F32)
                ent_b = jnp.zeros((SSD_STATE, GROUP_W), F32)
            for c in range(nc):
                if cross:
                    sent_s[c, gi, :, 0:GROUP_W] = ent_f.astype(BF16)
                ent_f = expand_f(dec_s[c, 0:1, :], gi) * ent_f + sloc_s[c, gi, :, 0:GROUP_W]
            for c in range(nc - 1, -1, -1):
                if cross:
                    sent_s[c, gi, :, GROUP_W:] = ent_b.astype(BF16)
                ent_b = expand_b(dec_s[c, 0:1, :], gi) * ent_b + sloc_s[c, gi, :, GROUP_W:]
            if emit_state:
                ent_ft = ent_f.T
                ent_bt = ent_b.T
                for k, hh in enumerate(heads):
                    nssd_ref[0, 0, 0, hh] = ent_ft[k * SSD_HD:(k + 1) * SSD_HD, :]
                    nssd_ref[0, 0, 1, hh] = ent_bt[k * SSD_HD:(k + 1) * SSD_HD, :]

    def emit_pieces(c, slot):
        rsl = [slice(hd * RET_HD, (hd + 1) * RET_HD) for hd in range(RET_HEADS)]
        gsl = [slice(gi * SSD_STATE, (gi + 1) * SSD_STATE) for gi in range(SSD_GROUPS)]
        xsl = [slice(gi * GROUP_W, (gi + 1) * GROUP_W) for gi in range(SSD_GROUPS)]
        st = {}

        def scores():
            ii = lax.broadcasted_iota(jnp.int32, (C, C), 0)
            jj = lax.broadcasted_iota(jnp.int32, (C, C), 1)
            st["causal"] = jj <= ii
            qs = [q_s[c, :, sl] for sl in rsl]
            cms = [c_s[c, :, sl] for sl in gsl]
            st["sc_s"] = [_dot(cms[gi], bT_s[c, gsl[gi], :]) for gi in range(SSD_GROUPS)]
            st["sc_r"] = [_dot(qs[hd], kT_s[c, rsl[hd], :]) for hd in range(RET_HEADS)]
            if cross:
                st["yc_s"] = [_dot(cms[gi], sent_s[c, gi]) for gi in range(SSD_GROUPS)]
                st["yc_r"] = [_dot(qs[hd], rent_s[c, hd]) for hd in range(RET_HEADS)]
            st["inc_col"] = inccol_s[c]
            st["exc_col"] = exccol_s[c]
            st["rowarg"] = rowarg_s[c]

        def ssd_head(gi, k):
            def run():
                hh = gi * HPG + k
                hb = SSD_HEADS + hh
                arg = jnp.where(st["causal"],
                                st["inc_col"][:, hh:hh + 1] - st["rowarg"][hh:hh + 1, :],
                                st["rowarg"][hb:hb + 1, :] - st["exc_col"][:, hb:hb + 1])
                m = (st["sc_s"][gi] * jnp.exp(arg)).astype(BF16)
                xh = xs_s[c, :, hh * SSD_HD:(hh + 1) * SSD_HD].astype(BF16)
                y_s[:, hh * SSD_HD:(hh + 1) * SSD_HD] = _dot(m, xh)
            return run

        def ssd_group(gi):
            def run():
                xg = xs_s[c, :, xsl[gi]]
                yg = y_s[:, xsl[gi]] + expand_f(dskip_row, gi) * xg
                if cross:
                    cfcb = cfcb_s[c]
                    yc = st["yc_s"][gi]
                    yg = yg + expand_f(cfcb, gi) * yc[:, 0:GROUP_W] + expand_b(cfcb, gi) * yc[:, GROUP_W:]
                y_s[:, xsl[gi]] = yg
            return run

        def ret_head(hd):
            def run():
                m = (st["sc_r"][hd] * wdec_s[hd]).astype(BF16)
                o = _dot(m, v_s[c, :, rsl[hd]])
                if cross:
                    yc = st["yc_r"][hd]
                    o = o + rvec_s[hd, RV_CROSS_F] * yc[:, 0:RET_HD] + rvec_s[hd, RV_CROSS_B] * yc[:, RET_HD:]
                o = o * lax.rsqrt(jnp.mean(o * o, axis=-1, keepdims=True) + EPS)
                mix_s[:, rsl[hd]] = (g_s[c, :, rsl[hd]].astype(F32) * o).astype(BF16)
            return run

        def ssd_norm():
            yz = y_s[...] * z_s[c].astype(F32)
            yn = yz * lax.rsqrt(jnp.mean(yz * yz, axis=-1, keepdims=True) + EPS) * normw_ref[...]
            mix_s[:, RET_W:] = yn.astype(BF16)

        def out_proj():
            g1 = mod_row(row_mix, 2)
            y = ALPHA * x_ref[0] + g1 * _dot(mix_s[...], wout_ref[...])
            x1_s[slot] = _layer_norm(y, ln1g_ref[...], ln1b_ref[...])

        pieces = [scores]
        for gi in range(SSD_GROUPS):
            pieces += [ssd_head(gi, k) for k in range(HPG)] + [ssd_group(gi)]
        pieces += [ret_head(hd) for hd in range(RET_HEADS)] + [ssd_norm, out_proj]
        return pieces

    def ffn_pieces(slot):
        st = {}

        def start():
            sh2 = mod_row(row_ffn, 3)
            sc2 = mod_row(row_ffn, 4)
            st["h2"] = (x1_s[slot] * (1.0 + sc2) + sh2).astype(BF16)

        def hidden(j):
            def run():
                js = slice(j * FF_BLK, (j + 1) * FF_BLK)
                h2 = st["h2"]
                hid_s[:, js] = (_silu(_dot(h2, wg_ref[:, js])) * _dot(h2, wu_ref[:, js])).astype(BF16)
            return run

        def finish():
            g2 = mod_row(row_ffn, 5)
            y = ALPHA * x1_s[slot] + g2 * _dot(hid_s[...], wd_ref[...])
            out_ref[0] = _layer_norm(y, ln2g_ref[...], ln2b_ref[...])

        return [start] + [hidden(j) for j in range(D_FF // FF_BLK)] + [finish]

    def run_all(pieces):
        for p in pieces:
            p()

    @pl.when((step < nc) & (seq < nb))
    def _():
        project(step)

    @pl.when((step == nc) & (seq < nb))
    def _():
        recurrences()

    kk = step - nc
    slot = (seq * nc + kk) & 1
    first = (seq == 0) & (step == nc)

    @pl.when(first)
    def _():
        for i in range(1, len(big_weights)):
            weight_copy(i).wait()
        run_all(emit_pieces(kk, slot))

    @pl.when((step >= nc) & (seq < nb) & jnp.logical_not(first))
    def _():
        _interleave(ffn_pieces(1 - slot), emit_pieces(kk, slot))

    @pl.when((seq == nb) & (step == nc))
    def _():
        run_all(ffn_pieces((nb * nc - 1) & 1))


def _const_spec(shape):
    nd = len(shape)
    return pl.BlockSpec(shape, lambda b, s: (0,) * nd, pipeline_mode=pl.Buffered(1))


def _layer_call(x, mod, mod_block, mod_per_seq, weights, states, emit_state, use_rope, name):
    nb, L, _ = x.shape
    nc = L // CHUNK
    hpc_blocks = CHUNK // HALO
    has_state = states is not None
    last = nb - 1

    def chunk_of(b, s):
        return jnp.where(b > last, nc - 1, jnp.where(s < nc, s, s - nc))

    def halo_chunk(b, s):
        return jnp.where(b > last, nc - 1, jnp.minimum(s, nc - 1))

    def seq_of(b):
        return jnp.minimum(b, last)

    def out_map(b, s):
        live = (s > nc) & (b < nb)
        idle_chunk = jnp.where(b == 0, 0, nc - 1)
        return (jnp.where(live, b, jnp.maximum(b - 1, 0)), jnp.where(live, s - nc - 1, idle_chunk), 0)

    assert not mod_per_seq or nb <= MOD_ROWS
    in_specs = [
        pl.BlockSpec((1, CHUNK, D_MODEL), lambda b, s: (seq_of(b), chunk_of(b, s), 0)),
        pl.BlockSpec((1, HALO, D_MODEL),
                     lambda b, s: (seq_of(b), jnp.maximum(halo_chunk(b, s) * hpc_blocks - 1, 0), 0)),
        pl.BlockSpec((1, HALO, D_MODEL),
                     lambda b, s: (seq_of(b), jnp.minimum((halo_chunk(b, s) + 1) * hpc_blocks, nc * hpc_blocks - 1), 0)),
        pl.BlockSpec((MOD_ROWS, 6 * D_MODEL), lambda b, s: (mod_block, 0)),
    ] + [pl.BlockSpec(memory_space=pl.ANY) if i in BIG_WEIGHTS else _const_spec(w.shape)
         for i, w in enumerate(weights)]
    args = [x, x, x, mod] + list(weights)
    ret_block = (1, 1, 2, RET_HEADS, RET_HD, RET_HD)
    ssd_block = (1, 1, 2, SSD_HEADS, SSD_HD, SSD_STATE)
    state_map = lambda b, s: (seq_of(b), 0, 0, 0, 0, 0)
    if has_state:
        in_specs += [pl.BlockSpec(ret_block, state_map, pipeline_mode=pl.Buffered(1)),
                     pl.BlockSpec(ssd_block, state_map, pipeline_mode=pl.Buffered(1))]
        args += list(states)
    out_shape = [jax.ShapeDtypeStruct((nb, L, D_MODEL), F32)]
    out_specs = [pl.BlockSpec((1, CHUNK, D_MODEL), out_map)]
    if emit_state:
        out_shape += [jax.ShapeDtypeStruct((nb,) + ret_block[1:], F32),
                      jax.ShapeDtypeStruct((nb,) + ssd_block[1:], F32)]
        out_specs += [pl.BlockSpec(ret_block, state_map), pl.BlockSpec(ssd_block, state_map)]
    state_dt = F32 if emit_state else BF16
    scratch = [
        pltpu.VMEM((nc, CHUNK, RET_W), BF16),
        pltpu.VMEM((nc, RET_W, CHUNK), BF16),
        pltpu.VMEM((nc, CHUNK, RET_W), BF16),
        pltpu.VMEM((nc, CHUNK, RET_W), BF16),
        pltpu.VMEM((nc, CHUNK, SSD_W), BF16),
        pltpu.VMEM((CHUNK + 2 * HALO, CONV_CH), F32),
        pltpu.VMEM((nc, CHUNK, SSD_W), F32),
        pltpu.VMEM((nc, SSD_GROUPS * SSD_STATE, CHUNK), BF16),
        pltpu.VMEM((nc, CHUNK, SSD_GROUPS * SSD_STATE), BF16),
        pltpu.VMEM((nc, CHUNK, LANES), F32),
        pltpu.VMEM((nc, CHUNK, LANES), F32),
        pltpu.VMEM((nc, CHUNK, LANES), F32),
        pltpu.VMEM((nc, 8, LANES), F32),
        pltpu.VMEM((nc, N_DT, CHUNK), F32),
        pltpu.VMEM((CHUNK, SSD_W), F32),
        pltpu.VMEM((CHUNK, 2 * RET_W), BF16),
        pltpu.VMEM((nc, RET_HEADS, RET_HD, 2 * RET_HD), state_dt),
        pltpu.VMEM((nc, SSD_GROUPS, SSD_STATE, 2 * GROUP_W), state_dt),
        pltpu.VMEM((nc, RET_HEADS, RET_HD, 2 * RET_HD), BF16),
        pltpu.VMEM((nc, SSD_GROUPS, SSD_STATE, 2 * GROUP_W), BF16),
        pltpu.VMEM((RET_HEADS, CHUNK, CHUNK), F32),
        pltpu.VMEM((RET_HEADS, 4, CHUNK, LANES), F32),
        pltpu.VMEM((2, CHUNK, D_MODEL), F32),
        pltpu.VMEM((CHUNK, D_FF), BF16),
        pltpu.VMEM((LANES, D_MODEL), BF16),
    ]
    scratch += [pltpu.VMEM(weights[i].shape, BF16) for i in BIG_WEIGHTS]
    scratch += [pltpu.SemaphoreType.DMA((len(BIG_WEIGHTS),))]
    if use_rope:
        scratch += [pltpu.VMEM((nc, CHUNK, LANES), F32)] * 2
    kern = functools.partial(_layer_kernel, L=L, nb=nb, mod_per_seq=mod_per_seq, has_state=has_state, use_rope=use_rope,
                             emit_state=emit_state)
    return pl.pallas_call(
        kern,
        grid=(nb + 1, 2 * nc),
        in_specs=in_specs,
        out_specs=out_specs,
        out_shape=out_shape,
        scratch_shapes=scratch,
        compiler_params=pltpu.CompilerParams(dimension_semantics=("arbitrary", "arbitrary"),
                                             vmem_limit_bytes=VMEM_LIMIT),
        name=name,
    )(*args)


def kernel(x_prompt, x_sample, state_ret, state_ssd, c, c_ctx, w_in, ret_decay_fwd, ret_decay_bwd, conv_w, conv_b, dt_bias_fwd, dt_bias_bwd, a_log_fwd, a_log_bwd, d_skip, ssd_norm_w, w_out, ln1_g, ln1_b, w_gate, w_up, w_down, ln2_g, ln2_b, w_ada, b_ada):
    depth = w_in.shape[0]
    assert depth == 1, "single trunk layer"
    bp, lp, _ = x_prompt.shape
    bs, ls, _ = x_sample.shape
    assert lp % CHUNK == 0 and ls % CHUNK == 0 and ls % GRID_W == 0 and D_FF % FF_BLK == 0


    convw = jnp.zeros((8, CONV_CH), F32).at[:CONV_W].set(conv_w[0])
    convb = conv_b[0][None, :]
    dt_bias = jnp.concatenate([dt_bias_fwd[0], dt_bias_bwd[0]])
    a_log = jnp.concatenate([a_log_fwd[0], a_log_bwd[0]])
    ret_decay = jnp.concatenate([ret_decay_fwd[0], ret_decay_bwd[0]])
    hp = jnp.zeros((8, LANES), F32)
    hp = hp.at[0, :N_DT].set(dt_bias).at[1, :N_DT].set(a_log)
    hp = hp.at[2, :2 * RET_HEADS].set(ret_decay).at[3, :SSD_HEADS].set(d_skip[0])
    hpc = jnp.zeros((N_DT, LANES), F32).at[:, 0].set(dt_bias).at[:, 1].set(a_log)
    mod, w_in_b, w_out_b, w_gate_b, w_up_b, w_down_b = _prep_call(
        c, c_ctx[None, :], w_ada[0], b_ada[0][None, :],
        jnp.swapaxes(w_in[0], 0, 1), [w_out[0], w_gate[0], w_up[0], w_down[0]])
    weights = (w_in_b, convw, convb, hp, hpc, ssd_norm_w[0][None, :],
               w_out_b, ln1_g[0][None, :], ln1_b[0][None, :],
               w_gate_b, w_up_b, w_down_b, ln2_g[0][None, :], ln2_b[0][None, :])

    yp, new_ret, new_ssd = _layer_call(x_prompt, mod, 1, False, weights, None, True, False, "layer_ctx")
    states = (state_ret, jnp.swapaxes(state_ssd, -1, -2))
    (ys,) = _layer_call(x_sample, mod, 0, True, weights, states, False, True, "layer_lat")
    return (yp, ys, new_ret, jnp.swapaxes(new_ssd, -1, -2))
```

```python
import functools
import math

import jax
import jax.numpy as jnp
from jax import lax
from jax.experimental import pallas as pl
from jax.experimental.pallas import tpu as pltpu

F32 = jnp.float32
BF16 = jnp.bfloat16

D_MODEL = 1024
RET_W = 512
RET_HEADS = 4
RET_HD = 128
SSD_W = 512
SSD_HD = 64
SSD_HEADS = 8
SSD_GROUPS = 2
SSD_STATE = 128
HPG = SSD_HEADS // SSD_GROUPS
GROUP_W = HPG * SSD_HD
CONV_W = 5
CONV_CH = SSD_W + 2 * SSD_GROUPS * SSD_STATE
D_FF = 2816
GRID_W = 64
GRID_SHIFT = 6
ROPE_BASE = 10000.0
EPS = 1e-6
ALPHA = 2.0 ** 0.25
MAIN_COLS = 4 * RET_W + SSD_W + CONV_CH
XBC_COL0 = 4 * RET_W + SSD_W
N_DT = 2 * SSD_HEADS

CHUNK = 256
HALO = 8
FF_BLK = 256
MOD_ROWS = 8
CAST_STEPS = 8
CAST_WT_ROWS = 512
LANES = 128
SUBLANES = 8
XBC_HALF = CONV_CH // 2
VMEM_LIMIT = 62 * 1024 * 1024


def _dot(a, b):
    return jnp.dot(a, b, preferred_element_type=F32)


def _dot_nt(a, b):
    return lax.dot_general(a, b, (((1,), (1,)), ((), ())), preferred_element_type=F32)


def _silu(x):
    return x * jax.nn.sigmoid(x)


def _softplus(x):
    return jnp.maximum(x, 0.0) + jnp.log1p(jnp.exp(-jnp.abs(x)))


def _layer_norm(y, g, b):
    mu = jnp.mean(y, axis=-1, keepdims=True)
    yc = y - mu
    var = jnp.mean(yc * yc, axis=-1, keepdims=True)
    return yc * lax.rsqrt(var + EPS) * g + b


def _cumsum(x, axis):
    n = x.shape[axis]
    idx = lax.broadcasted_iota(jnp.int32, x.shape, axis)
    s = 1
    while s < n:
        x = x + jnp.where(idx >= s, pltpu.roll(x, s, axis), 0.0)
        s *= 2
    return x


def _expand4(cols, lane):
    a = jnp.where(lane < SSD_HD, cols[0], cols[1])
    b = jnp.where(lane < SSD_HD, cols[2], cols[3])
    return jnp.concatenate([a, b], axis=1)


def _interleave(a, b):
    ia = ib = 0
    while ia < len(a) or ib < len(b):
        if ib >= len(b) or (ia < len(a) and ia * len(b) <= ib * len(a)):
            a[ia]()
            ia += 1
        else:
            b[ib]()
            ib += 1


def _prep_kernel(c_ref, cctx_ref, wada_ref, bada_ref, wt_ref, *refs, wt_rows):
    n = (len(refs) - 2) // 2
    srcs, mod_ref, wt_out, dsts = refs[:n], refs[n], refs[n + 1], refs[n + 2:]
    cond = jnp.concatenate([c_ref[...], jnp.broadcast_to(cctx_ref[...], (MOD_ROWS, D_MODEL))], axis=0)
    mod_ref[...] = _dot(_silu(cond).astype(BF16), wada_ref[...].astype(BF16)) + bada_ref[...]
    blk = wt_ref.shape[0]
    row = lax.broadcasted_iota(jnp.int32, (blk, 1), 0) + pl.program_id(0) * blk
    wt_out[...] = jnp.where(row < wt_rows, wt_ref[...], 0.0).T.astype(BF16)
    for src, dst in zip(srcs, dsts):
        dst[...] = src[...].astype(BF16)


def _prep_call(c, c_ctx, w_ada, b_ada, wt, ws):
    assert c.shape == (MOD_ROWS, D_MODEL), c.shape
    for w in ws:
        assert w.shape[0] % (CAST_STEPS * 16) == 0, w.shape
    assert wt.shape[0] <= CAST_STEPS * CAST_WT_ROWS
    n_mod = w_ada.shape[1]
    ada_cols = n_mod // CAST_STEPS
    assert ada_cols % LANES == 0
    specs = [pl.BlockSpec((w.shape[0] // CAST_STEPS, w.shape[1]), lambda i: (i, 0)) for w in ws]
    return pl.pallas_call(
        functools.partial(_prep_kernel, wt_rows=wt.shape[0]),
        grid=(CAST_STEPS,),
        in_specs=[pl.BlockSpec((MOD_ROWS, D_MODEL), lambda i: (0, 0)),
                  pl.BlockSpec((1, D_MODEL), lambda i: (0, 0)),
                  pl.BlockSpec((D_MODEL, ada_cols), lambda i: (0, i)),
                  pl.BlockSpec((1, ada_cols), lambda i: (0, i)),
                  pl.BlockSpec((CAST_WT_ROWS, wt.shape[1]), lambda i: (i, 0))] + specs,
        out_specs=[pl.BlockSpec((2 * MOD_ROWS, ada_cols), lambda i: (0, i)),
                   pl.BlockSpec((wt.shape[1], CAST_WT_ROWS), lambda i: (0, i))] + specs,
        out_shape=[jax.ShapeDtypeStruct((2 * MOD_ROWS, n_mod), F32),
                   jax.ShapeDtypeStruct((wt.shape[1], -(-wt.shape[0] // LANES) * LANES), BF16)]
        + [jax.ShapeDtypeStruct(w.shape, BF16) for w in ws],
        compiler_params=pltpu.CompilerParams(dimension_semantics=("arbitrary",),
                                             vmem_limit_bytes=VMEM_LIMIT),
        name="prep_mod_and_weights",
    )(c, c_ctx, w_ada, b_ada, wt, *ws)


RV_TAIL_F, RV_TAIL_B, RV_CROSS_F, RV_CROSS_B = range(4)
BIG_WEIGHTS = (0, 6, 9, 10, 11)


def _layer_kernel(*refs, L, nb, mod_per_seq, has_state, use_rope, emit_state):
    nc = L // CHUNK
    C = CHUNK
    cross = has_state or nc > 1
    it = iter(refs)
    x_ref, xprev_ref, xnext_ref, mod_ref, wmain_hbm = (next(it) for _ in range(5))
    convw_ref, convb_ref, hp_ref, hpc_ref, normw_ref = (next(it) for _ in range(5))
    wout_hbm, ln1g_ref, ln1b_ref = (next(it) for _ in range(3))
    wg_hbm, wu_hbm, wd_hbm, ln2g_ref, ln2b_ref = (next(it) for _ in range(5))
    if has_state:
        sret0_ref, sssd0_ref = next(it), next(it)
    out_ref = next(it)
    if emit_state:
        nret_ref, nssd_ref = next(it), next(it)
    (q_s, kT_s, v_s, g_s, z_s, stage_s, xs_s, bT_s, c_s, inccol_s, exccol_s, cfcb_s, dec_s,
     rowarg_s, y_s, mix_s, rloc_s, sloc_s, rent_s, sent_s, wdec_s, rvec_s, x1_s, hid_s, wdtr_s) = (
         next(it) for _ in range(25))
    wmain_ref, wout_ref, wg_ref, wu_ref, wd_ref, wsem = (next(it) for _ in range(6))
    if use_rope:
        cos_s, sin_s = next(it), next(it)

    seq = pl.program_id(0)
    step = pl.program_id(1)

    if mod_per_seq:
        row_mix = jnp.minimum(seq, nb - 1)
        row_ffn = jnp.minimum(jnp.where(step > nc, seq, jnp.maximum(seq - 1, 0)), nb - 1)
    else:
        row_mix = row_ffn = 0

    def mod_row(row, k):
        return mod_ref[pl.ds(row, 1), k * D_MODEL:(k + 1) * D_MODEL]

    big_weights = ((wmain_hbm, wmain_ref), (wout_hbm, wout_ref), (wg_hbm, wg_ref), (wu_hbm, wu_ref),
                   (wd_hbm, wd_ref))

    def weight_copy(i):
        return pltpu.make_async_copy(big_weights[i][0], big_weights[i][1], wsem.at[i])

    hp = hp_ref[...]
    dt_bias_row = hp[0:1, :]
    nega_row = -jnp.exp(hp[1:2, :])
    lg_row = -_softplus(-hp[2:3, :])
    dskip_row = hp[3:4, :]
    hpc = hpc_ref[...]
    dt_bias_col = hpc[:, 0:1]
    nega_col = -jnp.exp(hpc[:, 1:2])

    lane = lax.broadcasted_iota(jnp.int32, (1, LANES), 1)

    def ret_decays(hd):
        lgf = lg_row[:, hd:hd + 1]
        lgb = lg_row[:, RET_HEADS + hd:RET_HEADS + hd + 1]
        return lgf, lgb

    def group_heads(gi):
        return [gi * HPG + k for k in range(HPG)]

    def expand_f(arr, gi):
        return _expand4([arr[:, hh:hh + 1] for hh in group_heads(gi)], lane)

    def expand_b(arr, gi):
        return _expand4([arr[:, SSD_HEADS + hh:SSD_HEADS + hh + 1] for hh in group_heads(gi)], lane)

    @pl.when((seq == 0) & (step == 0))
    def _():
        for i in range(len(big_weights)):
            weight_copy(i).start()
        weight_copy(0).wait()
        ii = lax.broadcasted_iota(jnp.int32, (C, C), 0)
        jj = lax.broadcasted_iota(jnp.int32, (C, C), 1)
        dmat = (ii - jj).astype(F32)
        irow = lax.broadcasted_iota(jnp.int32, (C, LANES), 0).astype(F32)
        for hd in range(RET_HEADS):
            lgf, lgb = ret_decays(hd)
            wdec_s[hd] = jnp.exp(jnp.where(jj <= ii, dmat * lgf, -dmat * lgb))
            rvec_s[hd, RV_TAIL_F] = jnp.exp((C - 1.0 - irow) * lgf)
            rvec_s[hd, RV_TAIL_B] = jnp.exp(irow * lgb)
            rvec_s[hd, RV_CROSS_F] = jnp.exp((irow + 1.0) * lgf)
            rvec_s[hd, RV_CROSS_B] = jnp.exp((C - irow) * lgb)
        wdtr_s[...] = wmain_ref[:, MAIN_COLS:MAIN_COLS + LANES].astype(F32).T.astype(BF16)
        if use_rope:
            ln = lax.broadcasted_iota(jnp.int32, (C, LANES), 1)
            nf = RET_HD // 4
            inv = jnp.exp((ln & (nf - 1)).astype(F32) * (-math.log(ROPE_BASE) / nf))
            for cc in range(nc):
                t = lax.broadcasted_iota(jnp.int32, (C, LANES), 0) + cc * C
                pos = jnp.where((ln & (2 * nf - 1)) < nf, t >> GRID_SHIFT, t & (GRID_W - 1)).astype(F32)
                ang = pos * inv
                cos_s[cc] = jnp.cos(ang)
                sin_s[cc] = jnp.where(ln < RET_HD // 2, -jnp.sin(ang), jnp.sin(ang))

    def project(c):
        st = {}
        CB = 256
        assert CONV_W == 5 and HALO >= CONV_W // 2

        def v_mod():
            sh1 = mod_row(row_mix, 0)
            sc1 = mod_row(row_mix, 1)
            xe = jnp.concatenate([xprev_ref[0], x_ref[0], xnext_ref[0]], axis=0)
            xm = xe * (1.0 + sc1) + sh1
            st["he"] = xm.astype(BF16)
            st["h"] = xm[HALO:HALO + C].astype(BF16)

        def m_main(name, lo, hi, halo=False):
            def run():
                st[name] = _dot(st["he" if halo else "h"], wmain_ref[:, lo:hi])
            return run

        def m_dt():
            raw_c = _dot(st["h"], wmain_ref[:, MAIN_COLS:MAIN_COLS + LANES])
            st["dtc"] = raw_c + pltpu.roll(raw_c, SSD_HEADS, 1)
            raw_r = _dot_nt(wdtr_s[0:N_DT, :], st["h"])
            st["dtr"] = raw_r + pltpu.roll(raw_r, SSD_HEADS, 0)

        def v_dt():
            dt_c = _softplus(st["dtc"] + dt_bias_row)
            lac = dt_c * nega_row
            inc_col = _cumsum(lac, 0)
            exc_col = inc_col - lac
            tot_col = inc_col[C - 1:C, :]
            inccol_s[c] = inc_col
            exccol_s[c] = exc_col
            dec_s[c, 0:SUBLANES, :] = jnp.broadcast_to(jnp.exp(tot_col), (SUBLANES, LANES))
            if cross:
                cfcb_s[c] = jnp.exp(jnp.where(lane < SSD_HEADS, inc_col, tot_col - exc_col))
            st["sf"] = jnp.exp(tot_col - inc_col) * dt_c
            st["sb"] = jnp.exp(exc_col) * dt_c
            dt_r = _softplus(st["dtr"] + dt_bias_col)
            lar = dt_r * nega_col
            inc_row = _cumsum(lar, 1)
            ldt = jnp.log(dt_r)
            rid = lax.broadcasted_iota(jnp.int32, (N_DT, C), 0)
            rowarg_s[c] = jnp.where(rid < SSD_HEADS, inc_row - ldt, inc_row - lar + ldt)

        def v_stage(hf):
            def run():
                pe = st["pe%d" % hf]
                cs = slice(hf * XBC_HALF, (hf + 1) * XBC_HALF)
                stage_s[0:HALO, cs] = jnp.where(c > 0, pe[0:HALO], 0.0)
                stage_s[HALO:HALO + C, cs] = pe[HALO:HALO + C]
                stage_s[HALO + C:, cs] = jnp.where(c < nc - 1, pe[HALO + C:], 0.0)
            return run

        def conv_block(cb):
            cs = slice(cb * CB, (cb + 1) * CB)
            rows = C + 2 * HALO
            xin = stage_s[:, cs]
            taps = [convw_ref[k:k + 1, cs] * xin for k in range(CONV_W)]
            up = lambda a: pltpu.roll(a, rows - 1, 0)
            down = lambda a: pltpu.roll(a, 1, 0)
            acc = taps[2] + up(taps[3] + up(taps[4])) + down(taps[1] + down(taps[0]))
            return _silu(acc[HALO:HALO + C] + convb_ref[0:1, cs])

        def v_conv_x(cb):
            def run():
                xs_s[c, :, cb * CB:(cb + 1) * CB] = conv_block(cb)
            return run

        def v_conv_b():
            bT_s[c] = conv_block(SSD_W // CB).T.astype(BF16)

        def v_conv_c():
            c_s[c] = conv_block(SSD_W // CB + 1).astype(BF16)

        def rope(a):
            if not use_rope:
                return a
            return a * cos_s[c] + pltpu.roll(a, RET_HD // 2, 1) * sin_s[c]

        def v_q():
            for hd in range(RET_HEADS):
                sl = slice(hd * RET_HD, (hd + 1) * RET_HD)
                q_s[c, :, sl] = rope(st["pq"][:, sl]).astype(BF16)

        def v_k():
            for hd in range(RET_HEADS):
                sl = slice(hd * RET_HD, (hd + 1) * RET_HD)
                kh = rope(st["pk"][:, sl]) * (RET_HD ** -0.5)
                kT_s[c, sl, :] = kh.T.astype(BF16)

        def v_v():
            v_s[c] = st["pv"].astype(BF16)

        def v_g():
            g_s[c] = _silu(st["pg"]).astype(BF16)

        def v_z():
            z_s[c] = _silu(st["pz"]).astype(BF16)

        def m_sloc(gi):
            def run():
                xg = xs_s[c, :, gi * GROUP_W:(gi + 1) * GROUP_W]
                vcat = jnp.concatenate([xg * expand_f(st["sf"], gi), xg * expand_b(st["sb"], gi)],
                                       axis=1).astype(BF16)
                sloc_s[c, gi] = _dot(bT_s[c, gi * SSD_STATE:(gi + 1) * SSD_STATE, :], vcat).astype(sloc_s.dtype)
            return run

        def m_rloc(hd):
            def run():
                sl = slice(hd * RET_HD, (hd + 1) * RET_HD)
                vf = st["pv"][:, sl]
                vcat = jnp.concatenate([vf * rvec_s[hd, RV_TAIL_F], vf * rvec_s[hd, RV_TAIL_B]],
                                       axis=1).astype(BF16)
                rloc_s[c, hd] = _dot(kT_s[c, sl, :], vcat).astype(rloc_s.dtype)
            return run

        m_pe0 = m_main("pe0", XBC_COL0, XBC_COL0 + XBC_HALF, halo=True)
        m_pe1 = m_main("pe1", XBC_COL0 + XBC_HALF, MAIN_COLS, halo=True)
        m_q = m_main("pq", 0, RET_W)
        m_k = m_main("pk", RET_W, 2 * RET_W)
        m_v = m_main("pv", 2 * RET_W, 3 * RET_W)
        m_g = m_main("pg", 3 * RET_W, 4 * RET_W)
        m_z = m_main("pz", 4 * RET_W, XBC_COL0)
        order = [v_mod, m_dt, m_pe0, m_pe1, v_dt, m_q, v_stage(0), v_stage(1), m_k, v_conv_x(0), m_v,
                 v_conv_x(1), v_q, m_g, v_conv_b, v_k, m_z, v_conv_c, v_v,
                 m_rloc(0), m_rloc(1), v_g, m_rloc(2), m_rloc(3), m_sloc(0), m_sloc(1), v_z]
        for piece in order:
            piece()

    def recurrences():
        for hd in range(RET_HEADS):
            lgf, lgb = ret_decays(hd)
            dec_f = jnp.exp(C * lgf)
            dec_b = jnp.exp(C * lgb)
            if has_state:
                ent_f = sret0_ref[0, 0, 0, hd]
                ent_b = sret0_ref[0, 0, 1, hd]
            else:
                ent_f = jnp.zeros((RET_HD, RET_HD), F32)
                ent_b = jnp.zeros((RET_HD, RET_HD), F32)
            for c in range(nc):
                if cross:
                    rent_s[c, hd, :, 0:RET_HD] = ent_f.astype(BF16)
                ent_f = dec_f * ent_f + rloc_s[c, hd, :, 0:RET_HD]
            for c in range(nc - 1, -1, -1):
                if cross:
                    rent_s[c, hd, :, RET_HD:] = ent_b.astype(BF16)
                ent_b = dec_b * ent_b + rloc_s[c, hd, :, RET_HD:]
            if emit_state:
                nret_ref[0, 0, 0, hd] = ent_f
                nret_ref[0, 0, 1, hd] = ent_b

        for gi in range(SSD_GROUPS):
            heads = group_heads(gi)
            if has_state:
                ent_f = jnp.concatenate([sssd0_ref[0, 0, 0, hh] for hh in heads], axis=0).T
                ent_b = jnp.concatenate([sssd0_ref[0, 0, 1, hh] for hh in heads], axis=0).T
            else:
                ent_f = jnp.zeros((SSD_STATE, GROUP_W), F32)
                ent_b = jnp.zeros((SSD_STATE, GROUP_W), F32)
            for c in range(nc):
                if cross:
                    sent_s[c, gi, :, 0:GROUP_W] = ent_f.astype(BF16)
                ent_f = expand_f(dec_s[c, 0:1, :], gi) * ent_f + sloc_s[c, gi, :, 0:GROUP_W]
            for c in range(nc - 1, -1, -1):
                if cross:
                    sent_s[c, gi, :, GROUP_W:] = ent_b.astype(BF16)
                ent_b = expand_b(dec_s[c, 0:1, :], gi) * ent_b + sloc_s[c, gi, :, GROUP_W:]
            if emit_state:
                ent_ft = ent_f.T
                ent_bt = ent_b.T
                for k, hh in enumerate(heads):
                    nssd_ref[0, 0, 0, hh] = ent_ft[k * SSD_HD:(k + 1) * SSD_HD, :]
                    nssd_ref[0, 0, 1, hh] = ent_bt[k * SSD_HD:(k + 1) * SSD_HD, :]

    def emit_pieces(c, slot):
        rsl = [slice(hd * RET_HD, (hd + 1) * RET_HD) for hd in range(RET_HEADS)]
        gsl = [slice(gi * SSD_STATE, (gi + 1) * SSD_STATE) for gi in range(SSD_GROUPS)]
        xsl = [slice(gi * GROUP_W, (gi + 1) * GROUP_W) for gi in range(SSD_GROUPS)]
        st = {}

        def scores():
            ii = lax.broadcasted_iota(jnp.int32, (C, C), 0)
            jj = lax.broadcasted_iota(jnp.int32, (C, C), 1)
            st["causal"] = jj <= ii
            qs = [q_s[c, :, sl] for sl in rsl]
            cms = [c_s[c, :, sl] for sl in gsl]
            st["sc_s"] = [_dot(cms[gi], bT_s[c, gsl[gi], :]) for gi in range(SSD_GROUPS)]
            st["sc_r"] = [_dot(qs[hd], kT_s[c, rsl[hd], :]) for hd in range(RET_HEADS)]
            if cross:
                st["yc_s"] = [_dot(cms[gi], sent_s[c, gi]) for gi in range(SSD_GROUPS)]
                st["yc_r"] = [_dot(qs[hd], rent_s[c, hd]) for hd in range(RET_HEADS)]
            st["inc_col"] = inccol_s[c]
            st["exc_col"] = exccol_s[c]
            st["rowarg"] = rowarg_s[c]

        def ssd_head(gi, k):
            def run():
                hh = gi * HPG + k
                hb = SSD_HEADS + hh
                arg = jnp.where(st["causal"],
                                st["inc_col"][:, hh:hh + 1] - st["rowarg"][hh:hh + 1, :],
                                st["rowarg"][hb:hb + 1, :] - st["exc_col"][:, hb:hb + 1])
                m = (st["sc_s"][gi] * jnp.exp(arg)).astype(BF16)
                xh = xs_s[c, :, hh * SSD_HD:(hh + 1) * SSD_HD].astype(BF16)
                y_s[:, hh * SSD_HD:(hh + 1) * SSD_HD] = _dot(m, xh)
            return run

        def ssd_group(gi):
            def run():
                xg = xs_s[c, :, xsl[gi]]
                yg = y_s[:, xsl[gi]] + expand_f(dskip_row, gi) * xg
                if cross:
                    cfcb = cfcb_s[c]
                    yc = st["yc_s"][gi]
                    yg = yg + expand_f(cfcb, gi) * yc[:, 0:GROUP_W] + expand_b(cfcb, gi) * yc[:, GROUP_W:]
                y_s[:, xsl[gi]] = yg
            return run

        def ret_head(hd):
            def run():
                m = (st["sc_r"][hd] * wdec_s[hd]).astype(BF16)
                o = _dot(m, v_s[c, :, rsl[hd]])
                if cross:
                    yc = st["yc_r"][hd]
                    o = o + rvec_s[hd, RV_CROSS_F] * yc[:, 0:RET_HD] + rvec_s[hd, RV_CROSS_B] * yc[:, RET_HD:]
                o = o * lax.rsqrt(jnp.mean(o * o, axis=-1, keepdims=True) + EPS)
                mix_s[:, rsl[hd]] = (g_s[c, :, rsl[hd]].astype(F32) * o).astype(BF16)
            return run

        def ssd_norm():
            yz = y_s[...] * z_s[c].astype(F32)
            yn = yz * lax.rsqrt(jnp.mean(yz * yz, axis=-1, keepdims=True) + EPS) * normw_ref[...]
            mix_s[:, RET_W:] = yn.astype(BF16)

        def out_proj():
            g1 = mod_row(row_mix, 2)
            y = ALPHA * x_ref[0] + g1 * _dot(mix_s[...], wout_ref[...])
            x1_s[slot] = _layer_norm(y, ln1g_ref[...], ln1b_ref[...])

        pieces = [scores]
        for gi in range(SSD_GROUPS):
            pieces += [ssd_head(gi, k) for k in range(HPG)] + [ssd_group(gi)]
        pieces += [ret_head(hd) for hd in range(RET_HEADS)] + [ssd_norm, out_proj]
        return pieces

    def ffn_pieces(slot):
        st = {}

        def start():
            sh2 = mod_row(row_ffn, 3)
            sc2 = mod_row(row_ffn, 4)
            st["h2"] = (x1_s[slot] * (1.0 + sc2) + sh2).astype(BF16)

        def hidden(j):
            def run():
                js = slice(j * FF_BLK, (j + 1) * FF_BLK)
                h2 = st["h2"]
                hid_s[:, js] = (_silu(_dot(h2, wg_ref[:, js])) * _dot(h2, wu_ref[:, js])).astype(BF16)
            return run

        def finish():
            g2 = mod_row(row_ffn, 5)
            y = ALPHA * x1_s[slot] + g2 * _dot(hid_s[...], wd_ref[...])
            out_ref[0] = _layer_norm(y, ln2g_ref[...], ln2b_ref[...])

        return [start] + [hidden(j) for j in range(D_FF // FF_BLK)] + [finish]

    def run_all(pieces):
        for p in pieces:
            p()

    @pl.when((step < nc) & (seq < nb))
    def _():
        project(step)

    @pl.when((step == nc) & (seq < nb))
    def _():
        recurrences()

    kk = step - nc
    slot = (seq * nc + kk) & 1
    first = (seq == 0) & (step == nc)

    @pl.when(first)
    def _():
        for i in range(1, len(big_weights)):
            weight_copy(i).wait()
        run_all(emit_pieces(kk, slot))

    @pl.when((step >= nc) & (seq < nb) & jnp.logical_not(first))
    def _():
        _interleave(ffn_pieces(1 - slot), emit_pieces(kk, slot))

    @pl.when((seq == nb) & (step == nc))
    def _():
        run_all(ffn_pieces((nb * nc - 1) & 1))


def _const_spec(shape):
    nd = len(shape)
    return pl.BlockSpec(shape, lambda b, s: (0,) * nd, pipeline_mode=pl.Buffered(1))


def _layer_call(x, mod, mod_block, mod_per_seq, weights, states, emit_state, use_rope, name):
    nb, L, _ = x.shape
    nc = L // CHUNK
    hpc_blocks = CHUNK // HALO
    has_state = states is not None
    last = nb - 1

    def chunk_of(b, s):
        return jnp.where(b > last, nc - 1, jnp.where(s < nc, s, s - nc))

    def halo_chunk(b, s):
        return jnp.where(b > last, nc - 1, jnp.minimum(s, nc - 1))

    def seq_of(b):
        return jnp.minimum(b, last)

    def out_map(b, s):
        live = (s > nc) & (b < nb)
        idle_chunk = jnp.where(b == 0, 0, nc - 1)
        return (jnp.where(live, b, jnp.maximum(b - 1, 0)), jnp.where(live, s - nc - 1, idle_chunk), 0)

    assert not mod_per_seq or nb <= MOD_ROWS
    in_specs = [
        pl.BlockSpec((1, CHUNK, D_MODEL), lambda b, s: (seq_of(b), chunk_of(b, s), 0)),
        pl.BlockSpec((1, HALO, D_MODEL),
                     lambda b, s: (seq_of(b), jnp.maximum(halo_chunk(b, s) * hpc_blocks - 1, 0), 0)),
        pl.BlockSpec((1, HALO, D_MODEL),
                     lambda b, s: (seq_of(b), jnp.minimum((halo_chunk(b, s) + 1) * hpc_blocks, nc * hpc_blocks - 1), 0)),
        pl.BlockSpec((MOD_ROWS, 6 * D_MODEL), lambda b, s: (mod_block, 0)),
    ] + [pl.BlockSpec(memory_space=pl.ANY) if i in BIG_WEIGHTS else _const_spec(w.shape)
         for i, w in enumerate(weights)]
    args = [x, x, x, mod] + list(weights)
    ret_block = (1, 1, 2, RET_HEADS, RET_HD, RET_HD)
    ssd_block = (1, 1, 2, SSD_HEADS, SSD_HD, SSD_STATE)
    state_map = lambda b, s: (seq_of(b), 0, 0, 0, 0, 0)
    if has_state:
        in_specs += [pl.BlockSpec(ret_block, state_map, pipeline_mode=pl.Buffered(1)),
                     pl.BlockSpec(ssd_block, state_map, pipeline_mode=pl.Buffered(1))]
        args += list(states)
    out_shape = [jax.ShapeDtypeStruct((nb, L, D_MODEL), F32)]
    out_specs = [pl.BlockSpec((1, CHUNK, D_MODEL), out_map)]
    if emit_state:
        out_shape += [jax.ShapeDtypeStruct((nb,) + ret_block[1:], F32),
                      jax.ShapeDtypeStruct((nb,) + ssd_block[1:], F32)]
        out_specs += [pl.BlockSpec(ret_block, state_map), pl.BlockSpec(ssd_block, state_map)]
    state_dt = F32 if emit_state else BF16
    scratch = [
        pltpu.VMEM((nc, CHUNK, RET_W), BF16),
        pltpu.VMEM((nc, RET_W, CHUNK), BF16),
        pltpu.VMEM((nc, CHUNK, RET_W), BF16),
        pltpu.VMEM((nc, CHUNK, RET_W), BF16),
        pltpu.VMEM((nc, CHUNK, SSD_W), BF16),
        pltpu.VMEM((CHUNK + 2 * HALO, CONV_CH), F32),
        pltpu.VMEM((nc, CHUNK, SSD_W), F32),
        pltpu.VMEM((nc, SSD_GROUPS * SSD_STATE, CHUNK), BF16),
        pltpu.VMEM((nc, CHUNK, SSD_GROUPS * SSD_STATE), BF16),
        pltpu.VMEM((nc, CHUNK, LANES), F32),
        pltpu.VMEM((nc, CHUNK, LANES), F32),
        pltpu.VMEM((nc, CHUNK, LANES), F32),
        pltpu.VMEM((nc, 4 * SUBLANES, LANES), F32),
        pltpu.VMEM((nc, N_DT, CHUNK), F32),
        pltpu.VMEM((CHUNK, SSD_W), F32),
        pltpu.VMEM((CHUNK, 2 * RET_W), BF16),
        pltpu.VMEM((nc, RET_HEADS, RET_HD, 2 * RET_HD), state_dt),
        pltpu.VMEM((nc, SSD_GROUPS, SSD_STATE, 2 * GROUP_W), state_dt),
        pltpu.VMEM((nc, RET_HEADS, RET_HD, 2 * RET_HD), BF16),
        pltpu.VMEM((nc, SSD_GROUPS, SSD_STATE, 2 * GROUP_W), BF16),
        pltpu.VMEM((RET_HEADS, CHUNK, CHUNK), F32),
        pltpu.VMEM((RET_HEADS, 4, CHUNK, LANES), F32),
        pltpu.VMEM((2, CHUNK, D_MODEL), F32),
        pltpu.VMEM((CHUNK, D_FF), BF16),
        pltpu.VMEM((LANES, D_MODEL), BF16),
    ]
    scratch += [pltpu.VMEM(weights[i].shape, BF16) for i in BIG_WEIGHTS]
    scratch += [pltpu.SemaphoreType.DMA((len(BIG_WEIGHTS),))]
    if use_rope:
        scratch += [pltpu.VMEM((nc, CHUNK, LANES), F32)] * 2
    kern = functools.partial(_layer_kernel, L=L, nb=nb, mod_per_seq=mod_per_seq, has_state=has_state, use_rope=use_rope,
                             emit_state=emit_state)
    return pl.pallas_call(
        kern,
        grid=(nb + 1, 2 * nc),
        in_specs=in_specs,
        out_specs=out_specs,
        out_shape=out_shape,
        scratch_shapes=scratch,
        compiler_params=pltpu.CompilerParams(dimension_semantics=("arbitrary", "arbitrary"),
                                             vmem_limit_bytes=VMEM_LIMIT),
        name=name,
    )(*args)


def kernel(x_prompt, x_sample, state_ret, state_ssd, c, c_ctx, w_in, ret_decay_fwd, ret_decay_bwd, conv_w, conv_b, dt_bias_fwd, dt_bias_bwd, a_log_fwd, a_log_bwd, d_skip, ssd_norm_w, w_out, ln1_g, ln1_b, w_gate, w_up, w_down, ln2_g, ln2_b, w_ada, b_ada):
    depth = w_in.shape[0]
    assert depth == 1, "single trunk layer"
    bp, lp, _ = x_prompt.shape
    bs, ls, _ = x_sample.shape
    assert lp % CHUNK == 0 and ls % CHUNK == 0 and ls % GRID_W == 0 and D_FF % FF_BLK == 0


    convw = jnp.zeros((SUBLANES, CONV_CH), F32).at[:CONV_W].set(conv_w[0])
    convb = conv_b[0][None, :]
    dt_bias = jnp.concatenate([dt_bias_fwd[0], dt_bias_bwd[0]])
    a_log = jnp.concatenate([a_log_fwd[0], a_log_bwd[0]])
    ret_decay = jnp.concatenate([ret_decay_fwd[0], ret_decay_bwd[0]])
    hp = jnp.zeros((SUBLANES, LANES), F32)
    hp = hp.at[0, :N_DT].set(dt_bias).at[1, :N_DT].set(a_log)
    hp = hp.at[2, :2 * RET_HEADS].set(ret_decay).at[3, :SSD_HEADS].set(d_skip[0])
    hpc = jnp.zeros((N_DT, LANES), F32).at[:, 0].set(dt_bias).at[:, 1].set(a_log)
    mod, w_in_b, w_out_b, w_gate_b, w_up_b, w_down_b = _prep_call(
        c, c_ctx[None, :], w_ada[0], b_ada[0][None, :],
        jnp.swapaxes(w_in[0], 0, 1), [w_out[0], w_gate[0], w_up[0], w_down[0]])
    weights = (w_in_b, convw, convb, hp, hpc, ssd_norm_w[0][None, :],
               w_out_b, ln1_g[0][None, :], ln1_b[0][None, :],
               w_gate_b, w_up_b, w_down_b, ln2_g[0][None, :], ln2_b[0][None, :])

    yp, new_ret, new_ssd = _layer_call(x_prompt, mod, 1, False, weights, None, True, False, "layer_ctx")
    states = (state_ret, jnp.swapaxes(state_ssd, -1, -2))
    (ys,) = _layer_call(x_sample, mod, 0, True, weights, states, False, True, "layer_lat")
    return (yp, ys, new_ret, jnp.swapaxes(new_ssd, -1, -2))
```

```python
import functools
import math

import jax
import jax.numpy as jnp
from jax import lax
from jax.experimental import pallas as pl
from jax.experimental.pallas import tpu as pltpu

F32 = jnp.float32
BF16 = jnp.bfloat16

D_MODEL = 1024
RET_W = 512
RET_HEADS = 4
RET_HD = 128
SSD_W = 512
SSD_HD = 64
SSD_HEADS = 8
SSD_GROUPS = 2
SSD_STATE = 128
HPG = SSD_HEADS // SSD_GROUPS
GROUP_W = HPG * SSD_HD
CONV_W = 5
CONV_CH = SSD_W + 2 * SSD_GROUPS * SSD_STATE
D_FF = 2816
GRID_W = 64
GRID_SHIFT = 6
ROPE_BASE = 10000.0
EPS = 1e-6
ALPHA = 2.0 ** 0.25
MAIN_COLS = 4 * RET_W + SSD_W + CONV_CH
XBC_COL0 = 4 * RET_W + SSD_W
N_DT = 2 * SSD_HEADS

CHUNK = 256
HALO = 8
FF_BLK = 256
MOD_ROWS = 8
CAST_STEPS = 8
CAST_WT_ROWS = 512
LANES = 128
SUBLANES = 8
XBC_HALF = CONV_CH // 2
VMEM_LIMIT = 62 * 1024 * 1024


def _dot(a, b):
    return jnp.dot(a, b, preferred_element_type=F32)


def _dot_nt(a, b):
    return lax.dot_general(a, b, (((1,), (1,)), ((), ())), preferred_element_type=F32)


def _silu(x):
    return x * jax.nn.sigmoid(x)


def _softplus(x):
    return jnp.maximum(x, 0.0) + jnp.log1p(jnp.exp(-jnp.abs(x)))


def _layer_norm(y, g, b):
    mu = jnp.mean(y, axis=-1, keepdims=True)
    yc = y - mu
    var = jnp.mean(yc * yc, axis=-1, keepdims=True)
    return yc * lax.rsqrt(var + EPS) * g + b


def _cumsum(x, axis):
    n = x.shape[axis]
    idx = lax.broadcasted_iota(jnp.int32, x.shape, axis)
    s = 1
    while s < n:
        x = x + jnp.where(idx >= s, pltpu.roll(x, s, axis), 0.0)
        s *= 2
    return x


def _expand4(cols, lane):
    a = jnp.where(lane < SSD_HD, cols[0], cols[1])
    b = jnp.where(lane < SSD_HD, cols[2], cols[3])
    return jnp.concatenate([a, b], axis=1)


def _interleave(a, b):
    ia = ib = 0
    while ia < len(a) or ib < len(b):
        if ib >= len(b) or (ia < len(a) and ia * len(b) <= ib * len(a)):
            a[ia]()
            ia += 1
        else:
            b[ib]()
            ib += 1


def _prep_kernel(c_ref, cctx_ref, wada_ref, bada_ref, wt_ref, *refs, wt_rows):
    n = (len(refs) - 2) // 2
    srcs, mod_ref, wt_out, dsts = refs[:n], refs[n], refs[n + 1], refs[n + 2:]
    cond = jnp.concatenate([c_ref[...], jnp.broadcast_to(cctx_ref[...], (MOD_ROWS, D_MODEL))], axis=0)
    mod_ref[...] = _dot(_silu(cond).astype(BF16), wada_ref[...].astype(BF16)) + bada_ref[...]
    blk = wt_ref.shape[0]
    row = lax.broadcasted_iota(jnp.int32, (blk, 1), 0) + pl.program_id(0) * blk
    wt_out[...] = jnp.where(row < wt_rows, wt_ref[...], 0.0).T.astype(BF16)
    for src, dst in zip(srcs, dsts):
        dst[...] = src[...].astype(BF16)


def _prep_call(c, c_ctx, w_ada, b_ada, wt, ws):
    assert c.shape == (MOD_ROWS, D_MODEL), c.shape
    for w in ws:
        assert w.shape[0] % (CAST_STEPS * 16) == 0, w.shape
    assert wt.shape[0] <= CAST_STEPS * CAST_WT_ROWS
    n_mod = w_ada.shape[1]
    ada_cols = n_mod // CAST_STEPS
    assert ada_cols % LANES == 0
    specs = [pl.BlockSpec((w.shape[0] // CAST_STEPS, w.shape[1]), lambda i: (i, 0)) for w in ws]
    return pl.pallas_call(
        functools.partial(_prep_kernel, wt_rows=wt.shape[0]),
        grid=(CAST_STEPS,),
        in_specs=[pl.BlockSpec((MOD_ROWS, D_MODEL), lambda i: (0, 0)),
                  pl.BlockSpec((1, D_MODEL), lambda i: (0, 0)),
                  pl.BlockSpec((D_MODEL, ada_cols), lambda i: (0, i)),
                  pl.BlockSpec((1, ada_cols), lambda i: (0, i)),
                  pl.BlockSpec((CAST_WT_ROWS, wt.shape[1]), lambda i: (i, 0))] + specs,
        out_specs=[pl.BlockSpec((2 * MOD_ROWS, ada_cols), lambda i: (0, i)),
                   pl.BlockSpec((wt.shape[1], CAST_WT_ROWS), lambda i: (0, i))] + specs,
        out_shape=[jax.ShapeDtypeStruct((2 * MOD_ROWS, n_mod), F32),
                   jax.ShapeDtypeStruct((wt.shape[1], -(-wt.shape[0] // LANES) * LANES), BF16)]
        + [jax.ShapeDtypeStruct(w.shape, BF16) for w in ws],
        compiler_params=pltpu.CompilerParams(dimension_semantics=("arbitrary",),
                                             vmem_limit_bytes=VMEM_LIMIT),
        name="prep_mod_and_weights",
    )(c, c_ctx, w_ada, b_ada, wt, *ws)


RV_TAIL_F, RV_TAIL_B, RV_CROSS_F, RV_CROSS_B = range(4)
BIG_WEIGHTS = (0, 6, 9, 10, 11)


def _layer_kernel(*refs, L, nb, mod_per_seq, has_state, use_rope, emit_state):
    nc = L // CHUNK
    C = CHUNK
    cross = has_state or nc > 1
    it = iter(refs)
    x_ref, xprev_ref, xnext_ref, mod_ref, wmain_hbm = (next(it) for _ in range(5))
    convw_ref, convb_ref, hp_ref, hpc_ref, normw_ref = (next(it) for _ in range(5))
    wout_hbm, ln1g_ref, ln1b_ref = (next(it) for _ in range(3))
    wg_hbm, wu_hbm, wd_hbm, ln2g_ref, ln2b_ref = (next(it) for _ in range(5))
    if has_state:
        sret0_ref, sssd0_ref = next(it), next(it)
    out_ref = next(it)
    if emit_state:
        nret_ref, nssd_ref = next(it), next(it)
    (q_s, kT_s, v_s, g_s, z_s, stage_s, xs_s, bT_s, c_s, inccol_s, exccol_s, cfcb_s, dec_s,
     rowarg_s, y_s, mix_s, rloc_s, sloc_s, rent_s, sent_s, wdec_s, rvec_s, x1_s, hid_s, wdtr_s) = (
         next(it) for _ in range(25))
    wmain_ref, wout_ref, wg_ref, wu_ref, wd_ref, wsem = (next(it) for _ in range(6))
    if use_rope:
        cos_s, sin_s = next(it), next(it)

    seq = pl.program_id(0)
    step = pl.program_id(1)

    if mod_per_seq:
        row_mix = jnp.minimum(seq, nb - 1)
        row_ffn = jnp.minimum(jnp.where(step > nc, seq, jnp.maximum(seq - 1, 0)), nb - 1)
    else:
        row_mix = row_ffn = 0

    def mod_row(row, k):
        return mod_ref[pl.ds(row, 1), k * D_MODEL:(k + 1) * D_MODEL]

    big_weights = ((wmain_hbm, wmain_ref), (wout_hbm, wout_ref), (wg_hbm, wg_ref), (wu_hbm, wu_ref),
                   (wd_hbm, wd_ref))

    def weight_copy(i):
        return pltpu.make_async_copy(big_weights[i][0], big_weights[i][1], wsem.at[i])

    hp = hp_ref[...]
    dt_bias_row = hp[0:1, :]
    nega_row = -jnp.exp(hp[1:2, :])
    lg_row = -_softplus(-hp[2:3, :])
    dskip_row = hp[3:4, :]
    hpc = hpc_ref[...]
    dt_bias_col = hpc[:, 0:1]
    nega_col = -jnp.exp(hpc[:, 1:2])

    lane = lax.broadcasted_iota(jnp.int32, (1, LANES), 1)

    def ret_decays(hd):
        lgf = lg_row[:, hd:hd + 1]
        lgb = lg_row[:, RET_HEADS + hd:RET_HEADS + hd + 1]
        return lgf, lgb

    def group_heads(gi):
        return [gi * HPG + k for k in range(HPG)]

    def expand_f(arr, gi):
        return _expand4([arr[:, hh:hh + 1] for hh in group_heads(gi)], lane)

    def expand_b(arr, gi):
        return _expand4([arr[:, SSD_HEADS + hh:SSD_HEADS + hh + 1] for hh in group_heads(gi)], lane)

    @pl.when((seq == 0) & (step == 0))
    def _():
        for i in range(len(big_weights)):
            weight_copy(i).start()
        weight_copy(0).wait()
        ii = lax.broadcasted_iota(jnp.int32, (C, C), 0)
        jj = lax.broadcasted_iota(jnp.int32, (C, C), 1)
        dmat = (ii - jj).astype(F32)
        irow = lax.broadcasted_iota(jnp.int32, (C, LANES), 0).astype(F32)
        for hd in range(RET_HEADS):
            lgf, lgb = ret_decays(hd)
            wdec_s[hd] = jnp.exp(jnp.where(jj <= ii, dmat * lgf, -dmat * lgb))
            rvec_s[hd, RV_TAIL_F] = jnp.exp((C - 1.0 - irow) * lgf)
            rvec_s[hd, RV_TAIL_B] = jnp.exp(irow * lgb)
            rvec_s[hd, RV_CROSS_F] = jnp.exp((irow + 1.0) * lgf)
            rvec_s[hd, RV_CROSS_B] = jnp.exp((C - irow) * lgb)
        wdtr_s[...] = wmain_ref[:, MAIN_COLS:MAIN_COLS + LANES].astype(F32).T.astype(BF16)
        if use_rope:
            ln = lax.broadcasted_iota(jnp.int32, (C, LANES), 1)
            nf = RET_HD // 4
            inv = jnp.exp((ln & (nf - 1)).astype(F32) * (-math.log(ROPE_BASE) / nf))
            for cc in range(nc):
                t = lax.broadcasted_iota(jnp.int32, (C, LANES), 0) + cc * C
                pos = jnp.where((ln & (2 * nf - 1)) < nf, t >> GRID_SHIFT, t & (GRID_W - 1)).astype(F32)
                ang = pos * inv
                cos_s[cc] = jnp.cos(ang)
                sin_s[cc] = jnp.where(ln < RET_HD // 2, -jnp.sin(ang), jnp.sin(ang))

    def project(c):
        st = {}
        CB = 256
        assert CONV_W == 5 and HALO >= CONV_W // 2

        def v_mod():
            sh1 = mod_row(row_mix, 0)
            sc1 = mod_row(row_mix, 1)
            xe = jnp.concatenate([xprev_ref[0], x_ref[0], xnext_ref[0]], axis=0)
            xm = xe * (1.0 + sc1) + sh1
            st["he"] = xm.astype(BF16)
            st["h"] = xm[HALO:HALO + C].astype(BF16)

        def m_main(name, lo, hi, halo=False):
            def run():
                st[name] = _dot(st["he" if halo else "h"], wmain_ref[:, lo:hi])
            return run

        def m_dt():
            raw_c = _dot(st["h"], wmain_ref[:, MAIN_COLS:MAIN_COLS + LANES])
            st["dtc"] = raw_c + pltpu.roll(raw_c, SSD_HEADS, 1)
            raw_r = _dot_nt(wdtr_s[0:N_DT, :], st["h"])
            st["dtr"] = raw_r + pltpu.roll(raw_r, SSD_HEADS, 0)

        def v_dt():
            dt_c = _softplus(st["dtc"] + dt_bias_row)
            lac = dt_c * nega_row
            inc_col = _cumsum(lac, 0)
            exc_col = inc_col - lac
            tot_col = inc_col[C - 1:C, :]
            inccol_s[c] = inc_col
            exccol_s[c] = exc_col
            dec_s[c, 0:SUBLANES, :] = jnp.broadcast_to(jnp.exp(tot_col), (SUBLANES, LANES))
            if cross:
                cfcb_s[c] = jnp.exp(jnp.where(lane < SSD_HEADS, inc_col, tot_col - exc_col))
            st["sf"] = jnp.exp(tot_col - inc_col) * dt_c
            st["sb"] = jnp.exp(exc_col) * dt_c
            dt_r = _softplus(st["dtr"] + dt_bias_col)
            lar = dt_r * nega_col
            inc_row = _cumsum(lar, 1)
            ldt = jnp.log(dt_r)
            rid = lax.broadcasted_iota(jnp.int32, (N_DT, C), 0)
            rowarg_s[c] = jnp.where(rid < SSD_HEADS, inc_row - ldt, inc_row - lar + ldt)

        def v_stage(hf):
            def run():
                pe = st["pe%d" % hf]
                cs = slice(hf * XBC_HALF, (hf + 1) * XBC_HALF)
                stage_s[0:HALO, cs] = jnp.where(c > 0, pe[0:HALO], 0.0)
                stage_s[HALO:HALO + C, cs] = pe[HALO:HALO + C]
                stage_s[HALO + C:, cs] = jnp.where(c < nc - 1, pe[HALO + C:], 0.0)
            return run

        def conv_block(cb):
            cs = slice(cb * CB, (cb + 1) * CB)
            rows = C + 2 * HALO
            xin = stage_s[:, cs]
            taps = [convw_ref[k:k + 1, cs] * xin for k in range(CONV_W)]
            up = lambda a: pltpu.roll(a, rows - 1, 0)
            down = lambda a: pltpu.roll(a, 1, 0)
            acc = taps[2] + up(taps[3] + up(taps[4])) + down(taps[1] + down(taps[0]))
            return _silu(acc[HALO:HALO + C] + convb_ref[0:1, cs])

        def v_conv_x(cb):
            def run():
                xs_s[c, :, cb * CB:(cb + 1) * CB] = conv_block(cb)
            return run

        def v_conv_b():
            bT_s[c] = conv_block(SSD_W // CB).T.astype(BF16)

        def v_conv_c():
            c_s[c] = conv_block(SSD_W // CB + 1).astype(BF16)

        def rope(a):
            if not use_rope:
                return a
            return a * cos_s[c] + pltpu.roll(a, RET_HD // 2, 1) * sin_s[c]

        def v_q():
            for hd in range(RET_HEADS):
                sl = slice(hd * RET_HD, (hd + 1) * RET_HD)
                q_s[c, :, sl] = rope(st["pq"][:, sl]).astype(BF16)

        def v_k():
            for hd in range(RET_HEADS):
                sl = slice(hd * RET_HD, (hd + 1) * RET_HD)
                kh = rope(st["pk"][:, sl]) * (RET_HD ** -0.5)
                kT_s[c, sl, :] = kh.T.astype(BF16)

        def v_v():
            v_s[c] = st["pv"].astype(BF16)

        def v_g():
            g_s[c] = _silu(st["pg"]).astype(BF16)

        def v_z():
            z_s[c] = _silu(st["pz"]).astype(BF16)

        def m_sloc(gi):
            def run():
                xg = xs_s[c, :, gi * GROUP_W:(gi + 1) * GROUP_W]
                vcat = jnp.concatenate([xg * expand_f(st["sf"], gi), xg * expand_b(st["sb"], gi)],
                                       axis=1).astype(BF16)
                sloc_s[c, gi] = _dot(bT_s[c, gi * SSD_STATE:(gi + 1) * SSD_STATE, :], vcat).astype(sloc_s.dtype)
            return run

        def m_rloc(hd):
            def run():
                sl = slice(hd * RET_HD, (hd + 1) * RET_HD)
                vf = st["pv"][:, sl]
                vcat = jnp.concatenate([vf * rvec_s[hd, RV_TAIL_F], vf * rvec_s[hd, RV_TAIL_B]],
                                       axis=1).astype(BF16)
                rloc_s[c, hd] = _dot(kT_s[c, sl, :], vcat).astype(rloc_s.dtype)
            return run

        m_pe0 = m_main("pe0", XBC_COL0, XBC_COL0 + XBC_HALF, halo=True)
        m_pe1 = m_main("pe1", XBC_COL0 + XBC_HALF, MAIN_COLS, halo=True)
        m_q = m_main("pq", 0, RET_W)
        m_k = m_main("pk", RET_W, 2 * RET_W)
        m_v = m_main("pv", 2 * RET_W, 3 * RET_W)
        m_g = m_main("pg", 3 * RET_W, 4 * RET_W)
        m_z = m_main("pz", 4 * RET_W, XBC_COL0)
        order = [v_mod, m_dt, m_pe0, m_pe1, v_dt, m_q, v_stage(0), v_stage(1), m_k, v_conv_x(0), m_v,
                 v_conv_x(1), v_q, m_g, v_conv_b, v_k, m_z, v_conv_c, v_v,
                 m_rloc(0), m_rloc(1), v_g, m_rloc(2), m_rloc(3), m_sloc(0), m_sloc(1), v_z]
        for piece in order:
            piece()

    def recurrences():
        for hd in range(RET_HEADS):
            lgf, lgb = ret_decays(hd)
            dec_f = jnp.exp(C * lgf)
            dec_b = jnp.exp(C * lgb)
            if has_state:
                ent_f = sret0_ref[0, 0, 0, hd]
                ent_b = sret0_ref[0, 0, 1, hd]
            else:
                ent_f = jnp.zeros((RET_HD, RET_HD), F32)
                ent_b = jnp.zeros((RET_HD, RET_HD), F32)
            for c in range(nc):
                if cross:
                    rent_s[c, hd, :, 0:RET_HD] = ent_f.astype(BF16)
                ent_f = dec_f * ent_f + rloc_s[c, hd, :, 0:RET_HD]
            for c in range(nc - 1, -1, -1):
                if cross:
                    rent_s[c, hd, :, RET_HD:] = ent_b.astype(BF16)
                ent_b = dec_b * ent_b + rloc_s[c, hd, :, RET_HD:]
            if emit_state:
                nret_ref[0, 0, 0, hd] = ent_f
                nret_ref[0, 0, 1, hd] = ent_b

        for gi in range(SSD_GROUPS):
            heads = group_heads(gi)
            if has_state:
                ent_f = jnp.concatenate([sssd0_ref[0, 0, 0, hh] for hh in heads], axis=0).T
                ent_b = jnp.concatenate([sssd0_ref[0, 0, 1, hh] for hh in heads], axis=0).T
            else:
                ent_f = jnp.zeros((SSD_STATE, GROUP_W), F32)
                ent_b = jnp.zeros((SSD_STATE, GROUP_W), F32)
            for c in range(nc):
                if cross:
                    sent_s[c, gi, :, 0:GROUP_W] = ent_f.astype(BF16)
                ent_f = expand_f(dec_s[c, 0:1, :], gi) * ent_f + sloc_s[c, gi, :, 0:GROUP_W]
            for c in range(nc - 1, -1, -1):
                if cross:
                    sent_s[c, gi, :, GROUP_W:] = ent_b.astype(BF16)
                ent_b = expand_b(dec_s[c, 0:1, :], gi) * ent_b + sloc_s[c, gi, :, GROUP_W:]
            if emit_state:
                ent_ft = ent_f.T
                ent_bt = ent_b.T
                for k, hh in enumerate(heads):
                    nssd_ref[0, 0, 0, hh] = ent_ft[k * SSD_HD:(k + 1) * SSD_HD, :]
                    nssd_ref[0, 0, 1, hh] = ent_bt[k * SSD_HD:(k + 1) * SSD_HD, :]

    def emit_pieces(c, slot):
        rsl = [slice(hd * RET_HD, (hd + 1) * RET_HD) for hd in range(RET_HEADS)]
        gsl = [slice(gi * SSD_STATE, (gi + 1) * SSD_STATE) for gi in range(SSD_GROUPS)]
        xsl = [slice(gi * GROUP_W, (gi + 1) * GROUP_W) for gi in range(SSD_GROUPS)]
        st = {}

        def scores():
            ii = lax.broadcasted_iota(jnp.int32, (C, C), 0)
            jj = lax.broadcasted_iota(jnp.int32, (C, C), 1)
            st["causal"] = jj <= ii
            qs = [q_s[c, :, sl] for sl in rsl]
            cms = [c_s[c, :, sl] for sl in gsl]
            st["sc_s"] = [_dot(cms[gi], bT_s[c, gsl[gi], :]) for gi in range(SSD_GROUPS)]
            st["sc_r"] = [_dot(qs[hd], kT_s[c, rsl[hd], :]) for hd in range(RET_HEADS)]
            if cross:
                st["yc_s"] = [_dot(cms[gi], sent_s[c, gi]) for gi in range(SSD_GROUPS)]
                st["yc_r"] = [_dot(qs[hd], rent_s[c, hd]) for hd in range(RET_HEADS)]
            st["inc_col"] = inccol_s[c]
            st["exc_col"] = exccol_s[c]
            st["rowarg"] = rowarg_s[c]

        def ssd_head(gi, k):
            def run():
                hh = gi * HPG + k
                hb = SSD_HEADS + hh
                arg = jnp.where(st["causal"],
                                st["inc_col"][:, hh:hh + 1] - st["rowarg"][hh:hh + 1, :],
                                st["rowarg"][hb:hb + 1, :] - st["exc_col"][:, hb:hb + 1])
                m = (st["sc_s"][gi] * jnp.exp(arg)).astype(BF16)
                xh = xs_s[c, :, hh * SSD_HD:(hh + 1) * SSD_HD].astype(BF16)
                y_s[:, hh * SSD_HD:(hh + 1) * SSD_HD] = _dot(m, xh)
            return run

        def ssd_group(gi):
            def run():
                xg = xs_s[c, :, xsl[gi]]
                yg = y_s[:, xsl[gi]] + expand_f(dskip_row, gi) * xg
                if cross:
                    cfcb = cfcb_s[c]
                    yc = st["yc_s"][gi]
                    yg = yg + expand_f(cfcb, gi) * yc[:, 0:GROUP_W] + expand_b(cfcb, gi) * yc[:, GROUP_W:]
                y_s[:, xsl[gi]] = yg
            return run

        def ret_head(hd):
            def run():
                m = (st["sc_r"][hd] * wdec_s[hd]).astype(BF16)
                o = _dot(m, v_s[c, :, rsl[hd]])
                if cross:
                    yc = st["yc_r"][hd]
                    o = o + rvec_s[hd, RV_CROSS_F] * yc[:, 0:RET_HD] + rvec_s[hd, RV_CROSS_B] * yc[:, RET_HD:]
                o = o * lax.rsqrt(jnp.mean(o * o, axis=-1, keepdims=True) + EPS)
                mix_s[:, rsl[hd]] = (g_s[c, :, rsl[hd]].astype(F32) * o).astype(BF16)
            return run

        def ssd_norm():
            yz = y_s[...] * z_s[c].astype(F32)
            yn = yz * lax.rsqrt(jnp.mean(yz * yz, axis=-1, keepdims=True) + EPS) * normw_ref[...]
            mix_s[:, RET_W:] = yn.astype(BF16)

        def out_proj():
            g1 = mod_row(row_mix, 2)
            y = ALPHA * x_ref[0] + g1 * _dot(mix_s[...], wout_ref[...])
            x1_s[slot] = _layer_norm(y, ln1g_ref[...], ln1b_ref[...])

        pieces = [scores]
        for gi in range(SSD_GROUPS):
            pieces += [ssd_head(gi, k) for k in range(HPG)] + [ssd_group(gi)]
        pieces += [ret_head(hd) for hd in range(RET_HEADS)] + [ssd_norm, out_proj]
        return pieces

    def ffn_pieces(slot):
        st = {}

        def start():
            sh2 = mod_row(row_ffn, 3)
            sc2 = mod_row(row_ffn, 4)
            st["h2"] = (x1_s[slot] * (1.0 + sc2) + sh2).astype(BF16)

        def hidden(j):
            def run():
                js = slice(j * FF_BLK, (j + 1) * FF_BLK)
                h2 = st["h2"]
                hid_s[:, js] = (_silu(_dot(h2, wg_ref[:, js])) * _dot(h2, wu_ref[:, js])).astype(BF16)
            return run

        def finish():
            g2 = mod_row(row_ffn, 5)
            y = ALPHA * x1_s[slot] + g2 * _dot(hid_s[...], wd_ref[...])
            out_ref[0] = _layer_norm(y, ln2g_ref[...], ln2b_ref[...])

        return [start] + [hidden(j) for j in range(D_FF // FF_BLK)] + [finish]

    def run_all(pieces):
        for p in pieces:
            p()

    @pl.when((step < nc) & (seq < nb))
    def _():
        project(step)

    @pl.when((step == nc) & (seq < nb))
    def _():
        recurrences()

    kk = step - nc
    slot = (seq * nc + kk) & 1
    first = (seq == 0) & (step == nc)

    @pl.when(first)
    def _():
        for i in range(1, len(big_weights)):
            weight_copy(i).wait()
        run_all(emit_pieces(kk, slot))

    @pl.when((step >= nc) & (seq < nb) & jnp.logical_not(first))
    def _():
        _interleave(ffn_pieces(1 - slot), emit_pieces(kk, slot))

    @pl.when((seq == nb) & (step == nc))
    def _():
        run_all(ffn_pieces((nb * nc - 1) & 1))


def _const_spec(shape):
    nd = len(shape)
    return pl.BlockSpec(shape, lambda b, s: (0,) * nd, pipeline_mode=pl.Buffered(1))


def _layer_call(x, mod, mod_block, mod_per_seq, weights, states, emit_state, use_rope, name):
    nb, L, _ = x.shape
    nc = L // CHUNK
    hpc_blocks = CHUNK // HALO
    has_state = states is not None
    last = nb - 1

    def chunk_of(b, s):
        return jnp.where(b > last, nc - 1, jnp.where(s < nc, s, s - nc))

    def halo_chunk(b, s):
        return jnp.where(b > last, nc - 1, jnp.minimum(s, nc - 1))

    def seq_of(b):
        return jnp.minimum(b, last)

    def out_map(b, s):
        live = (s > nc) & (b < nb)
        idle_chunk = jnp.where(b == 0, 0, nc - 1)
        return (jnp.where(live, b, jnp.maximum(b - 1, 0)), jnp.where(live, s - nc - 1, idle_chunk), 0)

    assert not mod_per_seq or nb <= MOD_ROWS
    in_specs = [
        pl.BlockSpec((1, CHUNK, D_MODEL), lambda b, s: (seq_of(b), chunk_of(b, s), 0)),
        pl.BlockSpec((1, HALO, D_MODEL),
                     lambda b, s: (seq_of(b), jnp.maximum(halo_chunk(b, s) * hpc_blocks - 1, 0), 0)),
        pl.BlockSpec((1, HALO, D_MODEL),
                     lambda b, s: (seq_of(b), jnp.minimum((halo_chunk(b, s) + 1) * hpc_blocks, nc * hpc_blocks - 1), 0)),
        pl.BlockSpec((MOD_ROWS, 6 * D_MODEL), lambda b, s: (mod_block, 0)),
    ] + [pl.BlockSpec(memory_space=pl.ANY) if i in BIG_WEIGHTS else _const_spec(w.shape)
         for i, w in enumerate(weights)]
    args = [x, x, x, mod] + list(weights)
    ret_block = (1, 1, 2, RET_HEADS, RET_HD, RET_HD)
    ssd_block = (1, 1, 2, SSD_HEADS, SSD_HD, SSD_STATE)
    state_map = lambda b, s: (seq_of(b), 0, 0, 0, 0, 0)
    if has_state:
        in_specs += [pl.BlockSpec(ret_block, state_map, pipeline_mode=pl.Buffered(1)),
                     pl.BlockSpec(ssd_block, state_map, pipeline_mode=pl.Buffered(1))]
        args += list(states)
    out_shape = [jax.ShapeDtypeStruct((nb, L, D_MODEL), F32)]
    out_specs = [pl.BlockSpec((1, CHUNK, D_MODEL), out_map)]
    if emit_state:
        out_shape += [jax.ShapeDtypeStruct((nb,) + ret_block[1:], F32),
                      jax.ShapeDtypeStruct((nb,) + ssd_block[1:], F32)]
        out_specs += [pl.BlockSpec(ret_block, state_map), pl.BlockSpec(ssd_block, state_map)]
    state_dt = F32 if emit_state else BF16
    scratch = [
        pltpu.VMEM((nc, CHUNK, RET_W), BF16),
        pltpu.VMEM((nc, RET_W, CHUNK), BF16),
        pltpu.VMEM((nc, CHUNK, RET_W), BF16),
        pltpu.VMEM((nc, CHUNK, RET_W), BF16),
        pltpu.VMEM((nc, CHUNK, SSD_W), BF16),
        pltpu.VMEM((CHUNK + 2 * HALO, CONV_CH), F32),
        pltpu.VMEM((nc, CHUNK, SSD_W), F32),
        pltpu.VMEM((nc, SSD_GROUPS * SSD_STATE, CHUNK), BF16),
        pltpu.VMEM((nc, CHUNK, SSD_GROUPS * SSD_STATE), BF16),
        pltpu.VMEM((nc, CHUNK, LANES), F32),
        pltpu.VMEM((nc, CHUNK, LANES), F32),
        pltpu.VMEM((nc, CHUNK, LANES), F32),
        pltpu.VMEM((nc, 4 * SUBLANES, LANES), F32),
        pltpu.VMEM((nc, N_DT, CHUNK), F32),
        pltpu.VMEM((CHUNK, SSD_W), F32),
        pltpu.VMEM((CHUNK, 2 * RET_W), BF16),
        pltpu.VMEM((nc, RET_HEADS, RET_HD, 2 * RET_HD), state_dt),
        pltpu.VMEM((nc, SSD_GROUPS, SSD_STATE, 2 * GROUP_W), state_dt),
        pltpu.VMEM((nc, RET_HEADS, RET_HD, 2 * RET_HD), BF16),
        pltpu.VMEM((nc, SSD_GROUPS, SSD_STATE, 2 * GROUP_W), BF16),
        pltpu.VMEM((RET_HEADS, CHUNK, CHUNK), F32),
        pltpu.VMEM((RET_HEADS, 4, CHUNK, LANES), F32),
        pltpu.VMEM((2, CHUNK, D_MODEL), F32),
        pltpu.VMEM((CHUNK, D_FF), BF16),
        pltpu.VMEM((LANES, D_MODEL), BF16),
    ]
    scratch += [pltpu.VMEM(weights[i].shape, BF16) for i in BIG_WEIGHTS]
    scratch += [pltpu.SemaphoreType.DMA((len(BIG_WEIGHTS),))]
    if use_rope:
        scratch += [pltpu.VMEM((nc, CHUNK, LANES), F32)] * 2
    kern = functools.partial(_layer_kernel, L=L, nb=nb, mod_per_seq=mod_per_seq, has_state=has_state,
                             use_rope=use_rope, emit_state=emit_state)
    return pl.pallas_call(
        kern,
        grid=(nb + 1, 2 * nc),
        in_specs=in_specs,
        out_specs=out_specs,
        out_shape=out_shape,
        scratch_shapes=scratch,
        compiler_params=pltpu.CompilerParams(dimension_semantics=("arbitrary", "arbitrary"),
                                             vmem_limit_bytes=VMEM_LIMIT),
        name=name,
    )(*args)


def kernel(x_prompt, x_sample, state_ret, state_ssd, c, c_ctx, w_in, ret_decay_fwd, ret_decay_bwd, conv_w, conv_b, dt_bias_fwd, dt_bias_bwd, a_log_fwd, a_log_bwd, d_skip, ssd_norm_w, w_out, ln1_g, ln1_b, w_gate, w_up, w_down, ln2_g, ln2_b, w_ada, b_ada):
    depth = w_in.shape[0]
    assert depth == 1, "single trunk layer"
    bp, lp, _ = x_prompt.shape
    bs, ls, _ = x_sample.shape
    assert lp % CHUNK == 0 and ls % CHUNK == 0 and ls % GRID_W == 0 and D_FF % FF_BLK == 0


    convw = jnp.zeros((SUBLANES, CONV_CH), F32).at[:CONV_W].set(conv_w[0])
    convb = conv_b[0][None, :]
    dt_bias = jnp.concatenate([dt_bias_fwd[0], dt_bias_bwd[0]])
    a_log = jnp.concatenate([a_log_fwd[0], a_log_bwd[0]])
    ret_decay = jnp.concatenate([ret_decay_fwd[0], ret_decay_bwd[0]])
    hp = jnp.zeros((SUBLANES, LANES), F32)
    hp = hp.at[0, :N_DT].set(dt_bias).at[1, :N_DT].set(a_log)
    hp = hp.at[2, :2 * RET_HEADS].set(ret_decay).at[3, :SSD_HEADS].set(d_skip[0])
    hpc = jnp.zeros((N_DT, LANES), F32).at[:, 0].set(dt_bias).at[:, 1].set(a_log)
    mod, w_in_b, w_out_b, w_gate_b, w_up_b, w_down_b = _prep_call(
        c, c_ctx[None, :], w_ada[0], b_ada[0][None, :],
        jnp.swapaxes(w_in[0], 0, 1), [w_out[0], w_gate[0], w_up[0], w_down[0]])
    weights = (w_in_b, convw, convb, hp, hpc, ssd_norm_w[0][None, :],
               w_out_b, ln1_g[0][None, :], ln1_b[0][None, :],
               w_gate_b, w_up_b, w_down_b, ln2_g[0][None, :], ln2_b[0][None, :])

    yp, new_ret, new_ssd = _layer_call(x_prompt, mod, 1, False, weights, None, True, False, "layer_ctx")
    states = (state_ret, jnp.swapaxes(state_ssd, -1, -2))
    (ys,) = _layer_call(x_sample, mod, 0, True, weights, states, False, True, "layer_lat")
    return (yp, ys, new_ret, jnp.swapaxes(new_ssd, -1, -2))
```

```python
import functools
import math

import jax
import jax.numpy as jnp
from jax import lax
from jax.experimental import pallas as pl
from jax.experimental.pallas import tpu as pltpu

F32 = jnp.float32
BF16 = jnp.bfloat16

D_MODEL = 1024
RET_W = 512
RET_HEADS = 4
RET_HD = 128
SSD_W = 512
SSD_HD = 64
SSD_HEADS = 8
SSD_GROUPS = 2
SSD_STATE = 128
HPG = SSD_HEADS // SSD_GROUPS
GROUP_W = HPG * SSD_HD
CONV_W = 5
CONV_CH = SSD_W + 2 * SSD_GROUPS * SSD_STATE
D_FF = 2816
GRID_W = 64
GRID_SHIFT = 6
ROPE_BASE = 10000.0
EPS = 1e-6
ALPHA = 2.0 ** 0.25
MAIN_COLS = 4 * RET_W + SSD_W + CONV_CH
XBC_COL0 = 4 * RET_W + SSD_W
N_DT = 2 * SSD_HEADS

CHUNK = 256
HALO = 8
FF_BLK = 256
MOD_ROWS = 8
CAST_STEPS = 8
CAST_WT_ROWS = 512
LANES = 128
SUBLANES = 8
XBC_HALF = CONV_CH // 2
VMEM_LIMIT = 62 * 1024 * 1024


def _dot(a, b):
    return jnp.dot(a, b, preferred_element_type=F32)


def _dot_nt(a, b):
    return lax.dot_general(a, b, (((1,), (1,)), ((), ())), preferred_element_type=F32)


def _silu(x):
    return x * jax.nn.sigmoid(x)


def _softplus(x):
    return jnp.maximum(x, 0.0) + jnp.log1p(jnp.exp(-jnp.abs(x)))


def _layer_norm(y, g, b):
    mu = jnp.mean(y, axis=-1, keepdims=True)
    yc = y - mu
    var = jnp.mean(yc * yc, axis=-1, keepdims=True)
    return yc * lax.rsqrt(var + EPS) * g + b


def _cumsum(x, axis):
    n = x.shape[axis]
    idx = lax.broadcasted_iota(jnp.int32, x.shape, axis)
    s = 1
    while s < n:
        x = x + jnp.where(idx >= s, pltpu.roll(x, s, axis), 0.0)
        s *= 2
    return x


def _expand4(cols, lane):
    a = jnp.where(lane < SSD_HD, cols[0], cols[1])
    b = jnp.where(lane < SSD_HD, cols[2], cols[3])
    return jnp.concatenate([a, b], axis=1)


def _interleave(a, b):
    ia = ib = 0
    while ia < len(a) or ib < len(b):
        if ib >= len(b) or (ia < len(a) and ia * len(b) <= ib * len(a)):
            a[ia]()
            ia += 1
        else:
            b[ib]()
            ib += 1


def _prep_kernel(c_ref, cctx_ref, wada_ref, bada_ref, wt_ref, *refs, wt_rows):
    n = (len(refs) - 2) // 2
    srcs, mod_ref, wt_out, dsts = refs[:n], refs[n], refs[n + 1], refs[n + 2:]
    cond = jnp.concatenate([c_ref[...], jnp.broadcast_to(cctx_ref[...], (MOD_ROWS, D_MODEL))], axis=0)
    mod_ref[...] = _dot(_silu(cond).astype(BF16), wada_ref[...].astype(BF16)) + bada_ref[...]
    blk = wt_ref.shape[0]
    row = lax.broadcasted_iota(jnp.int32, (blk, 1), 0) + pl.program_id(0) * blk
    wt_out[...] = jnp.where(row < wt_rows, wt_ref[...], 0.0).T.astype(BF16)
    for src, dst in zip(srcs, dsts):
        dst[...] = src[...].astype(BF16)


def _prep_call(c, c_ctx, w_ada, b_ada, wt, ws):
    assert c.shape == (MOD_ROWS, D_MODEL), c.shape
    for w in ws:
        assert w.shape[0] % (CAST_STEPS * 16) == 0, w.shape
    assert wt.shape[0] <= CAST_STEPS * CAST_WT_ROWS
    n_mod = w_ada.shape[1]
    ada_cols = n_mod // CAST_STEPS
    assert ada_cols % LANES == 0
    specs = [pl.BlockSpec((w.shape[0] // CAST_STEPS, w.shape[1]), lambda i: (i, 0)) for w in ws]
    return pl.pallas_call(
        functools.partial(_prep_kernel, wt_rows=wt.shape[0]),
        grid=(CAST_STEPS,),
        in_specs=[pl.BlockSpec((MOD_ROWS, D_MODEL), lambda i: (0, 0)),
                  pl.BlockSpec((1, D_MODEL), lambda i: (0, 0)),
                  pl.BlockSpec((D_MODEL, ada_cols), lambda i: (0, i)),
                  pl.BlockSpec((1, ada_cols), lambda i: (0, i)),
                  pl.BlockSpec((CAST_WT_ROWS, wt.shape[1]), lambda i: (i, 0))] + specs,
        out_specs=[pl.BlockSpec((2 * MOD_ROWS, ada_cols), lambda i: (0, i)),
                   pl.BlockSpec((wt.shape[1], CAST_WT_ROWS), lambda i: (0, i))] + specs,
        out_shape=[jax.ShapeDtypeStruct((2 * MOD_ROWS, n_mod), F32),
                   jax.ShapeDtypeStruct((wt.shape[1], -(-wt.shape[0] // LANES) * LANES), BF16)]
        + [jax.ShapeDtypeStruct(w.shape, BF16) for w in ws],
        compiler_params=pltpu.CompilerParams(dimension_semantics=("arbitrary",),
                                             vmem_limit_bytes=VMEM_LIMIT),
        name="prep_mod_and_weights",
    )(c, c_ctx, w_ada, b_ada, wt, *ws)


RV_TAIL_F, RV_TAIL_B, RV_CROSS_F, RV_CROSS_B = range(4)
BIG_WEIGHTS = (0, 6, 9, 10, 11)


def _layer_kernel(*refs, L, nb, mod_per_seq, has_state, use_rope, emit_state):
    nc = L // CHUNK
    C = CHUNK
    cross = has_state or nc > 1
    it = iter(refs)
    x_ref, xprev_ref, xnext_ref, mod_ref, wmain_hbm = (next(it) for _ in range(5))
    convw_ref, convb_ref, hp_ref, hpc_ref, normw_ref = (next(it) for _ in range(5))
    wout_hbm, ln1g_ref, ln1b_ref = (next(it) for _ in range(3))
    wg_hbm, wu_hbm, wd_hbm, ln2g_ref, ln2b_ref = (next(it) for _ in range(5))
    if has_state:
        sret0_ref, sssd0_ref = next(it), next(it)
    out_ref = next(it)
    if emit_state:
        nret_ref, nssd_ref = next(it), next(it)
    (q_s, kT_s, v_s, g_s, z_s, stage_s, xs_s, bT_s, c_s, inccol_s, exccol_s, cfcb_s, dec_s,
     rowarg_s, y_s, mix_s, rloc_s, sloc_s, rent_s, sent_s, wdec_s, rvec_s, x1_s, hid_s, wdtr_s) = (
         next(it) for _ in range(25))
    wmain_ref, wout_ref, wg_ref, wu_ref, wd_ref, wsem = (next(it) for _ in range(6))
    if use_rope:
        cos_s, sin_s = next(it), next(it)

    seq = pl.program_id(0)
    step = pl.program_id(1)

    if mod_per_seq:
        row_mix = jnp.minimum(seq, nb - 1)
        row_ffn = jnp.minimum(jnp.where(step > nc, seq, jnp.maximum(seq - 1, 0)), nb - 1)
    else:
        row_mix = row_ffn = 0

    def mod_row(row, k):
        return mod_ref[pl.ds(row, 1), k * D_MODEL:(k + 1) * D_MODEL]

    big_weights = ((wmain_hbm, wmain_ref), (wout_hbm, wout_ref), (wg_hbm, wg_ref), (wu_hbm, wu_ref),
                   (wd_hbm, wd_ref))

    def weight_copy(i):
        return pltpu.make_async_copy(big_weights[i][0], big_weights[i][1], wsem.at[i])

    hp = hp_ref[...]
    dt_bias_row = hp[0:1, :]
    nega_row = -jnp.exp(hp[1:2, :])
    lg_row = -_softplus(-hp[2:3, :])
    dskip_row = hp[3:4, :]
    hpc = hpc_ref[...]
    dt_bias_col = hpc[:, 0:1]
    nega_col = -jnp.exp(hpc[:, 1:2])

    lane = lax.broadcasted_iota(jnp.int32, (1, LANES), 1)

    def ret_decays(hd):
        lgf = lg_row[:, hd:hd + 1]
        lgb = lg_row[:, RET_HEADS + hd:RET_HEADS + hd + 1]
        return lgf, lgb

    def group_heads(gi):
        return [gi * HPG + k for k in range(HPG)]

    def expand_f(arr, gi):
        return _expand4([arr[:, hh:hh + 1] for hh in group_heads(gi)], lane)

    def expand_b(arr, gi):
        return _expand4([arr[:, SSD_HEADS + hh:SSD_HEADS + hh + 1] for hh in group_heads(gi)], lane)

    @pl.when((seq == 0) & (step == 0))
    def _():
        for i in range(len(big_weights)):
            weight_copy(i).start()
        weight_copy(0).wait()
        ii = lax.broadcasted_iota(jnp.int32, (C, C), 0)
        jj = lax.broadcasted_iota(jnp.int32, (C, C), 1)
        dmat = (ii - jj).astype(F32)
        irow = lax.broadcasted_iota(jnp.int32, (C, LANES), 0).astype(F32)
        for hd in range(RET_HEADS):
            lgf, lgb = ret_decays(hd)
            wdec_s[hd] = jnp.exp(jnp.where(jj <= ii, dmat * lgf, -dmat * lgb))
            rvec_s[hd, RV_TAIL_F] = jnp.exp((C - 1.0 - irow) * lgf)
            rvec_s[hd, RV_TAIL_B] = jnp.exp(irow * lgb)
            rvec_s[hd, RV_CROSS_F] = jnp.exp((irow + 1.0) * lgf)
            rvec_s[hd, RV_CROSS_B] = jnp.exp((C - irow) * lgb)
        wdtr_s[...] = wmain_ref[:, MAIN_COLS:MAIN_COLS + LANES].astype(F32).T.astype(BF16)
        if use_rope:
            ln = lax.broadcasted_iota(jnp.int32, (C, LANES), 1)
            nf = RET_HD // 4
            inv = jnp.exp((ln & (nf - 1)).astype(F32) * (-math.log(ROPE_BASE) / nf))
            for cc in range(nc):
                t = lax.broadcasted_iota(jnp.int32, (C, LANES), 0) + cc * C
                pos = jnp.where((ln & (2 * nf - 1)) < nf, t >> GRID_SHIFT, t & (GRID_W - 1)).astype(F32)
                ang = pos * inv
                cos_s[cc] = jnp.cos(ang)
                sin_s[cc] = jnp.where(ln < RET_HD // 2, -jnp.sin(ang), jnp.sin(ang))

    def project(c):
        st = {}
        CB = 256
        assert CONV_W == 5 and HALO >= CONV_W // 2

        def v_mod():
            sh1 = mod_row(row_mix, 0)
            sc1 = mod_row(row_mix, 1)
            xe = jnp.concatenate([xprev_ref[0], x_ref[0], xnext_ref[0]], axis=0)
            xm = xe * (1.0 + sc1) + sh1
            st["he"] = xm.astype(BF16)
            st["h"] = xm[HALO:HALO + C].astype(BF16)

        def m_main(name, lo, hi, halo=False):
            def run():
                st[name] = _dot(st["he" if halo else "h"], wmain_ref[:, lo:hi])
            return run

        def m_dt():
            raw_c = _dot(st["h"], wmain_ref[:, MAIN_COLS:MAIN_COLS + LANES])
            st["dtc"] = raw_c + pltpu.roll(raw_c, SSD_HEADS, 1)
            raw_r = _dot_nt(wdtr_s[0:N_DT, :], st["h"])
            st["dtr"] = raw_r + pltpu.roll(raw_r, SSD_HEADS, 0)

        def v_dt():
            dt_c = _softplus(st["dtc"] + dt_bias_row)
            lac = dt_c * nega_row
            inc_col = _cumsum(lac, 0)
            exc_col = inc_col - lac
            tot_col = inc_col[C - 1:C, :]
            inccol_s[c] = inc_col
            exccol_s[c] = exc_col
            dec_s[c, 0:SUBLANES, :] = jnp.broadcast_to(jnp.exp(tot_col), (SUBLANES, LANES))
            if cross:
                cfcb_s[c] = jnp.exp(jnp.where(lane < SSD_HEADS, inc_col, tot_col - exc_col))
            st["sf"] = jnp.exp(tot_col - inc_col) * dt_c
            st["sb"] = jnp.exp(exc_col) * dt_c
            dt_r = _softplus(st["dtr"] + dt_bias_col)
            lar = dt_r * nega_col
            inc_row = _cumsum(lar, 1)
            ldt = jnp.log(dt_r)
            rid = lax.broadcasted_iota(jnp.int32, (N_DT, C), 0)
            rowarg_s[c] = jnp.where(rid < SSD_HEADS, inc_row - ldt, inc_row - lar + ldt)

        def v_stage(hf):
            def run():
                pe = st["pe%d" % hf]
                cs = slice(hf * XBC_HALF, (hf + 1) * XBC_HALF)
                stage_s[0:HALO, cs] = jnp.where(c > 0, pe[0:HALO], 0.0)
                stage_s[HALO:HALO + C, cs] = pe[HALO:HALO + C]
                stage_s[HALO + C:, cs] = jnp.where(c < nc - 1, pe[HALO + C:], 0.0)
            return run

        def conv_block(cb):
            cs = slice(cb * CB, (cb + 1) * CB)
            rows = C + 2 * HALO
            xin = stage_s[:, cs]
            taps = [convw_ref[k:k + 1, cs] * xin for k in range(CONV_W)]
            up = lambda a: pltpu.roll(a, rows - 1, 0)
            down = lambda a: pltpu.roll(a, 1, 0)
            acc = taps[2] + up(taps[3] + up(taps[4])) + down(taps[1] + down(taps[0]))
            return _silu(acc[HALO:HALO + C] + convb_ref[0:1, cs])

        def v_conv_x(cb):
            def run():
                xs_s[c, :, cb * CB:(cb + 1) * CB] = conv_block(cb)
            return run

        def v_conv_b():
            bT_s[c] = conv_block(SSD_W // CB).T.astype(BF16)

        def v_conv_c():
            c_s[c] = conv_block(SSD_W // CB + 1).astype(BF16)

        def rope(a):
            if not use_rope:
                return a
            return a * cos_s[c] + pltpu.roll(a, RET_HD // 2, 1) * sin_s[c]

        def v_q():
            for hd in range(RET_HEADS):
                sl = slice(hd * RET_HD, (hd + 1) * RET_HD)
                q_s[c, :, sl] = rope(st["pq"][:, sl]).astype(BF16)

        def v_k():
            for hd in range(RET_HEADS):
                sl = slice(hd * RET_HD, (hd + 1) * RET_HD)
                kh = rope(st["pk"][:, sl]) * (RET_HD ** -0.5)
                kT_s[c, sl, :] = kh.T.astype(BF16)

        def v_v():
            v_s[c] = st["pv"].astype(BF16)

        def v_g():
            g_s[c] = _silu(st["pg"]).astype(BF16)

        def v_z():
            z_s[c] = _silu(st["pz"]).astype(BF16)

        def m_sloc(gi):
            def run():
                xg = xs_s[c, :, gi * GROUP_W:(gi + 1) * GROUP_W]
                vcat = jnp.concatenate([xg * expand_f(st["sf"], gi), xg * expand_b(st["sb"], gi)],
                                       axis=1).astype(BF16)
                sloc_s[c, gi] = _dot(bT_s[c, gi * SSD_STATE:(gi + 1) * SSD_STATE, :], vcat).astype(sloc_s.dtype)
            return run

        def m_rloc(hd):
            def run():
                sl = slice(hd * RET_HD, (hd + 1) * RET_HD)
                vf = st["pv"][:, sl]
                vcat = jnp.concatenate([vf * rvec_s[hd, RV_TAIL_F], vf * rvec_s[hd, RV_TAIL_B]],
                                       axis=1).astype(BF16)
                rloc_s[c, hd] = _dot(kT_s[c, sl, :], vcat).astype(rloc_s.dtype)
            return run

        m_pe0 = m_main("pe0", XBC_COL0, XBC_COL0 + XBC_HALF, halo=True)
        m_pe1 = m_main("pe1", XBC_COL0 + XBC_HALF, MAIN_COLS, halo=True)
        m_q = m_main("pq", 0, RET_W)
        m_k = m_main("pk", RET_W, 2 * RET_W)
        m_v = m_main("pv", 2 * RET_W, 3 * RET_W)
        m_g = m_main("pg", 3 * RET_W, 4 * RET_W)
        m_z = m_main("pz", 4 * RET_W, XBC_COL0)
        order = [v_mod, m_dt, m_pe0, m_pe1, v_dt, m_q, v_stage(0), v_stage(1), m_k, v_conv_x(0), m_v,
                 v_conv_x(1), v_q, m_g, v_conv_b, v_k, m_z, v_conv_c, v_v,
                 m_rloc(0), m_rloc(1), v_g, m_rloc(2), m_rloc(3), m_sloc(0), m_sloc(1), v_z]
        for piece in order:
            piece()

    def recurrences():
        for hd in range(RET_HEADS):
            lgf, lgb = ret_decays(hd)
            dec_f = jnp.exp(C * lgf)
            dec_b = jnp.exp(C * lgb)
            if has_state:
                ent_f = sret0_ref[0, 0, 0, hd]
                ent_b = sret0_ref[0, 0, 1, hd]
            else:
                ent_f = jnp.zeros((RET_HD, RET_HD), F32)
                ent_b = jnp.zeros((RET_HD, RET_HD), F32)
            for c in range(nc):
                if cross:
                    rent_s[c, hd, :, 0:RET_HD] = ent_f.astype(BF16)
                ent_f = dec_f * ent_f + rloc_s[c, hd, :, 0:RET_HD]
            for c in range(nc - 1, -1, -1):
                if cross:
                    rent_s[c, hd, :, RET_HD:] = ent_b.astype(BF16)
                ent_b = dec_b * ent_b + rloc_s[c, hd, :, RET_HD:]
            if emit_state:
                nret_ref[0, 0, 0, hd] = ent_f
                nret_ref[0, 0, 1, hd] = ent_b

        for gi in range(SSD_GROUPS):
            heads = group_heads(gi)
            if has_state:
                ent_f = jnp.concatenate([sssd0_ref[0, 0, 0, hh] for hh in heads], axis=0).T
                ent_b = jnp.concatenate([sssd0_ref[0, 0, 1, hh] for hh in heads], axis=0).T
            else:
                ent_f = jnp.zeros((SSD_STATE, GROUP_W), F32)
                ent_b = jnp.zeros((SSD_STATE, GROUP_W), F32)
            for c in range(nc):
                if cross:
                    sent_s[c, gi, :, 0:GROUP_W] = ent_f.astype(BF16)
                ent_f = expand_f(dec_s[c, 0:1, :], gi) * ent_f + sloc_s[c, gi, :, 0:GROUP_W]
            for c in range(nc - 1, -1, -1):
                if cross:
                    sent_s[c, gi, :, GROUP_W:] = ent_b.astype(BF16)
                ent_b = expand_b(dec_s[c, 0:1, :], gi) * ent_b + sloc_s[c, gi, :, GROUP_W:]
            if emit_state:
                ent_ft = ent_f.T
                ent_bt = ent_b.T
                for k, hh in enumerate(heads):
                    nssd_ref[0, 0, 0, hh] = ent_ft[k * SSD_HD:(k + 1) * SSD_HD, :]
                    nssd_ref[0, 0, 1, hh] = ent_bt[k * SSD_HD:(k + 1) * SSD_HD, :]

    def emit_pieces(c, slot):
        rsl = [slice(hd * RET_HD, (hd + 1) * RET_HD) for hd in range(RET_HEADS)]
        gsl = [slice(gi * SSD_STATE, (gi + 1) * SSD_STATE) for gi in range(SSD_GROUPS)]
        xsl = [slice(gi * GROUP_W, (gi + 1) * GROUP_W) for gi in range(SSD_GROUPS)]
        st = {}

        def scores():
            ii = lax.broadcasted_iota(jnp.int32, (C, C), 0)
            jj = lax.broadcasted_iota(jnp.int32, (C, C), 1)
            st["causal"] = jj <= ii
            qs = [q_s[c, :, sl] for sl in rsl]
            cms = [c_s[c, :, sl] for sl in gsl]
            st["sc_s"] = [_dot(cms[gi], bT_s[c, gsl[gi], :]) for gi in range(SSD_GROUPS)]
            st["sc_r"] = [_dot(qs[hd], kT_s[c, rsl[hd], :]) for hd in range(RET_HEADS)]
            if cross:
                st["yc_s"] = [_dot(cms[gi], sent_s[c, gi]) for gi in range(SSD_GROUPS)]
                st["yc_r"] = [_dot(qs[hd], rent_s[c, hd]) for hd in range(RET_HEADS)]
            st["inc_col"] = inccol_s[c]
            st["exc_col"] = exccol_s[c]
            st["rowarg"] = rowarg_s[c]

        def ssd_head(gi, k):
            def run():
                hh = gi * HPG + k
                hb = SSD_HEADS + hh
                arg = jnp.where(st["causal"],
                                st["inc_col"][:, hh:hh + 1] - st["rowarg"][hh:hh + 1, :],
                                st["rowarg"][hb:hb + 1, :] - st["exc_col"][:, hb:hb + 1])
                m = (st["sc_s"][gi] * jnp.exp(arg)).astype(BF16)
                xh = xs_s[c, :, hh * SSD_HD:(hh + 1) * SSD_HD].astype(BF16)
                y_s[:, hh * SSD_HD:(hh + 1) * SSD_HD] = _dot(m, xh)
            return run

        def ssd_group(gi):
            def run():
                xg = xs_s[c, :, xsl[gi]]
                yg = y_s[:, xsl[gi]] + expand_f(dskip_row, gi) * xg
                if cross:
                    cfcb = cfcb_s[c]
                    yc = st["yc_s"][gi]
                    yg = yg + expand_f(cfcb, gi) * yc[:, 0:GROUP_W] + expand_b(cfcb, gi) * yc[:, GROUP_W:]
                y_s[:, xsl[gi]] = yg
            return run

        def ret_head(hd):
            def run():
                m = (st["sc_r"][hd] * wdec_s[hd]).astype(BF16)
                o = _dot(m, v_s[c, :, rsl[hd]])
                if cross:
                    yc = st["yc_r"][hd]
                    o = o + rvec_s[hd, RV_CROSS_F] * yc[:, 0:RET_HD] + rvec_s[hd, RV_CROSS_B] * yc[:, RET_HD:]
                o = o * lax.rsqrt(jnp.mean(o * o, axis=-1, keepdims=True) + EPS)
                mix_s[:, rsl[hd]] = (g_s[c, :, rsl[hd]].astype(F32) * o).astype(BF16)
            return run

        def ssd_norm():
            yz = y_s[...] * z_s[c].astype(F32)
            yn = yz * lax.rsqrt(jnp.mean(yz * yz, axis=-1, keepdims=True) + EPS) * normw_ref[...]
            mix_s[:, RET_W:] = yn.astype(BF16)

        def out_proj():
            g1 = mod_row(row_mix, 2)
            y = ALPHA * x_ref[0] + g1 * _dot(mix_s[...], wout_ref[...])
            x1_s[slot] = _layer_norm(y, ln1g_ref[...], ln1b_ref[...])

        pieces = [scores]
        for gi in range(SSD_GROUPS):
            pieces += [ssd_head(gi, k) for k in range(HPG)] + [ssd_group(gi)]
        pieces += [ret_head(hd) for hd in range(RET_HEADS)] + [ssd_norm, out_proj]
        return pieces

    def ffn_pieces(slot):
        st = {}

        def start():
            sh2 = mod_row(row_ffn, 3)
            sc2 = mod_row(row_ffn, 4)
            st["h2"] = (x1_s[slot] * (1.0 + sc2) + sh2).astype(BF16)

        def hidden(j):
            def run():
                js = slice(j * FF_BLK, (j + 1) * FF_BLK)
                h2 = st["h2"]
                hid_s[:, js] = (_silu(_dot(h2, wg_ref[:, js])) * _dot(h2, wu_ref[:, js])).astype(BF16)
            return run

        def finish():
            g2 = mod_row(row_ffn, 5)
            y = ALPHA * x1_s[slot] + g2 * _dot(hid_s[...], wd_ref[...])
            out_ref[0] = _layer_norm(y, ln2g_ref[...], ln2b_ref[...])

        return [start] + [hidden(j) for j in range(D_FF // FF_BLK)] + [finish]

    def run_all(pieces):
        for p in pieces:
            p()

    @pl.when((step < nc) & (seq < nb))
    def _():
        project(step)

    lead = [recurrences] if nc == 1 else []
    if nc > 1:
        @pl.when((step == nc) & (seq < nb))
        def _():
            recurrences()

    kk = step - nc
    slot = (seq * nc + kk) & 1
    first = (seq == 0) & (step == nc)

    @pl.when(first)
    def _():
        for i in range(1, len(big_weights)):
            weight_copy(i).wait()
        run_all(lead + emit_pieces(kk, slot))

    @pl.when((step >= nc) & (seq < nb) & jnp.logical_not(first))
    def _():
        _interleave(ffn_pieces(1 - slot), lead + emit_pieces(kk, slot))

    @pl.when((seq == nb) & (step == nc))
    def _():
        run_all(ffn_pieces((nb * nc - 1) & 1))


def _const_spec(shape):
    nd = len(shape)
    return pl.BlockSpec(shape, lambda b, s: (0,) * nd, pipeline_mode=pl.Buffered(1))


def _layer_call(x, mod, mod_block, mod_per_seq, weights, states, emit_state, use_rope, name):
    nb, L, _ = x.shape
    nc = L // CHUNK
    hpc_blocks = CHUNK // HALO
    has_state = states is not None
    last = nb - 1

    def chunk_of(b, s):
        return jnp.where(b > last, nc - 1, jnp.where(s < nc, s, s - nc))

    def halo_chunk(b, s):
        return jnp.where(b > last, nc - 1, jnp.minimum(s, nc - 1))

    def seq_of(b):
        return jnp.minimum(b, last)

    def out_map(b, s):
        live = (s > nc) & (b < nb)
        idle_chunk = jnp.where(b == 0, 0, nc - 1)
        return (jnp.where(live, b, jnp.maximum(b - 1, 0)), jnp.where(live, s - nc - 1, idle_chunk), 0)

    assert not mod_per_seq or nb <= MOD_ROWS
    in_specs = [
        pl.BlockSpec((1, CHUNK, D_MODEL), lambda b, s: (seq_of(b), chunk_of(b, s), 0)),
        pl.BlockSpec((1, HALO, D_MODEL),
                     lambda b, s: (seq_of(b), jnp.maximum(halo_chunk(b, s) * hpc_blocks - 1, 0), 0)),
        pl.BlockSpec((1, HALO, D_MODEL),
                     lambda b, s: (seq_of(b), jnp.minimum((halo_chunk(b, s) + 1) * hpc_blocks, nc * hpc_blocks - 1), 0)),
        pl.BlockSpec((MOD_ROWS, 6 * D_MODEL), lambda b, s: (mod_block, 0)),
    ] + [pl.BlockSpec(memory_space=pl.ANY) if i in BIG_WEIGHTS else _const_spec(w.shape)
         for i, w in enumerate(weights)]
    args = [x, x, x, mod] + list(weights)
    ret_block = (1, 1, 2, RET_HEADS, RET_HD, RET_HD)
    ssd_block = (1, 1, 2, SSD_HEADS, SSD_HD, SSD_STATE)
    state_map = lambda b, s: (seq_of(b), 0, 0, 0, 0, 0)
    if has_state:
        in_specs += [pl.BlockSpec(ret_block, state_map, pipeline_mode=pl.Buffered(1)),
                     pl.BlockSpec(ssd_block, state_map, pipeline_mode=pl.Buffered(1))]
        args += list(states)
    out_shape = [jax.ShapeDtypeStruct((nb, L, D_MODEL), F32)]
    out_specs = [pl.BlockSpec((1, CHUNK, D_MODEL), out_map)]
    if emit_state:
        out_shape += [jax.ShapeDtypeStruct((nb,) + ret_block[1:], F32),
                      jax.ShapeDtypeStruct((nb,) + ssd_block[1:], F32)]
        out_specs += [pl.BlockSpec(ret_block, state_map), pl.BlockSpec(ssd_block, state_map)]
    state_dt = F32 if emit_state else BF16
    scratch = [
        pltpu.VMEM((nc, CHUNK, RET_W), BF16),
        pltpu.VMEM((nc, RET_W, CHUNK), BF16),
        pltpu.VMEM((nc, CHUNK, RET_W), BF16),
        pltpu.VMEM((nc, CHUNK, RET_W), BF16),
        pltpu.VMEM((nc, CHUNK, SSD_W), BF16),
        pltpu.VMEM((CHUNK + 2 * HALO, CONV_CH), F32),
        pltpu.VMEM((nc, CHUNK, SSD_W), F32),
        pltpu.VMEM((nc, SSD_GROUPS * SSD_STATE, CHUNK), BF16),
        pltpu.VMEM((nc, CHUNK, SSD_GROUPS * SSD_STATE), BF16),
        pltpu.VMEM((nc, CHUNK, LANES), F32),
        pltpu.VMEM((nc, CHUNK, LANES), F32),
        pltpu.VMEM((nc, CHUNK, LANES), F32),
        pltpu.VMEM((nc, 4 * SUBLANES, LANES), F32),
        pltpu.VMEM((nc, N_DT, CHUNK), F32),
        pltpu.VMEM((CHUNK, SSD_W), F32),
        pltpu.VMEM((CHUNK, 2 * RET_W), BF16),
        pltpu.VMEM((nc, RET_HEADS, RET_HD, 2 * RET_HD), state_dt),
        pltpu.VMEM((nc, SSD_GROUPS, SSD_STATE, 2 * GROUP_W), state_dt),
        pltpu.VMEM((nc, RET_HEADS, RET_HD, 2 * RET_HD), BF16),
        pltpu.VMEM((nc, SSD_GROUPS, SSD_STATE, 2 * GROUP_W), BF16),
        pltpu.VMEM((RET_HEADS, CHUNK, CHUNK), F32),
        pltpu.VMEM((RET_HEADS, 4, CHUNK, LANES), F32),
        pltpu.VMEM((2, CHUNK, D_MODEL), F32),
        pltpu.VMEM((CHUNK, D_FF), BF16),
        pltpu.VMEM((LANES, D_MODEL), BF16),
    ]
    scratch += [pltpu.VMEM(weights[i].shape, BF16) for i in BIG_WEIGHTS]
    scratch += [pltpu.SemaphoreType.DMA((len(BIG_WEIGHTS),))]
    if use_rope:
        scratch += [pltpu.VMEM((nc, CHUNK, LANES), F32)] * 2
    kern = functools.partial(_layer_kernel, L=L, nb=nb, mod_per_seq=mod_per_seq, has_state=has_state,
                             use_rope=use_rope, emit_state=emit_state)
    return pl.pallas_call(
        kern,
        grid=(nb + 1, 2 * nc),
        in_specs=in_specs,
        out_specs=out_specs,
        out_shape=out_shape,
        scratch_shapes=scratch,
        compiler_params=pltpu.CompilerParams(dimension_semantics=("arbitrary", "arbitrary"),
                                             vmem_limit_bytes=VMEM_LIMIT),
        name=name,
    )(*args)


def kernel(x_prompt, x_sample, state_ret, state_ssd, c, c_ctx, w_in, ret_decay_fwd, ret_decay_bwd, conv_w, conv_b, dt_bias_fwd, dt_bias_bwd, a_log_fwd, a_log_bwd, d_skip, ssd_norm_w, w_out, ln1_g, ln1_b, w_gate, w_up, w_down, ln2_g, ln2_b, w_ada, b_ada):
    depth = w_in.shape[0]
    assert depth == 1, "single trunk layer"
    bp, lp, _ = x_prompt.shape
    bs, ls, _ = x_sample.shape
    assert lp % CHUNK == 0 and ls % CHUNK == 0 and ls % GRID_W == 0 and D_FF % FF_BLK == 0


    convw = jnp.zeros((SUBLANES, CONV_CH), F32).at[:CONV_W].set(conv_w[0])
    convb = conv_b[0][None, :]
    dt_bias = jnp.concatenate([dt_bias_fwd[0], dt_bias_bwd[0]])
    a_log = jnp.concatenate([a_log_fwd[0], a_log_bwd[0]])
    ret_decay = jnp.concatenate([ret_decay_fwd[0], ret_decay_bwd[0]])
    hp = jnp.zeros((SUBLANES, LANES), F32)
    hp = hp.at[0, :N_DT].set(dt_bias).at[1, :N_DT].set(a_log)
    hp = hp.at[2, :2 * RET_HEADS].set(ret_decay).at[3, :SSD_HEADS].set(d_skip[0])
    hpc = jnp.zeros((N_DT, LANES), F32).at[:, 0].set(dt_bias).at[:, 1].set(a_log)
    mod, w_in_b, w_out_b, w_gate_b, w_up_b, w_down_b = _prep_call(
        c, c_ctx[None, :], w_ada[0], b_ada[0][None, :],
        jnp.swapaxes(w_in[0], 0, 1), [w_out[0], w_gate[0], w_up[0], w_down[0]])
    weights = (w_in_b, convw, convb, hp, hpc, ssd_norm_w[0][None, :],
               w_out_b, ln1_g[0][None, :], ln1_b[0][None, :],
               w_gate_b, w_up_b, w_down_b, ln2_g[0][None, :], ln2_b[0][None, :])

    yp, new_ret, new_ssd = _layer_call(x_prompt, mod, 1, False, weights, None, True, False, "layer_ctx")
    states = (state_ret, jnp.swapaxes(state_ssd, -1, -2))
    (ys,) = _layer_call(x_sample, mod, 0, True, weights, states, False, True, "layer_lat")
    return (yp, ys, new_ret, jnp.swapaxes(new_ssd, -1, -2))
```

```python
import functools
import math

import jax
import jax.numpy as jnp
from jax import lax
from jax.experimental import pallas as pl
from jax.experimental.pallas import tpu as pltpu

F32 = jnp.float32
BF16 = jnp.bfloat16

D_MODEL = 1024
RET_W = 512
RET_HEADS = 4
RET_HD = 128
SSD_W = 512
SSD_HD = 64
SSD_HEADS = 8
SSD_GROUPS = 2
SSD_STATE = 128
HPG = SSD_HEADS // SSD_GROUPS
GROUP_W = HPG * SSD_HD
CONV_W = 5
CONV_CH = SSD_W + 2 * SSD_GROUPS * SSD_STATE
D_FF = 2816
GRID_W = 64
GRID_SHIFT = 6
ROPE_BASE = 10000.0
EPS = 1e-6
ALPHA = 2.0 ** 0.25
MAIN_COLS = 4 * RET_W + SSD_W + CONV_CH
XBC_COL0 = 4 * RET_W + SSD_W
N_DT = 2 * SSD_HEADS

CHUNK = 256
HALO = 8
FF_BLK = 256
MOD_ROWS = 8
CAST_STEPS = 8
CAST_WT_ROWS = 512
LANES = 128
SUBLANES = 8
XBC_HALF = CONV_CH // 2
VMEM_LIMIT = 62 * 1024 * 1024


def _dot(a, b):
    return jnp.dot(a, b, preferred_element_type=F32)


def _dot_nt(a, b):
    return lax.dot_general(a, b, (((1,), (1,)), ((), ())), preferred_element_type=F32)


def _silu(x):
    return x * jax.nn.sigmoid(x)


def _softplus(x):
    return jnp.maximum(x, 0.0) + jnp.log1p(jnp.exp(-jnp.abs(x)))


def _layer_norm(y, g, b):
    mu = jnp.mean(y, axis=-1, keepdims=True)
    yc = y - mu
    var = jnp.mean(yc * yc, axis=-1, keepdims=True)
    return yc * lax.rsqrt(var + EPS) * g + b


def _cumsum(x, axis):
    n = x.shape[axis]
    idx = lax.broadcasted_iota(jnp.int32, x.shape, axis)
    s = 1
    while s < n:
        x = x + jnp.where(idx >= s, pltpu.roll(x, s, axis), 0.0)
        s *= 2
    return x


def _expand4(cols, lane):
    a = jnp.where(lane < SSD_HD, cols[0], cols[1])
    b = jnp.where(lane < SSD_HD, cols[2], cols[3])
    return jnp.concatenate([a, b], axis=1)


def _interleave(a, b):
    ia = ib = 0
    while ia < len(a) or ib < len(b):
        if ib >= len(b) or (ia < len(a) and ia * len(b) <= ib * len(a)):
            a[ia]()
            ia += 1
        else:
            b[ib]()
            ib += 1


def _prep_kernel(c_ref, cctx_ref, wada_ref, bada_ref, wt_ref, *refs, wt_rows):
    n = (len(refs) - 2) // 2
    srcs, mod_ref, wt_out, dsts = refs[:n], refs[n], refs[n + 1], refs[n + 2:]
    cond = jnp.concatenate([c_ref[...], jnp.broadcast_to(cctx_ref[...], (MOD_ROWS, D_MODEL))], axis=0)
    mod_ref[...] = _dot(_silu(cond).astype(BF16), wada_ref[...].astype(BF16)) + bada_ref[...]
    blk = wt_ref.shape[0]
    row = lax.broadcasted_iota(jnp.int32, (blk, 1), 0) + pl.program_id(0) * blk
    wt_out[...] = jnp.where(row < wt_rows, wt_ref[...], 0.0).T.astype(BF16)
    for src, dst in zip(srcs, dsts):
        dst[...] = src[...].astype(BF16)


def _prep_call(c, c_ctx, w_ada, b_ada, wt, ws):
    assert c.shape == (MOD_ROWS, D_MODEL), c.shape
    for w in ws:
        assert w.shape[0] % (CAST_STEPS * 16) == 0, w.shape
    assert wt.shape[0] <= CAST_STEPS * CAST_WT_ROWS
    n_mod = w_ada.shape[1]
    ada_cols = n_mod // CAST_STEPS
    assert ada_cols % LANES == 0
    specs = [pl.BlockSpec((w.shape[0] // CAST_STEPS, w.shape[1]), lambda i: (i, 0)) for w in ws]
    return pl.pallas_call(
        functools.partial(_prep_kernel, wt_rows=wt.shape[0]),
        grid=(CAST_STEPS,),
        in_specs=[pl.BlockSpec((MOD_ROWS, D_MODEL), lambda i: (0, 0)),
                  pl.BlockSpec((1, D_MODEL), lambda i: (0, 0)),
                  pl.BlockSpec((D_MODEL, ada_cols), lambda i: (0, i)),
                  pl.BlockSpec((1, ada_cols), lambda i: (0, i)),
                  pl.BlockSpec((CAST_WT_ROWS, wt.shape[1]), lambda i: (i, 0))] + specs,
        out_specs=[pl.BlockSpec((2 * MOD_ROWS, ada_cols), lambda i: (0, i)),
                   pl.BlockSpec((wt.shape[1], CAST_WT_ROWS), lambda i: (0, i))] + specs,
        out_shape=[jax.ShapeDtypeStruct((2 * MOD_ROWS, n_mod), F32),
                   jax.ShapeDtypeStruct((wt.shape[1], -(-wt.shape[0] // LANES) * LANES), BF16)]
        + [jax.ShapeDtypeStruct(w.shape, BF16) for w in ws],
        compiler_params=pltpu.CompilerParams(dimension_semantics=("arbitrary",),
                                             vmem_limit_bytes=VMEM_LIMIT),
        name="prep_mod_and_weights",
    )(c, c_ctx, w_ada, b_ada, wt, *ws)


RV_TAIL_F, RV_TAIL_B, RV_CROSS_F, RV_CROSS_B = range(4)
BIG_WEIGHTS = (0, 6, 9, 10, 11)


def _layer_kernel(*refs, L, nb, mod_per_seq, has_state, use_rope, emit_state):
    nc = L // CHUNK
    C = CHUNK
    cross = has_state or nc > 1
    it = iter(refs)
    x_ref, xprev_ref, xnext_ref, mod_ref, wmain_hbm = (next(it) for _ in range(5))
    convw_ref, convb_ref, hp_ref, hpc_ref, normw_ref = (next(it) for _ in range(5))
    wout_hbm, ln1g_ref, ln1b_ref = (next(it) for _ in range(3))
    wg_hbm, wu_hbm, wd_hbm, ln2g_ref, ln2b_ref = (next(it) for _ in range(5))
    if has_state:
        sret0_ref, sssd0_ref = next(it), next(it)
    out_ref = next(it)
    if emit_state:
        nret_ref, nssd_ref = next(it), next(it)
    (q_s, kT_s, v_s, g_s, z_s, stage_s, xs_s, bT_s, c_s, inccol_s, exccol_s, cfcb_s, dec_s,
     rowarg_s, y_s, mix_s, rloc_s, sloc_s, rent_s, sent_s, wdec_s, rvec_s, x1_s, hid_s, wdtr_s) = (
         next(it) for _ in range(25))
    wmain_ref, wout_ref, wg_ref, wu_ref, wd_ref, wsem = (next(it) for _ in range(6))
    if use_rope:
        cos_s, sin_s = next(it), next(it)

    seq = pl.program_id(0)
    step = pl.program_id(1)

    if mod_per_seq:
        row_mix = jnp.minimum(seq, nb - 1)
        row_ffn = jnp.minimum(jnp.where(step > nc, seq, jnp.maximum(seq - 1, 0)), nb - 1)
    else:
        row_mix = row_ffn = 0

    def mod_row(row, k):
        return mod_ref[pl.ds(row, 1), k * D_MODEL:(k + 1) * D_MODEL]

    big_weights = ((wmain_hbm, wmain_ref), (wout_hbm, wout_ref), (wg_hbm, wg_ref), (wu_hbm, wu_ref),
                   (wd_hbm, wd_ref))

    def weight_copy(i):
        return pltpu.make_async_copy(big_weights[i][0], big_weights[i][1], wsem.at[i])

    hp = hp_ref[...]
    dt_bias_row = hp[0:1, :]
    nega_row = -jnp.exp(hp[1:2, :])
    lg_row = -_softplus(-hp[2:3, :])
    dskip_row = hp[3:4, :]
    hpc = hpc_ref[...]
    dt_bias_col = hpc[:, 0:1]
    nega_col = -jnp.exp(hpc[:, 1:2])

    lane = lax.broadcasted_iota(jnp.int32, (1, LANES), 1)

    def ret_decays(hd):
        lgf = lg_row[:, hd:hd + 1]
        lgb = lg_row[:, RET_HEADS + hd:RET_HEADS + hd + 1]
        return lgf, lgb

    def group_heads(gi):
        return [gi * HPG + k for k in range(HPG)]

    def expand_f(arr, gi):
        return _expand4([arr[:, hh:hh + 1] for hh in group_heads(gi)], lane)

    def expand_b(arr, gi):
        return _expand4([arr[:, SSD_HEADS + hh:SSD_HEADS + hh + 1] for hh in group_heads(gi)], lane)

    @pl.when((seq == 0) & (step == 0))
    def _():
        for i in range(len(big_weights)):
            weight_copy(i).start()
        weight_copy(0).wait()
        ii = lax.broadcasted_iota(jnp.int32, (C, C), 0)
        jj = lax.broadcasted_iota(jnp.int32, (C, C), 1)
        dmat = (ii - jj).astype(F32)
        irow = lax.broadcasted_iota(jnp.int32, (C, LANES), 0).astype(F32)
        for hd in range(RET_HEADS):
            lgf, lgb = ret_decays(hd)
            wdec_s[hd] = jnp.exp(jnp.where(jj <= ii, dmat * lgf, -dmat * lgb))
            rvec_s[hd, RV_TAIL_F] = jnp.exp((C - 1.0 - irow) * lgf)
            rvec_s[hd, RV_TAIL_B] = jnp.exp(irow * lgb)
            rvec_s[hd, RV_CROSS_F] = jnp.exp((irow + 1.0) * lgf)
            rvec_s[hd, RV_CROSS_B] = jnp.exp((C - irow) * lgb)
        wdtr_s[...] = wmain_ref[:, MAIN_COLS:MAIN_COLS + LANES].astype(F32).T.astype(BF16)
        if use_rope:
            ln = lax.broadcasted_iota(jnp.int32, (C, LANES), 1)
            nf = RET_HD // 4
            inv = jnp.exp((ln & (nf - 1)).astype(F32) * (-math.log(ROPE_BASE) / nf))
            for cc in range(nc):
                t = lax.broadcasted_iota(jnp.int32, (C, LANES), 0) + cc * C
                pos = jnp.where((ln & (2 * nf - 1)) < nf, t >> GRID_SHIFT, t & (GRID_W - 1)).astype(F32)
                ang = pos * inv
                cos_s[cc] = jnp.cos(ang)
                sin_s[cc] = jnp.where(ln < RET_HD // 2, -jnp.sin(ang), jnp.sin(ang))

    def project(c):
        st = {}
        CB = 256
        assert CONV_W == 5 and HALO >= CONV_W // 2

        def v_mod():
            sh1 = mod_row(row_mix, 0)
            sc1 = mod_row(row_mix, 1)
            xe = jnp.concatenate([xprev_ref[0], x_ref[0], xnext_ref[0]], axis=0)
            xm = xe * (1.0 + sc1) + sh1
            st["he"] = xm.astype(BF16)
            st["h"] = xm[HALO:HALO + C].astype(BF16)

        def m_main(name, lo, hi, halo=False):
            def run():
                st[name] = _dot(st["he" if halo else "h"], wmain_ref[:, lo:hi])
            return run

        def m_dt():
            raw_c = _dot(st["h"], wmain_ref[:, MAIN_COLS:MAIN_COLS + LANES])
            st["dtc"] = raw_c + pltpu.roll(raw_c, SSD_HEADS, 1)
            raw_r = _dot_nt(wdtr_s[0:N_DT, :], st["h"])
            st["dtr"] = raw_r + pltpu.roll(raw_r, SSD_HEADS, 0)

        def v_dt():
            dt_c = _softplus(st["dtc"] + dt_bias_row)
            lac = dt_c * nega_row
            inc_col = _cumsum(lac, 0)
            exc_col = inc_col - lac
            tot_col = inc_col[C - 1:C, :]
            inccol_s[c] = inc_col
            exccol_s[c] = exc_col
            dec_s[c, 0:SUBLANES, :] = jnp.broadcast_to(jnp.exp(tot_col), (SUBLANES, LANES))
            if cross:
                cfcb_s[c] = jnp.exp(jnp.where(lane < SSD_HEADS, inc_col, tot_col - exc_col))
            st["sf"] = jnp.exp(tot_col - inc_col) * dt_c
            st["sb"] = jnp.exp(exc_col) * dt_c
            dt_r = _softplus(st["dtr"] + dt_bias_col)
            lar = dt_r * nega_col
            inc_row = _cumsum(lar, 1)
            ldt = jnp.log(dt_r)
            rid = lax.broadcasted_iota(jnp.int32, (N_DT, C), 0)
            rowarg_s[c] = jnp.where(rid < SSD_HEADS, inc_row - ldt, inc_row - lar + ldt)

        def v_stage(hf):
            def run():
                pe = st["pe%d" % hf]
                cs = slice(hf * XBC_HALF, (hf + 1) * XBC_HALF)
                stage_s[0:HALO, cs] = jnp.where(c > 0, pe[0:HALO], 0.0)
                stage_s[HALO:HALO + C, cs] = pe[HALO:HALO + C]
                stage_s[HALO + C:, cs] = jnp.where(c < nc - 1, pe[HALO + C:], 0.0)
            return run

        def conv_block(cb):
            cs = slice(cb * CB, (cb + 1) * CB)
            rows = C + 2 * HALO
            xin = stage_s[:, cs]
            taps = [convw_ref[k:k + 1, cs] * xin for k in range(CONV_W)]
            up = lambda a: pltpu.roll(a, rows - 1, 0)
            down = lambda a: pltpu.roll(a, 1, 0)
            acc = taps[2] + up(taps[3] + up(taps[4])) + down(taps[1] + down(taps[0]))
            return _silu(acc[HALO:HALO + C] + convb_ref[0:1, cs])

        def v_conv_x(cb):
            def run():
                xs_s[c, :, cb * CB:(cb + 1) * CB] = conv_block(cb)
            return run

        def v_conv_b():
            bT_s[c] = conv_block(SSD_W // CB).T.astype(BF16)

        def v_conv_c():
            c_s[c] = conv_block(SSD_W // CB + 1).astype(BF16)

        def rope(a):
            if not use_rope:
                return a
            return a * cos_s[c] + pltpu.roll(a, RET_HD // 2, 1) * sin_s[c]

        def v_q():
            for hd in range(RET_HEADS):
                sl = slice(hd * RET_HD, (hd + 1) * RET_HD)
                q_s[c, :, sl] = rope(st["pq"][:, sl]).astype(BF16)

        def v_k():
            for hd in range(RET_HEADS):
                sl = slice(hd * RET_HD, (hd + 1) * RET_HD)
                kh = rope(st["pk"][:, sl]) * (RET_HD ** -0.5)
                kT_s[c, sl, :] = kh.T.astype(BF16)

        def v_v():
            v_s[c] = st["pv"].astype(BF16)

        def v_g():
            g_s[c] = _silu(st["pg"]).astype(BF16)

        def v_z():
            z_s[c] = _silu(st["pz"]).astype(BF16)

        def m_sloc(gi):
            def run():
                xg = xs_s[c, :, gi * GROUP_W:(gi + 1) * GROUP_W]
                vcat = jnp.concatenate([xg * expand_f(st["sf"], gi), xg * expand_b(st["sb"], gi)],
                                       axis=1).astype(BF16)
                sloc_s[c, gi] = _dot(bT_s[c, gi * SSD_STATE:(gi + 1) * SSD_STATE, :], vcat).astype(sloc_s.dtype)
            return run

        def m_rloc(hd):
            def run():
                sl = slice(hd * RET_HD, (hd + 1) * RET_HD)
                vf = st["pv"][:, sl]
                vcat = jnp.concatenate([vf * rvec_s[hd, RV_TAIL_F], vf * rvec_s[hd, RV_TAIL_B]],
                                       axis=1).astype(BF16)
                rloc_s[c, hd] = _dot(kT_s[c, sl, :], vcat).astype(rloc_s.dtype)
            return run

        m_pe0 = m_main("pe0", XBC_COL0, XBC_COL0 + XBC_HALF, halo=True)
        m_pe1 = m_main("pe1", XBC_COL0 + XBC_HALF, MAIN_COLS, halo=True)
        m_q = m_main("pq", 0, RET_W)
        m_k = m_main("pk", RET_W, 2 * RET_W)
        m_v = m_main("pv", 2 * RET_W, 3 * RET_W)
        m_g = m_main("pg", 3 * RET_W, 4 * RET_W)
        m_z = m_main("pz", 4 * RET_W, XBC_COL0)
        order = [v_mod, m_dt, m_pe0, m_pe1, v_dt, m_q, v_stage(0), v_stage(1), m_k, v_conv_x(0), m_v,
                 v_conv_x(1), v_q, m_g, v_conv_b, v_k, m_z, v_conv_c, v_v,
                 m_rloc(0), m_rloc(1), v_g, m_rloc(2), m_rloc(3), m_sloc(0), m_sloc(1), v_z]
        for piece in order:
            piece()

    def recurrences():
        for hd in range(RET_HEADS):
            lgf, lgb = ret_decays(hd)
            dec_f = jnp.exp(C * lgf)
            dec_b = jnp.exp(C * lgb)
            if has_state:
                ent_f = sret0_ref[0, 0, 0, hd]
                ent_b = sret0_ref[0, 0, 1, hd]
            else:
                ent_f = jnp.zeros((RET_HD, RET_HD), F32)
                ent_b = jnp.zeros((RET_HD, RET_HD), F32)
            for c in range(nc):
                if cross:
                    rent_s[c, hd, :, 0:RET_HD] = ent_f.astype(BF16)
                ent_f = dec_f * ent_f + rloc_s[c, hd, :, 0:RET_HD]
            for c in range(nc - 1, -1, -1):
                if cross:
                    rent_s[c, hd, :, RET_HD:] = ent_b.astype(BF16)
                ent_b = dec_b * ent_b + rloc_s[c, hd, :, RET_HD:]
            if emit_state:
                nret_ref[0, 0, 0, hd] = ent_f
                nret_ref[0, 0, 1, hd] = ent_b

        for gi in range(SSD_GROUPS):
            heads = group_heads(gi)
            if has_state:
                ent_f = jnp.concatenate([sssd0_ref[0, 0, 0, hh] for hh in heads], axis=0).T
                ent_b = jnp.concatenate([sssd0_ref[0, 0, 1, hh] for hh in heads], axis=0).T
            else:
                ent_f = jnp.zeros((SSD_STATE, GROUP_W), F32)
                ent_b = jnp.zeros((SSD_STATE, GROUP_W), F32)
            for c in range(nc):
                if cross:
                    sent_s[c, gi, :, 0:GROUP_W] = ent_f.astype(BF16)
                ent_f = expand_f(dec_s[c, 0:1, :], gi) * ent_f + sloc_s[c, gi, :, 0:GROUP_W]
            for c in range(nc - 1, -1, -1):
                if cross:
                    sent_s[c, gi, :, GROUP_W:] = ent_b.astype(BF16)
                ent_b = expand_b(dec_s[c, 0:1, :], gi) * ent_b + sloc_s[c, gi, :, GROUP_W:]
            if emit_state:
                ent_ft = ent_f.T
                ent_bt = ent_b.T
                for k, hh in enumerate(heads):
                    nssd_ref[0, 0, 0, hh] = ent_ft[k * SSD_HD:(k + 1) * SSD_HD, :]
                    nssd_ref[0, 0, 1, hh] = ent_bt[k * SSD_HD:(k + 1) * SSD_HD, :]

    def emit_pieces(c, slot):
        rsl = [slice(hd * RET_HD, (hd + 1) * RET_HD) for hd in range(RET_HEADS)]
        gsl = [slice(gi * SSD_STATE, (gi + 1) * SSD_STATE) for gi in range(SSD_GROUPS)]
        xsl = [slice(gi * GROUP_W, (gi + 1) * GROUP_W) for gi in range(SSD_GROUPS)]
        st = {}

        def scores():
            ii = lax.broadcasted_iota(jnp.int32, (C, C), 0)
            jj = lax.broadcasted_iota(jnp.int32, (C, C), 1)
            st["causal"] = jj <= ii
            qs = [q_s[c, :, sl] for sl in rsl]
            cms = [c_s[c, :, sl] for sl in gsl]
            st["sc_s"] = [_dot(cms[gi], bT_s[c, gsl[gi], :]) for gi in range(SSD_GROUPS)]
            st["sc_r"] = [_dot(qs[hd], kT_s[c, rsl[hd], :]) for hd in range(RET_HEADS)]
            if cross:
                st["yc_s"] = [_dot(cms[gi], sent_s[c, gi]) for gi in range(SSD_GROUPS)]
                st["yc_r"] = [_dot(qs[hd], rent_s[c, hd]) for hd in range(RET_HEADS)]
            st["inc_col"] = inccol_s[c]
            st["exc_col"] = exccol_s[c]
            st["rowarg"] = rowarg_s[c]

        def ssd_head(gi, k):
            def run():
                hh = gi * HPG + k
                hb = SSD_HEADS + hh
                arg = jnp.where(st["causal"],
                                st["inc_col"][:, hh:hh + 1] - st["rowarg"][hh:hh + 1, :],
                                st["rowarg"][hb:hb + 1, :] - st["exc_col"][:, hb:hb + 1])
                m = (st["sc_s"][gi] * jnp.exp(arg)).astype(BF16)
                xh = xs_s[c, :, hh * SSD_HD:(hh + 1) * SSD_HD].astype(BF16)
                y_s[:, hh * SSD_HD:(hh + 1) * SSD_HD] = _dot(m, xh)
            return run

        def ssd_group(gi):
            def run():
                xg = xs_s[c, :, xsl[gi]]
                yg = y_s[:, xsl[gi]] + expand_f(dskip_row, gi) * xg
                if cross:
                    cfcb = cfcb_s[c]
                    yc = st["yc_s"][gi]
                    yg = yg + expand_f(cfcb, gi) * yc[:, 0:GROUP_W] + expand_b(cfcb, gi) * yc[:, GROUP_W:]
                y_s[:, xsl[gi]] = yg
            return run

        def ret_head(hd):
            def run():
                m = (st["sc_r"][hd] * wdec_s[hd]).astype(BF16)
                o = _dot(m, v_s[c, :, rsl[hd]])
                if cross:
                    yc = st["yc_r"][hd]
                    o = o + rvec_s[hd, RV_CROSS_F] * yc[:, 0:RET_HD] + rvec_s[hd, RV_CROSS_B] * yc[:, RET_HD:]
                o = o * lax.rsqrt(jnp.mean(o * o, axis=-1, keepdims=True) + EPS)
                mix_s[:, rsl[hd]] = (g_s[c, :, rsl[hd]].astype(F32) * o).astype(BF16)
            return run

        def ssd_norm():
            yz = y_s[...] * z_s[c].astype(F32)
            yn = yz * lax.rsqrt(jnp.mean(yz * yz, axis=-1, keepdims=True) + EPS) * normw_ref[...]
            mix_s[:, RET_W:] = yn.astype(BF16)

        def out_proj():
            g1 = mod_row(row_mix, 2)
            y = ALPHA * x_ref[0] + g1 * _dot(mix_s[...], wout_ref[...])
            x1_s[slot] = _layer_norm(y, ln1g_ref[...], ln1b_ref[...])

        pieces = [scores]
        for gi in range(SSD_GROUPS):
            pieces += [ssd_head(gi, k) for k in range(HPG)] + [ssd_group(gi)]
        pieces += [ret_head(hd) for hd in range(RET_HEADS)] + [ssd_norm, out_proj]
        return pieces

    def ffn_pieces(slot):
        st = {}

        def start():
            sh2 = mod_row(row_ffn, 3)
            sc2 = mod_row(row_ffn, 4)
            st["h2"] = (x1_s[slot] * (1.0 + sc2) + sh2).astype(BF16)

        def hidden(j):
            def run():
                js = slice(j * FF_BLK, (j + 1) * FF_BLK)
                h2 = st["h2"]
                hid_s[:, js] = (_silu(_dot(h2, wg_ref[:, js])) * _dot(h2, wu_ref[:, js])).astype(BF16)
            return run

        def finish():
            g2 = mod_row(row_ffn, 5)
            y = ALPHA * x1_s[slot] + g2 * _dot(hid_s[...], wd_ref[...])
            out_ref[0] = _layer_norm(y, ln2g_ref[...], ln2b_ref[...])

        return [start] + [hidden(j) for j in range(D_FF // FF_BLK)] + [finish]

    def run_all(pieces):
        for p in pieces:
            p()

    if nc == 1:
        slot = seq & 1

        @pl.when(seq == 0)
        def _():
            project(0)
            for i in range(1, len(big_weights)):
                weight_copy(i).wait()
            run_all([recurrences] + emit_pieces(0, slot))

        @pl.when((seq > 0) & (seq < nb))
        def _():
            project(0)
            _interleave(ffn_pieces(1 - slot), [recurrences] + emit_pieces(0, slot))

        @pl.when(seq == nb)
        def _():
            run_all(ffn_pieces((nb - 1) & 1))
        return

    @pl.when((step < nc) & (seq < nb))
    def _():
        project(step)

    @pl.when((step == nc) & (seq < nb))
    def _():
        recurrences()

    kk = step - nc
    slot = (seq * nc + kk) & 1
    first = (seq == 0) & (step == nc)

    @pl.when(first)
    def _():
        for i in range(1, len(big_weights)):
            weight_copy(i).wait()
        run_all(emit_pieces(kk, slot))

    @pl.when((step >= nc) & (seq < nb) & jnp.logical_not(first))
    def _():
        _interleave(ffn_pieces(1 - slot), emit_pieces(kk, slot))

    @pl.when((seq == nb) & (step == nc))
    def _():
        run_all(ffn_pieces((nb * nc - 1) & 1))


def _const_spec(shape):
    nd = len(shape)
    return pl.BlockSpec(shape, lambda b, s: (0,) * nd, pipeline_mode=pl.Buffered(1))


def _layer_call(x, mod, mod_block, mod_per_seq, weights, states, emit_state, use_rope, name):
    nb, L, _ = x.shape
    nc = L // CHUNK
    hpc_blocks = CHUNK // HALO
    has_state = states is not None
    last = nb - 1

    def chunk_of(b, s):
        return jnp.where(b > last, nc - 1, jnp.where(s < nc, s, s - nc))

    def halo_chunk(b, s):
        return jnp.where(b > last, nc - 1, jnp.minimum(s, nc - 1))

    def seq_of(b):
        return jnp.minimum(b, last)

    def out_map(b, s):
        live = (s > nc) & (b < nb)
        idle_chunk = jnp.where(b == 0, 0, nc - 1)
        return (jnp.where(live, b, jnp.maximum(b - 1, 0)), jnp.where(live, s - nc - 1, idle_chunk), 0)

    assert not mod_per_seq or nb <= MOD_ROWS
    in_specs = [
        pl.BlockSpec((1, CHUNK, D_MODEL), lambda b, s: (seq_of(b), chunk_of(b, s), 0)),
        pl.BlockSpec((1, HALO, D_MODEL),
                     lambda b, s: (seq_of(b), jnp.maximum(halo_chunk(b, s) * hpc_blocks - 1, 0), 0)),
        pl.BlockSpec((1, HALO, D_MODEL),
                     lambda b, s: (seq_of(b), jnp.minimum((halo_chunk(b, s) + 1) * hpc_blocks, nc * hpc_blocks - 1), 0)),
        pl.BlockSpec((MOD_ROWS, 6 * D_MODEL), lambda b, s: (mod_block, 0)),
    ] + [pl.BlockSpec(memory_space=pl.ANY) if i in BIG_WEIGHTS else _const_spec(w.shape)
         for i, w in enumerate(weights)]
    args = [x, x, x, mod] + list(weights)
    ret_block = (1, 1, 2, RET_HEADS, RET_HD, RET_HD)
    ssd_block = (1, 1, 2, SSD_HEADS, SSD_HD, SSD_STATE)
    state_map = lambda b, s: (seq_of(b), 0, 0, 0, 0, 0)
    if has_state:
        in_specs += [pl.BlockSpec(ret_block, state_map, pipeline_mode=pl.Buffered(1)),
                     pl.BlockSpec(ssd_block, state_map, pipeline_mode=pl.Buffered(1))]
        args += list(states)
    out_shape = [jax.ShapeDtypeStruct((nb, L, D_MODEL), F32)]
    out_specs = [pl.BlockSpec((1, CHUNK, D_MODEL), out_map)]
    if emit_state:
        out_shape += [jax.ShapeDtypeStruct((nb,) + ret_block[1:], F32),
                      jax.ShapeDtypeStruct((nb,) + ssd_block[1:], F32)]
        out_specs += [pl.BlockSpec(ret_block, state_map), pl.BlockSpec(ssd_block, state_map)]
    state_dt = F32 if emit_state else BF16
    scratch = [
        pltpu.VMEM((nc, CHUNK, RET_W), BF16),
        pltpu.VMEM((nc, RET_W, CHUNK), BF16),
        pltpu.VMEM((nc, CHUNK, RET_W), BF16),
        pltpu.VMEM((nc, CHUNK, RET_W), BF16),
        pltpu.VMEM((nc, CHUNK, SSD_W), BF16),
        pltpu.VMEM((CHUNK + 2 * HALO, CONV_CH), F32),
        pltpu.VMEM((nc, CHUNK, SSD_W), F32),
        pltpu.VMEM((nc, SSD_GROUPS * SSD_STATE, CHUNK), BF16),
        pltpu.VMEM((nc, CHUNK, SSD_GROUPS * SSD_STATE), BF16),
        pltpu.VMEM((nc, CHUNK, LANES), F32),
        pltpu.VMEM((nc, CHUNK, LANES), F32),
        pltpu.VMEM((nc, CHUNK, LANES), F32),
        pltpu.VMEM((nc, 4 * SUBLANES, LANES), F32),
        pltpu.VMEM((nc, N_DT, CHUNK), F32),
        pltpu.VMEM((CHUNK, SSD_W), F32),
        pltpu.VMEM((CHUNK, 2 * RET_W), BF16),
        pltpu.VMEM((nc, RET_HEADS, RET_HD, 2 * RET_HD), state_dt),
        pltpu.VMEM((nc, SSD_GROUPS, SSD_STATE, 2 * GROUP_W), state_dt),
        pltpu.VMEM((nc, RET_HEADS, RET_HD, 2 * RET_HD), BF16),
        pltpu.VMEM((nc, SSD_GROUPS, SSD_STATE, 2 * GROUP_W), BF16),
        pltpu.VMEM((RET_HEADS, CHUNK, CHUNK), F32),
        pltpu.VMEM((RET_HEADS, 4, CHUNK, LANES), F32),
        pltpu.VMEM((2, CHUNK, D_MODEL), F32),
        pltpu.VMEM((CHUNK, D_FF), BF16),
        pltpu.VMEM((LANES, D_MODEL), BF16),
    ]
    scratch += [pltpu.VMEM(weights[i].shape, BF16) for i in BIG_WEIGHTS]
    scratch += [pltpu.SemaphoreType.DMA((len(BIG_WEIGHTS),))]
    if use_rope:
        scratch += [pltpu.VMEM((nc, CHUNK, LANES), F32)] * 2
    kern = functools.partial(_layer_kernel, L=L, nb=nb, mod_per_seq=mod_per_seq, has_state=has_state,
                             use_rope=use_rope, emit_state=emit_state)
    return pl.pallas_call(
        kern,
        grid=(nb + 1, 1 if nc == 1 else 2 * nc),
        in_specs=in_specs,
        out_specs=out_specs,
        out_shape=out_shape,
        scratch_shapes=scratch,
        compiler_params=pltpu.CompilerParams(dimension_semantics=("arbitrary", "arbitrary"),
                                             vmem_limit_bytes=VMEM_LIMIT),
        name=name,
    )(*args)


def kernel(x_prompt, x_sample, state_ret, state_ssd, c, c_ctx, w_in, ret_decay_fwd, ret_decay_bwd, conv_w, conv_b, dt_bias_fwd, dt_bias_bwd, a_log_fwd, a_log_bwd, d_skip, ssd_norm_w, w_out, ln1_g, ln1_b, w_gate, w_up, w_down, ln2_g, ln2_b, w_ada, b_ada):
    depth = w_in.shape[0]
    assert depth == 1, "single trunk layer"
    bp, lp, _ = x_prompt.shape
    bs, ls, _ = x_sample.shape
    assert lp % CHUNK == 0 and ls % CHUNK == 0 and ls % GRID_W == 0 and D_FF % FF_BLK == 0


    convw = jnp.zeros((SUBLANES, CONV_CH), F32).at[:CONV_W].set(conv_w[0])
    convb = conv_b[0][None, :]
    dt_bias = jnp.concatenate([dt_bias_fwd[0], dt_bias_bwd[0]])
    a_log = jnp.concatenate([a_log_fwd[0], a_log_bwd[0]])
    ret_decay = jnp.concatenate([ret_decay_fwd[0], ret_decay_bwd[0]])
    hp = jnp.zeros((SUBLANES, LANES), F32)
    hp = hp.at[0, :N_DT].set(dt_bias).at[1, :N_DT].set(a_log)
    hp = hp.at[2, :2 * RET_HEADS].set(ret_decay).at[3, :SSD_HEADS].set(d_skip[0])
    hpc = jnp.zeros((N_DT, LANES), F32).at[:, 0].set(dt_bias).at[:, 1].set(a_log)
    mod, w_in_b, w_out_b, w_gate_b, w_up_b, w_down_b = _prep_call(
        c, c_ctx[None, :], w_ada[0], b_ada[0][None, :],
        jnp.swapaxes(w_in[0], 0, 1), [w_out[0], w_gate[0], w_up[0], w_down[0]])
    weights = (w_in_b, convw, convb, hp, hpc, ssd_norm_w[0][None, :],
               w_out_b, ln1_g[0][None, :], ln1_b[0][None, :],
               w_gate_b, w_up_b, w_down_b, ln2_g[0][None, :], ln2_b[0][None, :])

    yp, new_ret, new_ssd = _layer_call(x_prompt, mod, 1, False, weights, None, True, False, "layer_ctx")
    states = (state_ret, jnp.swapaxes(state_ssd, -1, -2))
    (ys,) = _layer_call(x_sample, mod, 0, True, weights, states, False, True, "layer_lat")
    return (yp, ys, new_ret, jnp.swapaxes(new_ssd, -1, -2))
```

```python
import functools
import math

import jax
import jax.numpy as jnp
from jax import lax
from jax.experimental import pallas as pl
from jax.experimental.pallas import tpu as pltpu

F32 = jnp.float32
BF16 = jnp.bfloat16

D_MODEL = 1024
RET_W = 512
RET_HEADS = 4
RET_HD = 128
SSD_W = 512
SSD_HD = 64
SSD_HEADS = 8
SSD_GROUPS = 2
SSD_STATE = 128
HPG = SSD_HEADS // SSD_GROUPS
GROUP_W = HPG * SSD_HD
CONV_W = 5
CONV_CH = SSD_W + 2 * SSD_GROUPS * SSD_STATE
D_FF = 2816
GRID_W = 64
GRID_SHIFT = 6
ROPE_BASE = 10000.0
EPS = 1e-6
ALPHA = 2.0 ** 0.25
MAIN_COLS = 4 * RET_W + SSD_W + CONV_CH
XBC_COL0 = 4 * RET_W + SSD_W
N_DT = 2 * SSD_HEADS

CHUNK = 256
HALO = 8
FF_BLK = 256
MOD_ROWS = 8
CAST_STEPS = 8
CAST_WT_ROWS = 512
LANES = 128
SUBLANES = 8
XBC_HALF = CONV_CH // 2
VMEM_LIMIT = 62 * 1024 * 1024


def _dot(a, b):
    return jnp.dot(a, b, preferred_element_type=F32)


def _dot_nt(a, b):
    return lax.dot_general(a, b, (((1,), (1,)), ((), ())), preferred_element_type=F32)


def _silu(x):
    return x * jax.nn.sigmoid(x)


def _softplus(x):
    return jnp.maximum(x, 0.0) + jnp.log1p(jnp.exp(-jnp.abs(x)))


def _layer_norm(y, g, b):
    mu = jnp.mean(y, axis=-1, keepdims=True)
    yc = y - mu
    var = jnp.mean(yc * yc, axis=-1, keepdims=True)
    return yc * lax.rsqrt(var + EPS) * g + b


def _cumsum(x, axis):
    n = x.shape[axis]
    idx = lax.broadcasted_iota(jnp.int32, x.shape, axis)
    s = 1
    while s < n:
        x = x + jnp.where(idx >= s, pltpu.roll(x, s, axis), 0.0)
        s *= 2
    return x


def _expand4(cols, lane):
    a = jnp.where(lane < SSD_HD, cols[0], cols[1])
    b = jnp.where(lane < SSD_HD, cols[2], cols[3])
    return jnp.concatenate([a, b], axis=1)


def _interleave(a, b):
    ia = ib = 0
    while ia < len(a) or ib < len(b):
        if ib >= len(b) or (ia < len(a) and ia * len(b) <= ib * len(a)):
            a[ia]()
            ia += 1
        else:
            b[ib]()
            ib += 1


def _prep_kernel(c_ref, cctx_ref, wada_ref, bada_ref, wt_ref, *refs, wt_rows):
    n = (len(refs) - 2) // 2
    srcs, mod_ref, wt_out, dsts = refs[:n], refs[n], refs[n + 1], refs[n + 2:]
    cond = jnp.concatenate([c_ref[...], jnp.broadcast_to(cctx_ref[...], (MOD_ROWS, D_MODEL))], axis=0)
    mod_ref[...] = _dot(_silu(cond).astype(BF16), wada_ref[...].astype(BF16)) + bada_ref[...]
    blk = wt_ref.shape[0]
    row = lax.broadcasted_iota(jnp.int32, (blk, 1), 0) + pl.program_id(0) * blk
    wt_out[...] = jnp.where(row < wt_rows, wt_ref[...], 0.0).T.astype(BF16)
    for src, dst in zip(srcs, dsts):
        dst[...] = src[...].astype(BF16)


def _prep_call(c, c_ctx, w_ada, b_ada, wt, ws):
    assert c.shape == (MOD_ROWS, D_MODEL), c.shape
    for w in ws:
        assert w.shape[0] % (CAST_STEPS * 16) == 0, w.shape
    assert wt.shape[0] <= CAST_STEPS * CAST_WT_ROWS
    n_mod = w_ada.shape[1]
    ada_cols = n_mod // CAST_STEPS
    assert ada_cols % LANES == 0
    specs = [pl.BlockSpec((w.shape[0] // CAST_STEPS, w.shape[1]), lambda i: (i, 0)) for w in ws]
    return pl.pallas_call(
        functools.partial(_prep_kernel, wt_rows=wt.shape[0]),
        grid=(CAST_STEPS,),
        in_specs=[pl.BlockSpec((MOD_ROWS, D_MODEL), lambda i: (0, 0)),
                  pl.BlockSpec((1, D_MODEL), lambda i: (0, 0)),
                  pl.BlockSpec((D_MODEL, ada_cols), lambda i: (0, i)),
                  pl.BlockSpec((1, ada_cols), lambda i: (0, i)),
                  pl.BlockSpec((CAST_WT_ROWS, wt.shape[1]), lambda i: (i, 0))] + specs,
        out_specs=[pl.BlockSpec((2 * MOD_ROWS, ada_cols), lambda i: (0, i)),
                   pl.BlockSpec((wt.shape[1], CAST_WT_ROWS), lambda i: (0, i))] + specs,
        out_shape=[jax.ShapeDtypeStruct((2 * MOD_ROWS, n_mod), F32),
                   jax.ShapeDtypeStruct((wt.shape[1], -(-wt.shape[0] // LANES) * LANES), BF16)]
        + [jax.ShapeDtypeStruct(w.shape, BF16) for w in ws],
        compiler_params=pltpu.CompilerParams(dimension_semantics=("arbitrary",),
                                             vmem_limit_bytes=VMEM_LIMIT),
        name="prep_mod_and_weights",
    )(c, c_ctx, w_ada, b_ada, wt, *ws)


RV_TAIL_F, RV_TAIL_B, RV_CROSS_F, RV_CROSS_B = range(4)
BIG_WEIGHTS = (0, 6, 9, 10, 11)


def _layer_kernel(*refs, L, nb, mod_per_seq, has_state, use_rope, emit_state):
    nc = L // CHUNK
    C = CHUNK
    cross = has_state or nc > 1
    it = iter(refs)
    x_ref, xprev_ref, xnext_ref, mod_ref, wmain_hbm = (next(it) for _ in range(5))
    convw_ref, convb_ref, hp_ref, hpc_ref, normw_ref = (next(it) for _ in range(5))
    wout_hbm, ln1g_ref, ln1b_ref = (next(it) for _ in range(3))
    wg_hbm, wu_hbm, wd_hbm, ln2g_ref, ln2b_ref = (next(it) for _ in range(5))
    if has_state:
        sret0_ref, sssd0_ref = next(it), next(it)
    out_ref = next(it)
    if emit_state:
        nret_ref, nssd_ref = next(it), next(it)
    (q_s, kT_s, v_s, g_s, z_s, stage_s, xs_s, bT_s, c_s, inccol_s, exccol_s, cfcb_s, dec_s,
     rowarg_s, y_s, mix_s, rloc_s, sloc_s, rent_s, sent_s, wdec_s, rvec_s, x1_s, hid_s, wdtr_s) = (
         next(it) for _ in range(25))
    wmain_ref, wout_ref, wg_ref, wu_ref, wd_ref, wsem = (next(it) for _ in range(6))
    if use_rope:
        cos_s, sin_s = next(it), next(it)

    seq = pl.program_id(0)
    step = pl.program_id(1)

    if mod_per_seq:
        row_mix = jnp.minimum(seq, nb - 1)
        row_ffn = jnp.minimum(jnp.where(step > nc, seq, jnp.maximum(seq - 1, 0)), nb - 1)
    else:
        row_mix = row_ffn = 0

    def mod_row(row, k):
        return mod_ref[pl.ds(row, 1), k * D_MODEL:(k + 1) * D_MODEL]

    big_weights = ((wmain_hbm, wmain_ref), (wout_hbm, wout_ref), (wg_hbm, wg_ref), (wu_hbm, wu_ref),
                   (wd_hbm, wd_ref))

    def weight_copy(i):
        return pltpu.make_async_copy(big_weights[i][0], big_weights[i][1], wsem.at[i])

    hp = hp_ref[...]
    dt_bias_row = hp[0:1, :]
    nega_row = -jnp.exp(hp[1:2, :])
    lg_row = -_softplus(-hp[2:3, :])
    dskip_row = hp[3:4, :]
    hpc = hpc_ref[...]
    dt_bias_col = hpc[:, 0:1]
    nega_col = -jnp.exp(hpc[:, 1:2])

    lane = lax.broadcasted_iota(jnp.int32, (1, LANES), 1)

    def ret_decays(hd):
        lgf = lg_row[:, hd:hd + 1]
        lgb = lg_row[:, RET_HEADS + hd:RET_HEADS + hd + 1]
        return lgf, lgb

    def group_heads(gi):
        return [gi * HPG + k for k in range(HPG)]

    def expand_f(arr, gi):
        return _expand4([arr[:, hh:hh + 1] for hh in group_heads(gi)], lane)

    def expand_b(arr, gi):
        return _expand4([arr[:, SSD_HEADS + hh:SSD_HEADS + hh + 1] for hh in group_heads(gi)], lane)

    @pl.when((seq == 0) & (step == 0))
    def _():
        for i in range(len(big_weights)):
            weight_copy(i).start()
        weight_copy(0).wait()
        ii = lax.broadcasted_iota(jnp.int32, (C, C), 0)
        jj = lax.broadcasted_iota(jnp.int32, (C, C), 1)
        dmat = (ii - jj).astype(F32)
        irow = lax.broadcasted_iota(jnp.int32, (C, LANES), 0).astype(F32)
        for hd in range(RET_HEADS):
            lgf, lgb = ret_decays(hd)
            wdec_s[hd] = jnp.exp(jnp.where(jj <= ii, dmat * lgf, -dmat * lgb))
            rvec_s[hd, RV_TAIL_F] = jnp.exp((C - 1.0 - irow) * lgf)
            rvec_s[hd, RV_TAIL_B] = jnp.exp(irow * lgb)
            rvec_s[hd, RV_CROSS_F] = jnp.exp((irow + 1.0) * lgf)
            rvec_s[hd, RV_CROSS_B] = jnp.exp((C - irow) * lgb)
        wdtr_s[...] = wmain_ref[:, MAIN_COLS:MAIN_COLS + LANES].astype(F32).T.astype(BF16)
        if use_rope:
            ln = lax.broadcasted_iota(jnp.int32, (C, LANES), 1)
            nf = RET_HD // 4
            inv = jnp.exp((ln & (nf - 1)).astype(F32) * (-math.log(ROPE_BASE) / nf))
            for cc in range(nc):
                t = lax.broadcasted_iota(jnp.int32, (C, LANES), 0) + cc * C
                pos = jnp.where((ln & (2 * nf - 1)) < nf, t >> GRID_SHIFT, t & (GRID_W - 1)).astype(F32)
                ang = pos * inv
                cos_s[cc] = jnp.cos(ang)
                sin_s[cc] = jnp.where(ln < RET_HD // 2, -jnp.sin(ang), jnp.sin(ang))

    def project(c):
        st = {}
        CB = 256
        assert CONV_W == 5 and HALO >= CONV_W // 2

        def v_mod():
            sh1 = mod_row(row_mix, 0)
            sc1 = mod_row(row_mix, 1)
            xe = jnp.concatenate([xprev_ref[0], x_ref[0], xnext_ref[0]], axis=0)
            xm = xe * (1.0 + sc1) + sh1
            st["he"] = xm.astype(BF16)
            st["h"] = xm[HALO:HALO + C].astype(BF16)

        def m_main(name, lo, hi, halo=False):
            def run():
                st[name] = _dot(st["he" if halo else "h"], wmain_ref[:, lo:hi])
            return run

        def m_dt():
            raw_c = _dot(st["h"], wmain_ref[:, MAIN_COLS:MAIN_COLS + LANES])
            st["dtc"] = raw_c + pltpu.roll(raw_c, SSD_HEADS, 1)
            raw_r = _dot_nt(wdtr_s[0:N_DT, :], st["h"])
            st["dtr"] = raw_r + pltpu.roll(raw_r, SSD_HEADS, 0)

        def v_dt():
            dt_c = _softplus(st["dtc"] + dt_bias_row)
            lac = dt_c * nega_row
            inc_col = _cumsum(lac, 0)
            exc_col = inc_col - lac
            tot_col = inc_col[C - 1:C, :]
            inccol_s[c] = inc_col
            exccol_s[c] = exc_col
            dec_s[c, 0:SUBLANES, :] = jnp.broadcast_to(jnp.exp(tot_col), (SUBLANES, LANES))
            if cross:
                cfcb_s[c] = jnp.exp(jnp.where(lane < SSD_HEADS, inc_col, tot_col - exc_col))
            st["sf"] = jnp.exp(tot_col - inc_col) * dt_c
            st["sb"] = jnp.exp(exc_col) * dt_c
            dt_r = _softplus(st["dtr"] + dt_bias_col)
            lar = dt_r * nega_col
            inc_row = _cumsum(lar, 1)
            ldt = jnp.log(dt_r)
            rid = lax.broadcasted_iota(jnp.int32, (N_DT, C), 0)
            rowarg_s[c] = jnp.where(rid < SSD_HEADS, inc_row - ldt, inc_row - lar + ldt)

        def v_stage(hf):
            def run():
                pe = st["pe%d" % hf]
                cs = slice(hf * XBC_HALF, (hf + 1) * XBC_HALF)
                stage_s[0:HALO, cs] = jnp.where(c > 0, pe[0:HALO], 0.0)
                stage_s[HALO:HALO + C, cs] = pe[HALO:HALO + C]
                stage_s[HALO + C:, cs] = jnp.where(c < nc - 1, pe[HALO + C:], 0.0)
            return run

        def conv_block(cb):
            cs = slice(cb * CB, (cb + 1) * CB)
            rows = C + 2 * HALO
            xin = stage_s[:, cs]
            taps = [convw_ref[k:k + 1, cs] * xin for k in range(CONV_W)]
            up = lambda a: pltpu.roll(a, rows - 1, 0)
            down = lambda a: pltpu.roll(a, 1, 0)
            acc = taps[2] + up(taps[3] + up(taps[4])) + down(taps[1] + down(taps[0]))
            return _silu(acc[HALO:HALO + C] + convb_ref[0:1, cs])

        def v_conv_x(cb):
            def run():
                xs_s[c, :, cb * CB:(cb + 1) * CB] = conv_block(cb)
            return run

        def v_conv_b():
            bT_s[c] = conv_block(SSD_W // CB).T.astype(BF16)

        def v_conv_c():
            c_s[c] = conv_block(SSD_W // CB + 1).astype(BF16)

        def rope(a):
            if not use_rope:
                return a
            return a * cos_s[c] + pltpu.roll(a, RET_HD // 2, 1) * sin_s[c]

        def v_q():
            for hd in range(RET_HEADS):
                sl = slice(hd * RET_HD, (hd + 1) * RET_HD)
                q_s[c, :, sl] = rope(st["pq"][:, sl]).astype(BF16)

        def v_k():
            for hd in range(RET_HEADS):
                sl = slice(hd * RET_HD, (hd + 1) * RET_HD)
                kh = rope(st["pk"][:, sl]) * (RET_HD ** -0.5)
                kT_s[c, sl, :] = kh.T.astype(BF16)

        def v_v():
            v_s[c] = st["pv"].astype(BF16)

        def v_g():
            g_s[c] = _silu(st["pg"]).astype(BF16)

        def v_z():
            z_s[c] = _silu(st["pz"]).astype(BF16)

        def m_sloc(gi):
            def run():
                xg = xs_s[c, :, gi * GROUP_W:(gi + 1) * GROUP_W]
                vcat = jnp.concatenate([xg * expand_f(st["sf"], gi), xg * expand_b(st["sb"], gi)],
                                       axis=1).astype(BF16)
                sloc_s[c, gi] = _dot(bT_s[c, gi * SSD_STATE:(gi + 1) * SSD_STATE, :], vcat).astype(sloc_s.dtype)
            return run

        def m_rloc(hd):
            def run():
                sl = slice(hd * RET_HD, (hd + 1) * RET_HD)
                vf = st["pv"][:, sl]
                vcat = jnp.concatenate([vf * rvec_s[hd, RV_TAIL_F], vf * rvec_s[hd, RV_TAIL_B]],
                                       axis=1).astype(BF16)
                rloc_s[c, hd] = _dot(kT_s[c, sl, :], vcat).astype(rloc_s.dtype)
            return run

        m_pe0 = m_main("pe0", XBC_COL0, XBC_COL0 + XBC_HALF, halo=True)
        m_pe1 = m_main("pe1", XBC_COL0 + XBC_HALF, MAIN_COLS, halo=True)
        m_q = m_main("pq", 0, RET_W)
        m_k = m_main("pk", RET_W, 2 * RET_W)
        m_v = m_main("pv", 2 * RET_W, 3 * RET_W)
        m_g = m_main("pg", 3 * RET_W, 4 * RET_W)
        m_z = m_main("pz", 4 * RET_W, XBC_COL0)
        order = [v_mod, m_dt, m_pe0, m_pe1, v_dt, m_q, v_stage(0), v_stage(1), m_k, v_conv_x(0), m_v,
                 v_conv_x(1), v_q, m_g, v_conv_b, v_k, m_z, v_conv_c, v_v,
                 m_rloc(0), m_rloc(1), v_g, m_rloc(2), m_rloc(3), m_sloc(0), m_sloc(1), v_z]
        for piece in order:
            piece()

    def recurrences():
        for hd in range(RET_HEADS):
            lgf, lgb = ret_decays(hd)
            dec_f = jnp.exp(C * lgf)
            dec_b = jnp.exp(C * lgb)
            if has_state:
                ent_f = sret0_ref[0, 0, 0, hd]
                ent_b = sret0_ref[0, 0, 1, hd]
            else:
                ent_f = jnp.zeros((RET_HD, RET_HD), F32)
                ent_b = jnp.zeros((RET_HD, RET_HD), F32)
            for c in range(nc):
                if cross:
                    rent_s[c, hd, :, 0:RET_HD] = ent_f.astype(BF16)
                ent_f = dec_f * ent_f + rloc_s[c, hd, :, 0:RET_HD]
            for c in range(nc - 1, -1, -1):
                if cross:
                    rent_s[c, hd, :, RET_HD:] = ent_b.astype(BF16)
                ent_b = dec_b * ent_b + rloc_s[c, hd, :, RET_HD:]
            if emit_state:
                nret_ref[0, 0, 0, hd] = ent_f
                nret_ref[0, 0, 1, hd] = ent_b

        for gi in range(SSD_GROUPS):
            heads = group_heads(gi)
            if has_state:
                ent_f = jnp.concatenate([sssd0_ref[0, 0, 0, hh] for hh in heads], axis=0).T
                ent_b = jnp.concatenate([sssd0_ref[0, 0, 1, hh] for hh in heads], axis=0).T
            else:
                ent_f = jnp.zeros((SSD_STATE, GROUP_W), F32)
                ent_b = jnp.zeros((SSD_STATE, GROUP_W), F32)
            for c in range(nc):
                if cross:
                    sent_s[c, gi, :, 0:GROUP_W] = ent_f.astype(BF16)
                ent_f = expand_f(dec_s[c, 0:1, :], gi) * ent_f + sloc_s[c, gi, :, 0:GROUP_W]
            for c in range(nc - 1, -1, -1):
                if cross:
                    sent_s[c, gi, :, GROUP_W:] = ent_b.astype(BF16)
                ent_b = expand_b(dec_s[c, 0:1, :], gi) * ent_b + sloc_s[c, gi, :, GROUP_W:]
            if emit_state:
                ent_ft = ent_f.T
                ent_bt = ent_b.T
                for k, hh in enumerate(heads):
                    nssd_ref[0, 0, 0, hh] = ent_ft[k * SSD_HD:(k + 1) * SSD_HD, :]
                    nssd_ref[0, 0, 1, hh] = ent_bt[k * SSD_HD:(k + 1) * SSD_HD, :]

    def emit_pieces(c, slot):
        rsl = [slice(hd * RET_HD, (hd + 1) * RET_HD) for hd in range(RET_HEADS)]
        gsl = [slice(gi * SSD_STATE, (gi + 1) * SSD_STATE) for gi in range(SSD_GROUPS)]
        xsl = [slice(gi * GROUP_W, (gi + 1) * GROUP_W) for gi in range(SSD_GROUPS)]
        st = {}

        def scores():
            ii = lax.broadcasted_iota(jnp.int32, (C, C), 0)
            jj = lax.broadcasted_iota(jnp.int32, (C, C), 1)
            st["causal"] = jj <= ii
            qs = [q_s[c, :, sl] for sl in rsl]
            cms = [c_s[c, :, sl] for sl in gsl]
            st["sc_s"] = [_dot(cms[gi], bT_s[c, gsl[gi], :]) for gi in range(SSD_GROUPS)]
            st["sc_r"] = [_dot(qs[hd], kT_s[c, rsl[hd], :]) for hd in range(RET_HEADS)]
            if cross:
                st["yc_s"] = [_dot(cms[gi], sent_s[c, gi]) for gi in range(SSD_GROUPS)]
                st["yc_r"] = [_dot(qs[hd], rent_s[c, hd]) for hd in range(RET_HEADS)]
            st["inc_col"] = inccol_s[c]
            st["exc_col"] = exccol_s[c]
            st["rowarg"] = rowarg_s[c]

        def ssd_head(gi, k):
            def run():
                hh = gi * HPG + k
                hb = SSD_HEADS + hh
                arg = jnp.where(st["causal"],
                                st["inc_col"][:, hh:hh + 1] - st["rowarg"][hh:hh + 1, :],
                                st["rowarg"][hb:hb + 1, :] - st["exc_col"][:, hb:hb + 1])
                m = (st["sc_s"][gi] * jnp.exp(arg)).astype(BF16)
                xh = xs_s[c, :, hh * SSD_HD:(hh + 1) * SSD_HD].astype(BF16)
                y_s[:, hh * SSD_HD:(hh + 1) * SSD_HD] = _dot(m, xh)
            return run

        def ssd_group(gi):
            def run():
                xg = xs_s[c, :, xsl[gi]]
                yg = y_s[:, xsl[gi]] + expand_f(dskip_row, gi) * xg
                if cross:
                    cfcb = cfcb_s[c]
                    yc = st["yc_s"][gi]
                    yg = yg + expand_f(cfcb, gi) * yc[:, 0:GROUP_W] + expand_b(cfcb, gi) * yc[:, GROUP_W:]
                y_s[:, xsl[gi]] = yg
            return run

        def ret_head(hd):
            def run():
                m = (st["sc_r"][hd] * wdec_s[hd]).astype(BF16)
                o = _dot(m, v_s[c, :, rsl[hd]])
                if cross:
                    yc = st["yc_r"][hd]
                    o = o + rvec_s[hd, RV_CROSS_F] * yc[:, 0:RET_HD] + rvec_s[hd, RV_CROSS_B] * yc[:, RET_HD:]
                o = o * lax.rsqrt(jnp.mean(o * o, axis=-1, keepdims=True) + EPS)
                mix_s[:, rsl[hd]] = (g_s[c, :, rsl[hd]].astype(F32) * o).astype(BF16)
            return run

        def ssd_norm():
            yz = y_s[...] * z_s[c].astype(F32)
            yn = yz * lax.rsqrt(jnp.mean(yz * yz, axis=-1, keepdims=True) + EPS) * normw_ref[...]
            mix_s[:, RET_W:] = yn.astype(BF16)

        def out_proj():
            g1 = mod_row(row_mix, 2)
            y = ALPHA * x_ref[0] + g1 * _dot(mix_s[...], wout_ref[...])
            x1_s[slot] = _layer_norm(y, ln1g_ref[...], ln1b_ref[...])

        pieces = [scores]
        for gi in range(SSD_GROUPS):
            pieces += [ssd_head(gi, k) for k in range(HPG)] + [ssd_group(gi)]
        pieces += [ret_head(hd) for hd in range(RET_HEADS)] + [ssd_norm, out_proj]
        return pieces

    def ffn_pieces(slot):
        st = {}

        def start():
            sh2 = mod_row(row_ffn, 3)
            sc2 = mod_row(row_ffn, 4)
            st["h2"] = (x1_s[slot] * (1.0 + sc2) + sh2).astype(BF16)

        def hidden(j):
            def run():
                js = slice(j * FF_BLK, (j + 1) * FF_BLK)
                h2 = st["h2"]
                hid_s[:, js] = (_silu(_dot(h2, wg_ref[:, js])) * _dot(h2, wu_ref[:, js])).astype(BF16)
            return run

        def finish():
            g2 = mod_row(row_ffn, 5)
            x1_s[slot] = ALPHA * x1_s[slot] + g2 * _dot(hid_s[...], wd_ref[...])

        return [start] + [hidden(j) for j in range(D_FF // FF_BLK)] + [finish]

    def ln2(slot):
        def run():
            out_ref[0] = _layer_norm(x1_s[slot], ln2g_ref[...], ln2b_ref[...])
        return run

    def run_all(pieces):
        for p in pieces:
            p()

    def fused(ffn, others):
        _interleave(ffn[:-1], others)
        ffn[-1]()

    if nc == 1:
        slot = seq & 1

        @pl.when(seq == 0)
        def _():
            project(0)
            for i in range(1, len(big_weights)):
                weight_copy(i).wait()
            run_all([recurrences] + emit_pieces(0, slot))
            x1_s[1] = jnp.zeros((C, D_MODEL), F32)

        @pl.when((seq > 0) & (seq < nb))
        def _():
            project(0)
            fused(ffn_pieces(1 - slot), [ln2(slot), recurrences] + emit_pieces(0, slot))

        @pl.when(seq == nb)
        def _():
            _interleave(ffn_pieces(1 - slot), [ln2(slot)])

        @pl.when(seq == nb + 1)
        def _():
            ln2(slot)()
        return

    @pl.when((step < nc) & (seq < nb))
    def _():
        project(step)

    @pl.when((step == nc) & (seq < nb))
    def _():
        recurrences()

    kk = step - nc
    slot = (seq * nc + kk) & 1
    first = (seq == 0) & (step == nc)

    @pl.when(first)
    def _():
        for i in range(1, len(big_weights)):
            weight_copy(i).wait()
        run_all(emit_pieces(kk, slot))
        x1_s[1] = jnp.zeros((C, D_MODEL), F32)

    @pl.when((step >= nc) & (seq < nb) & jnp.logical_not(first))
    def _():
        fused(ffn_pieces(1 - slot), [ln2(slot)] + emit_pieces(kk, slot))

    @pl.when((seq == nb) & (step == nc))
    def _():
        _interleave(ffn_pieces(1 - slot), [ln2(slot)])

    @pl.when((seq == nb) & (step == nc + 1))
    def _():
        ln2(slot)()


def _const_spec(shape):
    nd = len(shape)
    return pl.BlockSpec(shape, lambda b, s: (0,) * nd, pipeline_mode=pl.Buffered(1))


def _layer_call(x, mod, mod_block, mod_per_seq, weights, states, emit_state, use_rope, name):
    nb, L, _ = x.shape
    nc = L // CHUNK
    hpc_blocks = CHUNK // HALO
    has_state = states is not None
    last = nb - 1

    def chunk_of(b, s):
        return jnp.where(b > last, nc - 1, jnp.where(s < nc, s, s - nc))

    def halo_chunk(b, s):
        return jnp.where(b > last, nc - 1, jnp.minimum(s, nc - 1))

    def seq_of(b):
        return jnp.minimum(b, last)

    def out_map(b, s):
        emits = b + 1 if nc == 1 else b * nc + jnp.maximum(s - nc, -1) + 1
        blk = jnp.clip(emits - 3, 0, nb * nc - 1)
        return (blk // nc, blk % nc, 0)

    assert not mod_per_seq or nb <= MOD_ROWS
    in_specs = [
        pl.BlockSpec((1, CHUNK, D_MODEL), lambda b, s: (seq_of(b), chunk_of(b, s), 0)),
        pl.BlockSpec((1, HALO, D_MODEL),
                     lambda b, s: (seq_of(b), jnp.maximum(halo_chunk(b, s) * hpc_blocks - 1, 0), 0)),
        pl.BlockSpec((1, HALO, D_MODEL),
                     lambda b, s: (seq_of(b), jnp.minimum((halo_chunk(b, s) + 1) * hpc_blocks, nc * hpc_blocks - 1), 0)),
        pl.BlockSpec((MOD_ROWS, 6 * D_MODEL), lambda b, s: (mod_block, 0)),
    ] + [pl.BlockSpec(memory_space=pl.ANY) if i in BIG_WEIGHTS else _const_spec(w.shape)
         for i, w in enumerate(weights)]
    args = [x, x, x, mod] + list(weights)
    ret_block = (1, 1, 2, RET_HEADS, RET_HD, RET_HD)
    ssd_block = (1, 1, 2, SSD_HEADS, SSD_HD, SSD_STATE)
    state_map = lambda b, s: (seq_of(b), 0, 0, 0, 0, 0)
    if has_state:
        in_specs += [pl.BlockSpec(ret_block, state_map, pipeline_mode=pl.Buffered(1)),
                     pl.BlockSpec(ssd_block, state_map, pipeline_mode=pl.Buffered(1))]
        args += list(states)
    out_shape = [jax.ShapeDtypeStruct((nb, L, D_MODEL), F32)]
    out_specs = [pl.BlockSpec((1, CHUNK, D_MODEL), out_map)]
    if emit_state:
        out_shape += [jax.ShapeDtypeStruct((nb,) + ret_block[1:], F32),
                      jax.ShapeDtypeStruct((nb,) + ssd_block[1:], F32)]
        out_specs += [pl.BlockSpec(ret_block, state_map), pl.BlockSpec(ssd_block, state_map)]
    state_dt = F32 if emit_state else BF16
    scratch = [
        pltpu.VMEM((nc, CHUNK, RET_W), BF16),
        pltpu.VMEM((nc, RET_W, CHUNK), BF16),
        pltpu.VMEM((nc, CHUNK, RET_W), BF16),
        pltpu.VMEM((nc, CHUNK, RET_W), BF16),
        pltpu.VMEM((nc, CHUNK, SSD_W), BF16),
        pltpu.VMEM((CHUNK + 2 * HALO, CONV_CH), F32),
        pltpu.VMEM((nc, CHUNK, SSD_W), F32),
        pltpu.VMEM((nc, SSD_GROUPS * SSD_STATE, CHUNK), BF16),
        pltpu.VMEM((nc, CHUNK, SSD_GROUPS * SSD_STATE), BF16),
        pltpu.VMEM((nc, CHUNK, LANES), F32),
        pltpu.VMEM((nc, CHUNK, LANES), F32),
        pltpu.VMEM((nc, CHUNK, LANES), F32),
        pltpu.VMEM((nc, 4 * SUBLANES, LANES), F32),
        pltpu.VMEM((nc, N_DT, CHUNK), F32),
        pltpu.VMEM((CHUNK, SSD_W), F32),
        pltpu.VMEM((CHUNK, 2 * RET_W), BF16),
        pltpu.VMEM((nc, RET_HEADS, RET_HD, 2 * RET_HD), state_dt),
        pltpu.VMEM((nc, SSD_GROUPS, SSD_STATE, 2 * GROUP_W), state_dt),
        pltpu.VMEM((nc, RET_HEADS, RET_HD, 2 * RET_HD), BF16),
        pltpu.VMEM((nc, SSD_GROUPS, SSD_STATE, 2 * GROUP_W), BF16),
        pltpu.VMEM((RET_HEADS, CHUNK, CHUNK), F32),
        pltpu.VMEM((RET_HEADS, 4, CHUNK, LANES), F32),
        pltpu.VMEM((2, CHUNK, D_MODEL), F32),
        pltpu.VMEM((CHUNK, D_FF), BF16),
        pltpu.VMEM((LANES, D_MODEL), BF16),
    ]
    scratch += [pltpu.VMEM(weights[i].shape, BF16) for i in BIG_WEIGHTS]
    scratch += [pltpu.SemaphoreType.DMA((len(BIG_WEIGHTS),))]
    if use_rope:
        scratch += [pltpu.VMEM((nc, CHUNK, LANES), F32)] * 2
    kern = functools.partial(_layer_kernel, L=L, nb=nb, mod_per_seq=mod_per_seq, has_state=has_state,
                             use_rope=use_rope, emit_state=emit_state)
    return pl.pallas_call(
        kern,
        grid=(nb + 2, 1) if nc == 1 else (nb + 1, 2 * nc),
        in_specs=in_specs,
        out_specs=out_specs,
        out_shape=out_shape,
        scratch_shapes=scratch,
        compiler_params=pltpu.CompilerParams(dimension_semantics=("arbitrary", "arbitrary"),
                                             vmem_limit_bytes=VMEM_LIMIT),
        name=name,
    )(*args)


def kernel(x_prompt, x_sample, state_ret, state_ssd, c, c_ctx, w_in, ret_decay_fwd, ret_decay_bwd, conv_w, conv_b, dt_bias_fwd, dt_bias_bwd, a_log_fwd, a_log_bwd, d_skip, ssd_norm_w, w_out, ln1_g, ln1_b, w_gate, w_up, w_down, ln2_g, ln2_b, w_ada, b_ada):
    depth = w_in.shape[0]
    assert depth == 1, "single trunk layer"
    bp, lp, _ = x_prompt.shape
    bs, ls, _ = x_sample.shape
    assert lp % CHUNK == 0 and ls % CHUNK == 0 and ls % GRID_W == 0 and D_FF % FF_BLK == 0


    convw = jnp.zeros((SUBLANES, CONV_CH), F32).at[:CONV_W].set(conv_w[0])
    convb = conv_b[0][None, :]
    dt_bias = jnp.concatenate([dt_bias_fwd[0], dt_bias_bwd[0]])
    a_log = jnp.concatenate([a_log_fwd[0], a_log_bwd[0]])
    ret_decay = jnp.concatenate([ret_decay_fwd[0], ret_decay_bwd[0]])
    hp = jnp.zeros((SUBLANES, LANES), F32)
    hp = hp.at[0, :N_DT].set(dt_bias).at[1, :N_DT].set(a_log)
    hp = hp.at[2, :2 * RET_HEADS].set(ret_decay).at[3, :SSD_HEADS].set(d_skip[0])
    hpc = jnp.zeros((N_DT, LANES), F32).at[:, 0].set(dt_bias).at[:, 1].set(a_log)
    mod, w_in_b, w_out_b, w_gate_b, w_up_b, w_down_b = _prep_call(
        c, c_ctx[None, :], w_ada[0], b_ada[0][None, :],
        jnp.swapaxes(w_in[0], 0, 1), [w_out[0], w_gate[0], w_up[0], w_down[0]])
    weights = (w_in_b, convw, convb, hp, hpc, ssd_norm_w[0][None, :],
               w_out_b, ln1_g[0][None, :], ln1_b[0][None, :],
               w_gate_b, w_up_b, w_down_b, ln2_g[0][None, :], ln2_b[0][None, :])

    yp, new_ret, new_ssd = _layer_call(x_prompt, mod, 1, False, weights, None, True, False, "layer_ctx")
    states = (state_ret, jnp.swapaxes(state_ssd, -1, -2))
    (ys,) = _layer_call(x_sample, mod, 0, True, weights, states, False, True, "layer_lat")
    return (yp, ys, new_ret, jnp.swapaxes(new_ssd, -1, -2))
```

```python
import functools
import math

import jax
import jax.numpy as jnp
from jax import lax
from jax.experimental import pallas as pl
from jax.experimental.pallas import tpu as pltpu

F32 = jnp.float32
BF16 = jnp.bfloat16

D_MODEL = 1024
RET_W = 512
RET_HEADS = 4
RET_HD = 128
SSD_W = 512
SSD_HD = 64
SSD_HEADS = 8
SSD_GROUPS = 2
SSD_STATE = 128
HPG = SSD_HEADS // SSD_GROUPS
GROUP_W = HPG * SSD_HD
CONV_W = 5
CONV_CH = SSD_W + 2 * SSD_GROUPS * SSD_STATE
D_FF = 2816
GRID_W = 64
GRID_SHIFT = 6
ROPE_BASE = 10000.0
EPS = 1e-6
ALPHA = 2.0 ** 0.25
MAIN_COLS = 4 * RET_W + SSD_W + CONV_CH
XBC_COL0 = 4 * RET_W + SSD_W
N_DT = 2 * SSD_HEADS

CHUNK = 256
HALO = 8
FF_BLK = 256
MOD_ROWS = 8
CAST_STEPS = 8
CAST_WT_ROWS = 512
LANES = 128
SUBLANES = 8
XBC_HALF = CONV_CH // 2
VMEM_LIMIT = 62 * 1024 * 1024


def _dot(a, b):
    return jnp.dot(a, b, preferred_element_type=F32)


def _dot_nt(a, b):
    return lax.dot_general(a, b, (((1,), (1,)), ((), ())), preferred_element_type=F32)


def _silu(x):
    return x * jax.nn.sigmoid(x)


def _softplus(x):
    return jnp.maximum(x, 0.0) + jnp.log1p(jnp.exp(-jnp.abs(x)))


def _layer_norm(y, g, b):
    mu = jnp.mean(y, axis=-1, keepdims=True)
    yc = y - mu
    var = jnp.mean(yc * yc, axis=-1, keepdims=True)
    return yc * lax.rsqrt(var + EPS) * g + b


def _cumsum(x, axis):
    n = x.shape[axis]
    idx = lax.broadcasted_iota(jnp.int32, x.shape, axis)
    s = 1
    while s < n:
        x = x + jnp.where(idx >= s, pltpu.roll(x, s, axis), 0.0)
        s *= 2
    return x


def _expand4(cols, lane):
    a = jnp.where(lane < SSD_HD, cols[0], cols[1])
    b = jnp.where(lane < SSD_HD, cols[2], cols[3])
    return jnp.concatenate([a, b], axis=1)


def _interleave(a, b):
    ia = ib = 0
    while ia < len(a) or ib < len(b):
        if ib >= len(b) or (ia < len(a) and ia * len(b) <= ib * len(a)):
            a[ia]()
            ia += 1
        else:
            b[ib]()
            ib += 1


def _prep_kernel(c_ref, cctx_ref, wada_ref, bada_ref, wt_ref, *refs, wt_rows):
    n = (len(refs) - 2) // 2
    srcs, mod_ref, wt_out, dsts = refs[:n], refs[n], refs[n + 1], refs[n + 2:]
    cond = jnp.concatenate([c_ref[...], jnp.broadcast_to(cctx_ref[...], (MOD_ROWS, D_MODEL))], axis=0)
    mod_ref[...] = _dot(_silu(cond).astype(BF16), wada_ref[...].astype(BF16)) + bada_ref[...]
    blk = wt_ref.shape[0]
    row = lax.broadcasted_iota(jnp.int32, (blk, 1), 0) + pl.program_id(0) * blk
    wt_out[...] = jnp.where(row < wt_rows, wt_ref[...], 0.0).T.astype(BF16)
    for src, dst in zip(srcs, dsts):
        dst[...] = src[...].astype(BF16)


def _prep_call(c, c_ctx, w_ada, b_ada, wt, ws):
    assert c.shape == (MOD_ROWS, D_MODEL), c.shape
    for w in ws:
        assert w.shape[0] % (CAST_STEPS * 16) == 0, w.shape
    assert wt.shape[0] <= CAST_STEPS * CAST_WT_ROWS
    n_mod = w_ada.shape[1]
    ada_cols = n_mod // CAST_STEPS
    assert ada_cols % LANES == 0
    specs = [pl.BlockSpec((w.shape[0] // CAST_STEPS, w.shape[1]), lambda i: (i, 0)) for w in ws]
    return pl.pallas_call(
        functools.partial(_prep_kernel, wt_rows=wt.shape[0]),
        grid=(CAST_STEPS,),
        in_specs=[pl.BlockSpec((MOD_ROWS, D_MODEL), lambda i: (0, 0)),
                  pl.BlockSpec((1, D_MODEL), lambda i: (0, 0)),
                  pl.BlockSpec((D_MODEL, ada_cols), lambda i: (0, i)),
                  pl.BlockSpec((1, ada_cols), lambda i: (0, i)),
                  pl.BlockSpec((CAST_WT_ROWS, wt.shape[1]), lambda i: (i, 0))] + specs,
        out_specs=[pl.BlockSpec((2 * MOD_ROWS, ada_cols), lambda i: (0, i)),
                   pl.BlockSpec((wt.shape[1], CAST_WT_ROWS), lambda i: (0, i))] + specs,
        out_shape=[jax.ShapeDtypeStruct((2 * MOD_ROWS, n_mod), F32),
                   jax.ShapeDtypeStruct((wt.shape[1], -(-wt.shape[0] // LANES) * LANES), BF16)]
        + [jax.ShapeDtypeStruct(w.shape, BF16) for w in ws],
        compiler_params=pltpu.CompilerParams(dimension_semantics=("arbitrary",),
                                             vmem_limit_bytes=VMEM_LIMIT),
        name="prep_mod_and_weights",
    )(c, c_ctx, w_ada, b_ada, wt, *ws)


RV_TAIL_F, RV_TAIL_B, RV_CROSS_F, RV_CROSS_B = range(4)
BIG_WEIGHTS = (0, 6, 9, 10, 11)


def _layer_kernel(*refs, L, nb, mod_per_seq, has_state, use_rope, emit_state):
    nc = L // CHUNK
    C = CHUNK
    cross = has_state or nc > 1
    it = iter(refs)
    x_ref, xprev_ref, xnext_ref, mod_ref, wmain_hbm = (next(it) for _ in range(5))
    convw_ref, convb_ref, hp_ref, hpc_ref, normw_ref = (next(it) for _ in range(5))
    wout_hbm, ln1g_ref, ln1b_ref = (next(it) for _ in range(3))
    wg_hbm, wu_hbm, wd_hbm, ln2g_ref, ln2b_ref = (next(it) for _ in range(5))
    if has_state:
        sret0_ref, sssd0_ref = next(it), next(it)
    out_ref = next(it)
    if emit_state:
        nret_ref, nssd_ref = next(it), next(it)
    (q_s, kT_s, v_s, g_s, z_s, stage_s, xs_s, bT_s, c_s, inccol_s, exccol_s, cfcb_s, dec_s,
     rowarg_s, y_s, mix_s, rloc_s, sloc_s, rent_s, sent_s, wdec_s, rvec_s, x1_s, hid_s, wdtr_s) = (
         next(it) for _ in range(25))
    wmain_ref, wout_ref, wg_ref, wu_ref, wd_ref, wsem = (next(it) for _ in range(6))
    if use_rope:
        cos_s, sin_s = next(it), next(it)

    seq = pl.program_id(0)
    step = pl.program_id(1)

    if mod_per_seq:
        row_mix = jnp.minimum(seq, nb - 1)
        row_ffn = jnp.minimum(jnp.where(step > nc, seq, jnp.maximum(seq - 1, 0)), nb - 1)
    else:
        row_mix = row_ffn = 0

    def mod_row(row, k):
        return mod_ref[pl.ds(row, 1), k * D_MODEL:(k + 1) * D_MODEL]

    big_weights = ((wmain_hbm, wmain_ref), (wout_hbm, wout_ref), (wg_hbm, wg_ref), (wu_hbm, wu_ref),
                   (wd_hbm, wd_ref))

    def weight_copy(i):
        return pltpu.make_async_copy(big_weights[i][0], big_weights[i][1], wsem.at[i])

    hp = hp_ref[...]
    dt_bias_row = hp[0:1, :]
    nega_row = -jnp.exp(hp[1:2, :])
    lg_row = -_softplus(-hp[2:3, :])
    dskip_row = hp[3:4, :]
    hpc = hpc_ref[...]
    dt_bias_col = hpc[:, 0:1]
    nega_col = -jnp.exp(hpc[:, 1:2])

    lane = lax.broadcasted_iota(jnp.int32, (1, LANES), 1)

    def ret_decays(hd):
        lgf = lg_row[:, hd:hd + 1]
        lgb = lg_row[:, RET_HEADS + hd:RET_HEADS + hd + 1]
        return lgf, lgb

    def group_heads(gi):
        return [gi * HPG + k for k in range(HPG)]

    def expand_f(arr, gi):
        return _expand4([arr[:, hh:hh + 1] for hh in group_heads(gi)], lane)

    def expand_b(arr, gi):
        return _expand4([arr[:, SSD_HEADS + hh:SSD_HEADS + hh + 1] for hh in group_heads(gi)], lane)

    @pl.when((seq == 0) & (step == 0))
    def _():
        weight_copy(0).start()
        ii = lax.broadcasted_iota(jnp.int32, (C, C), 0)
        jj = lax.broadcasted_iota(jnp.int32, (C, C), 1)
        dmat = (ii - jj).astype(F32)
        irow = lax.broadcasted_iota(jnp.int32, (C, LANES), 0).astype(F32)
        for hd in range(RET_HEADS):
            lgf, lgb = ret_decays(hd)
            wdec_s[hd] = jnp.exp(jnp.where(jj <= ii, dmat * lgf, -dmat * lgb))
            rvec_s[hd, RV_TAIL_F] = jnp.exp((C - 1.0 - irow) * lgf)
            rvec_s[hd, RV_TAIL_B] = jnp.exp(irow * lgb)
            rvec_s[hd, RV_CROSS_F] = jnp.exp((irow + 1.0) * lgf)
            rvec_s[hd, RV_CROSS_B] = jnp.exp((C - irow) * lgb)
        if use_rope:
            ln = lax.broadcasted_iota(jnp.int32, (C, LANES), 1)
            nf = RET_HD // 4
            inv = jnp.exp((ln & (nf - 1)).astype(F32) * (-math.log(ROPE_BASE) / nf))
            for cc in range(nc):
                t = lax.broadcasted_iota(jnp.int32, (C, LANES), 0) + cc * C
                pos = jnp.where((ln & (2 * nf - 1)) < nf, t >> GRID_SHIFT, t & (GRID_W - 1)).astype(F32)
                ang = pos * inv
                cos_s[cc] = jnp.cos(ang)
                sin_s[cc] = jnp.where(ln < RET_HD // 2, -jnp.sin(ang), jnp.sin(ang))
        weight_copy(0).wait()
        for i in range(1, len(big_weights)):
            weight_copy(i).start()
        wdtr_s[...] = wmain_ref[:, MAIN_COLS:MAIN_COLS + LANES].astype(F32).T.astype(BF16)

    def project(c):
        st = {}
        CB = 256
        assert CONV_W == 5 and HALO >= CONV_W // 2

        def v_mod():
            sh1 = mod_row(row_mix, 0)
            sc1 = mod_row(row_mix, 1)
            xe = jnp.concatenate([xprev_ref[0], x_ref[0], xnext_ref[0]], axis=0)
            xm = xe * (1.0 + sc1) + sh1
            st["he"] = xm.astype(BF16)
            st["h"] = xm[HALO:HALO + C].astype(BF16)

        def m_main(name, lo, hi, halo=False):
            def run():
                st[name] = _dot(st["he" if halo else "h"], wmain_ref[:, lo:hi])
            return run

        def m_dt():
            raw_c = _dot(st["h"], wmain_ref[:, MAIN_COLS:MAIN_COLS + LANES])
            st["dtc"] = raw_c + pltpu.roll(raw_c, SSD_HEADS, 1)
            raw_r = _dot_nt(wdtr_s[0:N_DT, :], st["h"])
            st["dtr"] = raw_r + pltpu.roll(raw_r, SSD_HEADS, 0)

        def v_dt():
            dt_c = _softplus(st["dtc"] + dt_bias_row)
            lac = dt_c * nega_row
            inc_col = _cumsum(lac, 0)
            exc_col = inc_col - lac
            tot_col = inc_col[C - 1:C, :]
            inccol_s[c] = inc_col
            exccol_s[c] = exc_col
            dec_s[c, 0:SUBLANES, :] = jnp.broadcast_to(jnp.exp(tot_col), (SUBLANES, LANES))
            if cross:
                cfcb_s[c] = jnp.exp(jnp.where(lane < SSD_HEADS, inc_col, tot_col - exc_col))
            st["sf"] = jnp.exp(tot_col - inc_col) * dt_c
            st["sb"] = jnp.exp(exc_col) * dt_c
            dt_r = _softplus(st["dtr"] + dt_bias_col)
            lar = dt_r * nega_col
            inc_row = _cumsum(lar, 1)
            ldt = jnp.log(dt_r)
            rid = lax.broadcasted_iota(jnp.int32, (N_DT, C), 0)
            rowarg_s[c] = jnp.where(rid < SSD_HEADS, inc_row - ldt, inc_row - lar + ldt)

        def v_stage(hf):
            def run():
                pe = st["pe%d" % hf]
                cs = slice(hf * XBC_HALF, (hf + 1) * XBC_HALF)
                stage_s[0:HALO, cs] = jnp.where(c > 0, pe[0:HALO], 0.0)
                stage_s[HALO:HALO + C, cs] = pe[HALO:HALO + C]
                stage_s[HALO + C:, cs] = jnp.where(c < nc - 1, pe[HALO + C:], 0.0)
            return run

        def conv_block(cb):
            cs = slice(cb * CB, (cb + 1) * CB)
            rows = C + 2 * HALO
            xin = stage_s[:, cs]
            taps = [convw_ref[k:k + 1, cs] * xin for k in range(CONV_W)]
            up = lambda a: pltpu.roll(a, rows - 1, 0)
            down = lambda a: pltpu.roll(a, 1, 0)
            acc = taps[2] + up(taps[3] + up(taps[4])) + down(taps[1] + down(taps[0]))
            return _silu(acc[HALO:HALO + C] + convb_ref[0:1, cs])

        def v_conv_x(cb):
            def run():
                xs_s[c, :, cb * CB:(cb + 1) * CB] = conv_block(cb)
            return run

        def v_conv_b():
            bT_s[c] = conv_block(SSD_W // CB).T.astype(BF16)

        def v_conv_c():
            c_s[c] = conv_block(SSD_W // CB + 1).astype(BF16)

        def rope(a):
            if not use_rope:
                return a
            return a * cos_s[c] + pltpu.roll(a, RET_HD // 2, 1) * sin_s[c]

        def v_q():
            for hd in range(RET_HEADS):
                sl = slice(hd * RET_HD, (hd + 1) * RET_HD)
                q_s[c, :, sl] = rope(st["pq"][:, sl]).astype(BF16)

        def v_k():
            for hd in range(RET_HEADS):
                sl = slice(hd * RET_HD, (hd + 1) * RET_HD)
                kh = rope(st["pk"][:, sl]) * (RET_HD ** -0.5)
                kT_s[c, sl, :] = kh.T.astype(BF16)

        def v_v():
            v_s[c] = st["pv"].astype(BF16)

        def v_g():
            g_s[c] = _silu(st["pg"]).astype(BF16)

        def v_z():
            z_s[c] = _silu(st["pz"]).astype(BF16)

        def m_sloc(gi):
            def run():
                xg = xs_s[c, :, gi * GROUP_W:(gi + 1) * GROUP_W]
                vcat = jnp.concatenate([xg * expand_f(st["sf"], gi), xg * expand_b(st["sb"], gi)],
                                       axis=1).astype(BF16)
                sloc_s[c, gi] = _dot(bT_s[c, gi * SSD_STATE:(gi + 1) * SSD_STATE, :], vcat).astype(sloc_s.dtype)
            return run

        def m_rloc(hd):
            def run():
                sl = slice(hd * RET_HD, (hd + 1) * RET_HD)
                vf = st["pv"][:, sl]
                vcat = jnp.concatenate([vf * rvec_s[hd, RV_TAIL_F], vf * rvec_s[hd, RV_TAIL_B]],
                                       axis=1).astype(BF16)
                rloc_s[c, hd] = _dot(kT_s[c, sl, :], vcat).astype(rloc_s.dtype)
            return run

        m_pe0 = m_main("pe0", XBC_COL0, XBC_COL0 + XBC_HALF, halo=True)
        m_pe1 = m_main("pe1", XBC_COL0 + XBC_HALF, MAIN_COLS, halo=True)
        m_q = m_main("pq", 0, RET_W)
        m_k = m_main("pk", RET_W, 2 * RET_W)
        m_v = m_main("pv", 2 * RET_W, 3 * RET_W)
        m_g = m_main("pg", 3 * RET_W, 4 * RET_W)
        m_z = m_main("pz", 4 * RET_W, XBC_COL0)
        order = [v_mod, m_dt, m_pe0, m_pe1, v_dt, m_q, v_stage(0), v_stage(1), m_k, v_conv_x(0), m_v,
                 v_conv_x(1), v_q, m_g, v_conv_b, v_k, m_z, v_conv_c, v_v,
                 m_rloc(0), m_rloc(1), v_g, m_rloc(2), m_rloc(3), m_sloc(0), m_sloc(1), v_z]
        for piece in order:
            piece()

    def recurrences():
        for hd in range(RET_HEADS):
            lgf, lgb = ret_decays(hd)
            dec_f = jnp.exp(C * lgf)
            dec_b = jnp.exp(C * lgb)
            if has_state:
                ent_f = sret0_ref[0, 0, 0, hd]
                ent_b = sret0_ref[0, 0, 1, hd]
            else:
                ent_f = jnp.zeros((RET_HD, RET_HD), F32)
                ent_b = jnp.zeros((RET_HD, RET_HD), F32)
            for c in range(nc):
                if cross:
                    rent_s[c, hd, :, 0:RET_HD] = ent_f.astype(BF16)
                ent_f = dec_f * ent_f + rloc_s[c, hd, :, 0:RET_HD]
            for c in range(nc - 1, -1, -1):
                if cross:
                    rent_s[c, hd, :, RET_HD:] = ent_b.astype(BF16)
                ent_b = dec_b * ent_b + rloc_s[c, hd, :, RET_HD:]
            if emit_state:
                nret_ref[0, 0, 0, hd] = ent_f
                nret_ref[0, 0, 1, hd] = ent_b

        for gi in range(SSD_GROUPS):
            heads = group_heads(gi)
            if has_state:
                ent_f = jnp.concatenate([sssd0_ref[0, 0, 0, hh] for hh in heads], axis=0).T
                ent_b = jnp.concatenate([sssd0_ref[0, 0, 1, hh] for hh in heads], axis=0).T
            else:
                ent_f = jnp.zeros((SSD_STATE, GROUP_W), F32)
                ent_b = jnp.zeros((SSD_STATE, GROUP_W), F32)
            for c in range(nc):
                if cross:
                    sent_s[c, gi, :, 0:GROUP_W] = ent_f.astype(BF16)
                ent_f = expand_f(dec_s[c, 0:1, :], gi) * ent_f + sloc_s[c, gi, :, 0:GROUP_W]
            for c in range(nc - 1, -1, -1):
                if cross:
                    sent_s[c, gi, :, GROUP_W:] = ent_b.astype(BF16)
                ent_b = expand_b(dec_s[c, 0:1, :], gi) * ent_b + sloc_s[c, gi, :, GROUP_W:]
            if emit_state:
                ent_ft = ent_f.T
                ent_bt = ent_b.T
                for k, hh in enumerate(heads):
                    nssd_ref[0, 0, 0, hh] = ent_ft[k * SSD_HD:(k + 1) * SSD_HD, :]
                    nssd_ref[0, 0, 1, hh] = ent_bt[k * SSD_HD:(k + 1) * SSD_HD, :]

    def emit_pieces(c, slot):
        rsl = [slice(hd * RET_HD, (hd + 1) * RET_HD) for hd in range(RET_HEADS)]
        gsl = [slice(gi * SSD_STATE, (gi + 1) * SSD_STATE) for gi in range(SSD_GROUPS)]
        xsl = [slice(gi * GROUP_W, (gi + 1) * GROUP_W) for gi in range(SSD_GROUPS)]
        st = {}

        def scores():
            ii = lax.broadcasted_iota(jnp.int32, (C, C), 0)
            jj = lax.broadcasted_iota(jnp.int32, (C, C), 1)
            st["causal"] = jj <= ii
            qs = [q_s[c, :, sl] for sl in rsl]
            cms = [c_s[c, :, sl] for sl in gsl]
            st["sc_s"] = [_dot(cms[gi], bT_s[c, gsl[gi], :]) for gi in range(SSD_GROUPS)]
            st["sc_r"] = [_dot(qs[hd], kT_s[c, rsl[hd], :]) for hd in range(RET_HEADS)]
            if cross:
                st["yc_s"] = [_dot(cms[gi], sent_s[c, gi]) for gi in range(SSD_GROUPS)]
                st["yc_r"] = [_dot(qs[hd], rent_s[c, hd]) for hd in range(RET_HEADS)]
            st["inc_col"] = inccol_s[c]
            st["exc_col"] = exccol_s[c]
            st["rowarg"] = rowarg_s[c]

        def ssd_head(gi, k):
            def run():
                hh = gi * HPG + k
                hb = SSD_HEADS + hh
                arg = jnp.where(st["causal"],
                                st["inc_col"][:, hh:hh + 1] - st["rowarg"][hh:hh + 1, :],
                                st["rowarg"][hb:hb + 1, :] - st["exc_col"][:, hb:hb + 1])
                m = (st["sc_s"][gi] * jnp.exp(arg)).astype(BF16)
                xh = xs_s[c, :, hh * SSD_HD:(hh + 1) * SSD_HD].astype(BF16)
                y_s[:, hh * SSD_HD:(hh + 1) * SSD_HD] = _dot(m, xh)
            return run

        def ssd_group(gi):
            def run():
                xg = xs_s[c, :, xsl[gi]]
                yg = y_s[:, xsl[gi]] + expand_f(dskip_row, gi) * xg
                if cross:
                    cfcb = cfcb_s[c]
                    yc = st["yc_s"][gi]
                    yg = yg + expand_f(cfcb, gi) * yc[:, 0:GROUP_W] + expand_b(cfcb, gi) * yc[:, GROUP_W:]
                y_s[:, xsl[gi]] = yg
            return run

        def ret_head(hd):
            def run():
                m = (st["sc_r"][hd] * wdec_s[hd]).astype(BF16)
                o = _dot(m, v_s[c, :, rsl[hd]])
                if cross:
                    yc = st["yc_r"][hd]
                    o = o + rvec_s[hd, RV_CROSS_F] * yc[:, 0:RET_HD] + rvec_s[hd, RV_CROSS_B] * yc[:, RET_HD:]
                o = o * lax.rsqrt(jnp.mean(o * o, axis=-1, keepdims=True) + EPS)
                mix_s[:, rsl[hd]] = (g_s[c, :, rsl[hd]].astype(F32) * o).astype(BF16)
            return run

        def ssd_norm():
            yz = y_s[...] * z_s[c].astype(F32)
            yn = yz * lax.rsqrt(jnp.mean(yz * yz, axis=-1, keepdims=True) + EPS) * normw_ref[...]
            mix_s[:, RET_W:] = yn.astype(BF16)

        def out_proj():
            g1 = mod_row(row_mix, 2)
            y = ALPHA * x_ref[0] + g1 * _dot(mix_s[...], wout_ref[...])
            x1_s[slot] = _layer_norm(y, ln1g_ref[...], ln1b_ref[...])

        pieces = [scores]
        for gi in range(SSD_GROUPS):
            pieces += [ssd_head(gi, k) for k in range(HPG)] + [ssd_group(gi)]
        pieces += [ret_head(hd) for hd in range(RET_HEADS)] + [ssd_norm, out_proj]
        return pieces

    def ffn_pieces(slot):
        st = {}

        def start():
            sh2 = mod_row(row_ffn, 3)
            sc2 = mod_row(row_ffn, 4)
            st["h2"] = (x1_s[slot] * (1.0 + sc2) + sh2).astype(BF16)

        def hidden(j):
            def run():
                js = slice(j * FF_BLK, (j + 1) * FF_BLK)
                h2 = st["h2"]
                hid_s[:, js] = (_silu(_dot(h2, wg_ref[:, js])) * _dot(h2, wu_ref[:, js])).astype(BF16)
            return run

        def finish():
            g2 = mod_row(row_ffn, 5)
            x1_s[slot] = ALPHA * x1_s[slot] + g2 * _dot(hid_s[...], wd_ref[...])

        return [start] + [hidden(j) for j in range(D_FF // FF_BLK)] + [finish]

    def ln2(slot):
        def run():
            out_ref[0] = _layer_norm(x1_s[slot], ln2g_ref[...], ln2b_ref[...])
        return run

    def run_all(pieces):
        for p in pieces:
            p()

    def fused(ffn, others):
        _interleave(ffn[:-1], others)
        ffn[-1]()

    if nc == 1:
        slot = seq & 1

        @pl.when(seq == 0)
        def _():
            project(0)
            for i in range(1, len(big_weights)):
                weight_copy(i).wait()
            run_all([recurrences] + emit_pieces(0, slot))
            x1_s[1] = jnp.zeros((C, D_MODEL), F32)

        @pl.when((seq > 0) & (seq < nb))
        def _():
            project(0)
            fused(ffn_pieces(1 - slot), [ln2(slot), recurrences] + emit_pieces(0, slot))

        @pl.when(seq == nb)
        def _():
            _interleave(ffn_pieces(1 - slot), [ln2(slot)])

        @pl.when(seq == nb + 1)
        def _():
            ln2(slot)()
        return

    @pl.when((step < nc) & (seq < nb))
    def _():
        project(step)

    @pl.when((step == nc) & (seq < nb))
    def _():
        recurrences()

    kk = step - nc
    slot = (seq * nc + kk) & 1
    first = (seq == 0) & (step == nc)

    @pl.when(first)
    def _():
        for i in range(1, len(big_weights)):
            weight_copy(i).wait()
        run_all(emit_pieces(kk, slot))
        x1_s[1] = jnp.zeros((C, D_MODEL), F32)

    @pl.when((step >= nc) & (seq < nb) & jnp.logical_not(first))
    def _():
        fused(ffn_pieces(1 - slot), [ln2(slot)] + emit_pieces(kk, slot))

    @pl.when((seq == nb) & (step == nc))
    def _():
        _interleave(ffn_pieces(1 - slot), [ln2(slot)])

    @pl.when((seq == nb) & (step == nc + 1))
    def _():
        ln2(slot)()


def _const_spec(shape):
    nd = len(shape)
    return pl.BlockSpec(shape, lambda b, s: (0,) * nd, pipeline_mode=pl.Buffered(1))


def _layer_call(x, mod, mod_block, mod_per_seq, weights, states, emit_state, use_rope, name):
    nb, L, _ = x.shape
    nc = L // CHUNK
    hpc_blocks = CHUNK // HALO
    has_state = states is not None
    last = nb - 1

    def chunk_of(b, s):
        return jnp.where(b > last, nc - 1, jnp.where(s < nc, s, s - nc))

    def halo_chunk(b, s):
        return jnp.where(b > last, nc - 1, jnp.minimum(s, nc - 1))

    def seq_of(b):
        return jnp.minimum(b, last)

    def out_map(b, s):
        emits = b + 1 if nc == 1 else b * nc + jnp.maximum(s - nc, -1) + 1
        blk = jnp.clip(emits - 3, 0, nb * nc - 1)
        return (blk // nc, blk % nc, 0)

    assert not mod_per_seq or nb <= MOD_ROWS
    in_specs = [
        pl.BlockSpec((1, CHUNK, D_MODEL), lambda b, s: (seq_of(b), chunk_of(b, s), 0)),
        pl.BlockSpec((1, HALO, D_MODEL),
                     lambda b, s: (seq_of(b), jnp.maximum(halo_chunk(b, s) * hpc_blocks - 1, 0), 0)),
        pl.BlockSpec((1, HALO, D_MODEL),
                     lambda b, s: (seq_of(b), jnp.minimum((halo_chunk(b, s) + 1) * hpc_blocks, nc * hpc_blocks - 1), 0)),
        pl.BlockSpec((MOD_ROWS, 6 * D_MODEL), lambda b, s: (mod_block, 0)),
    ] + [pl.BlockSpec(memory_space=pl.ANY) if i in BIG_WEIGHTS else _const_spec(w.shape)
         for i, w in enumerate(weights)]
    args = [x, x, x, mod] + list(weights)
    ret_block = (1, 1, 2, RET_HEADS, RET_HD, RET_HD)
    ssd_block = (1, 1, 2, SSD_HEADS, SSD_HD, SSD_STATE)
    state_map = lambda b, s: (seq_of(b), 0, 0, 0, 0, 0)
    if has_state:
        in_specs += [pl.BlockSpec(ret_block, state_map, pipeline_mode=pl.Buffered(1)),
                     pl.BlockSpec(ssd_block, state_map, pipeline_mode=pl.Buffered(1))]
        args += list(states)
    out_shape = [jax.ShapeDtypeStruct((nb, L, D_MODEL), F32)]
    out_specs = [pl.BlockSpec((1, CHUNK, D_MODEL), out_map)]
    if emit_state:
        out_shape += [jax.ShapeDtypeStruct((nb,) + ret_block[1:], F32),
                      jax.ShapeDtypeStruct((nb,) + ssd_block[1:], F32)]
        out_specs += [pl.BlockSpec(ret_block, state_map), pl.BlockSpec(ssd_block, state_map)]
    state_dt = F32 if emit_state else BF16
    scratch = [
        pltpu.VMEM((nc, CHUNK, RET_W), BF16),
        pltpu.VMEM((nc, RET_W, CHUNK), BF16),
        pltpu.VMEM((nc, CHUNK, RET_W), BF16),
        pltpu.VMEM((nc, CHUNK, RET_W), BF16),
        pltpu.VMEM((nc, CHUNK, SSD_W), BF16),
        pltpu.VMEM((CHUNK + 2 * HALO, CONV_CH), F32),
        pltpu.VMEM((nc, CHUNK, SSD_W), F32),
        pltpu.VMEM((nc, SSD_GROUPS * SSD_STATE, CHUNK), BF16),
        pltpu.VMEM((nc, CHUNK, SSD_GROUPS * SSD_STATE), BF16),
        pltpu.VMEM((nc, CHUNK, LANES), F32),
        pltpu.VMEM((nc, CHUNK, LANES), F32),
        pltpu.VMEM((nc, CHUNK, LANES), F32),
        pltpu.VMEM((nc, 4 * SUBLANES, LANES), F32),
        pltpu.VMEM((nc, N_DT, CHUNK), F32),
        pltpu.VMEM((CHUNK, SSD_W), F32),
        pltpu.VMEM((CHUNK, 2 * RET_W), BF16),
        pltpu.VMEM((nc, RET_HEADS, RET_HD, 2 * RET_HD), state_dt),
        pltpu.VMEM((nc, SSD_GROUPS, SSD_STATE, 2 * GROUP_W), state_dt),
        pltpu.VMEM((nc, RET_HEADS, RET_HD, 2 * RET_HD), BF16),
        pltpu.VMEM((nc, SSD_GROUPS, SSD_STATE, 2 * GROUP_W), BF16),
        pltpu.VMEM((RET_HEADS, CHUNK, CHUNK), F32),
        pltpu.VMEM((RET_HEADS, 4, CHUNK, LANES), F32),
        pltpu.VMEM((2, CHUNK, D_MODEL), F32),
        pltpu.VMEM((CHUNK, D_FF), BF16),
        pltpu.VMEM((LANES, D_MODEL), BF16),
    ]
    scratch += [pltpu.VMEM(weights[i].shape, BF16) for i in BIG_WEIGHTS]
    scratch += [pltpu.SemaphoreType.DMA((len(BIG_WEIGHTS),))]
    if use_rope:
        scratch += [pltpu.VMEM((nc, CHUNK, LANES), F32)] * 2
    kern = functools.partial(_layer_kernel, L=L, nb=nb, mod_per_seq=mod_per_seq, has_state=has_state,
                             use_rope=use_rope, emit_state=emit_state)
    return pl.pallas_call(
        kern,
        grid=(nb + 2, 1) if nc == 1 else (nb + 1, 2 * nc),
        in_specs=in_specs,
        out_specs=out_specs,
        out_shape=out_shape,
        scratch_shapes=scratch,
        compiler_params=pltpu.CompilerParams(dimension_semantics=("arbitrary", "arbitrary"),
                                             vmem_limit_bytes=VMEM_LIMIT),
        name=name,
    )(*args)


def kernel(x_prompt, x_sample, state_ret, state_ssd, c, c_ctx, w_in, ret_decay_fwd, ret_decay_bwd, conv_w, conv_b, dt_bias_fwd, dt_bias_bwd, a_log_fwd, a_log_bwd, d_skip, ssd_norm_w, w_out, ln1_g, ln1_b, w_gate, w_up, w_down, ln2_g, ln2_b, w_ada, b_ada):
    depth = w_in.shape[0]
    assert depth == 1, "single trunk layer"
    bp, lp, _ = x_prompt.shape
    bs, ls, _ = x_sample.shape
    assert lp % CHUNK == 0 and ls % CHUNK == 0 and ls % GRID_W == 0 and D_FF % FF_BLK == 0


    convw = jnp.zeros((SUBLANES, CONV_CH), F32).at[:CONV_W].set(conv_w[0])
    convb = conv_b[0][None, :]
    dt_bias = jnp.concatenate([dt_bias_fwd[0], dt_bias_bwd[0]])
    a_log = jnp.concatenate([a_log_fwd[0], a_log_bwd[0]])
    ret_decay = jnp.concatenate([ret_decay_fwd[0], ret_decay_bwd[0]])
    hp = jnp.zeros((SUBLANES, LANES), F32)
    hp = hp.at[0, :N_DT].set(dt_bias).at[1, :N_DT].set(a_log)
    hp = hp.at[2, :2 * RET_HEADS].set(ret_decay).at[3, :SSD_HEADS].set(d_skip[0])
    hpc = jnp.zeros((N_DT, LANES), F32).at[:, 0].set(dt_bias).at[:, 1].set(a_log)
    mod, w_in_b, w_out_b, w_gate_b, w_up_b, w_down_b = _prep_call(
        c, c_ctx[None, :], w_ada[0], b_ada[0][None, :],
        jnp.swapaxes(w_in[0], 0, 1), [w_out[0], w_gate[0], w_up[0], w_down[0]])
    weights = (w_in_b, convw, convb, hp, hpc, ssd_norm_w[0][None, :],
               w_out_b, ln1_g[0][None, :], ln1_b[0][None, :],
               w_gate_b, w_up_b, w_down_b, ln2_g[0][None, :], ln2_b[0][None, :])

    yp, new_ret, new_ssd = _layer_call(x_prompt, mod, 1, False, weights, None, True, False, "layer_ctx")
    states = (state_ret, jnp.swapaxes(state_ssd, -1, -2))
    (ys,) = _layer_call(x_sample, mod, 0, True, weights, states, False, True, "layer_lat")
    return (yp, ys, new_ret, jnp.swapaxes(new_ssd, -1, -2))
```

```python
import functools
import math

import jax
import jax.numpy as jnp
from jax import lax
from jax.experimental import pallas as pl
from jax.experimental.pallas import tpu as pltpu

F32 = jnp.float32
BF16 = jnp.bfloat16

D_MODEL = 1024
RET_W = 512
RET_HEADS = 4
RET_HD = 128
SSD_W = 512
SSD_HD = 64
SSD_HEADS = 8
SSD_GROUPS = 2
SSD_STATE = 128
HPG = SSD_HEADS // SSD_GROUPS
GROUP_W = HPG * SSD_HD
CONV_W = 5
CONV_CH = SSD_W + 2 * SSD_GROUPS * SSD_STATE
D_FF = 2816
GRID_W = 64
GRID_SHIFT = 6
ROPE_BASE = 10000.0
EPS = 1e-6
ALPHA = 2.0 ** 0.25
MAIN_COLS = 4 * RET_W + SSD_W + CONV_CH
XBC_COL0 = 4 * RET_W + SSD_W
N_DT = 2 * SSD_HEADS

CHUNK = 256
HALO = 8
FF_BLK = 256
MOD_ROWS = 8
CAST_STEPS = 8
CAST_WT_ROWS = 512
LANES = 128
SUBLANES = 8
XBC_HALF = CONV_CH // 2
VMEM_LIMIT = 62 * 1024 * 1024


def _dot(a, b):
    return jnp.dot(a, b, preferred_element_type=F32)


def _dot_nt(a, b):
    return lax.dot_general(a, b, (((1,), (1,)), ((), ())), preferred_element_type=F32)


def _silu(x):
    return x * jax.nn.sigmoid(x)


def _softplus(x):
    return jnp.maximum(x, 0.0) + jnp.log1p(jnp.exp(-jnp.abs(x)))


def _layer_norm(y, g, b):
    mu = jnp.mean(y, axis=-1, keepdims=True)
    yc = y - mu
    var = jnp.mean(yc * yc, axis=-1, keepdims=True)
    return yc * lax.rsqrt(var + EPS) * g + b


def _cumsum(x, axis):
    n = x.shape[axis]
    idx = lax.broadcasted_iota(jnp.int32, x.shape, axis)
    s = 1
    while s < n:
        x = x + jnp.where(idx >= s, pltpu.roll(x, s, axis), 0.0)
        s *= 2
    return x


def _expand4(cols, lane):
    a = jnp.where(lane < SSD_HD, cols[0], cols[1])
    b = jnp.where(lane < SSD_HD, cols[2], cols[3])
    return jnp.concatenate([a, b], axis=1)


def _interleave(a, b):
    ia = ib = 0
    while ia < len(a) or ib < len(b):
        if ib >= len(b) or (ia < len(a) and ia * len(b) <= ib * len(a)):
            a[ia]()
            ia += 1
        else:
            b[ib]()
            ib += 1


def _prep_kernel(c_ref, cctx_ref, wada_ref, bada_ref, wt_ref, *refs, wt_rows):
    n = (len(refs) - 2) // 2
    srcs, mod_ref, wt_out, dsts = refs[:n], refs[n], refs[n + 1], refs[n + 2:]
    cond = jnp.concatenate([c_ref[...], jnp.broadcast_to(cctx_ref[...], (MOD_ROWS, D_MODEL))], axis=0)
    mod_ref[...] = _dot(_silu(cond).astype(BF16), wada_ref[...].astype(BF16)) + bada_ref[...]
    blk = wt_ref.shape[0]
    row = lax.broadcasted_iota(jnp.int32, (blk, 1), 0) + pl.program_id(0) * blk
    wt_out[...] = jnp.where(row < wt_rows, wt_ref[...], 0.0).T.astype(BF16)
    for src, dst in zip(srcs, dsts):
        dst[...] = src[...].astype(BF16)


def _prep_call(c, c_ctx, w_ada, b_ada, wt, ws):
    assert c.shape == (MOD_ROWS, D_MODEL), c.shape
    for w in ws:
        assert w.shape[0] % (CAST_STEPS * 16) == 0, w.shape
    assert wt.shape[0] <= CAST_STEPS * CAST_WT_ROWS
    n_mod = w_ada.shape[1]
    ada_cols = n_mod // CAST_STEPS
    assert ada_cols % LANES == 0
    specs = [pl.BlockSpec((w.shape[0] // CAST_STEPS, w.shape[1]), lambda i: (i, 0)) for w in ws]
    return pl.pallas_call(
        functools.partial(_prep_kernel, wt_rows=wt.shape[0]),
        grid=(CAST_STEPS,),
        in_specs=[pl.BlockSpec((MOD_ROWS, D_MODEL), lambda i: (0, 0)),
                  pl.BlockSpec((1, D_MODEL), lambda i: (0, 0)),
                  pl.BlockSpec((D_MODEL, ada_cols), lambda i: (0, i)),
                  pl.BlockSpec((1, ada_cols), lambda i: (0, i)),
                  pl.BlockSpec((CAST_WT_ROWS, wt.shape[1]), lambda i: (i, 0))] + specs,
        out_specs=[pl.BlockSpec((2 * MOD_ROWS, ada_cols), lambda i: (0, i)),
                   pl.BlockSpec((wt.shape[1], CAST_WT_ROWS), lambda i: (0, i))] + specs,
        out_shape=[jax.ShapeDtypeStruct((2 * MOD_ROWS, n_mod), F32),
                   jax.ShapeDtypeStruct((wt.shape[1], -(-wt.shape[0] // LANES) * LANES), BF16)]
        + [jax.ShapeDtypeStruct(w.shape, BF16) for w in ws],
        compiler_params=pltpu.CompilerParams(dimension_semantics=("arbitrary",),
                                             vmem_limit_bytes=VMEM_LIMIT),
        name="prep_mod_and_weights",
    )(c, c_ctx, w_ada, b_ada, wt, *ws)


RV_TAIL_F, RV_TAIL_B, RV_CROSS_F, RV_CROSS_B = range(4)
BIG_WEIGHTS = (0, 6, 9, 10, 11)


def _layer_kernel(*refs, L, nb, mod_per_seq, has_state, use_rope, emit_state):
    nc = L // CHUNK
    C = CHUNK
    cross = has_state or nc > 1
    it = iter(refs)
    x_ref, xprev_ref, xnext_ref, mod_ref, wmain_hbm = (next(it) for _ in range(5))
    convw_ref, convb_ref, hp_ref, hpc_ref, normw_ref = (next(it) for _ in range(5))
    wout_hbm, ln1g_ref, ln1b_ref = (next(it) for _ in range(3))
    wg_hbm, wu_hbm, wd_hbm, ln2g_ref, ln2b_ref = (next(it) for _ in range(5))
    if has_state:
        sret0_ref, sssd0_ref = next(it), next(it)
    out_ref = next(it)
    if emit_state:
        nret_ref, nssd_ref = next(it), next(it)
    (q_s, kT_s, v_s, g_s, z_s, stage_s, xs_s, bT_s, c_s, inccol_s, exccol_s, cfcb_s, dec_s,
     rowarg_s, y_s, mix_s, rloc_s, sloc_s, rent_s, sent_s, wdec_s, rvec_s, x1_s, hid_s, wdtr_s) = (
         next(it) for _ in range(25))
    wmain_ref, wout_ref, wg_ref, wu_ref, wd_ref, wsem = (next(it) for _ in range(6))
    if use_rope:
        cos_s, sin_s = next(it), next(it)

    seq = pl.program_id(0)
    step = pl.program_id(1)

    if mod_per_seq:
        row_mix = jnp.minimum(seq, nb - 1)
        row_ffn = jnp.minimum(jnp.where(step > nc, seq, jnp.maximum(seq - 1, 0)), nb - 1)
    else:
        row_mix = row_ffn = 0

    def mod_row(row, k):
        return mod_ref[pl.ds(row, 1), k * D_MODEL:(k + 1) * D_MODEL]

    big_weights = ((wmain_hbm, wmain_ref), (wout_hbm, wout_ref), (wg_hbm, wg_ref), (wu_hbm, wu_ref),
                   (wd_hbm, wd_ref))

    def weight_copy(i):
        return pltpu.make_async_copy(big_weights[i][0], big_weights[i][1], wsem.at[i])

    hp = hp_ref[...]
    dt_bias_row = hp[0:1, :]
    nega_row = -jnp.exp(hp[1:2, :])
    lg_row = -_softplus(-hp[2:3, :])
    dskip_row = hp[3:4, :]
    hpc = hpc_ref[...]
    dt_bias_col = hpc[:, 0:1]
    nega_col = -jnp.exp(hpc[:, 1:2])

    lane = lax.broadcasted_iota(jnp.int32, (1, LANES), 1)

    def ret_decays(hd):
        lgf = lg_row[:, hd:hd + 1]
        lgb = lg_row[:, RET_HEADS + hd:RET_HEADS + hd + 1]
        return lgf, lgb

    def group_heads(gi):
        return [gi * HPG + k for k in range(HPG)]

    def expand_f(arr, gi):
        return _expand4([arr[:, hh:hh + 1] for hh in group_heads(gi)], lane)

    def expand_b(arr, gi):
        return _expand4([arr[:, SSD_HEADS + hh:SSD_HEADS + hh + 1] for hh in group_heads(gi)], lane)

    @pl.when((seq == 0) & (step == 0))
    def _():
        weight_copy(0).start()
        rest = range(1, len(big_weights))
        if nc == 1:
            for i in rest:
                weight_copy(i).start()
        ii = lax.broadcasted_iota(jnp.int32, (C, C), 0)
        jj = lax.broadcasted_iota(jnp.int32, (C, C), 1)
        dmat = (ii - jj).astype(F32)
        irow = lax.broadcasted_iota(jnp.int32, (C, LANES), 0).astype(F32)
        for hd in range(RET_HEADS):
            lgf, lgb = ret_decays(hd)
            wdec_s[hd] = jnp.exp(jnp.where(jj <= ii, dmat * lgf, -dmat * lgb))
            rvec_s[hd, RV_TAIL_F] = jnp.exp((C - 1.0 - irow) * lgf)
            rvec_s[hd, RV_TAIL_B] = jnp.exp(irow * lgb)
            rvec_s[hd, RV_CROSS_F] = jnp.exp((irow + 1.0) * lgf)
            rvec_s[hd, RV_CROSS_B] = jnp.exp((C - irow) * lgb)
        if use_rope:
            ln = lax.broadcasted_iota(jnp.int32, (C, LANES), 1)
            nf = RET_HD // 4
            inv = jnp.exp((ln & (nf - 1)).astype(F32) * (-math.log(ROPE_BASE) / nf))
            for cc in range(nc):
                t = lax.broadcasted_iota(jnp.int32, (C, LANES), 0) + cc * C
                pos = jnp.where((ln & (2 * nf - 1)) < nf, t >> GRID_SHIFT, t & (GRID_W - 1)).astype(F32)
                ang = pos * inv
                cos_s[cc] = jnp.cos(ang)
                sin_s[cc] = jnp.where(ln < RET_HD // 2, -jnp.sin(ang), jnp.sin(ang))
        weight_copy(0).wait()
        if nc > 1:
            for i in rest:
                weight_copy(i).start()
        wdtr_s[...] = wmain_ref[:, MAIN_COLS:MAIN_COLS + LANES].astype(F32).T.astype(BF16)

    def project(c):
        st = {}
        CB = 256
        assert CONV_W == 5 and HALO >= CONV_W // 2

        def v_mod():
            sh1 = mod_row(row_mix, 0)
            sc1 = mod_row(row_mix, 1)
            xe = jnp.concatenate([xprev_ref[0], x_ref[0], xnext_ref[0]], axis=0)
            xm = xe * (1.0 + sc1) + sh1
            st["he"] = xm.astype(BF16)
            st["h"] = xm[HALO:HALO + C].astype(BF16)

        def m_main(name, lo, hi, halo=False):
            def run():
                st[name] = _dot(st["he" if halo else "h"], wmain_ref[:, lo:hi])
            return run

        def m_dt():
            raw_c = _dot(st["h"], wmain_ref[:, MAIN_COLS:MAIN_COLS + LANES])
            st["dtc"] = raw_c + pltpu.roll(raw_c, SSD_HEADS, 1)
            raw_r = _dot_nt(wdtr_s[0:N_DT, :], st["h"])
            st["dtr"] = raw_r + pltpu.roll(raw_r, SSD_HEADS, 0)

        def v_dt():
            dt_c = _softplus(st["dtc"] + dt_bias_row)
            lac = dt_c * nega_row
            inc_col = _cumsum(lac, 0)
            exc_col = inc_col - lac
            tot_col = inc_col[C - 1:C, :]
            inccol_s[c] = inc_col
            exccol_s[c] = exc_col
            dec_s[c, 0:SUBLANES, :] = jnp.broadcast_to(jnp.exp(tot_col), (SUBLANES, LANES))
            if cross:
                cfcb_s[c] = jnp.exp(jnp.where(lane < SSD_HEADS, inc_col, tot_col - exc_col))
            st["sf"] = jnp.exp(tot_col - inc_col) * dt_c
            st["sb"] = jnp.exp(exc_col) * dt_c
            dt_r = _softplus(st["dtr"] + dt_bias_col)
            lar = dt_r * nega_col
            inc_row = _cumsum(lar, 1)
            ldt = jnp.log(dt_r)
            rid = lax.broadcasted_iota(jnp.int32, (N_DT, C), 0)
            rowarg_s[c] = jnp.where(rid < SSD_HEADS, inc_row - ldt, inc_row - lar + ldt)

        def v_stage(hf):
            def run():
                pe = st["pe%d" % hf]
                cs = slice(hf * XBC_HALF, (hf + 1) * XBC_HALF)
                stage_s[0:HALO, cs] = jnp.where(c > 0, pe[0:HALO], 0.0)
                stage_s[HALO:HALO + C, cs] = pe[HALO:HALO + C]
                stage_s[HALO + C:, cs] = jnp.where(c < nc - 1, pe[HALO + C:], 0.0)
            return run

        def conv_block(cb):
            cs = slice(cb * CB, (cb + 1) * CB)
            rows = C + 2 * HALO
            xin = stage_s[:, cs]
            taps = [convw_ref[k:k + 1, cs] * xin for k in range(CONV_W)]
            up = lambda a: pltpu.roll(a, rows - 1, 0)
            down = lambda a: pltpu.roll(a, 1, 0)
            acc = taps[2] + up(taps[3] + up(taps[4])) + down(taps[1] + down(taps[0]))
            return _silu(acc[HALO:HALO + C] + convb_ref[0:1, cs])

        def v_conv_x(cb):
            def run():
                xs_s[c, :, cb * CB:(cb + 1) * CB] = conv_block(cb)
            return run

        def v_conv_b():
            bT_s[c] = conv_block(SSD_W // CB).T.astype(BF16)

        def v_conv_c():
            c_s[c] = conv_block(SSD_W // CB + 1).astype(BF16)

        def rope(a):
            if not use_rope:
                return a
            return a * cos_s[c] + pltpu.roll(a, RET_HD // 2, 1) * sin_s[c]

        def v_q():
            for hd in range(RET_HEADS):
                sl = slice(hd * RET_HD, (hd + 1) * RET_HD)
                q_s[c, :, sl] = rope(st["pq"][:, sl]).astype(BF16)

        def v_k():
            for hd in range(RET_HEADS):
                sl = slice(hd * RET_HD, (hd + 1) * RET_HD)
                kh = rope(st["pk"][:, sl]) * (RET_HD ** -0.5)
                kT_s[c, sl, :] = kh.T.astype(BF16)

        def v_v():
            v_s[c] = st["pv"].astype(BF16)

        def v_g():
            g_s[c] = _silu(st["pg"]).astype(BF16)

        def v_z():
            z_s[c] = _silu(st["pz"]).astype(BF16)

        def m_sloc(gi):
            def run():
                xg = xs_s[c, :, gi * GROUP_W:(gi + 1) * GROUP_W]
                vcat = jnp.concatenate([xg * expand_f(st["sf"], gi), xg * expand_b(st["sb"], gi)],
                                       axis=1).astype(BF16)
                sloc_s[c, gi] = _dot(bT_s[c, gi * SSD_STATE:(gi + 1) * SSD_STATE, :], vcat).astype(sloc_s.dtype)
            return run

        def m_rloc(hd):
            def run():
                sl = slice(hd * RET_HD, (hd + 1) * RET_HD)
                vf = st["pv"][:, sl]
                vcat = jnp.concatenate([vf * rvec_s[hd, RV_TAIL_F], vf * rvec_s[hd, RV_TAIL_B]],
                                       axis=1).astype(BF16)
                rloc_s[c, hd] = _dot(kT_s[c, sl, :], vcat).astype(rloc_s.dtype)
            return run

        m_pe0 = m_main("pe0", XBC_COL0, XBC_COL0 + XBC_HALF, halo=True)
        m_pe1 = m_main("pe1", XBC_COL0 + XBC_HALF, MAIN_COLS, halo=True)
        m_q = m_main("pq", 0, RET_W)
        m_k = m_main("pk", RET_W, 2 * RET_W)
        m_v = m_main("pv", 2 * RET_W, 3 * RET_W)
        m_g = m_main("pg", 3 * RET_W, 4 * RET_W)
        m_z = m_main("pz", 4 * RET_W, XBC_COL0)
        order = [v_mod, m_dt, m_pe0, m_pe1, v_dt, m_q, v_stage(0), v_stage(1), m_k, v_conv_x(0), m_v,
                 v_conv_x(1), v_q, m_g, v_conv_b, v_k, m_z, v_conv_c, v_v,
                 m_rloc(0), m_rloc(1), v_g, m_rloc(2), m_rloc(3), m_sloc(0), m_sloc(1), v_z]
        for piece in order:
            piece()

    def recurrences():
        for hd in range(RET_HEADS):
            lgf, lgb = ret_decays(hd)
            dec_f = jnp.exp(C * lgf)
            dec_b = jnp.exp(C * lgb)
            if has_state:
                ent_f = sret0_ref[0, 0, 0, hd]
                ent_b = sret0_ref[0, 0, 1, hd]
            else:
                ent_f = jnp.zeros((RET_HD, RET_HD), F32)
                ent_b = jnp.zeros((RET_HD, RET_HD), F32)
            for c in range(nc):
                if cross:
                    rent_s[c, hd, :, 0:RET_HD] = ent_f.astype(BF16)
                ent_f = dec_f * ent_f + rloc_s[c, hd, :, 0:RET_HD]
            for c in range(nc - 1, -1, -1):
                if cross:
                    rent_s[c, hd, :, RET_HD:] = ent_b.astype(BF16)
                ent_b = dec_b * ent_b + rloc_s[c, hd, :, RET_HD:]
            if emit_state:
                nret_ref[0, 0, 0, hd] = ent_f
                nret_ref[0, 0, 1, hd] = ent_b

        for gi in range(SSD_GROUPS):
            heads = group_heads(gi)
            if has_state:
                ent_f = jnp.concatenate([sssd0_ref[0, 0, 0, hh] for hh in heads], axis=0).T
                ent_b = jnp.concatenate([sssd0_ref[0, 0, 1, hh] for hh in heads], axis=0).T
            else:
                ent_f = jnp.zeros((SSD_STATE, GROUP_W), F32)
                ent_b = jnp.zeros((SSD_STATE, GROUP_W), F32)
            for c in range(nc):
                if cross:
                    sent_s[c, gi, :, 0:GROUP_W] = ent_f.astype(BF16)
                ent_f = expand_f(dec_s[c, 0:1, :], gi) * ent_f + sloc_s[c, gi, :, 0:GROUP_W]
            for c in range(nc - 1, -1, -1):
                if cross:
                    sent_s[c, gi, :, GROUP_W:] = ent_b.astype(BF16)
                ent_b = expand_b(dec_s[c, 0:1, :], gi) * ent_b + sloc_s[c, gi, :, GROUP_W:]
            if emit_state:
                ent_ft = ent_f.T
                ent_bt = ent_b.T
                for k, hh in enumerate(heads):
                    nssd_ref[0, 0, 0, hh] = ent_ft[k * SSD_HD:(k + 1) * SSD_HD, :]
                    nssd_ref[0, 0, 1, hh] = ent_bt[k * SSD_HD:(k + 1) * SSD_HD, :]

    def emit_pieces(c, slot):
        rsl = [slice(hd * RET_HD, (hd + 1) * RET_HD) for hd in range(RET_HEADS)]
        gsl = [slice(gi * SSD_STATE, (gi + 1) * SSD_STATE) for gi in range(SSD_GROUPS)]
        xsl = [slice(gi * GROUP_W, (gi + 1) * GROUP_W) for gi in range(SSD_GROUPS)]
        st = {}

        def scores():
            ii = lax.broadcasted_iota(jnp.int32, (C, C), 0)
            jj = lax.broadcasted_iota(jnp.int32, (C, C), 1)
            st["causal"] = jj <= ii
            qs = [q_s[c, :, sl] for sl in rsl]
            cms = [c_s[c, :, sl] for sl in gsl]
            st["sc_s"] = [_dot(cms[gi], bT_s[c, gsl[gi], :]) for gi in range(SSD_GROUPS)]
            st["sc_r"] = [_dot(qs[hd], kT_s[c, rsl[hd], :]) for hd in range(RET_HEADS)]
            if cross:
                st["yc_s"] = [_dot(cms[gi], sent_s[c, gi]) for gi in range(SSD_GROUPS)]
                st["yc_r"] = [_dot(qs[hd], rent_s[c, hd]) for hd in range(RET_HEADS)]
            st["inc_col"] = inccol_s[c]
            st["exc_col"] = exccol_s[c]
            st["rowarg"] = rowarg_s[c]

        def ssd_head(gi, k):
            def run():
                hh = gi * HPG + k
                hb = SSD_HEADS + hh
                arg = jnp.where(st["causal"],
                                st["inc_col"][:, hh:hh + 1] - st["rowarg"][hh:hh + 1, :],
                                st["rowarg"][hb:hb + 1, :] - st["exc_col"][:, hb:hb + 1])
                m = (st["sc_s"][gi] * jnp.exp(arg)).astype(BF16)
                xh = xs_s[c, :, hh * SSD_HD:(hh + 1) * SSD_HD].astype(BF16)
                y_s[:, hh * SSD_HD:(hh + 1) * SSD_HD] = _dot(m, xh)
            return run

        def ssd_group(gi):
            def run():
                xg = xs_s[c, :, xsl[gi]]
                yg = y_s[:, xsl[gi]] + expand_f(dskip_row, gi) * xg
                if cross:
                    cfcb = cfcb_s[c]
                    yc = st["yc_s"][gi]
                    yg = yg + expand_f(cfcb, gi) * yc[:, 0:GROUP_W] + expand_b(cfcb, gi) * yc[:, GROUP_W:]
                y_s[:, xsl[gi]] = yg
            return run

        def ret_head(hd):
            def run():
                m = (st["sc_r"][hd] * wdec_s[hd]).astype(BF16)
                o = _dot(m, v_s[c, :, rsl[hd]])
                if cross:
                    yc = st["yc_r"][hd]
                    o = o + rvec_s[hd, RV_CROSS_F] * yc[:, 0:RET_HD] + rvec_s[hd, RV_CROSS_B] * yc[:, RET_HD:]
                o = o * lax.rsqrt(jnp.mean(o * o, axis=-1, keepdims=True) + EPS)
                mix_s[:, rsl[hd]] = (g_s[c, :, rsl[hd]].astype(F32) * o).astype(BF16)
            return run

        def ssd_norm():
            yz = y_s[...] * z_s[c].astype(F32)
            yn = yz * lax.rsqrt(jnp.mean(yz * yz, axis=-1, keepdims=True) + EPS) * normw_ref[...]
            mix_s[:, RET_W:] = yn.astype(BF16)

        def out_proj():
            g1 = mod_row(row_mix, 2)
            y = ALPHA * x_ref[0] + g1 * _dot(mix_s[...], wout_ref[...])
            x1_s[slot] = _layer_norm(y, ln1g_ref[...], ln1b_ref[...])

        pieces = [scores]
        for gi in range(SSD_GROUPS):
            pieces += [ssd_head(gi, k) for k in range(HPG)] + [ssd_group(gi)]
        pieces += [ret_head(hd) for hd in range(RET_HEADS)] + [ssd_norm, out_proj]
        return pieces

    def ffn_pieces(slot):
        st = {}

        def start():
            sh2 = mod_row(row_ffn, 3)
            sc2 = mod_row(row_ffn, 4)
            st["h2"] = (x1_s[slot] * (1.0 + sc2) + sh2).astype(BF16)

        def hidden(j):
            def run():
                js = slice(j * FF_BLK, (j + 1) * FF_BLK)
                h2 = st["h2"]
                hid_s[:, js] = (_silu(_dot(h2, wg_ref[:, js])) * _dot(h2, wu_ref[:, js])).astype(BF16)
            return run

        def finish():
            g2 = mod_row(row_ffn, 5)
            x1_s[slot] = ALPHA * x1_s[slot] + g2 * _dot(hid_s[...], wd_ref[...])

        return [start] + [hidden(j) for j in range(D_FF // FF_BLK)] + [finish]

    def ln2(slot):
        def run():
            out_ref[0] = _layer_norm(x1_s[slot], ln2g_ref[...], ln2b_ref[...])
        return run

    def run_all(pieces):
        for p in pieces:
            p()

    def fused(ffn, others):
        _interleave(ffn[:-1], others)
        ffn[-1]()

    if nc == 1:
        slot = seq & 1

        @pl.when(seq == 0)
        def _():
            project(0)
            for i in range(1, len(big_weights)):
                weight_copy(i).wait()
            run_all([recurrences] + emit_pieces(0, slot))
            x1_s[1] = jnp.zeros((C, D_MODEL), F32)

        @pl.when((seq > 0) & (seq < nb))
        def _():
            project(0)
            fused(ffn_pieces(1 - slot), [ln2(slot), recurrences] + emit_pieces(0, slot))

        @pl.when(seq == nb)
        def _():
            _interleave(ffn_pieces(1 - slot), [ln2(slot)])

        @pl.when(seq == nb + 1)
        def _():
            ln2(slot)()
        return

    @pl.when((step < nc) & (seq < nb))
    def _():
        project(step)

    @pl.when((step == nc) & (seq < nb))
    def _():
        recurrences()

    kk = step - nc
    slot = (seq * nc + kk) & 1
    first = (seq == 0) & (step == nc)

    @pl.when(first)
    def _():
        for i in range(1, len(big_weights)):
            weight_copy(i).wait()
        run_all(emit_pieces(kk, slot))
        x1_s[1] = jnp.zeros((C, D_MODEL), F32)

    @pl.when((step >= nc) & (seq < nb) & jnp.logical_not(first))
    def _():
        fused(ffn_pieces(1 - slot), [ln2(slot)] + emit_pieces(kk, slot))

    @pl.when((seq == nb) & (step == nc))
    def _():
        _interleave(ffn_pieces(1 - slot), [ln2(slot)])

    @pl.when((seq == nb) & (step == nc + 1))
    def _():
        ln2(slot)()


def _const_spec(shape):
    nd = len(shape)
    return pl.BlockSpec(shape, lambda b, s: (0,) * nd, pipeline_mode=pl.Buffered(1))


def _layer_call(x, mod, mod_block, mod_per_seq, weights, states, emit_state, use_rope, name):
    nb, L, _ = x.shape
    nc = L // CHUNK
    hpc_blocks = CHUNK // HALO
    has_state = states is not None
    last = nb - 1

    def chunk_of(b, s):
        return jnp.where(b > last, nc - 1, jnp.where(s < nc, s, s - nc))

    def halo_chunk(b, s):
        return jnp.where(b > last, nc - 1, jnp.minimum(s, nc - 1))

    def seq_of(b):
        return jnp.minimum(b, last)

    def out_map(b, s):
        emits = b + 1 if nc == 1 else b * nc + jnp.maximum(s - nc, -1) + 1
        blk = jnp.clip(emits - 3, 0, nb * nc - 1)
        return (blk // nc, blk % nc, 0)

    assert not mod_per_seq or nb <= MOD_ROWS
    in_specs = [
        pl.BlockSpec((1, CHUNK, D_MODEL), lambda b, s: (seq_of(b), chunk_of(b, s), 0)),
        pl.BlockSpec((1, HALO, D_MODEL),
                     lambda b, s: (seq_of(b), jnp.maximum(halo_chunk(b, s) * hpc_blocks - 1, 0), 0)),
        pl.BlockSpec((1, HALO, D_MODEL),
                     lambda b, s: (seq_of(b), jnp.minimum((halo_chunk(b, s) + 1) * hpc_blocks, nc * hpc_blocks - 1), 0)),
        pl.BlockSpec((MOD_ROWS, 6 * D_MODEL), lambda b, s: (mod_block, 0)),
    ] + [pl.BlockSpec(memory_space=pl.ANY) if i in BIG_WEIGHTS else _const_spec(w.shape)
         for i, w in enumerate(weights)]
    args = [x, x, x, mod] + list(weights)
    ret_block = (1, 1, 2, RET_HEADS, RET_HD, RET_HD)
    ssd_block = (1, 1, 2, SSD_HEADS, SSD_HD, SSD_STATE)
    state_map = lambda b, s: (seq_of(b), 0, 0, 0, 0, 0)
    if has_state:
        in_specs += [pl.BlockSpec(ret_block, state_map, pipeline_mode=pl.Buffered(1)),
                     pl.BlockSpec(ssd_block, state_map, pipeline_mode=pl.Buffered(1))]
        args += list(states)
    out_shape = [jax.ShapeDtypeStruct((nb, L, D_MODEL), F32)]
    out_specs = [pl.BlockSpec((1, CHUNK, D_MODEL), out_map)]
    if emit_state:
        out_shape += [jax.ShapeDtypeStruct((nb,) + ret_block[1:], F32),
                      jax.ShapeDtypeStruct((nb,) + ssd_block[1:], F32)]
        out_specs += [pl.BlockSpec(ret_block, state_map), pl.BlockSpec(ssd_block, state_map)]
    state_dt = F32 if emit_state else BF16
    scratch = [
        pltpu.VMEM((nc, CHUNK, RET_W), BF16),
        pltpu.VMEM((nc, RET_W, CHUNK), BF16),
        pltpu.VMEM((nc, CHUNK, RET_W), BF16),
        pltpu.VMEM((nc, CHUNK, RET_W), BF16),
        pltpu.VMEM((nc, CHUNK, SSD_W), BF16),
        pltpu.VMEM((CHUNK + 2 * HALO, CONV_CH), F32),
        pltpu.VMEM((nc, CHUNK, SSD_W), F32),
        pltpu.VMEM((nc, SSD_GROUPS * SSD_STATE, CHUNK), BF16),
        pltpu.VMEM((nc, CHUNK, SSD_GROUPS * SSD_STATE), BF16),
        pltpu.VMEM((nc, CHUNK, LANES), F32),
        pltpu.VMEM((nc, CHUNK, LANES), F32),
        pltpu.VMEM((nc, CHUNK, LANES), F32),
        pltpu.VMEM((nc, 4 * SUBLANES, LANES), F32),
        pltpu.VMEM((nc, N_DT, CHUNK), F32),
        pltpu.VMEM((CHUNK, SSD_W), F32),
        pltpu.VMEM((CHUNK, 2 * RET_W), BF16),
        pltpu.VMEM((nc, RET_HEADS, RET_HD, 2 * RET_HD), state_dt),
        pltpu.VMEM((nc, SSD_GROUPS, SSD_STATE, 2 * GROUP_W), state_dt),
        pltpu.VMEM((nc, RET_HEADS, RET_HD, 2 * RET_HD), BF16),
        pltpu.VMEM((nc, SSD_GROUPS, SSD_STATE, 2 * GROUP_W), BF16),
        pltpu.VMEM((RET_HEADS, CHUNK, CHUNK), F32),
        pltpu.VMEM((RET_HEADS, 4, CHUNK, LANES), F32),
        pltpu.VMEM((2, CHUNK, D_MODEL), F32),
        pltpu.VMEM((CHUNK, D_FF), BF16),
        pltpu.VMEM((LANES, D_MODEL), BF16),
    ]
    scratch += [pltpu.VMEM(weights[i].shape, BF16) for i in BIG_WEIGHTS]
    scratch += [pltpu.SemaphoreType.DMA((len(BIG_WEIGHTS),))]
    if use_rope:
        scratch += [pltpu.VMEM((nc, CHUNK, LANES), F32)] * 2
    kern = functools.partial(_layer_kernel, L=L, nb=nb, mod_per_seq=mod_per_seq, has_state=has_state,
                             use_rope=use_rope, emit_state=emit_state)
    return pl.pallas_call(
        kern,
        grid=(nb + 2, 1) if nc == 1 else (nb + 1, 2 * nc),
        in_specs=in_specs,
        out_specs=out_specs,
        out_shape=out_shape,
        scratch_shapes=scratch,
        compiler_params=pltpu.CompilerParams(dimension_semantics=("arbitrary", "arbitrary"),
                                             vmem_limit_bytes=VMEM_LIMIT),
        name=name,
    )(*args)


def kernel(x_prompt, x_sample, state_ret, state_ssd, c, c_ctx, w_in, ret_decay_fwd, ret_decay_bwd, conv_w, conv_b, dt_bias_fwd, dt_bias_bwd, a_log_fwd, a_log_bwd, d_skip, ssd_norm_w, w_out, ln1_g, ln1_b, w_gate, w_up, w_down, ln2_g, ln2_b, w_ada, b_ada):
    depth = w_in.shape[0]
    assert depth == 1, "single trunk layer"
    bp, lp, _ = x_prompt.shape
    bs, ls, _ = x_sample.shape
    assert lp % CHUNK == 0 and ls % CHUNK == 0 and ls % GRID_W == 0 and D_FF % FF_BLK == 0


    convw = jnp.zeros((SUBLANES, CONV_CH), F32).at[:CONV_W].set(conv_w[0])
    convb = conv_b[0][None, :]
    dt_bias = jnp.concatenate([dt_bias_fwd[0], dt_bias_bwd[0]])
    a_log = jnp.concatenate([a_log_fwd[0], a_log_bwd[0]])
    ret_decay = jnp.concatenate([ret_decay_fwd[0], ret_decay_bwd[0]])
    hp = jnp.zeros((SUBLANES, LANES), F32)
    hp = hp.at[0, :N_DT].set(dt_bias).at[1, :N_DT].set(a_log)
    hp = hp.at[2, :2 * RET_HEADS].set(ret_decay).at[3, :SSD_HEADS].set(d_skip[0])
    hpc = jnp.zeros((N_DT, LANES), F32).at[:, 0].set(dt_bias).at[:, 1].set(a_log)
    mod, w_in_b, w_out_b, w_gate_b, w_up_b, w_down_b = _prep_call(
        c, c_ctx[None, :], w_ada[0], b_ada[0][None, :],
        jnp.swapaxes(w_in[0], 0, 1), [w_out[0], w_gate[0], w_up[0], w_down[0]])
    weights = (w_in_b, convw, convb, hp, hpc, ssd_norm_w[0][None, :],
               w_out_b, ln1_g[0][None, :], ln1_b[0][None, :],
               w_gate_b, w_up_b, w_down_b, ln2_g[0][None, :], ln2_b[0][None, :])

    yp, new_ret, new_ssd = _layer_call(x_prompt, mod, 1, False, weights, None, True, False, "layer_ctx")
    states = (state_ret, jnp.swapaxes(state_ssd, -1, -2))
    (ys,) = _layer_call(x_sample, mod, 0, True, weights, states, False, True, "layer_lat")
    return (yp, ys, new_ret, jnp.swapaxes(new_ssd, -1, -2))
```

```python
import functools
import math

import jax
import jax.numpy as jnp
from jax import lax
from jax.experimental import pallas as pl
from jax.experimental.pallas import tpu as pltpu

F32 = jnp.float32
BF16 = jnp.bfloat16

D_MODEL = 1024
RET_W = 512
RET_HEADS = 4
RET_HD = 128
SSD_W = 512
SSD_HD = 64
SSD_HEADS = 8
SSD_GROUPS = 2
SSD_STATE = 128
HPG = SSD_HEADS // SSD_GROUPS
GROUP_W = HPG * SSD_HD
CONV_W = 5
CONV_CH = SSD_W + 2 * SSD_GROUPS * SSD_STATE
D_FF = 2816
GRID_W = 64
GRID_SHIFT = 6
ROPE_BASE = 10000.0
EPS = 1e-6
ALPHA = 2.0 ** 0.25
MAIN_COLS = 4 * RET_W + SSD_W + CONV_CH
XBC_COL0 = 4 * RET_W + SSD_W
N_DT = 2 * SSD_HEADS

CHUNK = 256
HALO = 8
FF_BLK = 256
MOD_ROWS = 8
CAST_STEPS = 8
CAST_WT_ROWS = 512
LANES = 128
SUBLANES = 8
XBC_HALF = CONV_CH // 2
VMEM_LIMIT = 62 * 1024 * 1024


def _dot(a, b):
    return jnp.dot(a, b, preferred_element_type=F32)


def _dot_nt(a, b):
    return lax.dot_general(a, b, (((1,), (1,)), ((), ())), preferred_element_type=F32)


def _silu(x):
    return x * jax.nn.sigmoid(x)


def _softplus(x):
    return jnp.maximum(x, 0.0) + jnp.log1p(jnp.exp(-jnp.abs(x)))


def _layer_norm(y, g, b):
    mu = jnp.mean(y, axis=-1, keepdims=True)
    yc = y - mu
    var = jnp.mean(yc * yc, axis=-1, keepdims=True)
    return yc * lax.rsqrt(var + EPS) * g + b


def _cumsum(x, axis):
    n = x.shape[axis]
    idx = lax.broadcasted_iota(jnp.int32, x.shape, axis)
    s = 1
    while s < n:
        x = x + jnp.where(idx >= s, pltpu.roll(x, s, axis), 0.0)
        s *= 2
    return x


def _expand4(cols, lane):
    a = jnp.where(lane < SSD_HD, cols[0], cols[1])
    b = jnp.where(lane < SSD_HD, cols[2], cols[3])
    return jnp.concatenate([a, b], axis=1)


def _interleave(a, b):
    ia = ib = 0
    while ia < len(a) or ib < len(b):
        if ib >= len(b) or (ia < len(a) and ia * len(b) <= ib * len(a)):
            a[ia]()
            ia += 1
        else:
            b[ib]()
            ib += 1


def _prep_kernel(c_ref, cctx_ref, wada_ref, bada_ref, wt_ref, *refs, wt_rows):
    n = (len(refs) - 2) // 2
    srcs, mod_ref, wt_out, dsts = refs[:n], refs[n], refs[n + 1], refs[n + 2:]
    cond = jnp.concatenate([c_ref[...], jnp.broadcast_to(cctx_ref[...], (MOD_ROWS, D_MODEL))], axis=0)
    mod_ref[...] = _dot(_silu(cond).astype(BF16), wada_ref[...].astype(BF16)) + bada_ref[...]
    blk = wt_ref.shape[0]
    row = lax.broadcasted_iota(jnp.int32, (blk, 1), 0) + pl.program_id(0) * blk
    wt_out[...] = jnp.where(row < wt_rows, wt_ref[...], 0.0).T.astype(BF16)
    for src, dst in zip(srcs, dsts):
        dst[...] = src[...].astype(BF16)


def _prep_call(c, c_ctx, w_ada, b_ada, wt, ws):
    assert c.shape == (MOD_ROWS, D_MODEL), c.shape
    for w in ws:
        assert w.shape[0] % (CAST_STEPS * 16) == 0, w.shape
    assert wt.shape[0] <= CAST_STEPS * CAST_WT_ROWS
    n_mod = w_ada.shape[1]
    ada_cols = n_mod // CAST_STEPS
    assert ada_cols % LANES == 0
    specs = [pl.BlockSpec((w.shape[0] // CAST_STEPS, w.shape[1]), lambda i: (i, 0)) for w in ws]
    return pl.pallas_call(
        functools.partial(_prep_kernel, wt_rows=wt.shape[0]),
        grid=(CAST_STEPS,),
        in_specs=[pl.BlockSpec((MOD_ROWS, D_MODEL), lambda i: (0, 0)),
                  pl.BlockSpec((1, D_MODEL), lambda i: (0, 0)),
                  pl.BlockSpec((D_MODEL, ada_cols), lambda i: (0, i)),
                  pl.BlockSpec((1, ada_cols), lambda i: (0, i)),
                  pl.BlockSpec((CAST_WT_ROWS, wt.shape[1]), lambda i: (i, 0))] + specs,
        out_specs=[pl.BlockSpec((2 * MOD_ROWS, ada_cols), lambda i: (0, i)),
                   pl.BlockSpec((wt.shape[1], CAST_WT_ROWS), lambda i: (0, i))] + specs,
        out_shape=[jax.ShapeDtypeStruct((2 * MOD_ROWS, n_mod), F32),
                   jax.ShapeDtypeStruct((wt.shape[1], -(-wt.shape[0] // LANES) * LANES), BF16)]
        + [jax.ShapeDtypeStruct(w.shape, BF16) for w in ws],
        compiler_params=pltpu.CompilerParams(dimension_semantics=("arbitrary",),
                                             vmem_limit_bytes=VMEM_LIMIT),
        name="prep_mod_and_weights",
    )(c, c_ctx, w_ada, b_ada, wt, *ws)


RV_TAIL_F, RV_TAIL_B, RV_CROSS_F, RV_CROSS_B = range(4)
BIG_WEIGHTS = (0, 6, 9, 10, 11)


def _layer_kernel(*refs, L, nb, mod_per_seq, has_state, use_rope, emit_state):
    nc = L // CHUNK
    C = CHUNK
    cross = has_state or nc > 1
    it = iter(refs)
    x_ref, xprev_ref, xnext_ref, mod_ref, wmain_hbm = (next(it) for _ in range(5))
    convw_ref, convb_ref, hp_ref, hpc_ref, normw_ref = (next(it) for _ in range(5))
    wout_hbm, ln1g_ref, ln1b_ref = (next(it) for _ in range(3))
    wg_hbm, wu_hbm, wd_hbm, ln2g_ref, ln2b_ref = (next(it) for _ in range(5))
    if has_state:
        sret0_ref, sssd0_ref = next(it), next(it)
    out_ref = next(it)
    if emit_state:
        nret_ref, nssd_ref = next(it), next(it)
    (q_s, kT_s, v_s, g_s, z_s, stage_s, xs_s, bT_s, c_s, inccol_s, exccol_s, cfcb_s, dec_s,
     rowarg_s, y_s, mix_s, rloc_s, sloc_s, rent_s, sent_s, wdec_s, rvec_s, x1_s, hid_s, wdtr_s) = (
         next(it) for _ in range(25))
    wmain_ref, wout_ref, wg_ref, wu_ref, wd_ref, wsem = (next(it) for _ in range(6))
    if use_rope:
        cos_s, sin_s = next(it), next(it)

    seq = pl.program_id(0)
    step = pl.program_id(1)

    if mod_per_seq:
        row_mix = jnp.minimum(seq, nb - 1)
        row_ffn = jnp.minimum(jnp.where(step > nc, seq, jnp.maximum(seq - 1, 0)), nb - 1)
    else:
        row_mix = row_ffn = 0

    def mod_row(row, k):
        return mod_ref[pl.ds(row, 1), k * D_MODEL:(k + 1) * D_MODEL]

    big_weights = ((wmain_hbm, wmain_ref), (wout_hbm, wout_ref), (wg_hbm, wg_ref), (wu_hbm, wu_ref),
                   (wd_hbm, wd_ref))

    def weight_copy(i):
        return pltpu.make_async_copy(big_weights[i][0], big_weights[i][1], wsem.at[i])

    hp = hp_ref[...]
    dt_bias_row = hp[0:1, :]
    nega_row = -jnp.exp(hp[1:2, :])
    lg_row = -_softplus(-hp[2:3, :])
    dskip_row = hp[3:4, :]
    hpc = hpc_ref[...]
    dt_bias_col = hpc[:, 0:1]
    nega_col = -jnp.exp(hpc[:, 1:2])

    lane = lax.broadcasted_iota(jnp.int32, (1, LANES), 1)

    def ret_decays(hd):
        lgf = lg_row[:, hd:hd + 1]
        lgb = lg_row[:, RET_HEADS + hd:RET_HEADS + hd + 1]
        return lgf, lgb

    def group_heads(gi):
        return [gi * HPG + k for k in range(HPG)]

    def expand_f(arr, gi):
        return _expand4([arr[:, hh:hh + 1] for hh in group_heads(gi)], lane)

    def expand_b(arr, gi):
        return _expand4([arr[:, SSD_HEADS + hh:SSD_HEADS + hh + 1] for hh in group_heads(gi)], lane)

    @pl.when((seq == 0) & (step == 0))
    def _():
        weight_copy(0).start()
        rest = range(1, len(big_weights))
        if nc == 1:
            for i in rest:
                weight_copy(i).start()
        ii = lax.broadcasted_iota(jnp.int32, (C, C), 0)
        jj = lax.broadcasted_iota(jnp.int32, (C, C), 1)
        dmat = (ii - jj).astype(F32)
        irow = lax.broadcasted_iota(jnp.int32, (C, LANES), 0).astype(F32)
        for hd in range(RET_HEADS):
            lgf, lgb = ret_decays(hd)
            wdec_s[hd] = jnp.exp(jnp.where(jj <= ii, dmat * lgf, -dmat * lgb))
            rvec_s[hd, RV_TAIL_F] = jnp.exp((C - 1.0 - irow) * lgf)
            rvec_s[hd, RV_TAIL_B] = jnp.exp(irow * lgb)
            rvec_s[hd, RV_CROSS_F] = jnp.exp((irow + 1.0) * lgf)
            rvec_s[hd, RV_CROSS_B] = jnp.exp((C - irow) * lgb)
        if use_rope:
            ln = lax.broadcasted_iota(jnp.int32, (C, LANES), 1)
            nf = RET_HD // 4
            inv = jnp.exp((ln & (nf - 1)).astype(F32) * (-math.log(ROPE_BASE) / nf))
            for cc in range(nc):
                t = lax.broadcasted_iota(jnp.int32, (C, LANES), 0) + cc * C
                pos = jnp.where((ln & (2 * nf - 1)) < nf, t >> GRID_SHIFT, t & (GRID_W - 1)).astype(F32)
                ang = pos * inv
                cos_s[cc] = jnp.cos(ang)
                sin_s[cc] = jnp.where(ln < RET_HD // 2, -jnp.sin(ang), jnp.sin(ang))
        weight_copy(0).wait()
        if nc > 1:
            for i in rest:
                weight_copy(i).start()
        wdtr_s[...] = wmain_ref[:, MAIN_COLS:MAIN_COLS + LANES].astype(F32).T.astype(BF16)

    def project(c):
        st = {}
        CB = 256
        assert CONV_W == 5 and HALO >= CONV_W // 2

        def v_mod():
            sh1 = mod_row(row_mix, 0)
            sc1 = mod_row(row_mix, 1)
            xe = jnp.concatenate([xprev_ref[0], x_ref[0], xnext_ref[0]], axis=0)
            xm = xe * (1.0 + sc1) + sh1
            st["he"] = xm.astype(BF16)
            st["h"] = xm[HALO:HALO + C].astype(BF16)

        def m_main(name, lo, hi, halo=False):
            def run():
                st[name] = _dot(st["he" if halo else "h"], wmain_ref[:, lo:hi])
            return run

        def m_dt():
            raw_c = _dot(st["h"], wmain_ref[:, MAIN_COLS:MAIN_COLS + LANES])
            st["dtc"] = raw_c + pltpu.roll(raw_c, SSD_HEADS, 1)
            raw_r = _dot_nt(wdtr_s[0:N_DT, :], st["h"])
            st["dtr"] = raw_r + pltpu.roll(raw_r, SSD_HEADS, 0)

        def v_dt():
            dt_c = _softplus(st["dtc"] + dt_bias_row)
            lac = dt_c * nega_row
            inc_col = _cumsum(lac, 0)
            exc_col = inc_col - lac
            tot_col = inc_col[C - 1:C, :]
            inccol_s[c] = inc_col
            exccol_s[c] = exc_col
            dec_s[c, 0:SUBLANES, :] = jnp.broadcast_to(jnp.exp(tot_col), (SUBLANES, LANES))
            if cross:
                cfcb_s[c] = jnp.exp(jnp.where(lane < SSD_HEADS, inc_col, tot_col - exc_col))
            st["sf"] = jnp.exp(tot_col - inc_col) * dt_c
            st["sb"] = jnp.exp(exc_col) * dt_c
            dt_r = _softplus(st["dtr"] + dt_bias_col)
            lar = dt_r * nega_col
            inc_row = _cumsum(lar, 1)
            ldt = jnp.log(dt_r)
            rid = lax.broadcasted_iota(jnp.int32, (N_DT, C), 0)
            rowarg_s[c] = jnp.where(rid < SSD_HEADS, inc_row - ldt, inc_row - lar + ldt)

        def v_stage(hf):
            def run():
                pe = st["pe%d" % hf]
                cs = slice(hf * XBC_HALF, (hf + 1) * XBC_HALF)
                stage_s[0:HALO, cs] = jnp.where(c > 0, pe[0:HALO], 0.0)
                stage_s[HALO:HALO + C, cs] = pe[HALO:HALO + C]
                stage_s[HALO + C:, cs] = jnp.where(c < nc - 1, pe[HALO + C:], 0.0)
            return run

        def conv_block(cb):
            cs = slice(cb * CB, (cb + 1) * CB)
            rows = C + 2 * HALO
            xin = stage_s[:, cs]
            taps = [convw_ref[k:k + 1, cs] * xin for k in range(CONV_W)]
            up = lambda a: pltpu.roll(a, rows - 1, 0)
            down = lambda a: pltpu.roll(a, 1, 0)
            acc = taps[2] + up(taps[3] + up(taps[4])) + down(taps[1] + down(taps[0]))
            return _silu(acc[HALO:HALO + C] + convb_ref[0:1, cs])

        def v_conv_x(cb):
            def run():
                xs_s[c, :, cb * CB:(cb + 1) * CB] = conv_block(cb)
            return run

        def v_conv_b():
            bT_s[c] = conv_block(SSD_W // CB).T.astype(BF16)

        def v_conv_c():
            c_s[c] = conv_block(SSD_W // CB + 1).astype(BF16)

        def rope(a):
            if not use_rope:
                return a
            return a * cos_s[c] + pltpu.roll(a, RET_HD // 2, 1) * sin_s[c]

        def v_q():
            for hd in range(RET_HEADS):
                sl = slice(hd * RET_HD, (hd + 1) * RET_HD)
                q_s[c, :, sl] = rope(st["pq"][:, sl]).astype(BF16)

        def v_k():
            for hd in range(RET_HEADS):
                sl = slice(hd * RET_HD, (hd + 1) * RET_HD)
                kh = rope(st["pk"][:, sl]) * (RET_HD ** -0.5)
                kT_s[c, sl, :] = kh.T.astype(BF16)

        def v_v():
            v_s[c] = st["pv"].astype(BF16)

        def v_g():
            g_s[c] = _silu(st["pg"]).astype(BF16)

        def v_z():
            z_s[c] = _silu(st["pz"]).astype(BF16)

        def m_sloc(gi):
            def run():
                xg = xs_s[c, :, gi * GROUP_W:(gi + 1) * GROUP_W]
                vcat = jnp.concatenate([xg * expand_f(st["sf"], gi), xg * expand_b(st["sb"], gi)],
                                       axis=1).astype(BF16)
                sloc_s[c, gi] = _dot(bT_s[c, gi * SSD_STATE:(gi + 1) * SSD_STATE, :], vcat).astype(sloc_s.dtype)
            return run

        def m_rloc(hd):
            def run():
                sl = slice(hd * RET_HD, (hd + 1) * RET_HD)
                vf = st["pv"][:, sl]
                vcat = jnp.concatenate([vf * rvec_s[hd, RV_TAIL_F], vf * rvec_s[hd, RV_TAIL_B]],
                                       axis=1).astype(BF16)
                rloc_s[c, hd] = _dot(kT_s[c, sl, :], vcat).astype(rloc_s.dtype)
            return run

        m_pe0 = m_main("pe0", XBC_COL0, XBC_COL0 + XBC_HALF, halo=True)
        m_pe1 = m_main("pe1", XBC_COL0 + XBC_HALF, MAIN_COLS, halo=True)
        m_q = m_main("pq", 0, RET_W)
        m_k = m_main("pk", RET_W, 2 * RET_W)
        m_v = m_main("pv", 2 * RET_W, 3 * RET_W)
        m_g = m_main("pg", 3 * RET_W, 4 * RET_W)
        m_z = m_main("pz", 4 * RET_W, XBC_COL0)
        order = [v_mod, m_dt, m_pe0, m_pe1, v_dt, m_q, v_stage(0), v_stage(1), m_k, v_conv_x(0), m_v,
                 v_conv_x(1), v_q, m_g, v_conv_b, v_k, m_z, v_conv_c, v_v,
                 m_rloc(0), m_rloc(1), v_g, m_rloc(2), m_rloc(3), m_sloc(0), m_sloc(1), v_z]
        for piece in order:
            piece()

    def recurrences():
        for hd in range(RET_HEADS):
            lgf, lgb = ret_decays(hd)
            dec_f = jnp.exp(C * lgf)
            dec_b = jnp.exp(C * lgb)
            if has_state:
                ent_f = sret0_ref[0, 0, 0, hd]
                ent_b = sret0_ref[0, 0, 1, hd]
            else:
                ent_f = jnp.zeros((RET_HD, RET_HD), F32)
                ent_b = jnp.zeros((RET_HD, RET_HD), F32)
            for c in range(nc):
                if cross:
                    rent_s[c, hd, :, 0:RET_HD] = ent_f.astype(BF16)
                ent_f = dec_f * ent_f + rloc_s[c, hd, :, 0:RET_HD]
            for c in range(nc - 1, -1, -1):
                if cross:
                    rent_s[c, hd, :, RET_HD:] = ent_b.astype(BF16)
                ent_b = dec_b * ent_b + rloc_s[c, hd, :, RET_HD:]
            if emit_state:
                nret_ref[0, 0, 0, hd] = ent_f
                nret_ref[0, 0, 1, hd] = ent_b

        for gi in range(SSD_GROUPS):
            heads = group_heads(gi)
            if has_state:
                ent_f = jnp.concatenate([sssd0_ref[0, 0, 0, hh] for hh in heads], axis=0).T
                ent_b = jnp.concatenate([sssd0_ref[0, 0, 1, hh] for hh in heads], axis=0).T
            else:
                ent_f = jnp.zeros((SSD_STATE, GROUP_W), F32)
                ent_b = jnp.zeros((SSD_STATE, GROUP_W), F32)
            for c in range(nc):
                if cross:
                    sent_s[c, gi, :, 0:GROUP_W] = ent_f.astype(BF16)
                ent_f = expand_f(dec_s[c, 0:1, :], gi) * ent_f + sloc_s[c, gi, :, 0:GROUP_W]
            for c in range(nc - 1, -1, -1):
                if cross:
                    sent_s[c, gi, :, GROUP_W:] = ent_b.astype(BF16)
                ent_b = expand_b(dec_s[c, 0:1, :], gi) * ent_b + sloc_s[c, gi, :, GROUP_W:]
            if emit_state:
                ent_ft = ent_f.T
                ent_bt = ent_b.T
                for k, hh in enumerate(heads):
                    nssd_ref[0, 0, 0, hh] = ent_ft[k * SSD_HD:(k + 1) * SSD_HD, :]
                    nssd_ref[0, 0, 1, hh] = ent_bt[k * SSD_HD:(k + 1) * SSD_HD, :]

    def emit_pieces(c, slot):
        rsl = [slice(hd * RET_HD, (hd + 1) * RET_HD) for hd in range(RET_HEADS)]
        gsl = [slice(gi * SSD_STATE, (gi + 1) * SSD_STATE) for gi in range(SSD_GROUPS)]
        xsl = [slice(gi * GROUP_W, (gi + 1) * GROUP_W) for gi in range(SSD_GROUPS)]
        st = {}

        def scores():
            ii = lax.broadcasted_iota(jnp.int32, (C, C), 0)
            jj = lax.broadcasted_iota(jnp.int32, (C, C), 1)
            st["causal"] = jj <= ii
            qs = [q_s[c, :, sl] for sl in rsl]
            cms = [c_s[c, :, sl] for sl in gsl]
            st["sc_s"] = [_dot(cms[gi], bT_s[c, gsl[gi], :]) for gi in range(SSD_GROUPS)]
            st["sc_r"] = [_dot(qs[hd], kT_s[c, rsl[hd], :]) for hd in range(RET_HEADS)]
            if cross:
                st["yc_s"] = [_dot(cms[gi], sent_s[c, gi]) for gi in range(SSD_GROUPS)]
                st["yc_r"] = [_dot(qs[hd], rent_s[c, hd]) for hd in range(RET_HEADS)]
            st["inc_col"] = inccol_s[c]
            st["exc_col"] = exccol_s[c]
            st["rowarg"] = rowarg_s[c]

        def ssd_head(gi, k):
            def run():
                hh = gi * HPG + k
                hb = SSD_HEADS + hh
                arg = jnp.where(st["causal"],
                                st["inc_col"][:, hh:hh + 1] - st["rowarg"][hh:hh + 1, :],
                                st["rowarg"][hb:hb + 1, :] - st["exc_col"][:, hb:hb + 1])
                m = (st["sc_s"][gi] * jnp.exp(arg)).astype(BF16)
                xh = xs_s[c, :, hh * SSD_HD:(hh + 1) * SSD_HD].astype(BF16)
                y_s[:, hh * SSD_HD:(hh + 1) * SSD_HD] = _dot(m, xh)
            return run

        def ssd_group(gi):
            def run():
                xg = xs_s[c, :, xsl[gi]]
                yg = y_s[:, xsl[gi]] + expand_f(dskip_row, gi) * xg
                if cross:
                    cfcb = cfcb_s[c]
                    yc = st["yc_s"][gi]
                    yg = yg + expand_f(cfcb, gi) * yc[:, 0:GROUP_W] + expand_b(cfcb, gi) * yc[:, GROUP_W:]
                y_s[:, xsl[gi]] = yg
            return run

        def ret_head(hd):
            def run():
                m = (st["sc_r"][hd] * wdec_s[hd]).astype(BF16)
                o = _dot(m, v_s[c, :, rsl[hd]])
                if cross:
                    yc = st["yc_r"][hd]
                    o = o + rvec_s[hd, RV_CROSS_F] * yc[:, 0:RET_HD] + rvec_s[hd, RV_CROSS_B] * yc[:, RET_HD:]
                o = o * lax.rsqrt(jnp.mean(o * o, axis=-1, keepdims=True) + EPS)
                mix_s[:, rsl[hd]] = (g_s[c, :, rsl[hd]].astype(F32) * o).astype(BF16)
            return run

        def ssd_norm():
            yz = y_s[...] * z_s[c].astype(F32)
            yn = yz * lax.rsqrt(jnp.mean(yz * yz, axis=-1, keepdims=True) + EPS) * normw_ref[...]
            mix_s[:, RET_W:] = yn.astype(BF16)

        def out_proj():
            g1 = mod_row(row_mix, 2)
            y = ALPHA * x_ref[0] + g1 * _dot(mix_s[...], wout_ref[...])
            x1_s[slot] = _layer_norm(y, ln1g_ref[...], ln1b_ref[...])

        pieces = [scores]
        for gi in range(SSD_GROUPS):
            pieces += [ssd_head(gi, k) for k in range(HPG)] + [ssd_group(gi)]
        pieces += [ret_head(hd) for hd in range(RET_HEADS)] + [ssd_norm, out_proj]
        return pieces

    def ffn_pieces(slot):
        st = {}

        def start():
            sh2 = mod_row(row_ffn, 3)
            sc2 = mod_row(row_ffn, 4)
            st["h2"] = (x1_s[slot] * (1.0 + sc2) + sh2).astype(BF16)

        def hidden(j):
            def run():
                js = slice(j * FF_BLK, (j + 1) * FF_BLK)
                h2 = st["h2"]
                hid_s[:, js] = (_silu(_dot(h2, wg_ref[:, js])) * _dot(h2, wu_ref[:, js])).astype(BF16)
            return run

        def finish():
            g2 = mod_row(row_ffn, 5)
            x1_s[slot] = ALPHA * x1_s[slot] + g2 * _dot(hid_s[...], wd_ref[...])

        return [start] + [hidden(j) for j in range(D_FF // FF_BLK)] + [finish]

    def ln2(slot):
        def run():
            out_ref[0] = _layer_norm(x1_s[slot], ln2g_ref[...], ln2b_ref[...])
        return run

    def run_all(pieces):
        for p in pieces:
            p()

    def fused(ffn, others):
        _interleave(ffn[:-1], others)
        ffn[-1]()

    if nc == 1:
        slot = seq & 1

        @pl.when(seq == 0)
        def _():
            project(0)
            for i in range(1, len(big_weights)):
                weight_copy(i).wait()
            run_all([recurrences] + emit_pieces(0, slot))
            x1_s[1] = jnp.zeros((C, D_MODEL), F32)

        @pl.when((seq > 0) & (seq < nb))
        def _():
            project(0)
            fused(ffn_pieces(1 - slot), [ln2(slot), recurrences] + emit_pieces(0, slot))

        @pl.when(seq == nb)
        def _():
            _interleave(ffn_pieces(1 - slot), [ln2(slot)])

        @pl.when(seq == nb + 1)
        def _():
            ln2(slot)()
        return

    @pl.when((step < nc) & (seq < nb))
    def _():
        project(step)

    @pl.when((step == nc) & (seq < nb))
    def _():
        recurrences()

    kk = step - nc
    slot = (seq * nc + kk) & 1
    first = (seq == 0) & (step == nc)

    @pl.when(first)
    def _():
        for i in range(1, len(big_weights)):
            weight_copy(i).wait()
        run_all(emit_pieces(kk, slot))
        x1_s[1] = jnp.zeros((C, D_MODEL), F32)

    @pl.when((step >= nc) & (seq < nb) & jnp.logical_not(first))
    def _():
        fused(ffn_pieces(1 - slot), [ln2(slot)] + emit_pieces(kk, slot))

    @pl.when((seq == nb) & (step == nc))
    def _():
        _interleave(ffn_pieces(1 - slot), [ln2(slot)])

    @pl.when((seq == nb) & (step == nc + 1))
    def _():
        ln2(slot)()


def _const_spec(shape):
    nd = len(shape)
    return pl.BlockSpec(shape, lambda b, s: (0,) * nd, pipeline_mode=pl.Buffered(1))


def _layer_call(x, mod, mod_block, mod_per_seq, weights, states, emit_state, use_rope, name):
    nb, L, _ = x.shape
    nc = L // CHUNK
    hpc_blocks = CHUNK // HALO
    has_state = states is not None
    last = nb - 1

    def chunk_of(b, s):
        return jnp.where(b > last, nc - 1, jnp.where(s < nc, s, s - nc))

    def halo_chunk(b, s):
        return jnp.where(b > last, nc - 1, jnp.minimum(s, nc - 1))

    def seq_of(b):
        return jnp.minimum(b, last)

    def out_map(b, s):
        emits = b + 1 if nc == 1 else b * nc + jnp.maximum(s - nc, -1) + 1
        blk = jnp.clip(emits - 3, 0, nb * nc - 1)
        return (blk // nc, blk % nc, 0)

    assert not mod_per_seq or nb <= MOD_ROWS
    in_specs = [
        pl.BlockSpec((1, CHUNK, D_MODEL), lambda b, s: (seq_of(b), chunk_of(b, s), 0)),
        pl.BlockSpec((1, HALO, D_MODEL),
                     lambda b, s: (seq_of(b), jnp.maximum(halo_chunk(b, s) * hpc_blocks - 1, 0), 0)),
        pl.BlockSpec((1, HALO, D_MODEL),
                     lambda b, s: (seq_of(b), jnp.minimum((halo_chunk(b, s) + 1) * hpc_blocks, nc * hpc_blocks - 1), 0)),
        pl.BlockSpec((MOD_ROWS, 6 * D_MODEL), lambda b, s: (mod_block, 0)),
    ] + [pl.BlockSpec(memory_space=pl.ANY) if i in BIG_WEIGHTS else _const_spec(w.shape)
         for i, w in enumerate(weights)]
    args = [x, x, x, mod] + list(weights)
    ret_block = (1, 1, 2, RET_HEADS, RET_HD, RET_HD)
    ssd_block = (1, 1, 2, SSD_HEADS, SSD_HD, SSD_STATE)
    state_map = lambda b, s: (seq_of(b), 0, 0, 0, 0, 0)
    if has_state:
        in_specs += [pl.BlockSpec(ret_block, state_map), pl.BlockSpec(ssd_block, state_map)]
        args += list(states)
    out_shape = [jax.ShapeDtypeStruct((nb, L, D_MODEL), F32)]
    out_specs = [pl.BlockSpec((1, CHUNK, D_MODEL), out_map)]
    if emit_state:
        out_shape += [jax.ShapeDtypeStruct((nb,) + ret_block[1:], F32),
                      jax.ShapeDtypeStruct((nb,) + ssd_block[1:], F32)]
        out_specs += [pl.BlockSpec(ret_block, state_map), pl.BlockSpec(ssd_block, state_map)]
    state_dt = F32 if emit_state else BF16
    scratch = [
        pltpu.VMEM((nc, CHUNK, RET_W), BF16),
        pltpu.VMEM((nc, RET_W, CHUNK), BF16),
        pltpu.VMEM((nc, CHUNK, RET_W), BF16),
        pltpu.VMEM((nc, CHUNK, RET_W), BF16),
        pltpu.VMEM((nc, CHUNK, SSD_W), BF16),
        pltpu.VMEM((CHUNK + 2 * HALO, CONV_CH), F32),
        pltpu.VMEM((nc, CHUNK, SSD_W), F32),
        pltpu.VMEM((nc, SSD_GROUPS * SSD_STATE, CHUNK), BF16),
        pltpu.VMEM((nc, CHUNK, SSD_GROUPS * SSD_STATE), BF16),
        pltpu.VMEM((nc, CHUNK, LANES), F32),
        pltpu.VMEM((nc, CHUNK, LANES), F32),
        pltpu.VMEM((nc, CHUNK, LANES), F32),
        pltpu.VMEM((nc, 4 * SUBLANES, LANES), F32),
        pltpu.VMEM((nc, N_DT, CHUNK), F32),
        pltpu.VMEM((CHUNK, SSD_W), F32),
        pltpu.VMEM((CHUNK, 2 * RET_W), BF16),
        pltpu.VMEM((nc, RET_HEADS, RET_HD, 2 * RET_HD), state_dt),
        pltpu.VMEM((nc, SSD_GROUPS, SSD_STATE, 2 * GROUP_W), state_dt),
        pltpu.VMEM((nc, RET_HEADS, RET_HD, 2 * RET_HD), BF16),
        pltpu.VMEM((nc, SSD_GROUPS, SSD_STATE, 2 * GROUP_W), BF16),
        pltpu.VMEM((RET_HEADS, CHUNK, CHUNK), F32),
        pltpu.VMEM((RET_HEADS, 4, CHUNK, LANES), F32),
        pltpu.VMEM((2, CHUNK, D_MODEL), F32),
        pltpu.VMEM((CHUNK, D_FF), BF16),
        pltpu.VMEM((LANES, D_MODEL), BF16),
    ]
    scratch += [pltpu.VMEM(weights[i].shape, BF16) for i in BIG_WEIGHTS]
    scratch += [pltpu.SemaphoreType.DMA((len(BIG_WEIGHTS),))]
    if use_rope:
        scratch += [pltpu.VMEM((nc, CHUNK, LANES), F32)] * 2
    kern = functools.partial(_layer_kernel, L=L, nb=nb, mod_per_seq=mod_per_seq, has_state=has_state,
                             use_rope=use_rope, emit_state=emit_state)
    return pl.pallas_call(
        kern,
        grid=(nb + 2, 1) if nc == 1 else (nb + 1, 2 * nc),
        in_specs=in_specs,
        out_specs=out_specs,
        out_shape=out_shape,
        scratch_shapes=scratch,
        compiler_params=pltpu.CompilerParams(dimension_semantics=("arbitrary", "arbitrary"),
                                             vmem_limit_bytes=VMEM_LIMIT),
        name=name,
    )(*args)


def kernel(x_prompt, x_sample, state_ret, state_ssd, c, c_ctx, w_in, ret_decay_fwd, ret_decay_bwd, conv_w, conv_b, dt_bias_fwd, dt_bias_bwd, a_log_fwd, a_log_bwd, d_skip, ssd_norm_w, w_out, ln1_g, ln1_b, w_gate, w_up, w_down, ln2_g, ln2_b, w_ada, b_ada):
    depth = w_in.shape[0]
    assert depth == 1, "single trunk layer"
    bp, lp, _ = x_prompt.shape
    bs, ls, _ = x_sample.shape
    assert lp % CHUNK == 0 and ls % CHUNK == 0 and ls % GRID_W == 0 and D_FF % FF_BLK == 0


    convw = jnp.zeros((SUBLANES, CONV_CH), F32).at[:CONV_W].set(conv_w[0])
    convb = conv_b[0][None, :]
    dt_bias = jnp.concatenate([dt_bias_fwd[0], dt_bias_bwd[0]])
    a_log = jnp.concatenate([a_log_fwd[0], a_log_bwd[0]])
    ret_decay = jnp.concatenate([ret_decay_fwd[0], ret_decay_bwd[0]])
    hp = jnp.zeros((SUBLANES, LANES), F32)
    hp = hp.at[0, :N_DT].set(dt_bias).at[1, :N_DT].set(a_log)
    hp = hp.at[2, :2 * RET_HEADS].set(ret_decay).at[3, :SSD_HEADS].set(d_skip[0])
    hpc = jnp.zeros((N_DT, LANES), F32).at[:, 0].set(dt_bias).at[:, 1].set(a_log)
    mod, w_in_b, w_out_b, w_gate_b, w_up_b, w_down_b = _prep_call(
        c, c_ctx[None, :], w_ada[0], b_ada[0][None, :],
        jnp.swapaxes(w_in[0], 0, 1), [w_out[0], w_gate[0], w_up[0], w_down[0]])
    weights = (w_in_b, convw, convb, hp, hpc, ssd_norm_w[0][None, :],
               w_out_b, ln1_g[0][None, :], ln1_b[0][None, :],
               w_gate_b, w_up_b, w_down_b, ln2_g[0][None, :], ln2_b[0][None, :])

    yp, new_ret, new_ssd = _layer_call(x_prompt, mod, 1, False, weights, None, True, False, "layer_ctx")
    states = (state_ret, jnp.swapaxes(state_ssd, -1, -2))
    (ys,) = _layer_call(x_sample, mod, 0, True, weights, states, False, True, "layer_lat")
    return (yp, ys, new_ret, jnp.swapaxes(new_ssd, -1, -2))
```

```python
import functools
import math

import jax
import jax.numpy as jnp
from jax import lax
from jax.experimental import pallas as pl
from jax.experimental.pallas import tpu as pltpu

F32 = jnp.float32
BF16 = jnp.bfloat16

D_MODEL = 1024
RET_W = 512
RET_HEADS = 4
RET_HD = 128
SSD_W = 512
SSD_HD = 64
SSD_HEADS = 8
SSD_GROUPS = 2
SSD_STATE = 128
HPG = SSD_HEADS // SSD_GROUPS
GROUP_W = HPG * SSD_HD
CONV_W = 5
CONV_CH = SSD_W + 2 * SSD_GROUPS * SSD_STATE
D_FF = 2816
GRID_W = 64
GRID_SHIFT = 6
ROPE_BASE = 10000.0
EPS = 1e-6
ALPHA = 2.0 ** 0.25
MAIN_COLS = 4 * RET_W + SSD_W + CONV_CH
XBC_COL0 = 4 * RET_W + SSD_W
N_DT = 2 * SSD_HEADS

CHUNK = 256
HALO = 8
FF_BLK = 256
MOD_ROWS = 8
CAST_STEPS = 8
CAST_WT_ROWS = 512
LANES = 128
SUBLANES = 8
XBC_HALF = CONV_CH // 2
VMEM_LIMIT = 62 * 1024 * 1024


def _dot(a, b):
    return jnp.dot(a, b, preferred_element_type=F32)


def _dot_nt(a, b):
    return lax.dot_general(a, b, (((1,), (1,)), ((), ())), preferred_element_type=F32)


def _silu(x):
    return x * jax.nn.sigmoid(x)


def _softplus(x):
    return jnp.maximum(x, 0.0) + jnp.log1p(jnp.exp(-jnp.abs(x)))


def _layer_norm(y, g, b):
    mu = jnp.mean(y, axis=-1, keepdims=True)
    yc = y - mu
    var = jnp.mean(yc * yc, axis=-1, keepdims=True)
    return yc * lax.rsqrt(var + EPS) * g + b


def _cumsum(x, axis):
    n = x.shape[axis]
    idx = lax.broadcasted_iota(jnp.int32, x.shape, axis)
    s = 1
    while s < n:
        x = x + jnp.where(idx >= s, pltpu.roll(x, s, axis), 0.0)
        s *= 2
    return x


def _expand4(cols, lane):
    a = jnp.where(lane < SSD_HD, cols[0], cols[1])
    b = jnp.where(lane < SSD_HD, cols[2], cols[3])
    return jnp.concatenate([a, b], axis=1)


def _interleave(a, b):
    ia = ib = 0
    while ia < len(a) or ib < len(b):
        if ib >= len(b) or (ia < len(a) and ia * len(b) <= ib * len(a)):
            a[ia]()
            ia += 1
        else:
            b[ib]()
            ib += 1


def _prep_kernel(c_ref, cctx_ref, wada_ref, bada_ref, wt_ref, *refs, wt_rows):
    n = (len(refs) - 2) // 2
    srcs, mod_ref, wt_out, dsts = refs[:n], refs[n], refs[n + 1], refs[n + 2:]
    cond = jnp.concatenate([c_ref[...], jnp.broadcast_to(cctx_ref[...], (MOD_ROWS, D_MODEL))], axis=0)
    mod_ref[...] = _dot(_silu(cond).astype(BF16), wada_ref[...].astype(BF16)) + bada_ref[...]
    blk = wt_ref.shape[0]
    row = lax.broadcasted_iota(jnp.int32, (blk, 1), 0) + pl.program_id(0) * blk
    wt_out[...] = jnp.where(row < wt_rows, wt_ref[...], 0.0).T.astype(BF16)
    for src, dst in zip(srcs, dsts):
        dst[...] = src[...].astype(BF16)


def _prep_call(c, c_ctx, w_ada, b_ada, wt, ws):
    assert c.shape == (MOD_ROWS, D_MODEL), c.shape
    for w in ws:
        assert w.shape[0] % (CAST_STEPS * 16) == 0, w.shape
    assert wt.shape[0] <= CAST_STEPS * CAST_WT_ROWS
    n_mod = w_ada.shape[1]
    ada_cols = n_mod // CAST_STEPS
    assert ada_cols % LANES == 0
    specs = [pl.BlockSpec((w.shape[0] // CAST_STEPS, w.shape[1]), lambda i: (i, 0)) for w in ws]
    return pl.pallas_call(
        functools.partial(_prep_kernel, wt_rows=wt.shape[0]),
        grid=(CAST_STEPS,),
        in_specs=[pl.BlockSpec((MOD_ROWS, D_MODEL), lambda i: (0, 0)),
                  pl.BlockSpec((1, D_MODEL), lambda i: (0, 0)),
                  pl.BlockSpec((D_MODEL, ada_cols), lambda i: (0, i)),
                  pl.BlockSpec((1, ada_cols), lambda i: (0, i)),
                  pl.BlockSpec((CAST_WT_ROWS, wt.shape[1]), lambda i: (i, 0))] + specs,
        out_specs=[pl.BlockSpec((2 * MOD_ROWS, ada_cols), lambda i: (0, i)),
                   pl.BlockSpec((wt.shape[1], CAST_WT_ROWS), lambda i: (0, i))] + specs,
        out_shape=[jax.ShapeDtypeStruct((2 * MOD_ROWS, n_mod), F32),
                   jax.ShapeDtypeStruct((wt.shape[1], -(-wt.shape[0] // LANES) * LANES), BF16)]
        + [jax.ShapeDtypeStruct(w.shape, BF16) for w in ws],
        compiler_params=pltpu.CompilerParams(dimension_semantics=("arbitrary",),
                                             vmem_limit_bytes=VMEM_LIMIT),
        name="prep_mod_and_weights",
    )(c, c_ctx, w_ada, b_ada, wt, *ws)


RV_TAIL_F, RV_TAIL_B, RV_CROSS_F, RV_CROSS_B = range(4)
BIG_WEIGHTS = (0, 6, 9, 10, 11)


def _layer_kernel(*refs, L, nb, mod_per_seq, has_state, use_rope, emit_state):
    nc = L // CHUNK
    C = CHUNK
    cross = has_state or nc > 1
    it = iter(refs)
    x_ref, xprev_ref, xnext_ref, mod_ref, wmain_hbm = (next(it) for _ in range(5))
    convw_ref, convb_ref, hp_ref, hpc_ref, normw_ref = (next(it) for _ in range(5))
    wout_hbm, ln1g_ref, ln1b_ref = (next(it) for _ in range(3))
    wg_hbm, wu_hbm, wd_hbm, ln2g_ref, ln2b_ref = (next(it) for _ in range(5))
    if has_state:
        sret0_ref, sssd0_ref = next(it), next(it)
    out_ref = next(it)
    if emit_state:
        nret_ref, nssd_ref = next(it), next(it)
    (q_s, kT_s, v_s, g_s, z_s, stage_s, xs_s, bT_s, c_s, inccol_s, exccol_s, cfcb_s, dec_s,
     rowarg_s, y_s, mix_s, rloc_s, sloc_s, rent_s, sent_s, wdec_s, rvec_s, x1_s, hid_s, wdtr_s) = (
         next(it) for _ in range(25))
    wmain_ref, wout_ref, wg_ref, wu_ref, wd_ref, wsem = (next(it) for _ in range(6))
    if use_rope:
        cos_s, sin_s = next(it), next(it)

    seq = pl.program_id(0)
    step = pl.program_id(1)

    if mod_per_seq:
        row_mix = jnp.minimum(seq, nb - 1)
        row_ffn = jnp.minimum(jnp.where(step > nc, seq, jnp.maximum(seq - 1, 0)), nb - 1)
    else:
        row_mix = row_ffn = 0

    def mod_row(row, k):
        return mod_ref[pl.ds(row, 1), k * D_MODEL:(k + 1) * D_MODEL]

    big_weights = ((wmain_hbm, wmain_ref), (wout_hbm, wout_ref), (wg_hbm, wg_ref), (wu_hbm, wu_ref),
                   (wd_hbm, wd_ref))

    def weight_copy(i):
        return pltpu.make_async_copy(big_weights[i][0], big_weights[i][1], wsem.at[i])

    hp = hp_ref[...]
    dt_bias_row = hp[0:1, :]
    nega_row = -jnp.exp(hp[1:2, :])
    lg_row = -_softplus(-hp[2:3, :])
    dskip_row = hp[3:4, :]
    hpc = hpc_ref[...]
    dt_bias_col = hpc[:, 0:1]
    nega_col = -jnp.exp(hpc[:, 1:2])

    lane = lax.broadcasted_iota(jnp.int32, (1, LANES), 1)

    def ret_decays(hd):
        lgf = lg_row[:, hd:hd + 1]
        lgb = lg_row[:, RET_HEADS + hd:RET_HEADS + hd + 1]
        return lgf, lgb

    def group_heads(gi):
        return [gi * HPG + k for k in range(HPG)]

    def expand_f(arr, gi):
        return _expand4([arr[:, hh:hh + 1] for hh in group_heads(gi)], lane)

    def expand_b(arr, gi):
        return _expand4([arr[:, SSD_HEADS + hh:SSD_HEADS + hh + 1] for hh in group_heads(gi)], lane)

    @pl.when((seq == 0) & (step == 0))
    def _():
        weight_copy(0).start()
        rest = range(1, len(big_weights)) if nc > 1 else range(2, len(big_weights))
        if nc == 1:
            weight_copy(1).start()
        ii = lax.broadcasted_iota(jnp.int32, (C, C), 0)
        jj = lax.broadcasted_iota(jnp.int32, (C, C), 1)
        dmat = (ii - jj).astype(F32)
        irow = lax.broadcasted_iota(jnp.int32, (C, LANES), 0).astype(F32)
        for hd in range(RET_HEADS):
            lgf, lgb = ret_decays(hd)
            wdec_s[hd] = jnp.exp(jnp.where(jj <= ii, dmat * lgf, -dmat * lgb))
            rvec_s[hd, RV_TAIL_F] = jnp.exp((C - 1.0 - irow) * lgf)
            rvec_s[hd, RV_TAIL_B] = jnp.exp(irow * lgb)
            rvec_s[hd, RV_CROSS_F] = jnp.exp((irow + 1.0) * lgf)
            rvec_s[hd, RV_CROSS_B] = jnp.exp((C - irow) * lgb)
        if use_rope:
            ln = lax.broadcasted_iota(jnp.int32, (C, LANES), 1)
            nf = RET_HD // 4
            inv = jnp.exp((ln & (nf - 1)).astype(F32) * (-math.log(ROPE_BASE) / nf))
            for cc in range(nc):
                t = lax.broadcasted_iota(jnp.int32, (C, LANES), 0) + cc * C
                pos = jnp.where((ln & (2 * nf - 1)) < nf, t >> GRID_SHIFT, t & (GRID_W - 1)).astype(F32)
                ang = pos * inv
                cos_s[cc] = jnp.cos(ang)
                sin_s[cc] = jnp.where(ln < RET_HD // 2, -jnp.sin(ang), jnp.sin(ang))
        weight_copy(0).wait()
        for i in rest:
            weight_copy(i).start()
        wdtr_s[...] = wmain_ref[:, MAIN_COLS:MAIN_COLS + LANES].astype(F32).T.astype(BF16)

    def project(c):
        st = {}
        CB = 256
        assert CONV_W == 5 and HALO >= CONV_W // 2

        def v_mod():
            sh1 = mod_row(row_mix, 0)
            sc1 = mod_row(row_mix, 1)
            xe = jnp.concatenate([xprev_ref[0], x_ref[0], xnext_ref[0]], axis=0)
            xm = xe * (1.0 + sc1) + sh1
            st["he"] = xm.astype(BF16)
            st["h"] = xm[HALO:HALO + C].astype(BF16)

        def m_main(name, lo, hi, halo=False):
            def run():
                st[name] = _dot(st["he" if halo else "h"], wmain_ref[:, lo:hi])
            return run

        def m_dt():
            raw_c = _dot(st["h"], wmain_ref[:, MAIN_COLS:MAIN_COLS + LANES])
            st["dtc"] = raw_c + pltpu.roll(raw_c, SSD_HEADS, 1)
            raw_r = _dot_nt(wdtr_s[0:N_DT, :], st["h"])
            st["dtr"] = raw_r + pltpu.roll(raw_r, SSD_HEADS, 0)

        def v_dt():
            dt_c = _softplus(st["dtc"] + dt_bias_row)
            lac = dt_c * nega_row
            inc_col = _cumsum(lac, 0)
            exc_col = inc_col - lac
            tot_col = inc_col[C - 1:C, :]
            inccol_s[c] = inc_col
            exccol_s[c] = exc_col
            dec_s[c, 0:SUBLANES, :] = jnp.broadcast_to(jnp.exp(tot_col), (SUBLANES, LANES))
            if cross:
                cfcb_s[c] = jnp.exp(jnp.where(lane < SSD_HEADS, inc_col, tot_col - exc_col))
            st["sf"] = jnp.exp(tot_col - inc_col) * dt_c
            st["sb"] = jnp.exp(exc_col) * dt_c
            dt_r = _softplus(st["dtr"] + dt_bias_col)
            lar = dt_r * nega_col
            inc_row = _cumsum(lar, 1)
            ldt = jnp.log(dt_r)
            rid = lax.broadcasted_iota(jnp.int32, (N_DT, C), 0)
            rowarg_s[c] = jnp.where(rid < SSD_HEADS, inc_row - ldt, inc_row - lar + ldt)

        def v_stage(hf):
            def run():
                pe = st["pe%d" % hf]
                cs = slice(hf * XBC_HALF, (hf + 1) * XBC_HALF)
                stage_s[0:HALO, cs] = jnp.where(c > 0, pe[0:HALO], 0.0)
                stage_s[HALO:HALO + C, cs] = pe[HALO:HALO + C]
                stage_s[HALO + C:, cs] = jnp.where(c < nc - 1, pe[HALO + C:], 0.0)
            return run

        def conv_block(cb):
            cs = slice(cb * CB, (cb + 1) * CB)
            rows = C + 2 * HALO
            xin = stage_s[:, cs]
            taps = [convw_ref[k:k + 1, cs] * xin for k in range(CONV_W)]
            up = lambda a: pltpu.roll(a, rows - 1, 0)
            down = lambda a: pltpu.roll(a, 1, 0)
            acc = taps[2] + up(taps[3] + up(taps[4])) + down(taps[1] + down(taps[0]))
            return _silu(acc[HALO:HALO + C] + convb_ref[0:1, cs])

        def v_conv_x(cb):
            def run():
                xs_s[c, :, cb * CB:(cb + 1) * CB] = conv_block(cb)
            return run

        def v_conv_b():
            bT_s[c] = conv_block(SSD_W // CB).T.astype(BF16)

        def v_conv_c():
            c_s[c] = conv_block(SSD_W // CB + 1).astype(BF16)

        def rope(a):
            if not use_rope:
                return a
            return a * cos_s[c] + pltpu.roll(a, RET_HD // 2, 1) * sin_s[c]

        def v_q():
            for hd in range(RET_HEADS):
                sl = slice(hd * RET_HD, (hd + 1) * RET_HD)
                q_s[c, :, sl] = rope(st["pq"][:, sl]).astype(BF16)

        def v_k():
            for hd in range(RET_HEADS):
                sl = slice(hd * RET_HD, (hd + 1) * RET_HD)
                kh = rope(st["pk"][:, sl]) * (RET_HD ** -0.5)
                kT_s[c, sl, :] = kh.T.astype(BF16)

        def v_v():
            v_s[c] = st["pv"].astype(BF16)

        def v_g():
            g_s[c] = _silu(st["pg"]).astype(BF16)

        def v_z():
            z_s[c] = _silu(st["pz"]).astype(BF16)

        def m_sloc(gi):
            def run():
                xg = xs_s[c, :, gi * GROUP_W:(gi + 1) * GROUP_W]
                vcat = jnp.concatenate([xg * expand_f(st["sf"], gi), xg * expand_b(st["sb"], gi)],
                                       axis=1).astype(BF16)
                sloc_s[c, gi] = _dot(bT_s[c, gi * SSD_STATE:(gi + 1) * SSD_STATE, :], vcat).astype(sloc_s.dtype)
            return run

        def m_rloc(hd):
            def run():
                sl = slice(hd * RET_HD, (hd + 1) * RET_HD)
                vf = st["pv"][:, sl]
                vcat = jnp.concatenate([vf * rvec_s[hd, RV_TAIL_F], vf * rvec_s[hd, RV_TAIL_B]],
                                       axis=1).astype(BF16)
                rloc_s[c, hd] = _dot(kT_s[c, sl, :], vcat).astype(rloc_s.dtype)
            return run

        m_pe0 = m_main("pe0", XBC_COL0, XBC_COL0 + XBC_HALF, halo=True)
        m_pe1 = m_main("pe1", XBC_COL0 + XBC_HALF, MAIN_COLS, halo=True)
        m_q = m_main("pq", 0, RET_W)
        m_k = m_main("pk", RET_W, 2 * RET_W)
        m_v = m_main("pv", 2 * RET_W, 3 * RET_W)
        m_g = m_main("pg", 3 * RET_W, 4 * RET_W)
        m_z = m_main("pz", 4 * RET_W, XBC_COL0)
        order = [v_mod, m_dt, m_pe0, m_pe1, v_dt, m_q, v_stage(0), v_stage(1), m_k, v_conv_x(0), m_v,
                 v_conv_x(1), v_q, m_g, v_conv_b, v_k, m_z, v_conv_c, v_v,
                 m_rloc(0), m_rloc(1), v_g, m_rloc(2), m_rloc(3), m_sloc(0), m_sloc(1), v_z]
        for piece in order:
            piece()

    def recurrences():
        for hd in range(RET_HEADS):
            lgf, lgb = ret_decays(hd)
            dec_f = jnp.exp(C * lgf)
            dec_b = jnp.exp(C * lgb)
            if has_state:
                ent_f = sret0_ref[0, 0, 0, hd]
                ent_b = sret0_ref[0, 0, 1, hd]
            else:
                ent_f = jnp.zeros((RET_HD, RET_HD), F32)
                ent_b = jnp.zeros((RET_HD, RET_HD), F32)
            for c in range(nc):
                if cross:
                    rent_s[c, hd, :, 0:RET_HD] = ent_f.astype(BF16)
                ent_f = dec_f * ent_f + rloc_s[c, hd, :, 0:RET_HD]
            for c in range(nc - 1, -1, -1):
                if cross:
                    rent_s[c, hd, :, RET_HD:] = ent_b.astype(BF16)
                ent_b = dec_b * ent_b + rloc_s[c, hd, :, RET_HD:]
            if emit_state:
                nret_ref[0, 0, 0, hd] = ent_f
                nret_ref[0, 0, 1, hd] = ent_b

        for gi in range(SSD_GROUPS):
            heads = group_heads(gi)
            if has_state:
                ent_f = jnp.concatenate([sssd0_ref[0, 0, 0, hh] for hh in heads], axis=0).T
                ent_b = jnp.concatenate([sssd0_ref[0, 0, 1, hh] for hh in heads], axis=0).T
            else:
                ent_f = jnp.zeros((SSD_STATE, GROUP_W), F32)
                ent_b = jnp.zeros((SSD_STATE, GROUP_W), F32)
            for c in range(nc):
                if cross:
                    sent_s[c, gi, :, 0:GROUP_W] = ent_f.astype(BF16)
                ent_f = expand_f(dec_s[c, 0:1, :], gi) * ent_f + sloc_s[c, gi, :, 0:GROUP_W]
            for c in range(nc - 1, -1, -1):
                if cross:
                    sent_s[c, gi, :, GROUP_W:] = ent_b.astype(BF16)
                ent_b = expand_b(dec_s[c, 0:1, :], gi) * ent_b + sloc_s[c, gi, :, GROUP_W:]
            if emit_state:
                ent_ft = ent_f.T
                ent_bt = ent_b.T
                for k, hh in enumerate(heads):
                    nssd_ref[0, 0, 0, hh] = ent_ft[k * SSD_HD:(k + 1) * SSD_HD, :]
                    nssd_ref[0, 0, 1, hh] = ent_bt[k * SSD_HD:(k + 1) * SSD_HD, :]

    def emit_pieces(c, slot):
        rsl = [slice(hd * RET_HD, (hd + 1) * RET_HD) for hd in range(RET_HEADS)]
        gsl = [slice(gi * SSD_STATE, (gi + 1) * SSD_STATE) for gi in range(SSD_GROUPS)]
        xsl = [slice(gi * GROUP_W, (gi + 1) * GROUP_W) for gi in range(SSD_GROUPS)]
        st = {}

        def scores():
            ii = lax.broadcasted_iota(jnp.int32, (C, C), 0)
            jj = lax.broadcasted_iota(jnp.int32, (C, C), 1)
            st["causal"] = jj <= ii
            qs = [q_s[c, :, sl] for sl in rsl]
            cms = [c_s[c, :, sl] for sl in gsl]
            st["sc_s"] = [_dot(cms[gi], bT_s[c, gsl[gi], :]) for gi in range(SSD_GROUPS)]
            st["sc_r"] = [_dot(qs[hd], kT_s[c, rsl[hd], :]) for hd in range(RET_HEADS)]
            if cross:
                st["yc_s"] = [_dot(cms[gi], sent_s[c, gi]) for gi in range(SSD_GROUPS)]
                st["yc_r"] = [_dot(qs[hd], rent_s[c, hd]) for hd in range(RET_HEADS)]
            st["inc_col"] = inccol_s[c]
            st["exc_col"] = exccol_s[c]
            st["rowarg"] = rowarg_s[c]

        def ssd_head(gi, k):
            def run():
                hh = gi * HPG + k
                hb = SSD_HEADS + hh
                arg = jnp.where(st["causal"],
                                st["inc_col"][:, hh:hh + 1] - st["rowarg"][hh:hh + 1, :],
                                st["rowarg"][hb:hb + 1, :] - st["exc_col"][:, hb:hb + 1])
                m = (st["sc_s"][gi] * jnp.exp(arg)).astype(BF16)
                xh = xs_s[c, :, hh * SSD_HD:(hh + 1) * SSD_HD].astype(BF16)
                y_s[:, hh * SSD_HD:(hh + 1) * SSD_HD] = _dot(m, xh)
            return run

        def ssd_group(gi):
            def run():
                xg = xs_s[c, :, xsl[gi]]
                yg = y_s[:, xsl[gi]] + expand_f(dskip_row, gi) * xg
                if cross:
                    cfcb = cfcb_s[c]
                    yc = st["yc_s"][gi]
                    yg = yg + expand_f(cfcb, gi) * yc[:, 0:GROUP_W] + expand_b(cfcb, gi) * yc[:, GROUP_W:]
                y_s[:, xsl[gi]] = yg
            return run

        def ret_head(hd):
            def run():
                m = (st["sc_r"][hd] * wdec_s[hd]).astype(BF16)
                o = _dot(m, v_s[c, :, rsl[hd]])
                if cross:
                    yc = st["yc_r"][hd]
                    o = o + rvec_s[hd, RV_CROSS_F] * yc[:, 0:RET_HD] + rvec_s[hd, RV_CROSS_B] * yc[:, RET_HD:]
                o = o * lax.rsqrt(jnp.mean(o * o, axis=-1, keepdims=True) + EPS)
                mix_s[:, rsl[hd]] = (g_s[c, :, rsl[hd]].astype(F32) * o).astype(BF16)
            return run

        def ssd_norm():
            yz = y_s[...] * z_s[c].astype(F32)
            yn = yz * lax.rsqrt(jnp.mean(yz * yz, axis=-1, keepdims=True) + EPS) * normw_ref[...]
            mix_s[:, RET_W:] = yn.astype(BF16)

        def out_proj():
            g1 = mod_row(row_mix, 2)
            y = ALPHA * x_ref[0] + g1 * _dot(mix_s[...], wout_ref[...])
            x1_s[slot] = _layer_norm(y, ln1g_ref[...], ln1b_ref[...])

        pieces = [scores]
        for gi in range(SSD_GROUPS):
            pieces += [ssd_head(gi, k) for k in range(HPG)] + [ssd_group(gi)]
        pieces += [ret_head(hd) for hd in range(RET_HEADS)] + [ssd_norm, out_proj]
        return pieces

    def ffn_pieces(slot):
        st = {}

        def start():
            sh2 = mod_row(row_ffn, 3)
            sc2 = mod_row(row_ffn, 4)
            st["h2"] = (x1_s[slot] * (1.0 + sc2) + sh2).astype(BF16)

        def hidden(j):
            def run():
                js = slice(j * FF_BLK, (j + 1) * FF_BLK)
                h2 = st["h2"]
                hid_s[:, js] = (_silu(_dot(h2, wg_ref[:, js])) * _dot(h2, wu_ref[:, js])).astype(BF16)
            return run

        def finish():
            g2 = mod_row(row_ffn, 5)
            x1_s[slot] = ALPHA * x1_s[slot] + g2 * _dot(hid_s[...], wd_ref[...])

        return [start] + [hidden(j) for j in range(D_FF // FF_BLK)] + [finish]

    def ln2(slot):
        def run():
            out_ref[0] = _layer_norm(x1_s[slot], ln2g_ref[...], ln2b_ref[...])
        return run

    def run_all(pieces):
        for p in pieces:
            p()

    def fused(ffn, others):
        _interleave(ffn[:-1], others)
        ffn[-1]()

    if nc == 1:
        slot = seq & 1

        @pl.when(seq == 0)
        def _():
            project(0)
            weight_copy(1).wait()
            run_all([recurrences] + emit_pieces(0, slot))
            x1_s[1] = jnp.zeros((C, D_MODEL), F32)

        @pl.when(seq == 1)
        def _():
            for i in range(2, len(big_weights)):
                weight_copy(i).wait()

        @pl.when((seq > 0) & (seq < nb))
        def _():
            project(0)
            fused(ffn_pieces(1 - slot), [ln2(slot), recurrences] + emit_pieces(0, slot))

        @pl.when(seq == nb)
        def _():
            _interleave(ffn_pieces(1 - slot), [ln2(slot)])

        @pl.when(seq == nb + 1)
        def _():
            ln2(slot)()
        return

    @pl.when((step < nc) & (seq < nb))
    def _():
        project(step)

    @pl.when((step == nc) & (seq < nb))
    def _():
        recurrences()

    kk = step - nc
    slot = (seq * nc + kk) & 1
    first = (seq == 0) & (step == nc)

    @pl.when(first)
    def _():
        for i in range(1, len(big_weights)):
            weight_copy(i).wait()
        run_all(emit_pieces(kk, slot))
        x1_s[1] = jnp.zeros((C, D_MODEL), F32)

    @pl.when((step >= nc) & (seq < nb) & jnp.logical_not(first))
    def _():
        fused(ffn_pieces(1 - slot), [ln2(slot)] + emit_pieces(kk, slot))

    @pl.when((seq == nb) & (step == nc))
    def _():
        _interleave(ffn_pieces(1 - slot), [ln2(slot)])

    @pl.when((seq == nb) & (step == nc + 1))
    def _():
        ln2(slot)()


def _const_spec(shape):
    nd = len(shape)
    return pl.BlockSpec(shape, lambda b, s: (0,) * nd, pipeline_mode=pl.Buffered(1))


def _layer_call(x, mod, mod_block, mod_per_seq, weights, states, emit_state, use_rope, name):
    nb, L, _ = x.shape
    nc = L // CHUNK
    hpc_blocks = CHUNK // HALO
    has_state = states is not None
    last = nb - 1

    def chunk_of(b, s):
        return jnp.where(b > last, nc - 1, jnp.where(s < nc, s, s - nc))

    def halo_chunk(b, s):
        return jnp.where(b > last, nc - 1, jnp.minimum(s, nc - 1))

    def seq_of(b):
        return jnp.minimum(b, last)

    def out_map(b, s):
        emits = b + 1 if nc == 1 else b * nc + jnp.maximum(s - nc, -1) + 1
        blk = jnp.clip(emits - 3, 0, nb * nc - 1)
        return (blk // nc, blk % nc, 0)

    assert not mod_per_seq or nb <= MOD_ROWS
    in_specs = [
        pl.BlockSpec((1, CHUNK, D_MODEL), lambda b, s: (seq_of(b), chunk_of(b, s), 0)),
        pl.BlockSpec((1, HALO, D_MODEL),
                     lambda b, s: (seq_of(b), jnp.maximum(halo_chunk(b, s) * hpc_blocks - 1, 0), 0)),
        pl.BlockSpec((1, HALO, D_MODEL),
                     lambda b, s: (seq_of(b), jnp.minimum((halo_chunk(b, s) + 1) * hpc_blocks, nc * hpc_blocks - 1), 0)),
        pl.BlockSpec((MOD_ROWS, 6 * D_MODEL), lambda b, s: (mod_block, 0)),
    ] + [pl.BlockSpec(memory_space=pl.ANY) if i in BIG_WEIGHTS else _const_spec(w.shape)
         for i, w in enumerate(weights)]
    args = [x, x, x, mod] + list(weights)
    ret_block = (1, 1, 2, RET_HEADS, RET_HD, RET_HD)
    ssd_block = (1, 1, 2, SSD_HEADS, SSD_HD, SSD_STATE)
    state_map = lambda b, s: (seq_of(b), 0, 0, 0, 0, 0)
    if has_state:
        in_specs += [pl.BlockSpec(ret_block, state_map), pl.BlockSpec(ssd_block, state_map)]
        args += list(states)
    out_shape = [jax.ShapeDtypeStruct((nb, L, D_MODEL), F32)]
    out_specs = [pl.BlockSpec((1, CHUNK, D_MODEL), out_map)]
    if emit_state:
        out_shape += [jax.ShapeDtypeStruct((nb,) + ret_block[1:], F32),
                      jax.ShapeDtypeStruct((nb,) + ssd_block[1:], F32)]
        out_specs += [pl.BlockSpec(ret_block, state_map), pl.BlockSpec(ssd_block, state_map)]
    state_dt = F32 if emit_state else BF16
    scratch = [
        pltpu.VMEM((nc, CHUNK, RET_W), BF16),
        pltpu.VMEM((nc, RET_W, CHUNK), BF16),
        pltpu.VMEM((nc, CHUNK, RET_W), BF16),
        pltpu.VMEM((nc, CHUNK, RET_W), BF16),
        pltpu.VMEM((nc, CHUNK, SSD_W), BF16),
        pltpu.VMEM((CHUNK + 2 * HALO, CONV_CH), F32),
        pltpu.VMEM((nc, CHUNK, SSD_W), F32),
        pltpu.VMEM((nc, SSD_GROUPS * SSD_STATE, CHUNK), BF16),
        pltpu.VMEM((nc, CHUNK, SSD_GROUPS * SSD_STATE), BF16),
        pltpu.VMEM((nc, CHUNK, LANES), F32),
        pltpu.VMEM((nc, CHUNK, LANES), F32),
        pltpu.VMEM((nc, CHUNK, LANES), F32),
        pltpu.VMEM((nc, 4 * SUBLANES, LANES), F32),
        pltpu.VMEM((nc, N_DT, CHUNK), F32),
        pltpu.VMEM((CHUNK, SSD_W), F32),
        pltpu.VMEM((CHUNK, 2 * RET_W), BF16),
        pltpu.VMEM((nc, RET_HEADS, RET_HD, 2 * RET_HD), state_dt),
        pltpu.VMEM((nc, SSD_GROUPS, SSD_STATE, 2 * GROUP_W), state_dt),
        pltpu.VMEM((nc, RET_HEADS, RET_HD, 2 * RET_HD), BF16),
        pltpu.VMEM((nc, SSD_GROUPS, SSD_STATE, 2 * GROUP_W), BF16),
        pltpu.VMEM((RET_HEADS, CHUNK, CHUNK), F32),
        pltpu.VMEM((RET_HEADS, 4, CHUNK, LANES), F32),
        pltpu.VMEM((2, CHUNK, D_MODEL), F32),
        pltpu.VMEM((CHUNK, D_FF), BF16),
        pltpu.VMEM((LANES, D_MODEL), BF16),
    ]
    scratch += [pltpu.VMEM(weights[i].shape, BF16) for i in BIG_WEIGHTS]
    scratch += [pltpu.SemaphoreType.DMA((len(BIG_WEIGHTS),))]
    if use_rope:
        scratch += [pltpu.VMEM((nc, CHUNK, LANES), F32)] * 2
    kern = functools.partial(_layer_kernel, L=L, nb=nb, mod_per_seq=mod_per_seq, has_state=has_state,
                             use_rope=use_rope, emit_state=emit_state)
    return pl.pallas_call(
        kern,
        grid=(nb + 2, 1) if nc == 1 else (nb + 1, 2 * nc),
        in_specs=in_specs,
        out_specs=out_specs,
        out_shape=out_shape,
        scratch_shapes=scratch,
        compiler_params=pltpu.CompilerParams(dimension_semantics=("arbitrary", "arbitrary"),
                                             vmem_limit_bytes=VMEM_LIMIT),
        name=name,
    )(*args)


def kernel(x_prompt, x_sample, state_ret, state_ssd, c, c_ctx, w_in, ret_decay_fwd, ret_decay_bwd, conv_w, conv_b, dt_bias_fwd, dt_bias_bwd, a_log_fwd, a_log_bwd, d_skip, ssd_norm_w, w_out, ln1_g, ln1_b, w_gate, w_up, w_down, ln2_g, ln2_b, w_ada, b_ada):
    depth = w_in.shape[0]
    assert depth == 1, "single trunk layer"
    bp, lp, _ = x_prompt.shape
    bs, ls, _ = x_sample.shape
    assert lp % CHUNK == 0 and ls % CHUNK == 0 and ls % GRID_W == 0 and D_FF % FF_BLK == 0


    convw = jnp.zeros((SUBLANES, CONV_CH), F32).at[:CONV_W].set(conv_w[0])
    convb = conv_b[0][None, :]
    dt_bias = jnp.concatenate([dt_bias_fwd[0], dt_bias_bwd[0]])
    a_log = jnp.concatenate([a_log_fwd[0], a_log_bwd[0]])
    ret_decay = jnp.concatenate([ret_decay_fwd[0], ret_decay_bwd[0]])
    hp = jnp.zeros((SUBLANES, LANES), F32)
    hp = hp.at[0, :N_DT].set(dt_bias).at[1, :N_DT].set(a_log)
    hp = hp.at[2, :2 * RET_HEADS].set(ret_decay).at[3, :SSD_HEADS].set(d_skip[0])
    hpc = jnp.zeros((N_DT, LANES), F32).at[:, 0].set(dt_bias).at[:, 1].set(a_log)
    mod, w_in_b, w_out_b, w_gate_b, w_up_b, w_down_b = _prep_call(
        c, c_ctx[None, :], w_ada[0], b_ada[0][None, :],
        jnp.swapaxes(w_in[0], 0, 1), [w_out[0], w_gate[0], w_up[0], w_down[0]])
    weights = (w_in_b, convw, convb, hp, hpc, ssd_norm_w[0][None, :],
               w_out_b, ln1_g[0][None, :], ln1_b[0][None, :],
               w_gate_b, w_up_b, w_down_b, ln2_g[0][None, :], ln2_b[0][None, :])

    yp, new_ret, new_ssd = _layer_call(x_prompt, mod, 1, False, weights, None, True, False, "layer_ctx")
    states = (state_ret, jnp.swapaxes(state_ssd, -1, -2))
    (ys,) = _layer_call(x_sample, mod, 0, True, weights, states, False, True, "layer_lat")
    return (yp, ys, new_ret, jnp.swapaxes(new_ssd, -1, -2))
```
